```python
import math
import jax, jax.numpy as jnp
from jax import lax
import numpy as np

D_MODEL = 2048
BATCH = 8
SEQ = 2048
DEPTH = 1

HEAD_DIM = 128
N_HEADS_FOX = 8
N_HEADS_DIL = 8
D_FOX = N_HEADS_FOX * HEAD_DIM
D_DIL = N_HEADS_DIL * HEAD_DIM
MIX_WIDTH = D_FOX + D_DIL
D_IN_PROJ = 3 * D_FOX + N_HEADS_FOX + 3 * D_DIL
D_FF = 256 * (-(-8 * D_MODEL // (3 * 256)))
PLE_DIM = 256
Q_BLOCK = 128
DILATED_PATTERNS = ((128, 1), (512, 4), (2048, 16))
N_REL_BUCKETS = 32
REL_MAX_DISTANCE = 2048
RMS_EPS = 1e-6
NEG_INF = -1e30

kernel_name = "hybrid_fox_dilated_macaron_block"


def rmsnorm(x, g):
    xf = x.astype(jnp.float32)
    y = xf * lax.rsqrt(jnp.mean(xf * xf, axis=-1, keepdims=True) + RMS_EPS)
    return (y * g.astype(jnp.float32)).astype(x.dtype)


def swiglu(h, w_gate, w_up, w_down):
    return (jax.nn.silu(h @ w_gate) * (h @ w_up)) @ w_down


def t5_bucket(dist):
    max_exact = N_REL_BUCKETS // 2
    d = jnp.maximum(dist, 1).astype(jnp.float32)
    large = max_exact + (jnp.log(d / max_exact) / math.log(REL_MAX_DISTANCE / max_exact)
                         * (N_REL_BUCKETS - max_exact)).astype(jnp.int32)
    large = jnp.minimum(large, N_REL_BUCKETS - 1)
    return jnp.where(dist < max_exact, dist, large)


def forgetting_attention(q, k, v, log_f):
    B, S, H, Dh = q.shape
    scale = HEAD_DIM ** -0.5
    c = jnp.cumsum(log_f, axis=1).transpose(0, 2, 1)
    outs = []
    for i in range(S // Q_BLOCK):
        t0, t1 = i * Q_BLOCK, (i + 1) * Q_BLOCK
        s = jnp.einsum('bqhd,bkhd->bhqk', q[:, t0:t1], k[:, :t1]).astype(jnp.float32) * scale
        decay = c[:, :, t0:t1, None] - c[:, :, None, :t1]
        causal = (t0 + jnp.arange(Q_BLOCK))[:, None] >= jnp.arange(t1)[None, :]
        s = jnp.where(causal, s + decay, NEG_INF)
        pr = jax.nn.softmax(s, axis=-1)
        outs.append(jnp.einsum('bhqk,bkhd->bqhd', pr.astype(v.dtype), v[:, :t1]))
    return jnp.concatenate(outs, axis=1)


def dilated_window_attention(q, k, v, rel_table, window, dilation):
    B, S, H, Dh = q.shape
    n = window // dilation
    blk = n
    L = S // dilation
    nb = -(-L // blk)
    Lp = nb * blk
    scale = HEAD_DIM ** -0.5

    def sub(a):
        return a.reshape(B, L, dilation, H, Dh).transpose(0, 2, 1, 3, 4)

    qs = jnp.pad(sub(q), ((0, 0), (0, 0), (0, Lp - L), (0, 0), (0, 0)))
    kv_pad = ((0, 0), (0, 0), (blk, Lp - L), (0, 0), (0, 0))
    ks = jnp.pad(sub(k), kv_pad)
    vs = jnp.pad(sub(v), kv_pad)

    def windows(a):
        prev = a[:, :, :Lp].reshape(B, dilation, nb, blk, H, Dh)
        cur = a[:, :, blk:].reshape(B, dilation, nb, blk, H, Dh)
        return jnp.concatenate([prev, cur], axis=3)

    qb = qs.reshape(B, dilation, nb, blk, H, Dh)
    kw, vw = windows(ks), windows(vs)

    qi = jnp.arange(blk)[:, None]
    ki = jnp.arange(2 * blk)[None, :]
    rel = qi + blk - ki
    key_sub = (jnp.arange(nb) * blk)[:, None, None] - blk + ki[None]
    valid = (rel >= 0)[None] & (rel <= n)[None] & (key_sub >= 0)
    bias = rel_table[t5_bucket(jnp.maximum(rel, 0) * dilation)].transpose(2, 0, 1)

    s = jnp.einsum('brnqhd,brnkhd->brnhqk', qb, kw).astype(jnp.float32) * scale
    s = jnp.where(valid[:, None], s + bias.astype(jnp.float32), NEG_INF)
    m = jnp.max(s, axis=-1, keepdims=True)
    e = jnp.exp(s - m)
    denom = jnp.sum(e, axis=-1, keepdims=True)
    o = jnp.einsum('brnhqk,brnkhd->brnqhd', (e / denom).astype(v.dtype), vw)
    lse = (m + jnp.log(denom))[..., 0]

    o = o.reshape(B, dilation, Lp, H, Dh)[:, :, :L].transpose(0, 2, 1, 3, 4).reshape(B, S, H, Dh)
    lse = lse.transpose(0, 1, 2, 4, 3).reshape(B, dilation, Lp, H)[:, :, :L]
    lse = lse.transpose(0, 2, 1, 3).reshape(B, S, H)
    return o, lse


def token_mixer(h, w_in, b_f, w_o, rel_table):
    B, S, _ = h.shape
    u = h @ w_in
    splits = np.cumsum([D_FOX, D_FOX, D_FOX, N_HEADS_FOX, D_DIL, D_DIL]).tolist()
    qa, ka, va, f_logit, qb, kb, vb = jnp.split(u, splits, axis=-1)
    heads = lambda a, H: a.reshape(B, S, H, HEAD_DIM)

    log_f = jax.nn.log_sigmoid((f_logit + b_f).astype(jnp.float32))
    o_a = forgetting_attention(heads(qa, N_HEADS_FOX), heads(ka, N_HEADS_FOX),
                               heads(va, N_HEADS_FOX), log_f)

    qb, kb, vb = heads(qb, N_HEADS_DIL), heads(kb, N_HEADS_DIL), heads(vb, N_HEADS_DIL)
    outs, lses = [], []
    for window, dilation in DILATED_PATTERNS:
        o_i, lse_i = dilated_window_attention(qb, kb, vb, rel_table, window, dilation)
        outs.append(o_i)
        lses.append(lse_i)
    wts = jax.nn.softmax(jnp.stack(lses, axis=0), axis=0)
    o_b = jnp.sum(wts[..., None] * jnp.stack(outs, axis=0).astype(jnp.float32), axis=0).astype(h.dtype)

    cat = jnp.concatenate([o_a.reshape(B, S, D_FOX), o_b.reshape(B, S, D_DIL)], axis=-1)
    return cat @ w_o


def setup_inputs(seed: int = 0) -> dict:
    key = jax.random.key(seed)
    ks = jax.random.split(key, 24)
    nrm = lambda k, shape, s: jax.random.normal(k, shape, jnp.float32) * s
    gain = lambda k: 1.0 + 0.05 * jax.random.normal(k, (DEPTH, D_MODEL), jnp.float32)
    return {
        "x": nrm(ks[0], (BATCH, SEQ, D_MODEL), 1.0),
        "p": nrm(ks[1], (DEPTH, BATCH, SEQ, PLE_DIM), 1.0),
        "norm_ffn1": gain(ks[2]),
        "ffn1_w_gate": nrm(ks[3], (DEPTH, D_MODEL, D_FF), D_MODEL ** -0.5),
        "ffn1_w_up": nrm(ks[4], (DEPTH, D_MODEL, D_FF), D_MODEL ** -0.5),
        "ffn1_w_down": nrm(ks[5], (DEPTH, D_FF, D_MODEL), D_FF ** -0.5),
        "norm_mix": gain(ks[6]),
        "w_in": nrm(ks[7], (DEPTH, D_MODEL, D_IN_PROJ), D_MODEL ** -0.5),
        "b_f": 2.0 + nrm(ks[8], (DEPTH, N_HEADS_FOX), 0.1),
        "w_o": nrm(ks[9], (DEPTH, MIX_WIDTH, D_MODEL), MIX_WIDTH ** -0.5),
        "norm_ffn2": gain(ks[10]),
        "ffn2_w_gate": nrm(ks[11], (DEPTH, D_MODEL, D_FF), D_MODEL ** -0.5),
        "ffn2_w_up": nrm(ks[12], (DEPTH, D_MODEL, D_FF), D_MODEL ** -0.5),
        "ffn2_w_down": nrm(ks[13], (DEPTH, D_FF, D_MODEL), D_FF ** -0.5),
        "norm_ple": gain(ks[14]),
        "w_ple_gate": nrm(ks[15], (DEPTH, D_MODEL, D_MODEL), D_MODEL ** -0.5),
        "w_ple_proj": nrm(ks[16], (DEPTH, PLE_DIM, D_MODEL), PLE_DIM ** -0.5),
        "rel_table": nrm(ks[17], (N_REL_BUCKETS, N_HEADS_DIL), 0.5),
        "norm_final": 1.0 + 0.05 * jax.random.normal(ks[18], (D_MODEL,), jnp.float32),
    }


def reference(x, p, norm_ffn1, ffn1_w_gate, ffn1_w_up, ffn1_w_down, norm_mix, w_in, b_f, w_o,
              norm_ffn2, ffn2_w_gate, ffn2_w_up, ffn2_w_down, norm_ple, w_ple_gate, w_ple_proj,
              rel_table, norm_final):
    for i in range(DEPTH):
        x = x + 0.5 * swiglu(rmsnorm(x, norm_ffn1[i]), ffn1_w_gate[i], ffn1_w_up[i], ffn1_w_down[i])
        x = x + token_mixer(rmsnorm(x, norm_mix[i]), w_in[i], b_f[i], w_o[i], rel_table)
        x = x + 0.5 * swiglu(rmsnorm(x, norm_ffn2[i]), ffn2_w_gate[i], ffn2_w_up[i], ffn2_w_down[i])
        gate = jax.nn.sigmoid(rmsnorm(x, norm_ple[i]) @ w_ple_gate[i])
        x = x + gate * (p[i] @ w_ple_proj[i])
    return rmsnorm(x, norm_final)
```

```python
import functools
import math

import jax
import jax.numpy as jnp
import numpy as np
from jax import lax
from jax.experimental import pallas as pl
from jax.experimental.pallas import tpu as pltpu

F32 = jnp.float32
BF16 = jnp.bfloat16

HEAD_DIM = 128
N_HEADS_FOX = 8
N_HEADS_DIL = 8
D_FOX = N_HEADS_FOX * HEAD_DIM
D_DIL = N_HEADS_DIL * HEAD_DIM
DILATED_PATTERNS = ((128, 1), (512, 4), (2048, 16))
WINDOW_KEYS = 128
N_REL_BUCKETS = 32
REL_MAX_DISTANCE = 2048
RMS_EPS = 1e-6
NEG_INF = -1e30
SCALE = HEAD_DIM ** -0.5

V7X_LANES = 128
V7X_VMEM_LIMIT_BYTES = 56 * 1024 * 1024

FFN_ROW_TILE = 512
FFN_FF_TILE = 512
PROJ_ROW_TILE = 512
FOX_Q_TILE = 256


def _params(*semantics):
  return pltpu.CompilerParams(dimension_semantics=semantics,
                              vmem_limit_bytes=V7X_VMEM_LIMIT_BYTES)


def _rmsnorm(x, g):
  ms = jnp.mean(x * x, axis=-1, keepdims=True)
  return x * lax.rsqrt(ms + RMS_EPS) * g


def _dot(a, b):
  return jnp.dot(a, b, preferred_element_type=F32)


def _dot_nt(a, b):
  return lax.dot_general(a, b, (((1,), (1,)), ((), ())), preferred_element_type=F32)


def _ffn_kernel(x_ref, g_ref, wg_ref, wu_ref, wd_ref, o_ref, h_ref):
  j = pl.program_id(1)

  @pl.when(j == 0)
  def _():
    x = x_ref[...]
    h_ref[...] = _rmsnorm(x, g_ref[...]).astype(BF16)
    o_ref[...] = x

  h = h_ref[...]
  gate = _dot(h, wg_ref[...])
  up = _dot(h, wu_ref[...])
  act = (gate * jax.nn.sigmoid(gate)) * up
  o_ref[...] += _dot(act.astype(BF16), wd_ref[...])


def _ffn(x, g, wg, wu, wd_half):
  m, d = x.shape
  dff = wg.shape[1]
  tm, tf = FFN_ROW_TILE, FFN_FF_TILE
  return pl.pallas_call(
      _ffn_kernel,
      grid=(m // tm, dff // tf),
      in_specs=[
          pl.BlockSpec((tm, d), lambda i, j: (i, 0)),
          pl.BlockSpec((1, d), lambda i, j: (0, 0)),
          pl.BlockSpec((d, tf), lambda i, j: (0, j)),
          pl.BlockSpec((d, tf), lambda i, j: (0, j)),
          pl.BlockSpec((tf, d), lambda i, j: (j, 0)),
      ],
      out_specs=pl.BlockSpec((tm, d), lambda i, j: (i, 0)),
      out_shape=jax.ShapeDtypeStruct((m, d), F32),
      scratch_shapes=[pltpu.VMEM((tm, d), BF16)],
      compiler_params=_params("parallel", "arbitrary"),
      name="ffn",
  )(x, g, wg, wu, wd_half)


def _norm_matmul_kernel(x_ref, g_ref, w_ref, o_ref, h_ref):
  @pl.when(pl.program_id(1) == 0)
  def _():
    h_ref[...] = _rmsnorm(x_ref[...], g_ref[...]).astype(BF16)

  o_ref[...] = _dot(h_ref[...], w_ref[...]).astype(o_ref.dtype)


def _norm_matmul(x, g, w, out_dtype, tn):
  m, d = x.shape
  n = w.shape[1]
  tm = PROJ_ROW_TILE
  return pl.pallas_call(
      _norm_matmul_kernel,
      grid=(m // tm, n // tn),
      in_specs=[
          pl.BlockSpec((tm, d), lambda i, j: (i, 0)),
          pl.BlockSpec((1, d), lambda i, j: (0, 0)),
          pl.BlockSpec((d, tn), lambda i, j: (0, j)),
      ],
      out_specs=pl.BlockSpec((tm, tn), lambda i, j: (i, j)),
      out_shape=jax.ShapeDtypeStruct((m, n), out_dtype),
      scratch_shapes=[pltpu.VMEM((tm, d), BF16)],
      compiler_params=_params("parallel", "arbitrary"),
      name="norm_matmul",
  )(x, g, w)


def _cumsum_kernel(fl_ref, bf_ref, c_ref):
  s = fl_ref.shape[0]
  z = fl_ref[...] + bf_ref[...]
  logf = jnp.minimum(z, 0.0) - jnp.log1p(jnp.exp(-jnp.abs(z)))
  lt = logf.T[0:N_HEADS_FOX, :]
  row = lax.broadcasted_iota(jnp.int32, (V7X_LANES, V7X_LANES), 0)
  col = lax.broadcasted_iota(jnp.int32, (V7X_LANES, V7X_LANES), 1)
  upper = (row <= col).astype(F32)
  carry = jnp.zeros((N_HEADS_FOX, 1), F32)
  for j in range(s // V7X_LANES):
    blk = lt[:, j * V7X_LANES:(j + 1) * V7X_LANES]
    cs = jnp.dot(blk, upper, preferred_element_type=F32,
                 precision=lax.Precision.HIGHEST) + carry
    c_ref[:, j * V7X_LANES:(j + 1) * V7X_LANES] = cs
    carry = cs[:, V7X_LANES - 1:V7X_LANES]


def _fox_decay(f_logit, b_f_row, batch, seq):
  return pl.pallas_call(
      _cumsum_kernel,
      grid=(batch,),
      in_specs=[
          pl.BlockSpec((seq, V7X_LANES), lambda b: (b, 0)),
          pl.BlockSpec((1, V7X_LANES), lambda b: (0, 0)),
      ],
      out_specs=pl.BlockSpec((None, N_HEADS_FOX, seq), lambda b: (b, 0, 0)),
      out_shape=jax.ShapeDtypeStruct((batch, N_HEADS_FOX, seq), F32),
      compiler_params=_params("parallel"),
      name="fox_decay",
  )(f_logit, b_f_row)


def _fox_kernel(q_ref, k_ref, v_ref, c_ref, o_ref):
  seq = q_ref.shape[0]
  tq = FOX_Q_TILE
  h = pl.program_id(1)
  crow = c_ref[pl.ds(h, 1), :]
  row = lax.broadcasted_iota(jnp.int32, (tq, tq), 0)
  col = lax.broadcasted_iota(jnp.int32, (tq, tq), 1)
  diag_mask = jnp.where(col > row, NEG_INF, 0.0).astype(F32)
  for i in range(seq // tq):
    t0, t1 = i * tq, (i + 1) * tq
    q = q_ref[t0:t1, :]
    bias = crow[:, t1 - 1:t1] - crow[:, 0:t1]
    s_diag = _dot_nt(q, k_ref[t0:t1, :]) * SCALE + bias[:, t0:t1] + diag_mask
    m = jnp.max(s_diag, axis=-1, keepdims=True)
    if i > 0:
      s_off = _dot_nt(q, k_ref[0:t0, :]) * SCALE + bias[:, 0:t0]
      m = jnp.maximum(m, jnp.max(s_off, axis=-1, keepdims=True))
    e_diag = jnp.exp(s_diag - m)
    l = jnp.sum(e_diag, axis=-1, keepdims=True)
    o = _dot(e_diag.astype(BF16), v_ref[t0:t1, :])
    if i > 0:
      e_off = jnp.exp(s_off - m)
      l = l + jnp.sum(e_off, axis=-1, keepdims=True)
      o = o + _dot(e_off.astype(BF16), v_ref[0:t0, :])
    o_ref[t0:t1, :] = (o / l).astype(o_ref.dtype)


def _fox_attention(u_a, c, batch, seq):
  blk = lambda off: pl.BlockSpec((seq, HEAD_DIM), lambda b, h: (b, off + h))
  return pl.pallas_call(
      _fox_kernel,
      grid=(batch, N_HEADS_FOX),
      in_specs=[
          blk(0), blk(N_HEADS_FOX), blk(2 * N_HEADS_FOX),
          pl.BlockSpec((None, N_HEADS_FOX, seq), lambda b, h: (b, 0, 0)),
      ],
      out_specs=pl.BlockSpec((seq, HEAD_DIM), lambda b, h: (b, h)),
      out_shape=jax.ShapeDtypeStruct((batch * seq, D_FOX), BF16),
      compiler_params=_params("parallel", "arbitrary"),
      name="fox_attention",
  )(u_a, u_a, u_a, c)


def _t5_bucket_np(dist):
  max_exact = N_REL_BUCKETS // 2
  d = np.maximum(dist, 1).astype(np.float32)
  large = max_exact + (np.log(d / np.float32(max_exact))
                       / np.float32(math.log(REL_MAX_DISTANCE / max_exact))
                       * np.float32(N_REL_BUCKETS - max_exact)).astype(np.int32)
  large = np.minimum(large, N_REL_BUCKETS - 1)
  return np.where(dist < max_exact, dist, large).astype(np.int32)


def _band_buckets():
  n = WINDOW_KEYS
  qi = np.arange(n)[:, None]
  ki = np.arange(2 * n)[None, :]
  rel = qi + n - ki
  valid = (rel >= 0) & (rel <= n)
  tiles = []
  for _, dilation in DILATED_PATTERNS:
    bucket = _t5_bucket_np(np.maximum(rel, 0) * dilation)
    tiles.append(np.where(valid, bucket, -1))
  return np.stack(tiles).astype(np.int32)


def _bias_kernel(tab_ref, bkt_ref, o_ref):
  bkt = bkt_ref[...]
  for h in range(N_HEADS_DIL):
    acc = jnp.full(bkt.shape, NEG_INF, F32)
    for b in range(N_REL_BUCKETS):
      acc = jnp.where(bkt == b, tab_ref[b, h], acc)
    o_ref[h] = acc


def _band_bias(rel_table):
  buckets = jnp.asarray(_band_buckets())
  p, n, n2 = buckets.shape
  return pl.pallas_call(
      _bias_kernel,
      grid=(p,),
      in_specs=[
          pl.BlockSpec(memory_space=pltpu.SMEM),
          pl.BlockSpec((None, n, n2), lambda i: (i, 0, 0)),
      ],
      out_specs=pl.BlockSpec((None, N_HEADS_DIL, n, n2), lambda i: (i, 0, 0, 0)),
      out_shape=jax.ShapeDtypeStruct((p, N_HEADS_DIL, n, n2), F32),
      compiler_params=_params("parallel"),
      name="band_bias",
  )(rel_table, buckets)


def _dil_kernel(q_ref, k_ref, v_ref, bm_ref, o_ref,
                acc0, acc1, acc2, lse0, lse1, lse2):
  seq = q_ref.shape[0]
  n = WINDOW_KEYS
  accs = (acc0, acc1, acc2)
  lses = (lse0, lse1, lse2)

  def block(p, q_idx, k_idx, first):
    q = q_ref[q_idx, :].astype(BF16)
    k = k_ref[k_idx, :].astype(BF16)
    v = v_ref[k_idx, :].astype(BF16)
    bm = bm_ref[p, :, n:2 * n] if first else bm_ref[p]
    s = _dot_nt(q, k) * SCALE + bm
    m = jnp.max(s, axis=-1, keepdims=True)
    e = jnp.exp(s - m)
    l = jnp.sum(e, axis=-1, keepdims=True)
    o = _dot(e.astype(BF16), v) / l
    accs[p][q_idx, :] = o
    lses[p][q_idx, :] = jnp.broadcast_to(m + jnp.log(l), (n, HEAD_DIM))

  for p, (_, d) in enumerate(DILATED_PATTERNS):
    span = n * d
    n_blocks = seq // span
    if d == 1:
      block(p, pl.ds(0, n), pl.ds(0, n), True)

      def body(nb, carry):
        start = pl.multiple_of(nb * n, n)
        block(0, pl.ds(start, n), pl.ds(start - n, 2 * n), False)
        return carry

      lax.fori_loop(1, n_blocks, body, 0)
    else:
      for r in range(d):
        for nb in range(n_blocks):
          base = nb * span + r
          if nb == 0:
            block(p, pl.ds(base, n, d), pl.ds(base, n, d), True)
          else:
            block(p, pl.ds(base, n, d), pl.ds(base - span, 2 * n, d), False)

  def merge(c, carry):
    rows = pl.ds(pl.multiple_of(c * n, n), n)
    l0, l1, l2 = lse0[rows, :], lse1[rows, :], lse2[rows, :]
    mx = jnp.maximum(jnp.maximum(l0, l1), l2)
    w0, w1, w2 = jnp.exp(l0 - mx), jnp.exp(l1 - mx), jnp.exp(l2 - mx)
    mixed = (w0 * acc0[rows, :] + w1 * acc1[rows, :] + w2 * acc2[rows, :]) / (w0 + w1 + w2)
    o_ref[rows, :] = mixed.astype(o_ref.dtype)
    return carry

  lax.fori_loop(0, seq // n, merge, 0)


def _dilated_attention(u_b, band_bias, batch, seq):
  blk = lambda off: pl.BlockSpec((seq, HEAD_DIM), lambda b, h: (b, off + h))
  n_pat = len(DILATED_PATTERNS)
  scratch = [pltpu.VMEM((seq, HEAD_DIM), F32) for _ in range(2 * n_pat)]
  return pl.pallas_call(
      _dil_kernel,
      grid=(batch, N_HEADS_DIL),
      in_specs=[
          blk(0), blk(N_HEADS_DIL), blk(2 * N_HEADS_DIL),
          pl.BlockSpec((n_pat, None, WINDOW_KEYS, 2 * WINDOW_KEYS), lambda b, h: (0, h, 0, 0)),
      ],
      out_specs=pl.BlockSpec((seq, HEAD_DIM), lambda b, h: (b, h)),
      out_shape=jax.ShapeDtypeStruct((batch * seq, D_DIL), BF16),
      scratch_shapes=scratch,
      compiler_params=_params("parallel", "arbitrary"),
      name="dilated_attention",
  )(u_b, u_b, u_b, band_bias)


def _out_proj_kernel(x_ref, a_ref, b_ref, wa_ref, wb_ref, o_ref):
  o_ref[...] = x_ref[...] + _dot(a_ref[...], wa_ref[...]) + _dot(b_ref[...], wb_ref[...])


def _out_proj(x, o_a, o_b, wo_a, wo_b):
  m, d = x.shape
  tm = PROJ_ROW_TILE
  return pl.pallas_call(
      _out_proj_kernel,
      grid=(m // tm,),
      in_specs=[
          pl.BlockSpec((tm, d), lambda i: (i, 0)),
          pl.BlockSpec((tm, o_a.shape[1]), lambda i: (i, 0)),
          pl.BlockSpec((tm, o_b.shape[1]), lambda i: (i, 0)),
          pl.BlockSpec(wo_a.shape, lambda i: (0, 0)),
          pl.BlockSpec(wo_b.shape, lambda i: (0, 0)),
      ],
      out_specs=pl.BlockSpec((tm, d), lambda i: (i, 0)),
      out_shape=jax.ShapeDtypeStruct((m, d), F32),
      compiler_params=_params("parallel"),
      name="out_proj",
  )(x, o_a, o_b, wo_a, wo_b)


def _ple_kernel(x_ref, p_ref, g_ref, wg_ref, wp_ref, gf_ref, o_ref, *, final_norm):
  x = x_ref[...]
  h = _rmsnorm(x, g_ref[...]).astype(BF16)
  gate = jax.nn.sigmoid(_dot(h, wg_ref[...]))
  y = x + gate * _dot(p_ref[...].astype(BF16), wp_ref[...])
  o_ref[...] = _rmsnorm(y, gf_ref[...]) if final_norm else y


def _ple(x, p, g, w_gate, w_proj, g_final, final_norm):
  m, d = x.shape
  tm = PROJ_ROW_TILE
  return pl.pallas_call(
      functools.partial(_ple_kernel, final_norm=final_norm),
      grid=(m // tm,),
      in_specs=[
          pl.BlockSpec((tm, d), lambda i: (i, 0)),
          pl.BlockSpec((tm, p.shape[1]), lambda i: (i, 0)),
          pl.BlockSpec((1, d), lambda i: (0, 0)),
          pl.BlockSpec(w_gate.shape, lambda i: (0, 0)),
          pl.BlockSpec(w_proj.shape, lambda i: (0, 0)),
          pl.BlockSpec((1, d), lambda i: (0, 0)),
      ],
      out_specs=pl.BlockSpec((tm, d), lambda i: (i, 0)),
      out_shape=jax.ShapeDtypeStruct((m, d), F32),
      compiler_params=_params("parallel"),
      name="ple",
  )(x, p, g, w_gate, w_proj, g_final)


def kernel(x, p, norm_ffn1, ffn1_w_gate, ffn1_w_up, ffn1_w_down, norm_mix, w_in, b_f, w_o,
           norm_ffn2, ffn2_w_gate, ffn2_w_up, ffn2_w_down, norm_ple, w_ple_gate, w_ple_proj,
           rel_table, norm_final):
  batch, seq, d = x.shape
  depth = p.shape[0]
  m = batch * seq
  bf = lambda w: w.astype(BF16)
  row = lambda g: g.reshape(1, -1).astype(F32)

  band_bias = _band_bias(rel_table.astype(F32))
  xs = x.reshape(m, d).astype(F32)
  for i in range(depth):
    xs = _ffn(xs, row(norm_ffn1[i]), bf(ffn1_w_gate[i]), bf(ffn1_w_up[i]), bf(0.5 * ffn1_w_down[i]))

    w = w_in[i]
    f0, f1 = 3 * D_FOX, 3 * D_FOX + N_HEADS_FOX
    w_f = jnp.pad(w[:, f0:f1], ((0, 0), (0, V7X_LANES - N_HEADS_FOX)))
    b_f_row = jnp.pad(b_f[i].astype(F32), (0, V7X_LANES - N_HEADS_FOX)).reshape(1, V7X_LANES)
    g_mix = row(norm_mix[i])
    u_a = _norm_matmul(xs, g_mix, bf(w[:, :f0]), BF16, 1024)
    u_b = _norm_matmul(xs, g_mix, bf(w[:, f1:]), F32, 1024)
    f_logit = _norm_matmul(xs, g_mix, bf(w_f), F32, V7X_LANES)

    c = _fox_decay(f_logit, b_f_row, batch, seq)
    o_a = _fox_attention(u_a, c, batch, seq)
    o_b = _dilated_attention(u_b, band_bias, batch, seq)
    xs = _out_proj(xs, o_a, o_b, bf(w_o[i][:D_FOX]), bf(w_o[i][D_FOX:]))

    xs = _ffn(xs, row(norm_ffn2[i]), bf(ffn2_w_gate[i]), bf(ffn2_w_up[i]), bf(0.5 * ffn2_w_down[i]))
    last = i == depth - 1
    xs = _ple(xs, p[i].reshape(m, -1), row(norm_ple[i]), bf(w_ple_gate[i]), bf(w_ple_proj[i]),
              row(norm_final), final_norm=last)
  return xs.reshape(batch, seq, d).astype(x.dtype)
```

```python
import functools
import math

import jax
import jax.numpy as jnp
import numpy as np
from jax import lax
from jax.experimental import pallas as pl
from jax.experimental.pallas import tpu as pltpu

F32 = jnp.float32
BF16 = jnp.bfloat16

HEAD_DIM = 128
N_HEADS_FOX = 8
N_HEADS_DIL = 8
D_FOX = N_HEADS_FOX * HEAD_DIM
D_DIL = N_HEADS_DIL * HEAD_DIM
DILATED_PATTERNS = ((128, 1), (512, 4), (2048, 16))
WINDOW_KEYS = 128
N_REL_BUCKETS = 32
REL_MAX_DISTANCE = 2048
RMS_EPS = 1e-6
NEG_INF = -1e30
SCALE = HEAD_DIM ** -0.5

V7X_LANES = 128
V7X_VMEM_LIMIT_BYTES = 56 * 1024 * 1024

RESIDUES = max(d for _, d in DILATED_PATTERNS)
RES_PER_TILE = 4

FFN_ROW_TILE = 512
FFN_FF_TILE = 512
PROJ_ROW_TILE = 512
FOX_Q_TILE = 256
DIL_BLOCK_GROUP = 8


def _params(*semantics):
  return pltpu.CompilerParams(dimension_semantics=semantics,
                              vmem_limit_bytes=V7X_VMEM_LIMIT_BYTES)


def _rmsnorm(x, g):
  ms = jnp.mean(x * x, axis=-1, keepdims=True)
  return x * lax.rsqrt(ms + RMS_EPS) * g


def _dot(a, b):
  return jnp.dot(a, b, preferred_element_type=F32)


def _dot_nt(a, b):
  return lax.dot_general(a, b, (((1,), (1,)), ((), ())), preferred_element_type=F32)


def _ffn_kernel(x_ref, g_ref, wg_ref, wu_ref, wd_ref, o_ref, h_ref):
  j = pl.program_id(1)

  @pl.when(j == 0)
  def _():
    x = x_ref[...]
    h_ref[...] = _rmsnorm(x, g_ref[...]).astype(BF16)
    o_ref[...] = x

  h = h_ref[...]
  gate = _dot(h, wg_ref[...])
  up = _dot(h, wu_ref[...])
  act = (gate * jax.nn.sigmoid(gate)) * up
  o_ref[...] += _dot(act.astype(BF16), wd_ref[...])


def _ffn(x, g, wg, wu, wd_half):
  m, d = x.shape
  dff = wg.shape[1]
  tm, tf = FFN_ROW_TILE, FFN_FF_TILE
  return pl.pallas_call(
      _ffn_kernel,
      grid=(m // tm, dff // tf),
      in_specs=[
          pl.BlockSpec((tm, d), lambda i, j: (i, 0)),
          pl.BlockSpec((1, d), lambda i, j: (0, 0)),
          pl.BlockSpec((d, tf), lambda i, j: (0, j)),
          pl.BlockSpec((d, tf), lambda i, j: (0, j)),
          pl.BlockSpec((tf, d), lambda i, j: (j, 0)),
      ],
      out_specs=pl.BlockSpec((tm, d), lambda i, j: (i, 0)),
      out_shape=jax.ShapeDtypeStruct((m, d), F32),
      scratch_shapes=[pltpu.VMEM((tm, d), BF16)],
      compiler_params=_params("parallel", "arbitrary"),
      name="ffn",
  )(x, g, wg, wu, wd_half)


def _norm_matmul_kernel(x_ref, g_ref, w_ref, *rest, pieces, narrow):
  if narrow:
    wn_ref, o_ref, on_ref, h_ref = rest
  else:
    o_ref, h_ref = rest
  rows = x_ref.shape[0]
  d = x_ref.shape[1] // pieces

  @pl.when(pl.program_id(1) == 0)
  def _():
    for k in range(pieces):
      h_ref[k * rows:(k + 1) * rows, :] = _rmsnorm(
          x_ref[:, k * d:(k + 1) * d], g_ref[...]).astype(BF16)
    if narrow:
      on_ref[...] = _dot(h_ref[...], wn_ref[...])

  o_ref[...] = _dot(h_ref[...], w_ref[...]).astype(o_ref.dtype)


def _residue_view(a, batch, seq):
  return a.reshape(batch * (seq // RESIDUES), RESIDUES * a.shape[1])


def _residue_index(i, j):
  return (i // (RESIDUES // RES_PER_TILE), i % (RESIDUES // RES_PER_TILE))


def _norm_matmul(x, g, w, out_dtype, tn, residue_major=None, w_narrow=None):
  m, d = x.shape
  n = w.shape[1]
  tm = PROJ_ROW_TILE
  narrow = w_narrow is not None
  if residue_major is None:
    pieces = 1
    x_spec = pl.BlockSpec((tm, d), lambda i, j: (i, 0))
  else:
    batch, seq = residue_major
    pieces = RES_PER_TILE
    assert tm == pieces * (seq // RESIDUES)
    x = _residue_view(x, batch, seq)
    x_spec = pl.BlockSpec((seq // RESIDUES, pieces * d), _residue_index)
  in_specs = [
      x_spec,
      pl.BlockSpec((1, d), lambda i, j: (0, 0)),
      pl.BlockSpec((d, tn), lambda i, j: (0, j)),
  ]
  out_specs = pl.BlockSpec((tm, tn), lambda i, j: (i, j))
  out_shape = jax.ShapeDtypeStruct((m, n), out_dtype)
  args = (x, g, w)
  if narrow:
    in_specs.append(pl.BlockSpec(w_narrow.shape, lambda i, j: (0, 0)))
    out_specs = (out_specs, pl.BlockSpec((tm, w_narrow.shape[1]), lambda i, j: (i, 0)))
    out_shape = (out_shape, jax.ShapeDtypeStruct((m, w_narrow.shape[1]), F32))
    args = args + (w_narrow,)
  return pl.pallas_call(
      functools.partial(_norm_matmul_kernel, pieces=pieces, narrow=narrow),
      grid=(m // tm, n // tn),
      in_specs=in_specs,
      out_specs=out_specs,
      out_shape=out_shape,
      scratch_shapes=[pltpu.VMEM((tm, d), BF16)],
      compiler_params=_params("parallel", "arbitrary"),
      name="norm_matmul",
  )(*args)


def _cumsum_kernel(fl_ref, bf_ref, c_ref):
  s = fl_ref.shape[0]
  z = fl_ref[...] + bf_ref[...]
  logf = jnp.minimum(z, 0.0) - jnp.log1p(jnp.exp(-jnp.abs(z)))
  lt = logf.T[0:N_HEADS_FOX, :]
  row = lax.broadcasted_iota(jnp.int32, (V7X_LANES, V7X_LANES), 0)
  col = lax.broadcasted_iota(jnp.int32, (V7X_LANES, V7X_LANES), 1)
  upper = (row <= col).astype(F32)
  carry = jnp.zeros((N_HEADS_FOX, 1), F32)
  for j in range(s // V7X_LANES):
    blk = lt[:, j * V7X_LANES:(j + 1) * V7X_LANES]
    cs = jnp.dot(blk, upper, preferred_element_type=F32,
                 precision=lax.Precision.HIGHEST) + carry
    c_ref[:, j * V7X_LANES:(j + 1) * V7X_LANES] = cs
    carry = cs[:, V7X_LANES - 1:V7X_LANES]


def _fox_decay(f_logit, b_f_row, batch, seq):
  return pl.pallas_call(
      _cumsum_kernel,
      grid=(batch,),
      in_specs=[
          pl.BlockSpec((seq, V7X_LANES), lambda b: (b, 0)),
          pl.BlockSpec((1, V7X_LANES), lambda b: (0, 0)),
      ],
      out_specs=pl.BlockSpec((None, N_HEADS_FOX, seq), lambda b: (b, 0, 0)),
      out_shape=jax.ShapeDtypeStruct((batch, N_HEADS_FOX, seq), F32),
      compiler_params=_params("parallel"),
      name="fox_decay",
  )(f_logit, b_f_row)


def _fox_kernel(q_ref, k_ref, v_ref, c_ref, o_ref):
  seq = q_ref.shape[0]
  tq = FOX_Q_TILE
  h = pl.program_id(1)
  crow = c_ref[pl.ds(h, 1), :]
  row = lax.broadcasted_iota(jnp.int32, (tq, tq), 0)
  col = lax.broadcasted_iota(jnp.int32, (tq, tq), 1)
  diag_mask = jnp.where(col > row, NEG_INF, 0.0).astype(F32)
  for i in range(seq // tq):
    t0, t1 = i * tq, (i + 1) * tq
    q = q_ref[t0:t1, :]
    bias = crow[:, t1 - 1:t1] - crow[:, 0:t1]
    s_diag = _dot_nt(q, k_ref[t0:t1, :]) * SCALE + bias[:, t0:t1] + diag_mask
    m = jnp.max(s_diag, axis=-1, keepdims=True)
    if i > 0:
      s_off = _dot_nt(q, k_ref[0:t0, :]) * SCALE + bias[:, 0:t0]
      m = jnp.maximum(m, jnp.max(s_off, axis=-1, keepdims=True))
    e_diag = jnp.exp(s_diag - m)
    l = jnp.sum(e_diag, axis=-1, keepdims=True)
    o = _dot(e_diag.astype(BF16), v_ref[t0:t1, :])
    if i > 0:
      e_off = jnp.exp(s_off - m)
      l = l + jnp.sum(e_off, axis=-1, keepdims=True)
      o = o + _dot(e_off.astype(BF16), v_ref[0:t0, :])
    o_ref[t0:t1, :] = (o / l).astype(o_ref.dtype)


def _fox_attention(u_a, c, batch, seq):
  blk = lambda off: pl.BlockSpec((seq, HEAD_DIM), lambda b, h: (b, off + h))
  return pl.pallas_call(
      _fox_kernel,
      grid=(batch, N_HEADS_FOX),
      in_specs=[
          blk(0), blk(N_HEADS_FOX), blk(2 * N_HEADS_FOX),
          pl.BlockSpec((None, N_HEADS_FOX, seq), lambda b, h: (b, 0, 0)),
      ],
      out_specs=pl.BlockSpec((seq, HEAD_DIM), lambda b, h: (b, h)),
      out_shape=jax.ShapeDtypeStruct((batch * seq, D_FOX), BF16),
      compiler_params=_params("parallel", "arbitrary"),
      name="fox_attention",
  )(u_a, u_a, u_a, c)


def _t5_bucket_np(dist):
  max_exact = N_REL_BUCKETS // 2
  d = np.maximum(dist, 1).astype(np.float32)
  large = max_exact + (np.log(d / np.float32(max_exact))
                       / np.float32(math.log(REL_MAX_DISTANCE / max_exact))
                       * np.float32(N_REL_BUCKETS - max_exact)).astype(np.int32)
  large = np.minimum(large, N_REL_BUCKETS - 1)
  return np.where(dist < max_exact, dist, large).astype(np.int32)


def _block_positions(dilation):
  n = WINDOW_KEYS
  m = RESIDUES // dilation
  rows = n // m
  j = np.arange(m)[:, None]
  qpos = (n + m * np.arange(rows)[None, :] + j).reshape(-1)
  kpos = (m * np.arange(2 * rows)[None, :] + j).reshape(-1)
  return qpos, kpos


def _band_buckets():
  n = WINDOW_KEYS
  tiles = []
  for _, dilation in DILATED_PATTERNS:
    qpos, kpos = _block_positions(dilation)
    rel = qpos[:, None] - np.concatenate([kpos, qpos])[None, :]
    valid = (rel >= 0) & (rel <= n)
    bucket = _t5_bucket_np(np.maximum(rel, 0) * dilation)
    tiles.append(np.where(valid, bucket, -1))
  return np.stack(tiles).astype(np.int32)


def _bias_kernel(tab_ref, bkt_ref, o_ref):
  bkt = bkt_ref[...]
  for h in range(N_HEADS_DIL):
    acc = jnp.full(bkt.shape, NEG_INF, F32)
    for b in range(N_REL_BUCKETS):
      acc = jnp.where(bkt == b, tab_ref[b, h], acc)
    o_ref[h] = acc


def _band_bias(rel_table):
  buckets = jnp.asarray(_band_buckets())
  p, n, n2 = buckets.shape
  return pl.pallas_call(
      _bias_kernel,
      grid=(p,),
      in_specs=[
          pl.BlockSpec(memory_space=pltpu.SMEM),
          pl.BlockSpec((None, n, n2), lambda i: (i, 0, 0)),
      ],
      out_specs=pl.BlockSpec((None, N_HEADS_DIL, n, n2), lambda i: (i, 0, 0, 0)),
      out_shape=jax.ShapeDtypeStruct((p, N_HEADS_DIL, n, n2), F32),
      compiler_params=_params("parallel"),
      name="band_bias",
  )(rel_table, buckets)


def _dil_kernel(q_ref, k_ref, v_ref, bm_ref, o_ref,
                acc0, acc1, acc2, lse0, lse1, lse2):
  seq = q_ref.shape[0]
  n = WINDOW_KEYS
  seg = seq // RESIDUES
  accs = (acc0, acc1, acc2)
  lses = (lse0, lse1, lse2)

  def gather(ref, starts, size):
    return jnp.concatenate([ref[st:st + size, :] for st in starts], axis=0).astype(BF16)

  def block_rows(segments, nb):
    rows = n // len(segments)
    q_starts = [s * seg + nb * rows for s in segments]
    if nb == 0:
      return rows, q_starts, q_starts, rows
    return rows, q_starts, [st - rows for st in q_starts], 2 * rows

  def scores(p, segments, nb):
    rows, q_starts, k_starts, k_rows = block_rows(segments, nb)
    bm = bm_ref[p, :, 2 * n:3 * n] if nb == 0 else bm_ref[p, :, 0:2 * n]
    return _dot_nt(gather(q_ref, q_starts, rows), gather(k_ref, k_starts, k_rows)) * SCALE + bm

  def finish(p, segments, nb, e, l, lse):
    rows, q_starts, k_starts, k_rows = block_rows(segments, nb)
    o = _dot(e.astype(BF16), gather(v_ref, k_starts, k_rows)) / l
    lse = jnp.broadcast_to(lse, (n, HEAD_DIM))
    for j, st in enumerate(q_starts):
      accs[p][st:st + rows, :] = o[j * rows:(j + 1) * rows, :]
      lses[p][st:st + rows, :] = lse[j * rows:(j + 1) * rows, :]

  blocks = [(p, list(range(r, RESIDUES, d)), nb)
            for p, (_, d) in enumerate(DILATED_PATTERNS)
            for r in range(d) for nb in range(seq // (n * d))]
  for g in range(0, len(blocks), DIL_BLOCK_GROUP):
    group = blocks[g:g + DIL_BLOCK_GROUP]
    ss = [scores(*blk) for blk in group]
    ms = [jnp.max(s, axis=-1, keepdims=True) for s in ss]
    es = [jnp.exp(s - m) for s, m in zip(ss, ms)]
    ls = [jnp.sum(e, axis=-1, keepdims=True) for e in es]
    for blk, e, l, m in zip(group, es, ls, ms):
      finish(*blk, e, l, m + jnp.log(l))

  def merge(c, carry):
    rows = pl.ds(pl.multiple_of(c * n, n), n)
    l0, l1, l2 = lse0[rows, :], lse1[rows, :], lse2[rows, :]
    mx = jnp.maximum(jnp.maximum(l0, l1), l2)
    w0, w1, w2 = jnp.exp(l0 - mx), jnp.exp(l1 - mx), jnp.exp(l2 - mx)
    mixed = (w0 * acc0[rows, :] + w1 * acc1[rows, :] + w2 * acc2[rows, :]) / (w0 + w1 + w2)
    o_ref[rows, :] = mixed.astype(o_ref.dtype)
    return carry

  lax.fori_loop(0, seq // n, merge, 0)


def _dilated_attention(u_b, band_bias, batch, seq):
  assert seq // RESIDUES == WINDOW_KEYS
  blk = lambda off: pl.BlockSpec((seq, HEAD_DIM), lambda b, h: (b, off + h))
  n_pat = len(DILATED_PATTERNS)
  scratch = [pltpu.VMEM((seq, HEAD_DIM), F32) for _ in range(2 * n_pat)]
  return pl.pallas_call(
      _dil_kernel,
      grid=(batch, N_HEADS_DIL),
      in_specs=[
          blk(0), blk(N_HEADS_DIL), blk(2 * N_HEADS_DIL),
          pl.BlockSpec((n_pat, None, WINDOW_KEYS, 3 * WINDOW_KEYS), lambda b, h: (0, h, 0, 0)),
      ],
      out_specs=pl.BlockSpec((seq, HEAD_DIM), lambda b, h: (b, h)),
      out_shape=jax.ShapeDtypeStruct((batch * seq, D_DIL), BF16),
      scratch_shapes=scratch,
      compiler_params=_params("parallel", "arbitrary"),
      name="dilated_attention",
  )(u_b, u_b, u_b, band_bias)


def _out_proj_kernel(x_ref, a_ref, b_ref, wa_ref, wb_ref, o_ref):
  rows = x_ref.shape[0]
  d = x_ref.shape[1] // RES_PER_TILE
  da = a_ref.shape[1] // RES_PER_TILE
  a = jnp.concatenate([a_ref[:, k * da:(k + 1) * da] for k in range(RES_PER_TILE)], axis=0)
  r = _dot(a, wa_ref[...]) + _dot(b_ref[...], wb_ref[...])
  for k in range(RES_PER_TILE):
    o_ref[:, k * d:(k + 1) * d] = x_ref[:, k * d:(k + 1) * d] + r[k * rows:(k + 1) * rows, :]


def _out_proj(x, o_a, o_b, wo_a, wo_b, batch, seq):
  m, d = x.shape
  seg = seq // RESIDUES
  tm = RES_PER_TILE * seg
  view_spec = lambda c: pl.BlockSpec((seg, RES_PER_TILE * c), lambda i: _residue_index(i, 0))
  out = pl.pallas_call(
      _out_proj_kernel,
      grid=(m // tm,),
      in_specs=[
          view_spec(d),
          view_spec(o_a.shape[1]),
          pl.BlockSpec((tm, o_b.shape[1]), lambda i: (i, 0)),
          pl.BlockSpec(wo_a.shape, lambda i: (0, 0)),
          pl.BlockSpec(wo_b.shape, lambda i: (0, 0)),
      ],
      out_specs=view_spec(d),
      out_shape=jax.ShapeDtypeStruct((m // RESIDUES, RESIDUES * d), F32),
      compiler_params=_params("parallel"),
      name="out_proj",
  )(_residue_view(x, batch, seq), _residue_view(o_a, batch, seq), o_b, wo_a, wo_b)
  return out.reshape(m, d)


def _ple_kernel(x_ref, p_ref, g_ref, wg_ref, wp_ref, gf_ref, o_ref, *, final_norm):
  x = x_ref[...]
  h = _rmsnorm(x, g_ref[...]).astype(BF16)
  gate = jax.nn.sigmoid(_dot(h, wg_ref[...]))
  y = x + gate * _dot(p_ref[...].astype(BF16), wp_ref[...])
  o_ref[...] = _rmsnorm(y, gf_ref[...]) if final_norm else y


def _ple(x, p, g, w_gate, w_proj, g_final, final_norm):
  m, d = x.shape
  tm = PROJ_ROW_TILE
  return pl.pallas_call(
      functools.partial(_ple_kernel, final_norm=final_norm),
      grid=(m // tm,),
      in_specs=[
          pl.BlockSpec((tm, d), lambda i: (i, 0)),
          pl.BlockSpec((tm, p.shape[1]), lambda i: (i, 0)),
          pl.BlockSpec((1, d), lambda i: (0, 0)),
          pl.BlockSpec(w_gate.shape, lambda i: (0, 0)),
          pl.BlockSpec(w_proj.shape, lambda i: (0, 0)),
          pl.BlockSpec((1, d), lambda i: (0, 0)),
      ],
      out_specs=pl.BlockSpec((tm, d), lambda i: (i, 0)),
      out_shape=jax.ShapeDtypeStruct((m, d), F32),
      compiler_params=_params("parallel"),
      name="ple",
  )(x, p, g, w_gate, w_proj, g_final)


def kernel(x, p, norm_ffn1, ffn1_w_gate, ffn1_w_up, ffn1_w_down, norm_mix, w_in, b_f, w_o,
           norm_ffn2, ffn2_w_gate, ffn2_w_up, ffn2_w_down, norm_ple, w_ple_gate, w_ple_proj,
           rel_table, norm_final):
  batch, seq, d = x.shape
  depth = p.shape[0]
  m = batch * seq
  bf = lambda w: w.astype(BF16)
  row = lambda g: g.reshape(1, -1).astype(F32)

  band_bias = _band_bias(rel_table.astype(F32))
  xs = x.reshape(m, d).astype(F32)
  for i in range(depth):
    xs = _ffn(xs, row(norm_ffn1[i]), bf(ffn1_w_gate[i]), bf(ffn1_w_up[i]), bf(0.5 * ffn1_w_down[i]))

    w = w_in[i]
    f0, f1 = 3 * D_FOX, 3 * D_FOX + N_HEADS_FOX
    w_f = jnp.pad(w[:, f0:f1], ((0, 0), (0, V7X_LANES - N_HEADS_FOX)))
    b_f_row = jnp.pad(b_f[i].astype(F32), (0, V7X_LANES - N_HEADS_FOX)).reshape(1, V7X_LANES)
    g_mix = row(norm_mix[i])
    u_a, f_logit = _norm_matmul(xs, g_mix, bf(w[:, :f0]), BF16, 1024, w_narrow=bf(w_f))
    u_b = _norm_matmul(xs, g_mix, bf(w[:, f1:]), F32, 1024, residue_major=(batch, seq))

    c = _fox_decay(f_logit, b_f_row, batch, seq)
    o_a = _fox_attention(u_a, c, batch, seq)
    o_b = _dilated_attention(u_b, band_bias, batch, seq)
    xs = _out_proj(xs, o_a, o_b, bf(w_o[i][:D_FOX]), bf(w_o[i][D_FOX:]), batch, seq)

    xs = _ffn(xs, row(norm_ffn2[i]), bf(ffn2_w_gate[i]), bf(ffn2_w_up[i]), bf(0.5 * ffn2_w_down[i]))
    last = i == depth - 1
    xs = _ple(xs, p[i].reshape(m, -1), row(norm_ple[i]), bf(w_ple_gate[i]), bf(w_ple_proj[i]),
              row(norm_final), final_norm=last)
  return xs.reshape(batch, seq, d).astype(x.dtype)
```

```python
import functools
import math

import jax
import jax.numpy as jnp
import numpy as np
from jax import lax
from jax.experimental import pallas as pl
from jax.experimental.pallas import tpu as pltpu

F32 = jnp.float32
BF16 = jnp.bfloat16

HEAD_DIM = 128
N_HEADS_FOX = 8
N_HEADS_DIL = 8
D_FOX = N_HEADS_FOX * HEAD_DIM
D_DIL = N_HEADS_DIL * HEAD_DIM
DILATED_PATTERNS = ((128, 1), (512, 4), (2048, 16))
WINDOW_KEYS = 128
N_REL_BUCKETS = 32
REL_MAX_DISTANCE = 2048
RMS_EPS = 1e-6
NEG_INF = -1e30
SCALE = HEAD_DIM ** -0.5

V7X_LANES = 128
V7X_VMEM_LIMIT_BYTES = 56 * 1024 * 1024

RESIDUES = max(d for _, d in DILATED_PATTERNS)
PERM_ROWS = RESIDUES * RESIDUES

FFN_ROW_TILE = 512
FFN_FF_TILE = 512
PROJ_ROW_TILE = 512
FOX_Q_TILE = 256
DIL_BLOCK_GROUP = 8


def _params(*semantics):
  return pltpu.CompilerParams(dimension_semantics=semantics,
                              vmem_limit_bytes=V7X_VMEM_LIMIT_BYTES)


def _rmsnorm(x, g):
  ms = jnp.mean(x * x, axis=-1, keepdims=True)
  return x * lax.rsqrt(ms + RMS_EPS) * g


def _dot(a, b):
  return jnp.dot(a, b, preferred_element_type=F32)


def _dot_nt(a, b):
  return lax.dot_general(a, b, (((1,), (1,)), ((), ())), preferred_element_type=F32)


def _ffn_kernel(x_ref, g_ref, wg_ref, wu_ref, wd_ref, o_ref, h_ref):
  j = pl.program_id(1)

  @pl.when(j == 0)
  def _():
    x = x_ref[...]
    h_ref[...] = _rmsnorm(x, g_ref[...]).astype(BF16)
    o_ref[...] = x

  h = h_ref[...]
  gate = _dot(h, wg_ref[...])
  up = _dot(h, wu_ref[...])
  act = (gate * jax.nn.sigmoid(gate)) * up
  o_ref[...] += _dot(act.astype(BF16), wd_ref[...])


def _ffn(x, g, wg, wu, wd_half):
  m, d = x.shape
  dff = wg.shape[1]
  tm, tf = FFN_ROW_TILE, FFN_FF_TILE
  return pl.pallas_call(
      _ffn_kernel,
      grid=(m // tm, dff // tf),
      in_specs=[
          pl.BlockSpec((tm, d), lambda i, j: (i, 0)),
          pl.BlockSpec((1, d), lambda i, j: (0, 0)),
          pl.BlockSpec((d, tf), lambda i, j: (0, j)),
          pl.BlockSpec((d, tf), lambda i, j: (0, j)),
          pl.BlockSpec((tf, d), lambda i, j: (j, 0)),
      ],
      out_specs=pl.BlockSpec((tm, d), lambda i, j: (i, 0)),
      out_shape=jax.ShapeDtypeStruct((m, d), F32),
      scratch_shapes=[pltpu.VMEM((tm, d), BF16)],
      compiler_params=_params("parallel", "arbitrary"),
      name="ffn",
  )(x, g, wg, wu, wd_half)


def _swap_matrix():
  i = np.arange(PERM_ROWS)
  src = (i % RESIDUES) * RESIDUES + i // RESIDUES
  return jnp.asarray(np.eye(PERM_ROWS, dtype=np.float32)[src], BF16)


def _norm_matmul_kernel(x_ref, g_ref, w_ref, *rest, permute, narrow):
  rest = list(rest)
  swap_ref = rest.pop(0) if permute else None
  wn_ref = rest.pop(0) if narrow else None
  o_ref = rest.pop(0)
  on_ref = rest.pop(0) if narrow else None
  h_ref, = rest
  tm = x_ref.shape[0]

  @pl.when(pl.program_id(1) == 0)
  def _():
    h = _rmsnorm(x_ref[...], g_ref[...]).astype(BF16)
    if permute:
      for a in range(tm // PERM_ROWS):
        rows = slice(a * PERM_ROWS, (a + 1) * PERM_ROWS)
        h_ref[rows, :] = _dot(swap_ref[...], h[rows, :]).astype(BF16)
    else:
      h_ref[...] = h
    if narrow:
      on_ref[...] = _dot(h_ref[...], wn_ref[...])

  res = _dot(h_ref[...], w_ref[...]).astype(o_ref.dtype)
  if permute:
    per = PERM_ROWS // RESIDUES
    for a in range(tm // PERM_ROWS):
      for r in range(RESIDUES):
        start = a * PERM_ROWS + r * per
        o_ref[r, a * per:(a + 1) * per, :] = res[start:start + per, :]
  else:
    o_ref[...] = res


def _norm_matmul(x, g, w, out_dtype, tn, residue_major=None, w_narrow=None):
  m, d = x.shape
  n = w.shape[1]
  tm = PROJ_ROW_TILE
  narrow = w_narrow is not None
  permute = residue_major is not None
  in_specs = [
      pl.BlockSpec((tm, d), lambda i, j: (i, 0)),
      pl.BlockSpec((1, d), lambda i, j: (0, 0)),
      pl.BlockSpec((d, tn), lambda i, j: (0, j)),
  ]
  args = [x, g, w]
  if permute:
    batch, seq = residue_major
    seg = seq // RESIDUES
    tiles = seq // tm
    in_specs.append(pl.BlockSpec((PERM_ROWS, PERM_ROWS), lambda i, j: (0, 0)))
    args.append(_swap_matrix())
    out_specs = pl.BlockSpec((RESIDUES, tm // RESIDUES, tn),
                             lambda i, j: (i // tiles, i % tiles, j))
    out_shape = jax.ShapeDtypeStruct((batch * RESIDUES, seg, n), out_dtype)
  else:
    out_specs = pl.BlockSpec((tm, tn), lambda i, j: (i, j))
    out_shape = jax.ShapeDtypeStruct((m, n), out_dtype)
  if narrow:
    in_specs.append(pl.BlockSpec(w_narrow.shape, lambda i, j: (0, 0)))
    args.append(w_narrow)
    out_specs = (out_specs, pl.BlockSpec((tm, w_narrow.shape[1]), lambda i, j: (i, 0)))
    out_shape = (out_shape, jax.ShapeDtypeStruct((m, w_narrow.shape[1]), F32))
  out = pl.pallas_call(
      functools.partial(_norm_matmul_kernel, permute=permute, narrow=narrow),
      grid=(m // tm, n // tn),
      in_specs=in_specs,
      out_specs=out_specs,
      out_shape=out_shape,
      scratch_shapes=[pltpu.VMEM((tm, d), BF16)],
      compiler_params=_params("parallel", "arbitrary"),
      name="norm_matmul",
  )(*args)
  if permute:
    return out.reshape(m, n)
  return out


def _cumsum_kernel(fl_ref, bf_ref, c_ref):
  s = fl_ref.shape[0]
  z = fl_ref[...] + bf_ref[...]
  logf = jnp.minimum(z, 0.0) - jnp.log1p(jnp.exp(-jnp.abs(z)))
  lt = logf.T[0:N_HEADS_FOX, :]
  row = lax.broadcasted_iota(jnp.int32, (V7X_LANES, V7X_LANES), 0)
  col = lax.broadcasted_iota(jnp.int32, (V7X_LANES, V7X_LANES), 1)
  upper = (row <= col).astype(F32)
  carry = jnp.zeros((N_HEADS_FOX, 1), F32)
  for j in range(s // V7X_LANES):
    blk = lt[:, j * V7X_LANES:(j + 1) * V7X_LANES]
    cs = jnp.dot(blk, upper, preferred_element_type=F32,
                 precision=lax.Precision.HIGHEST) + carry
    c_ref[:, j * V7X_LANES:(j + 1) * V7X_LANES] = cs
    carry = cs[:, V7X_LANES - 1:V7X_LANES]


def _fox_decay(f_logit, b_f_row, batch, seq):
  return pl.pallas_call(
      _cumsum_kernel,
      grid=(batch,),
      in_specs=[
          pl.BlockSpec((seq, V7X_LANES), lambda b: (b, 0)),
          pl.BlockSpec((1, V7X_LANES), lambda b: (0, 0)),
      ],
      out_specs=pl.BlockSpec((None, N_HEADS_FOX, seq), lambda b: (b, 0, 0)),
      out_shape=jax.ShapeDtypeStruct((batch, N_HEADS_FOX, seq), F32),
      compiler_params=_params("parallel"),
      name="fox_decay",
  )(f_logit, b_f_row)


def _fox_kernel(q_ref, k_ref, v_ref, c_ref, o_ref):
  seq = q_ref.shape[0]
  tq = FOX_Q_TILE
  h = pl.program_id(1)
  crow = c_ref[pl.ds(h, 1), :]
  row = lax.broadcasted_iota(jnp.int32, (tq, tq), 0)
  col = lax.broadcasted_iota(jnp.int32, (tq, tq), 1)
  diag_mask = jnp.where(col > row, NEG_INF, 0.0).astype(F32)
  for i in range(seq // tq):
    t0, t1 = i * tq, (i + 1) * tq
    q = q_ref[t0:t1, :]
    bias = crow[:, t1 - 1:t1] - crow[:, 0:t1]
    s_diag = _dot_nt(q, k_ref[t0:t1, :]) * SCALE + bias[:, t0:t1] + diag_mask
    m = jnp.max(s_diag, axis=-1, keepdims=True)
    if i > 0:
      s_off = _dot_nt(q, k_ref[0:t0, :]) * SCALE + bias[:, 0:t0]
      m = jnp.maximum(m, jnp.max(s_off, axis=-1, keepdims=True))
    e_diag = jnp.exp(s_diag - m)
    l = jnp.sum(e_diag, axis=-1, keepdims=True)
    o = _dot(e_diag.astype(BF16), v_ref[t0:t1, :])
    if i > 0:
      e_off = jnp.exp(s_off - m)
      l = l + jnp.sum(e_off, axis=-1, keepdims=True)
      o = o + _dot(e_off.astype(BF16), v_ref[0:t0, :])
    o_ref[t0:t1, :] = (o / l).astype(o_ref.dtype)


def _fox_attention(u_a, c, batch, seq):
  blk = lambda off: pl.BlockSpec((seq, HEAD_DIM), lambda b, h: (b, off + h))
  return pl.pallas_call(
      _fox_kernel,
      grid=(batch, N_HEADS_FOX),
      in_specs=[
          blk(0), blk(N_HEADS_FOX), blk(2 * N_HEADS_FOX),
          pl.BlockSpec((None, N_HEADS_FOX, seq), lambda b, h: (b, 0, 0)),
      ],
      out_specs=pl.BlockSpec((seq, HEAD_DIM), lambda b, h: (b, h)),
      out_shape=jax.ShapeDtypeStruct((batch * seq, D_FOX), BF16),
      compiler_params=_params("parallel", "arbitrary"),
      name="fox_attention",
  )(u_a, u_a, u_a, c)


def _t5_bucket_np(dist):
  max_exact = N_REL_BUCKETS // 2
  d = np.maximum(dist, 1).astype(np.float32)
  large = max_exact + (np.log(d / np.float32(max_exact))
                       / np.float32(math.log(REL_MAX_DISTANCE / max_exact))
                       * np.float32(N_REL_BUCKETS - max_exact)).astype(np.int32)
  large = np.minimum(large, N_REL_BUCKETS - 1)
  return np.where(dist < max_exact, dist, large).astype(np.int32)


def _block_positions(dilation):
  n = WINDOW_KEYS
  m = RESIDUES // dilation
  rows = n // m
  j = np.arange(m)[:, None]
  qpos = (n + m * np.arange(rows)[None, :] + j).reshape(-1)
  kpos = (m * np.arange(2 * rows)[None, :] + j).reshape(-1)
  return qpos, kpos


def _band_buckets():
  n = WINDOW_KEYS
  tiles = []
  for _, dilation in DILATED_PATTERNS:
    qpos, kpos = _block_positions(dilation)
    rel = qpos[:, None] - np.concatenate([kpos, qpos])[None, :]
    valid = (rel >= 0) & (rel <= n)
    bucket = _t5_bucket_np(np.maximum(rel, 0) * dilation)
    tiles.append(np.where(valid, bucket, -1))
  return np.stack(tiles).astype(np.int32)


def _bias_kernel(tab_ref, bkt_ref, o_ref):
  bkt = bkt_ref[...]
  for h in range(N_HEADS_DIL):
    acc = jnp.full(bkt.shape, NEG_INF, F32)
    for b in range(N_REL_BUCKETS):
      acc = jnp.where(bkt == b, tab_ref[b, h], acc)
    o_ref[h] = acc


def _band_bias(rel_table):
  buckets = jnp.asarray(_band_buckets())
  p, n, n2 = buckets.shape
  return pl.pallas_call(
      _bias_kernel,
      grid=(p,),
      in_specs=[
          pl.BlockSpec(memory_space=pltpu.SMEM),
          pl.BlockSpec((None, n, n2), lambda i: (i, 0, 0)),
      ],
      out_specs=pl.BlockSpec((None, N_HEADS_DIL, n, n2), lambda i: (i, 0, 0, 0)),
      out_shape=jax.ShapeDtypeStruct((p, N_HEADS_DIL, n, n2), F32),
      compiler_params=_params("parallel"),
      name="band_bias",
  )(rel_table, buckets)


def _dil_kernel(q_ref, k_ref, v_ref, bm_ref, o_ref,
                acc0, acc1, acc2, lse0, lse1, lse2):
  seq = q_ref.shape[0]
  n = WINDOW_KEYS
  seg = seq // RESIDUES
  accs = (acc0, acc1, acc2)
  lses = (lse0, lse1, lse2)

  def gather(ref, starts, size):
    return jnp.concatenate([ref[st:st + size, :] for st in starts], axis=0).astype(BF16)

  def block_rows(segments, nb):
    rows = n // len(segments)
    q_starts = [s * seg + nb * rows for s in segments]
    if nb == 0:
      return rows, q_starts, q_starts, rows
    return rows, q_starts, [st - rows for st in q_starts], 2 * rows

  def scores(p, segments, nb):
    rows, q_starts, k_starts, k_rows = block_rows(segments, nb)
    bm = bm_ref[p, :, 2 * n:3 * n] if nb == 0 else bm_ref[p, :, 0:2 * n]
    return _dot_nt(gather(q_ref, q_starts, rows), gather(k_ref, k_starts, k_rows)) * SCALE + bm

  def finish(p, segments, nb, e, l, lse):
    rows, q_starts, k_starts, k_rows = block_rows(segments, nb)
    o = _dot(e.astype(BF16), gather(v_ref, k_starts, k_rows)) / l
    lse = jnp.broadcast_to(lse, (n, HEAD_DIM))
    for j, st in enumerate(q_starts):
      accs[p][st:st + rows, :] = o[j * rows:(j + 1) * rows, :]
      lses[p][st:st + rows, :] = lse[j * rows:(j + 1) * rows, :]

  blocks = [(p, list(range(r, RESIDUES, d)), nb)
            for p, (_, d) in enumerate(DILATED_PATTERNS)
            for r in range(d) for nb in range(seq // (n * d))]
  for g in range(0, len(blocks), DIL_BLOCK_GROUP):
    group = blocks[g:g + DIL_BLOCK_GROUP]
    ss = [scores(*blk) for blk in group]
    ms = [jnp.max(s, axis=-1, keepdims=True) for s in ss]
    es = [jnp.exp(s - m) for s, m in zip(ss, ms)]
    ls = [jnp.sum(e, axis=-1, keepdims=True) for e in es]
    for blk, e, l, m in zip(group, es, ls, ms):
      finish(*blk, e, l, m + jnp.log(l))

  def merge(c, carry):
    rows = pl.ds(pl.multiple_of(c * n, n), n)
    l0, l1, l2 = lse0[rows, :], lse1[rows, :], lse2[rows, :]
    mx = jnp.maximum(jnp.maximum(l0, l1), l2)
    w0, w1, w2 = jnp.exp(l0 - mx), jnp.exp(l1 - mx), jnp.exp(l2 - mx)
    mixed = (w0 * acc0[rows, :] + w1 * acc1[rows, :] + w2 * acc2[rows, :]) / (w0 + w1 + w2)
    o_ref[rows, :] = mixed.astype(o_ref.dtype)
    return carry

  lax.fori_loop(0, seq // n, merge, 0)


def _dilated_attention(u_b, band_bias, batch, seq):
  assert seq // RESIDUES == WINDOW_KEYS
  blk = lambda off: pl.BlockSpec((seq, HEAD_DIM), lambda b, h: (b, off + h))
  n_pat = len(DILATED_PATTERNS)
  scratch = [pltpu.VMEM((seq, HEAD_DIM), F32) for _ in range(2 * n_pat)]
  return pl.pallas_call(
      _dil_kernel,
      grid=(batch, N_HEADS_DIL),
      in_specs=[
          blk(0), blk(N_HEADS_DIL), blk(2 * N_HEADS_DIL),
          pl.BlockSpec((n_pat, None, WINDOW_KEYS, 3 * WINDOW_KEYS), lambda b, h: (0, h, 0, 0)),
      ],
      out_specs=pl.BlockSpec((seq, HEAD_DIM), lambda b, h: (b, h)),
      out_shape=jax.ShapeDtypeStruct((batch * seq, D_DIL), BF16),
      scratch_shapes=scratch,
      compiler_params=_params("parallel", "arbitrary"),
      name="dilated_attention",
  )(u_b, u_b, u_b, band_bias)


def _out_proj_kernel(x_ref, a_ref, b_ref, swap_ref, wa_ref, wb_ref, o_ref):
  per = PERM_ROWS // RESIDUES
  for a in range(x_ref.shape[0] // PERM_ROWS):
    rows = slice(a * PERM_ROWS, (a + 1) * PERM_ROWS)
    slab = jnp.concatenate([b_ref[r, a * per:(a + 1) * per, :] for r in range(RESIDUES)], axis=0)
    o_b = _dot(swap_ref[...], slab).astype(BF16)
    o_ref[rows, :] = x_ref[rows, :] + _dot(a_ref[rows, :], wa_ref[...]) + _dot(o_b, wb_ref[...])


def _out_proj(x, o_a, o_b, wo_a, wo_b, batch, seq):
  m, d = x.shape
  tm = PROJ_ROW_TILE
  tiles = seq // tm
  c = o_b.shape[1]
  return pl.pallas_call(
      _out_proj_kernel,
      grid=(m // tm,),
      in_specs=[
          pl.BlockSpec((tm, d), lambda i: (i, 0)),
          pl.BlockSpec((tm, o_a.shape[1]), lambda i: (i, 0)),
          pl.BlockSpec((RESIDUES, tm // RESIDUES, c), lambda i: (i // tiles, i % tiles, 0)),
          pl.BlockSpec((PERM_ROWS, PERM_ROWS), lambda i: (0, 0)),
          pl.BlockSpec(wo_a.shape, lambda i: (0, 0)),
          pl.BlockSpec(wo_b.shape, lambda i: (0, 0)),
      ],
      out_specs=pl.BlockSpec((tm, d), lambda i: (i, 0)),
      out_shape=jax.ShapeDtypeStruct((m, d), F32),
      compiler_params=_params("parallel"),
      name="out_proj",
  )(x, o_a, o_b.reshape(batch * RESIDUES, seq // RESIDUES, c), _swap_matrix(), wo_a, wo_b)


def _ple_kernel(x_ref, p_ref, g_ref, wg_ref, wp_ref, gf_ref, o_ref, *, final_norm):
  x = x_ref[...]
  h = _rmsnorm(x, g_ref[...]).astype(BF16)
  gate = jax.nn.sigmoid(_dot(h, wg_ref[...]))
  y = x + gate * _dot(p_ref[...].astype(BF16), wp_ref[...])
  o_ref[...] = _rmsnorm(y, gf_ref[...]) if final_norm else y


def _ple(x, p, g, w_gate, w_proj, g_final, final_norm):
  m, d = x.shape
  tm = PROJ_ROW_TILE
  return pl.pallas_call(
      functools.partial(_ple_kernel, final_norm=final_norm),
      grid=(m // tm,),
      in_specs=[
          pl.BlockSpec((tm, d), lambda i: (i, 0)),
          pl.BlockSpec((tm, p.shape[1]), lambda i: (i, 0)),
          pl.BlockSpec((1, d), lambda i: (0, 0)),
          pl.BlockSpec(w_gate.shape, lambda i: (0, 0)),
          pl.BlockSpec(w_proj.shape, lambda i: (0, 0)),
          pl.BlockSpec((1, d), lambda i: (0, 0)),
      ],
      out_specs=pl.BlockSpec((tm, d), lambda i: (i, 0)),
      out_shape=jax.ShapeDtypeStruct((m, d), F32),
      compiler_params=_params("parallel"),
      name="ple",
  )(x, p, g, w_gate, w_proj, g_final)


def kernel(x, p, norm_ffn1, ffn1_w_gate, ffn1_w_up, ffn1_w_down, norm_mix, w_in, b_f, w_o,
           norm_ffn2, ffn2_w_gate, ffn2_w_up, ffn2_w_down, norm_ple, w_ple_gate, w_ple_proj,
           rel_table, norm_final):
  batch, seq, d = x.shape
  depth = p.shape[0]
  m = batch * seq
  bf = lambda w: w.astype(BF16)
  row = lambda g: g.reshape(1, -1).astype(F32)

  band_bias = _band_bias(rel_table.astype(F32))
  xs = x.reshape(m, d).astype(F32)
  for i in range(depth):
    xs = _ffn(xs, row(norm_ffn1[i]), bf(ffn1_w_gate[i]), bf(ffn1_w_up[i]), bf(0.5 * ffn1_w_down[i]))

    w = w_in[i]
    f0, f1 = 3 * D_FOX, 3 * D_FOX + N_HEADS_FOX
    w_f = jnp.pad(w[:, f0:f1], ((0, 0), (0, V7X_LANES - N_HEADS_FOX)))
    b_f_row = jnp.pad(b_f[i].astype(F32), (0, V7X_LANES - N_HEADS_FOX)).reshape(1, V7X_LANES)
    g_mix = row(norm_mix[i])
    u_a, f_logit = _norm_matmul(xs, g_mix, bf(w[:, :f0]), BF16, 1024, w_narrow=bf(w_f))
    u_b = _norm_matmul(xs, g_mix, bf(w[:, f1:]), F32, 1024, residue_major=(batch, seq))

    c = _fox_decay(f_logit, b_f_row, batch, seq)
    o_a = _fox_attention(u_a, c, batch, seq)
    o_b = _dilated_attention(u_b, band_bias, batch, seq)
    xs = _out_proj(xs, o_a, o_b, bf(w_o[i][:D_FOX]), bf(w_o[i][D_FOX:]), batch, seq)

    xs = _ffn(xs, row(norm_ffn2[i]), bf(ffn2_w_gate[i]), bf(ffn2_w_up[i]), bf(0.5 * ffn2_w_down[i]))
    last = i == depth - 1
    xs = _ple(xs, p[i].reshape(m, -1), row(norm_ple[i]), bf(w_ple_gate[i]), bf(w_ple_proj[i]),
              row(norm_final), final_norm=last)
  return xs.reshape(batch, seq, d).astype(x.dtype)
```

```python
import functools
import math

import jax
import jax.numpy as jnp
import numpy as np
from jax import lax
from jax.experimental import pallas as pl
from jax.experimental.pallas import tpu as pltpu

F32 = jnp.float32
BF16 = jnp.bfloat16

HEAD_DIM = 128
N_HEADS_FOX = 8
N_HEADS_DIL = 8
D_FOX = N_HEADS_FOX * HEAD_DIM
D_DIL = N_HEADS_DIL * HEAD_DIM
DILATED_PATTERNS = ((128, 1), (512, 4), (2048, 16))
WINDOW_KEYS = 128
N_REL_BUCKETS = 32
REL_MAX_DISTANCE = 2048
RMS_EPS = 1e-6
NEG_INF = -1e30
SCALE = HEAD_DIM ** -0.5

V7X_LANES = 128
V7X_VMEM_LIMIT_BYTES = 56 * 1024 * 1024

RESIDUES = max(d for _, d in DILATED_PATTERNS)
PERM_ROWS = RESIDUES * RESIDUES

FFN_ROW_TILE = 1024
FFN_SUB_ROWS = 512
FFN_FF_TILE = 512
PROJ_ROW_TILE = 512
FOX_Q_TILE = 256
DIL_BLOCK_GROUP = 8


def _params(*semantics):
  return pltpu.CompilerParams(dimension_semantics=semantics,
                              vmem_limit_bytes=V7X_VMEM_LIMIT_BYTES)


def _rmsnorm(x, g):
  ms = jnp.mean(x * x, axis=-1, keepdims=True)
  return x * lax.rsqrt(ms + RMS_EPS) * g


def _dot(a, b):
  return jnp.dot(a, b, preferred_element_type=F32)


def _dot_nt(a, b):
  return lax.dot_general(a, b, (((1,), (1,)), ((), ())), preferred_element_type=F32)


def _ffn_kernel(x_ref, g_ref, wg_ref, wu_ref, wd_ref, o_ref, h_ref):
  j = pl.program_id(1)

  @pl.when(j == 0)
  def _():
    x = x_ref[...]
    h_ref[...] = _rmsnorm(x, g_ref[...]).astype(BF16)
    o_ref[...] = x

  for r in range(h_ref.shape[0] // FFN_SUB_ROWS):
    rows = slice(r * FFN_SUB_ROWS, (r + 1) * FFN_SUB_ROWS)
    h = h_ref[rows, :]
    gate = _dot(h, wg_ref[...])
    up = _dot(h, wu_ref[...])
    act = (gate * jax.nn.sigmoid(gate)) * up
    o_ref[rows, :] += _dot(act.astype(BF16), wd_ref[...])


def _ffn(x, g, wg, wu, wd_half):
  m, d = x.shape
  dff = wg.shape[1]
  tm, tf = FFN_ROW_TILE, FFN_FF_TILE
  return pl.pallas_call(
      _ffn_kernel,
      grid=(m // tm, dff // tf),
      in_specs=[
          pl.BlockSpec((tm, d), lambda i, j: (i, 0)),
          pl.BlockSpec((1, d), lambda i, j: (0, 0)),
          pl.BlockSpec((d, tf), lambda i, j: (0, j)),
          pl.BlockSpec((d, tf), lambda i, j: (0, j)),
          pl.BlockSpec((tf, d), lambda i, j: (j, 0)),
      ],
      out_specs=pl.BlockSpec((tm, d), lambda i, j: (i, 0)),
      out_shape=jax.ShapeDtypeStruct((m, d), F32),
      scratch_shapes=[pltpu.VMEM((tm, d), BF16)],
      compiler_params=_params("parallel", "arbitrary"),
      name="ffn",
  )(x, g, wg, wu, wd_half)


def _swap_matrix():
  i = np.arange(PERM_ROWS)
  src = (i % RESIDUES) * RESIDUES + i // RESIDUES
  return jnp.asarray(np.eye(PERM_ROWS, dtype=np.float32)[src], BF16)


def _norm_matmul_kernel(x_ref, g_ref, w_ref, *rest, permute, narrow):
  rest = list(rest)
  swap_ref = rest.pop(0) if permute else None
  wn_ref = rest.pop(0) if narrow else None
  o_ref = rest.pop(0)
  on_ref = rest.pop(0) if narrow else None
  h_ref, = rest
  tm = x_ref.shape[0]

  @pl.when(pl.program_id(1) == 0)
  def _():
    h = _rmsnorm(x_ref[...], g_ref[...]).astype(BF16)
    if permute:
      for a in range(tm // PERM_ROWS):
        rows = slice(a * PERM_ROWS, (a + 1) * PERM_ROWS)
        h_ref[rows, :] = _dot(swap_ref[...], h[rows, :]).astype(BF16)
    else:
      h_ref[...] = h
    if narrow:
      on_ref[...] = _dot(h_ref[...], wn_ref[...])

  res = _dot(h_ref[...], w_ref[...]).astype(o_ref.dtype)
  if permute:
    per = PERM_ROWS // RESIDUES
    for a in range(tm // PERM_ROWS):
      for r in range(RESIDUES):
        start = a * PERM_ROWS + r * per
        o_ref[r, a * per:(a + 1) * per, :] = res[start:start + per, :]
  else:
    o_ref[...] = res


def _norm_matmul(x, g, w, out_dtype, tn, residue_major=None, w_narrow=None):
  m, d = x.shape
  n = w.shape[1]
  tm = PROJ_ROW_TILE
  narrow = w_narrow is not None
  permute = residue_major is not None
  in_specs = [
      pl.BlockSpec((tm, d), lambda i, j: (i, 0)),
      pl.BlockSpec((1, d), lambda i, j: (0, 0)),
      pl.BlockSpec((d, tn), lambda i, j: (0, j)),
  ]
  args = [x, g, w]
  if permute:
    batch, seq = residue_major
    seg = seq // RESIDUES
    tiles = seq // tm
    in_specs.append(pl.BlockSpec((PERM_ROWS, PERM_ROWS), lambda i, j: (0, 0)))
    args.append(_swap_matrix())
    out_specs = pl.BlockSpec((RESIDUES, tm // RESIDUES, tn),
                             lambda i, j: (i // tiles, i % tiles, j))
    out_shape = jax.ShapeDtypeStruct((batch * RESIDUES, seg, n), out_dtype)
  else:
    out_specs = pl.BlockSpec((tm, tn), lambda i, j: (i, j))
    out_shape = jax.ShapeDtypeStruct((m, n), out_dtype)
  if narrow:
    in_specs.append(pl.BlockSpec(w_narrow.shape, lambda i, j: (0, 0)))
    args.append(w_narrow)
    out_specs = (out_specs, pl.BlockSpec((tm, w_narrow.shape[1]), lambda i, j: (i, 0)))
    out_shape = (out_shape, jax.ShapeDtypeStruct((m, w_narrow.shape[1]), F32))
  out = pl.pallas_call(
      functools.partial(_norm_matmul_kernel, permute=permute, narrow=narrow),
      grid=(m // tm, n // tn),
      in_specs=in_specs,
      out_specs=out_specs,
      out_shape=out_shape,
      scratch_shapes=[pltpu.VMEM((tm, d), BF16)],
      compiler_params=_params("parallel", "arbitrary"),
      name="norm_matmul",
  )(*args)
  if permute:
    return out.reshape(m, n)
  return out


def _cumsum_kernel(fl_ref, bf_ref, c_ref):
  s = fl_ref.shape[0]
  z = fl_ref[...] + bf_ref[...]
  logf = jnp.minimum(z, 0.0) - jnp.log1p(jnp.exp(-jnp.abs(z)))
  lt = logf.T[0:N_HEADS_FOX, :]
  row = lax.broadcasted_iota(jnp.int32, (V7X_LANES, V7X_LANES), 0)
  col = lax.broadcasted_iota(jnp.int32, (V7X_LANES, V7X_LANES), 1)
  upper = (row <= col).astype(F32)
  carry = jnp.zeros((N_HEADS_FOX, 1), F32)
  for j in range(s // V7X_LANES):
    blk = lt[:, j * V7X_LANES:(j + 1) * V7X_LANES]
    cs = jnp.dot(blk, upper, preferred_element_type=F32,
                 precision=lax.Precision.HIGHEST) + carry
    c_ref[:, j * V7X_LANES:(j + 1) * V7X_LANES] = cs
    carry = cs[:, V7X_LANES - 1:V7X_LANES]


def _fox_decay(f_logit, b_f_row, batch, seq):
  return pl.pallas_call(
      _cumsum_kernel,
      grid=(batch,),
      in_specs=[
          pl.BlockSpec((seq, V7X_LANES), lambda b: (b, 0)),
          pl.BlockSpec((1, V7X_LANES), lambda b: (0, 0)),
      ],
      out_specs=pl.BlockSpec((None, N_HEADS_FOX, seq), lambda b: (b, 0, 0)),
      out_shape=jax.ShapeDtypeStruct((batch, N_HEADS_FOX, seq), F32),
      compiler_params=_params("parallel"),
      name="fox_decay",
  )(f_logit, b_f_row)


def _fox_kernel(q_ref, k_ref, v_ref, c_ref, o_ref):
  seq = q_ref.shape[0]
  tq = FOX_Q_TILE
  h = pl.program_id(1)
  crow = c_ref[pl.ds(h, 1), :]
  row = lax.broadcasted_iota(jnp.int32, (tq, tq), 0)
  col = lax.broadcasted_iota(jnp.int32, (tq, tq), 1)
  diag_mask = jnp.where(col > row, NEG_INF, 0.0).astype(F32)
  for i in range(seq // tq):
    t0, t1 = i * tq, (i + 1) * tq
    q = q_ref[t0:t1, :]
    bias = crow[:, t1 - 1:t1] - crow[:, 0:t1]
    s_diag = _dot_nt(q, k_ref[t0:t1, :]) * SCALE + bias[:, t0:t1] + diag_mask
    m = jnp.max(s_diag, axis=-1, keepdims=True)
    if i > 0:
      s_off = _dot_nt(q, k_ref[0:t0, :]) * SCALE + bias[:, 0:t0]
      m = jnp.maximum(m, jnp.max(s_off, axis=-1, keepdims=True))
    e_diag = jnp.exp(s_diag - m)
    l = jnp.sum(e_diag, axis=-1, keepdims=True)
    o = _dot(e_diag.astype(BF16), v_ref[t0:t1, :])
    if i > 0:
      e_off = jnp.exp(s_off - m)
      l = l + jnp.sum(e_off, axis=-1, keepdims=True)
      o = o + _dot(e_off.astype(BF16), v_ref[0:t0, :])
    o_ref[t0:t1, :] = (o / l).astype(o_ref.dtype)


def _fox_attention(u_a, c, batch, seq):
  blk = lambda off: pl.BlockSpec((seq, HEAD_DIM), lambda b, h: (b, off + h))
  return pl.pallas_call(
      _fox_kernel,
      grid=(batch, N_HEADS_FOX),
      in_specs=[
          blk(0), blk(N_HEADS_FOX), blk(2 * N_HEADS_FOX),
          pl.BlockSpec((None, N_HEADS_FOX, seq), lambda b, h: (b, 0, 0)),
      ],
      out_specs=pl.BlockSpec((seq, HEAD_DIM), lambda b, h: (b, h)),
      out_shape=jax.ShapeDtypeStruct((batch * seq, D_FOX), BF16),
      compiler_params=_params("parallel", "arbitrary"),
      name="fox_attention",
  )(u_a, u_a, u_a, c)


def _t5_bucket_np(dist):
  max_exact = N_REL_BUCKETS // 2
  d = np.maximum(dist, 1).astype(np.float32)
  large = max_exact + (np.log(d / np.float32(max_exact))
                       / np.float32(math.log(REL_MAX_DISTANCE / max_exact))
                       * np.float32(N_REL_BUCKETS - max_exact)).astype(np.int32)
  large = np.minimum(large, N_REL_BUCKETS - 1)
  return np.where(dist < max_exact, dist, large).astype(np.int32)


def _block_positions(dilation):
  n = WINDOW_KEYS
  m = RESIDUES // dilation
  rows = n // m
  j = np.arange(m)[:, None]
  qpos = (n + m * np.arange(rows)[None, :] + j).reshape(-1)
  kpos = (m * np.arange(2 * rows)[None, :] + j).reshape(-1)
  return qpos, kpos


def _band_buckets():
  n = WINDOW_KEYS
  tiles = []
  for _, dilation in DILATED_PATTERNS:
    qpos, kpos = _block_positions(dilation)
    rel = qpos[:, None] - np.concatenate([kpos, qpos])[None, :]
    valid = (rel >= 0) & (rel <= n)
    bucket = _t5_bucket_np(np.maximum(rel, 0) * dilation)
    tiles.append(np.where(valid, bucket, -1))
  return np.stack(tiles).astype(np.int32)


def _bias_kernel(tab_ref, bkt_ref, o_ref):
  bkt = bkt_ref[...]
  for h in range(N_HEADS_DIL):
    acc = jnp.full(bkt.shape, NEG_INF, F32)
    for b in range(N_REL_BUCKETS):
      acc = jnp.where(bkt == b, tab_ref[b, h], acc)
    o_ref[h] = acc


def _band_bias(rel_table):
  buckets = jnp.asarray(_band_buckets())
  p, n, n2 = buckets.shape
  return pl.pallas_call(
      _bias_kernel,
      grid=(p,),
      in_specs=[
          pl.BlockSpec(memory_space=pltpu.SMEM),
          pl.BlockSpec((None, n, n2), lambda i: (i, 0, 0)),
      ],
      out_specs=pl.BlockSpec((None, N_HEADS_DIL, n, n2), lambda i: (i, 0, 0, 0)),
      out_shape=jax.ShapeDtypeStruct((p, N_HEADS_DIL, n, n2), F32),
      compiler_params=_params("parallel"),
      name="band_bias",
  )(rel_table, buckets)


def _dil_kernel(q_ref, k_ref, v_ref, bm_ref, o_ref,
                acc0, acc1, acc2, lse0, lse1, lse2):
  seq = q_ref.shape[0]
  n = WINDOW_KEYS
  seg = seq // RESIDUES
  accs = (acc0, acc1, acc2)
  lses = (lse0, lse1, lse2)

  def gather(ref, starts, size):
    return jnp.concatenate([ref[st:st + size, :] for st in starts], axis=0).astype(BF16)

  def block_rows(segments, nb):
    rows = n // len(segments)
    q_starts = [s * seg + nb * rows for s in segments]
    if nb == 0:
      return rows, q_starts, q_starts, rows
    return rows, q_starts, [st - rows for st in q_starts], 2 * rows

  def scores(p, segments, nb):
    rows, q_starts, k_starts, k_rows = block_rows(segments, nb)
    bm = bm_ref[p, :, 2 * n:3 * n] if nb == 0 else bm_ref[p, :, 0:2 * n]
    return _dot_nt(gather(q_ref, q_starts, rows), gather(k_ref, k_starts, k_rows)) * SCALE + bm

  def finish(p, segments, nb, e, l, lse):
    rows, q_starts, k_starts, k_rows = block_rows(segments, nb)
    o = _dot(e.astype(BF16), gather(v_ref, k_starts, k_rows)) / l
    lse = jnp.broadcast_to(lse, (n, HEAD_DIM))
    for j, st in enumerate(q_starts):
      accs[p][st:st + rows, :] = o[j * rows:(j + 1) * rows, :]
      lses[p][st:st + rows, :] = lse[j * rows:(j + 1) * rows, :]

  blocks = [(p, list(range(r, RESIDUES, d)), nb)
            for p, (_, d) in enumerate(DILATED_PATTERNS)
            for r in range(d) for nb in range(seq // (n * d))]
  for g in range(0, len(blocks), DIL_BLOCK_GROUP):
    group = blocks[g:g + DIL_BLOCK_GROUP]
    ss = [scores(*blk) for blk in group]
    ms = [jnp.max(s, axis=-1, keepdims=True) for s in ss]
    es = [jnp.exp(s - m) for s, m in zip(ss, ms)]
    ls = [jnp.sum(e, axis=-1, keepdims=True) for e in es]
    for blk, e, l, m in zip(group, es, ls, ms):
      finish(*blk, e, l, m + jnp.log(l))

  def merge(c, carry):
    rows = pl.ds(pl.multiple_of(c * n, n), n)
    l0, l1, l2 = lse0[rows, :], lse1[rows, :], lse2[rows, :]
    mx = jnp.maximum(jnp.maximum(l0, l1), l2)
    w0, w1, w2 = jnp.exp(l0 - mx), jnp.exp(l1 - mx), jnp.exp(l2 - mx)
    mixed = (w0 * acc0[rows, :] + w1 * acc1[rows, :] + w2 * acc2[rows, :]) / (w0 + w1 + w2)
    o_ref[rows, :] = mixed.astype(o_ref.dtype)
    return carry

  lax.fori_loop(0, seq // n, merge, 0)


def _dilated_attention(u_b, band_bias, batch, seq):
  assert seq // RESIDUES == WINDOW_KEYS
  blk = lambda off: pl.BlockSpec((seq, HEAD_DIM), lambda b, h: (b, off + h))
  n_pat = len(DILATED_PATTERNS)
  scratch = [pltpu.VMEM((seq, HEAD_DIM), F32) for _ in range(2 * n_pat)]
  return pl.pallas_call(
      _dil_kernel,
      grid=(batch, N_HEADS_DIL),
      in_specs=[
          blk(0), blk(N_HEADS_DIL), blk(2 * N_HEADS_DIL),
          pl.BlockSpec((n_pat, None, WINDOW_KEYS, 3 * WINDOW_KEYS), lambda b, h: (0, h, 0, 0)),
      ],
      out_specs=pl.BlockSpec((seq, HEAD_DIM), lambda b, h: (b, h)),
      out_shape=jax.ShapeDtypeStruct((batch * seq, D_DIL), BF16),
      scratch_shapes=scratch,
      compiler_params=_params("parallel", "arbitrary"),
      name="dilated_attention",
  )(u_b, u_b, u_b, band_bias)


def _out_proj_kernel(x_ref, a_ref, b_ref, swap_ref, wa_ref, wb_ref, o_ref):
  per = PERM_ROWS // RESIDUES
  for a in range(x_ref.shape[0] // PERM_ROWS):
    rows = slice(a * PERM_ROWS, (a + 1) * PERM_ROWS)
    slab = jnp.concatenate([b_ref[r, a * per:(a + 1) * per, :] for r in range(RESIDUES)], axis=0)
    o_b = _dot(swap_ref[...], slab).astype(BF16)
    o_ref[rows, :] = x_ref[rows, :] + _dot(a_ref[rows, :], wa_ref[...]) + _dot(o_b, wb_ref[...])


def _out_proj(x, o_a, o_b, wo_a, wo_b, batch, seq):
  m, d = x.shape
  tm = PROJ_ROW_TILE
  tiles = seq // tm
  c = o_b.shape[1]
  return pl.pallas_call(
      _out_proj_kernel,
      grid=(m // tm,),
      in_specs=[
          pl.BlockSpec((tm, d), lambda i: (i, 0)),
          pl.BlockSpec((tm, o_a.shape[1]), lambda i: (i, 0)),
          pl.BlockSpec((RESIDUES, tm // RESIDUES, c), lambda i: (i // tiles, i % tiles, 0)),
          pl.BlockSpec((PERM_ROWS, PERM_ROWS), lambda i: (0, 0)),
          pl.BlockSpec(wo_a.shape, lambda i: (0, 0)),
          pl.BlockSpec(wo_b.shape, lambda i: (0, 0)),
      ],
      out_specs=pl.BlockSpec((tm, d), lambda i: (i, 0)),
      out_shape=jax.ShapeDtypeStruct((m, d), F32),
      compiler_params=_params("parallel"),
      name="out_proj",
  )(x, o_a, o_b.reshape(batch * RESIDUES, seq // RESIDUES, c), _swap_matrix(), wo_a, wo_b)


def _ple_kernel(x_ref, p_ref, g_ref, wg_ref, wp_ref, gf_ref, o_ref, *, final_norm):
  x = x_ref[...]
  h = _rmsnorm(x, g_ref[...]).astype(BF16)
  gate = jax.nn.sigmoid(_dot(h, wg_ref[...]))
  y = x + gate * _dot(p_ref[...].astype(BF16), wp_ref[...])
  o_ref[...] = _rmsnorm(y, gf_ref[...]) if final_norm else y


def _ple(x, p, g, w_gate, w_proj, g_final, final_norm):
  m, d = x.shape
  tm = PROJ_ROW_TILE
  return pl.pallas_call(
      functools.partial(_ple_kernel, final_norm=final_norm),
      grid=(m // tm,),
      in_specs=[
          pl.BlockSpec((tm, d), lambda i: (i, 0)),
          pl.BlockSpec((tm, p.shape[1]), lambda i: (i, 0)),
          pl.BlockSpec((1, d), lambda i: (0, 0)),
          pl.BlockSpec(w_gate.shape, lambda i: (0, 0)),
          pl.BlockSpec(w_proj.shape, lambda i: (0, 0)),
          pl.BlockSpec((1, d), lambda i: (0, 0)),
      ],
      out_specs=pl.BlockSpec((tm, d), lambda i: (i, 0)),
      out_shape=jax.ShapeDtypeStruct((m, d), F32),
      compiler_params=_params("parallel"),
      name="ple",
  )(x, p, g, w_gate, w_proj, g_final)


def kernel(x, p, norm_ffn1, ffn1_w_gate, ffn1_w_up, ffn1_w_down, norm_mix, w_in, b_f, w_o,
           norm_ffn2, ffn2_w_gate, ffn2_w_up, ffn2_w_down, norm_ple, w_ple_gate, w_ple_proj,
           rel_table, norm_final):
  batch, seq, d = x.shape
  depth = p.shape[0]
  m = batch * seq
  bf = lambda w: w.astype(BF16)
  row = lambda g: g.reshape(1, -1).astype(F32)

  band_bias = _band_bias(rel_table.astype(F32))
  xs = x.reshape(m, d).astype(F32)
  for i in range(depth):
    xs = _ffn(xs, row(norm_ffn1[i]), bf(ffn1_w_gate[i]), bf(ffn1_w_up[i]), bf(0.5 * ffn1_w_down[i]))

    w = w_in[i]
    f0, f1 = 3 * D_FOX, 3 * D_FOX + N_HEADS_FOX
    w_f = jnp.pad(w[:, f0:f1], ((0, 0), (0, V7X_LANES - N_HEADS_FOX)))
    b_f_row = jnp.pad(b_f[i].astype(F32), (0, V7X_LANES - N_HEADS_FOX)).reshape(1, V7X_LANES)
    g_mix = row(norm_mix[i])
    u_a, f_logit = _norm_matmul(xs, g_mix, bf(w[:, :f0]), BF16, f0, w_narrow=bf(w_f))
    u_b = _norm_matmul(xs, g_mix, bf(w[:, f1:]), F32, 3 * D_DIL, residue_major=(batch, seq))

    c = _fox_decay(f_logit, b_f_row, batch, seq)
    o_a = _fox_attention(u_a, c, batch, seq)
    o_b = _dilated_attention(u_b, band_bias, batch, seq)
    xs = _out_proj(xs, o_a, o_b, bf(w_o[i][:D_FOX]), bf(w_o[i][D_FOX:]), batch, seq)

    xs = _ffn(xs, row(norm_ffn2[i]), bf(ffn2_w_gate[i]), bf(ffn2_w_up[i]), bf(0.5 * ffn2_w_down[i]))
    last = i == depth - 1
    xs = _ple(xs, p[i].reshape(m, -1), row(norm_ple[i]), bf(w_ple_gate[i]), bf(w_ple_proj[i]),
              row(norm_final), final_norm=last)
  return xs.reshape(batch, seq, d).astype(x.dtype)
```

```python
import functools
import math

import jax
import jax.numpy as jnp
import numpy as np
from jax import lax
from jax.experimental import pallas as pl
from jax.experimental.pallas import tpu as pltpu

F32 = jnp.float32
BF16 = jnp.bfloat16

HEAD_DIM = 128
N_HEADS_FOX = 8
N_HEADS_DIL = 8
D_FOX = N_HEADS_FOX * HEAD_DIM
D_DIL = N_HEADS_DIL * HEAD_DIM
DILATED_PATTERNS = ((128, 1), (512, 4), (2048, 16))
WINDOW_KEYS = 128
N_REL_BUCKETS = 32
REL_MAX_DISTANCE = 2048
RMS_EPS = 1e-6
NEG_INF = -1e30
SCALE = HEAD_DIM ** -0.5
LOG2_E = math.log2(math.e)

V7X_LANES = 128
V7X_VMEM_LIMIT_BYTES = 56 * 1024 * 1024

RESIDUES = max(d for _, d in DILATED_PATTERNS)
PERM_ROWS = RESIDUES * RESIDUES

FFN_ROW_TILE = 1024
FFN_SUB_ROWS = 512
FFN_FF_TILE = 512
PROJ_ROW_TILE = 512
FOX_Q_TILE = 256
DIL_BLOCK_GROUP = 8


def _params(*semantics):
  return pltpu.CompilerParams(dimension_semantics=semantics,
                              vmem_limit_bytes=V7X_VMEM_LIMIT_BYTES)


def _rmsnorm(x, g):
  ms = jnp.mean(x * x, axis=-1, keepdims=True)
  return x * lax.rsqrt(ms + RMS_EPS) * g


def _dot(a, b):
  return jnp.dot(a, b, preferred_element_type=F32)


def _dot_nt(a, b):
  return lax.dot_general(a, b, (((1,), (1,)), ((), ())), preferred_element_type=F32)


def _ffn_kernel(x_ref, g_ref, wg_ref, wu_ref, wd_ref, o_ref, h_ref):
  j = pl.program_id(1)

  @pl.when(j == 0)
  def _():
    x = x_ref[...]
    h_ref[...] = _rmsnorm(x, g_ref[...]).astype(BF16)
    o_ref[...] = x

  for r in range(h_ref.shape[0] // FFN_SUB_ROWS):
    rows = slice(r * FFN_SUB_ROWS, (r + 1) * FFN_SUB_ROWS)
    h = h_ref[rows, :]
    gate = _dot(h, wg_ref[...])
    up = _dot(h, wu_ref[...])
    act = (gate * jax.nn.sigmoid(gate)) * up
    o_ref[rows, :] += _dot(act.astype(BF16), wd_ref[...])


def _ffn(x, g, wg, wu, wd_half):
  m, d = x.shape
  dff = wg.shape[1]
  tm, tf = FFN_ROW_TILE, FFN_FF_TILE
  return pl.pallas_call(
      _ffn_kernel,
      grid=(m // tm, dff // tf),
      in_specs=[
          pl.BlockSpec((tm, d), lambda i, j: (i, 0)),
          pl.BlockSpec((1, d), lambda i, j: (0, 0)),
          pl.BlockSpec((d, tf), lambda i, j: (0, j)),
          pl.BlockSpec((d, tf), lambda i, j: (0, j)),
          pl.BlockSpec((tf, d), lambda i, j: (j, 0)),
      ],
      out_specs=pl.BlockSpec((tm, d), lambda i, j: (i, 0)),
      out_shape=jax.ShapeDtypeStruct((m, d), F32),
      scratch_shapes=[pltpu.VMEM((tm, d), BF16)],
      compiler_params=_params("parallel", "arbitrary"),
      name="ffn",
  )(x, g, wg, wu, wd_half)


def _swap_matrix():
  i = np.arange(PERM_ROWS)
  src = (i % RESIDUES) * RESIDUES + i // RESIDUES
  return jnp.asarray(np.eye(PERM_ROWS, dtype=np.float32)[src], BF16)


def _norm_matmul_kernel(x_ref, g_ref, w_ref, *rest, permute, narrow):
  rest = list(rest)
  swap_ref = rest.pop(0) if permute else None
  wn_ref = rest.pop(0) if narrow else None
  o_ref = rest.pop(0)
  on_ref = rest.pop(0) if narrow else None
  assert not rest, "unexpected extra refs"
  tm = x_ref.shape[0]

  h = _rmsnorm(x_ref[...], g_ref[...]).astype(BF16)
  if permute:
    h = jnp.concatenate(
        [_dot(swap_ref[...], h[a * PERM_ROWS:(a + 1) * PERM_ROWS, :]).astype(BF16)
         for a in range(tm // PERM_ROWS)], axis=0)
  if narrow:
    on_ref[...] = _dot(h, wn_ref[...])
  res = _dot(h, w_ref[...]).astype(o_ref.dtype)
  if permute:
    per = PERM_ROWS // RESIDUES
    for a in range(tm // PERM_ROWS):
      for r in range(RESIDUES):
        start = a * PERM_ROWS + r * per
        o_ref[r, a * per:(a + 1) * per, :] = res[start:start + per, :]
  else:
    o_ref[...] = res


def _norm_matmul(x, g, w, n, col, out_dtype, residue_major=None, narrow_col=None):
  m, d = x.shape
  tm = PROJ_ROW_TILE
  narrow = narrow_col is not None
  permute = residue_major is not None
  assert not (narrow and permute)
  resident = pl.Buffered(1)
  in_specs = [
      pl.BlockSpec((tm, d), lambda i: (i, 0)),
      pl.BlockSpec((1, d), lambda i: (0, 0)),
      pl.BlockSpec((d, n), lambda i: (0, col), pipeline_mode=resident),
  ]
  args = [x, g, w]
  if permute:
    batch, seq = residue_major
    tiles = seq // tm
    in_specs.append(pl.BlockSpec((PERM_ROWS, PERM_ROWS), lambda i: (0, 0)))
    args.append(_swap_matrix())
    out_specs = pl.BlockSpec((RESIDUES, tm // RESIDUES, n), lambda i: (i // tiles, i % tiles, 0))
    out_shape = jax.ShapeDtypeStruct((batch * RESIDUES, seq // RESIDUES, n), out_dtype)
  else:
    out_specs = pl.BlockSpec((tm, n), lambda i: (i, 0))
    out_shape = jax.ShapeDtypeStruct((m, n), out_dtype)
  if narrow:
    in_specs.append(pl.BlockSpec((d, V7X_LANES), lambda i: (0, narrow_col), pipeline_mode=resident))
    args.append(w)
    out_specs = (out_specs, pl.BlockSpec((tm, V7X_LANES), lambda i: (i, 0)))
    out_shape = (out_shape, jax.ShapeDtypeStruct((m, V7X_LANES), F32))
  out = pl.pallas_call(
      functools.partial(_norm_matmul_kernel, permute=permute, narrow=narrow),
      grid=(m // tm,),
      in_specs=in_specs,
      out_specs=out_specs,
      out_shape=out_shape,
      compiler_params=_params("parallel"),
      name="norm_matmul",
  )(*args)
  if permute:
    return out.reshape(m, n)
  return out


def _cumsum_kernel(fl_ref, bf_ref, c_ref):
  s = fl_ref.shape[0]
  z = fl_ref[...] + bf_ref[...]
  logf = jnp.minimum(z, 0.0) - jnp.log1p(jnp.exp(-jnp.abs(z)))
  lt = logf.T[0:N_HEADS_FOX, :]
  row = lax.broadcasted_iota(jnp.int32, (V7X_LANES, V7X_LANES), 0)
  col = lax.broadcasted_iota(jnp.int32, (V7X_LANES, V7X_LANES), 1)
  upper = (row <= col).astype(F32)
  carry = jnp.zeros((N_HEADS_FOX, 1), F32)
  for j in range(s // V7X_LANES):
    blk = lt[:, j * V7X_LANES:(j + 1) * V7X_LANES]
    cs = jnp.dot(blk, upper, preferred_element_type=F32,
                 precision=lax.Precision.HIGHEST) + carry
    c_ref[:, j * V7X_LANES:(j + 1) * V7X_LANES] = cs
    carry = cs[:, V7X_LANES - 1:V7X_LANES]


def _fox_decay(f_logit, b_f_row, batch, seq):
  return pl.pallas_call(
      _cumsum_kernel,
      grid=(batch,),
      in_specs=[
          pl.BlockSpec((seq, V7X_LANES), lambda b: (b, 0)),
          pl.BlockSpec((1, V7X_LANES), lambda b: (0, 0)),
      ],
      out_specs=pl.BlockSpec((None, N_HEADS_FOX, seq), lambda b: (b, 0, 0)),
      out_shape=jax.ShapeDtypeStruct((batch, N_HEADS_FOX, seq), F32),
      compiler_params=_params("parallel"),
      name="fox_decay",
  )(f_logit, b_f_row)


def _fox_kernel(q_ref, k_ref, v_ref, c_ref, o_ref):
  seq = q_ref.shape[0]
  tq = FOX_Q_TILE
  h = pl.program_id(1)
  crow = c_ref[pl.ds(h, 1), :] * LOG2_E
  row = lax.broadcasted_iota(jnp.int32, (tq, tq), 0)
  col = lax.broadcasted_iota(jnp.int32, (tq, tq), 1)
  diag_mask = jnp.where(col > row, NEG_INF, 0.0).astype(F32)

  def scores(i):
    t0, t1 = i * tq, (i + 1) * tq
    q = q_ref[t0:t1, :]
    bias = crow[:, t1 - 1:t1] - crow[:, 0:t1]
    s_diag = _dot_nt(q, k_ref[t0:t1, :]) * (SCALE * LOG2_E) + bias[:, t0:t1] + diag_mask
    s_off = _dot_nt(q, k_ref[0:t0, :]) * (SCALE * LOG2_E) + bias[:, 0:t0] if i > 0 else None
    return s_diag, s_off

  def finish(i, s_diag, s_off):
    t0, t1 = i * tq, (i + 1) * tq
    m = jnp.max(s_diag, axis=-1, keepdims=True)
    if i > 0:
      m = jnp.maximum(m, jnp.max(s_off, axis=-1, keepdims=True))
    e_diag = jnp.exp2(s_diag - m)
    l = jnp.sum(e_diag, axis=-1, keepdims=True)
    o = _dot(e_diag.astype(BF16), v_ref[t0:t1, :])
    if i > 0:
      e_off = jnp.exp2(s_off - m)
      l = l + jnp.sum(e_off, axis=-1, keepdims=True)
      o = o + _dot(e_off.astype(BF16), v_ref[0:t0, :])
    o_ref[t0:t1, :] = (o / l).astype(o_ref.dtype)

  n_tiles = seq // tq
  pending = scores(0)
  for i in range(n_tiles):
    upcoming = scores(i + 1) if i + 1 < n_tiles else None
    finish(i, *pending)
    pending = upcoming


def _fox_attention(u_a, c, batch, seq):
  blk = lambda off: pl.BlockSpec((seq, HEAD_DIM), lambda b, h: (b, off + h))
  return pl.pallas_call(
      _fox_kernel,
      grid=(batch, N_HEADS_FOX),
      in_specs=[
          blk(0), blk(N_HEADS_FOX), blk(2 * N_HEADS_FOX),
          pl.BlockSpec((None, N_HEADS_FOX, seq), lambda b, h: (b, 0, 0)),
      ],
      out_specs=pl.BlockSpec((seq, HEAD_DIM), lambda b, h: (b, h)),
      out_shape=jax.ShapeDtypeStruct((batch * seq, D_FOX), BF16),
      compiler_params=_params("parallel", "arbitrary"),
      name="fox_attention",
  )(u_a, u_a, u_a, c)


def _t5_bucket_np(dist):
  max_exact = N_REL_BUCKETS // 2
  d = np.maximum(dist, 1).astype(np.float32)
  large = max_exact + (np.log(d / np.float32(max_exact))
                       / np.float32(math.log(REL_MAX_DISTANCE / max_exact))
                       * np.float32(N_REL_BUCKETS - max_exact)).astype(np.int32)
  large = np.minimum(large, N_REL_BUCKETS - 1)
  return np.where(dist < max_exact, dist, large).astype(np.int32)


def _block_positions(dilation):
  n = WINDOW_KEYS
  m = RESIDUES // dilation
  rows = n // m
  j = np.arange(m)[:, None]
  qpos = (n + m * np.arange(rows)[None, :] + j).reshape(-1)
  kpos = (m * np.arange(2 * rows)[None, :] + j).reshape(-1)
  return qpos, kpos


def _band_buckets():
  n = WINDOW_KEYS
  tiles = []
  for _, dilation in DILATED_PATTERNS:
    qpos, kpos = _block_positions(dilation)
    rel = qpos[:, None] - np.concatenate([kpos, qpos])[None, :]
    valid = (rel >= 0) & (rel <= n)
    bucket = _t5_bucket_np(np.maximum(rel, 0) * dilation)
    tiles.append(np.where(valid, bucket, -1))
  return np.stack(tiles).astype(np.int32)


def _bias_kernel(tab_ref, bkt_ref, o_ref):
  bkt = bkt_ref[...]
  for h in range(N_HEADS_DIL):
    acc = jnp.full(bkt.shape, NEG_INF, F32)
    for b in range(N_REL_BUCKETS):
      acc = jnp.where(bkt == b, tab_ref[b, h] * LOG2_E, acc)
    o_ref[h] = acc


def _band_bias(rel_table):
  buckets = jnp.asarray(_band_buckets())
  p, n, n2 = buckets.shape
  return pl.pallas_call(
      _bias_kernel,
      grid=(p,),
      in_specs=[
          pl.BlockSpec(memory_space=pltpu.SMEM),
          pl.BlockSpec((None, n, n2), lambda i: (i, 0, 0)),
      ],
      out_specs=pl.BlockSpec((None, N_HEADS_DIL, n, n2), lambda i: (i, 0, 0, 0)),
      out_shape=jax.ShapeDtypeStruct((p, N_HEADS_DIL, n, n2), F32),
      compiler_params=_params("parallel"),
      name="band_bias",
  )(rel_table, buckets)


def _dil_kernel(q_ref, k_ref, v_ref, bm_ref, o_ref,
                acc0, acc1, acc2, lse0, lse1, lse2):
  seq = q_ref.shape[0]
  n = WINDOW_KEYS
  seg = seq // RESIDUES
  accs = (acc0, acc1, acc2)
  lses = (lse0, lse1, lse2)

  def gather(ref, starts, size):
    return jnp.concatenate([ref[st:st + size, :] for st in starts], axis=0).astype(BF16)

  def block_rows(segments, nb):
    rows = n // len(segments)
    q_starts = [s * seg + nb * rows for s in segments]
    if nb == 0:
      return rows, q_starts, q_starts, rows
    return rows, q_starts, [st - rows for st in q_starts], 2 * rows

  def scores(p, segments, nb):
    rows, q_starts, k_starts, k_rows = block_rows(segments, nb)
    bm = bm_ref[p, :, 2 * n:3 * n] if nb == 0 else bm_ref[p, :, 0:2 * n]
    qk = _dot_nt(gather(q_ref, q_starts, rows), gather(k_ref, k_starts, k_rows))
    return qk * (SCALE * LOG2_E) + bm

  def finish(p, segments, nb, e, l, lse):
    rows, q_starts, k_starts, k_rows = block_rows(segments, nb)
    o = _dot(e.astype(BF16), gather(v_ref, k_starts, k_rows)) / l
    lse = jnp.broadcast_to(lse, (n, HEAD_DIM))
    for j, st in enumerate(q_starts):
      accs[p][st:st + rows, :] = o[j * rows:(j + 1) * rows, :]
      lses[p][st:st + rows, :] = lse[j * rows:(j + 1) * rows, :]

  blocks = [(p, list(range(r, RESIDUES, d)), nb)
            for p, (_, d) in enumerate(DILATED_PATTERNS)
            for r in range(d) for nb in range(seq // (n * d))]
  for g in range(0, len(blocks), DIL_BLOCK_GROUP):
    group = blocks[g:g + DIL_BLOCK_GROUP]
    ss = [scores(*blk) for blk in group]
    ms = [jnp.max(s, axis=-1, keepdims=True) for s in ss]
    es = [jnp.exp2(s - m) for s, m in zip(ss, ms)]
    ls = [jnp.sum(e, axis=-1, keepdims=True) for e in es]
    for blk, e, l, m in zip(group, es, ls, ms):
      finish(*blk, e, l, m + jnp.log2(l))

  def merge(c, carry):
    rows = pl.ds(pl.multiple_of(c * n, n), n)
    l0, l1, l2 = lse0[rows, :], lse1[rows, :], lse2[rows, :]
    mx = jnp.maximum(jnp.maximum(l0, l1), l2)
    w0, w1, w2 = jnp.exp2(l0 - mx), jnp.exp2(l1 - mx), jnp.exp2(l2 - mx)
    mixed = (w0 * acc0[rows, :] + w1 * acc1[rows, :] + w2 * acc2[rows, :]) / (w0 + w1 + w2)
    o_ref[rows, :] = mixed.astype(o_ref.dtype)
    return carry

  lax.fori_loop(0, seq // n, merge, 0)


def _dilated_attention(u_b, band_bias, batch, seq):
  assert seq // RESIDUES == WINDOW_KEYS
  blk = lambda off: pl.BlockSpec((seq, HEAD_DIM), lambda b, h: (b, off + h))
  n_pat = len(DILATED_PATTERNS)
  scratch = [pltpu.VMEM((seq, HEAD_DIM), F32) for _ in range(2 * n_pat)]
  return pl.pallas_call(
      _dil_kernel,
      grid=(batch, N_HEADS_DIL),
      in_specs=[
          blk(0), blk(N_HEADS_DIL), blk(2 * N_HEADS_DIL),
          pl.BlockSpec((n_pat, None, WINDOW_KEYS, 3 * WINDOW_KEYS), lambda b, h: (0, h, 0, 0)),
      ],
      out_specs=pl.BlockSpec((seq, HEAD_DIM), lambda b, h: (b, h)),
      out_shape=jax.ShapeDtypeStruct((batch * seq, D_DIL), BF16),
      scratch_shapes=scratch,
      compiler_params=_params("parallel", "arbitrary"),
      name="dilated_attention",
  )(u_b, u_b, u_b, band_bias)


def _out_proj_kernel(x_ref, a_ref, b_ref, swap_ref, wa_ref, wb_ref, o_ref):
  per = PERM_ROWS // RESIDUES
  for a in range(x_ref.shape[0] // PERM_ROWS):
    rows = slice(a * PERM_ROWS, (a + 1) * PERM_ROWS)
    slab = jnp.concatenate([b_ref[r, a * per:(a + 1) * per, :] for r in range(RESIDUES)], axis=0)
    o_b = _dot(swap_ref[...], slab).astype(BF16)
    o_ref[rows, :] = x_ref[rows, :] + _dot(a_ref[rows, :], wa_ref[...]) + _dot(o_b, wb_ref[...])


def _out_proj(x, o_a, o_b, wo_a, wo_b, batch, seq):
  m, d = x.shape
  tm = PROJ_ROW_TILE
  tiles = seq // tm
  c = o_b.shape[1]
  return pl.pallas_call(
      _out_proj_kernel,
      grid=(m // tm,),
      in_specs=[
          pl.BlockSpec((tm, d), lambda i: (i, 0)),
          pl.BlockSpec((tm, o_a.shape[1]), lambda i: (i, 0)),
          pl.BlockSpec((RESIDUES, tm // RESIDUES, c), lambda i: (i // tiles, i % tiles, 0)),
          pl.BlockSpec((PERM_ROWS, PERM_ROWS), lambda i: (0, 0)),
          pl.BlockSpec(wo_a.shape, lambda i: (0, 0)),
          pl.BlockSpec(wo_b.shape, lambda i: (0, 0)),
      ],
      out_specs=pl.BlockSpec((tm, d), lambda i: (i, 0)),
      out_shape=jax.ShapeDtypeStruct((m, d), F32),
      compiler_params=_params("parallel"),
      name="out_proj",
  )(x, o_a, o_b.reshape(batch * RESIDUES, seq // RESIDUES, c), _swap_matrix(), wo_a, wo_b)


def _ple_kernel(x_ref, p_ref, g_ref, wg_ref, wp_ref, gf_ref, o_ref, *, final_norm):
  x = x_ref[...]
  h = _rmsnorm(x, g_ref[...]).astype(BF16)
  gate = jax.nn.sigmoid(_dot(h, wg_ref[...]))
  y = x + gate * _dot(p_ref[...].astype(BF16), wp_ref[...])
  o_ref[...] = _rmsnorm(y, gf_ref[...]) if final_norm else y


def _ple(x, p, g, w_gate, w_proj, g_final, final_norm):
  m, d = x.shape
  tm = PROJ_ROW_TILE
  return pl.pallas_call(
      functools.partial(_ple_kernel, final_norm=final_norm),
      grid=(m // tm,),
      in_specs=[
          pl.BlockSpec((tm, d), lambda i: (i, 0)),
          pl.BlockSpec((tm, p.shape[1]), lambda i: (i, 0)),
          pl.BlockSpec((1, d), lambda i: (0, 0)),
          pl.BlockSpec(w_gate.shape, lambda i: (0, 0)),
          pl.BlockSpec(w_proj.shape, lambda i: (0, 0)),
          pl.BlockSpec((1, d), lambda i: (0, 0)),
      ],
      out_specs=pl.BlockSpec((tm, d), lambda i: (i, 0)),
      out_shape=jax.ShapeDtypeStruct((m, d), F32),
      compiler_params=_params("parallel"),
      name="ple",
  )(x, p, g, w_gate, w_proj, g_final)


def kernel(x, p, norm_ffn1, ffn1_w_gate, ffn1_w_up, ffn1_w_down, norm_mix, w_in, b_f, w_o,
           norm_ffn2, ffn2_w_gate, ffn2_w_up, ffn2_w_down, norm_ple, w_ple_gate, w_ple_proj,
           rel_table, norm_final):
  batch, seq, d = x.shape
  depth = p.shape[0]
  m = batch * seq
  bf = lambda w: w.astype(BF16)
  row = lambda g: g.reshape(1, -1).astype(F32)

  band_bias = _band_bias(rel_table.astype(F32))
  xs = x.reshape(m, d).astype(F32)
  for i in range(depth):
    xs = _ffn(xs, row(norm_ffn1[i]), bf(ffn1_w_gate[i]), bf(ffn1_w_up[i]), bf(0.5 * ffn1_w_down[i]))

    w = w_in[i]
    f0, f1 = 3 * D_FOX, 3 * D_FOX + N_HEADS_FOX
    w_f = jnp.pad(w[:, f0:f1], ((0, 0), (0, V7X_LANES - N_HEADS_FOX)))
    w_cat = bf(jnp.concatenate([w[:, :f0], w[:, f1:], w_f], axis=1))
    b_f_row = jnp.pad(b_f[i].astype(F32), (0, V7X_LANES - N_HEADS_FOX)).reshape(1, V7X_LANES)
    g_mix = row(norm_mix[i])
    u_a, f_logit = _norm_matmul(xs, g_mix, w_cat, f0, 0, BF16,
                                narrow_col=(f0 + 3 * D_DIL) // V7X_LANES)
    assert f0 % (3 * D_DIL) == 0
    u_b = _norm_matmul(xs, g_mix, w_cat, 3 * D_DIL, f0 // (3 * D_DIL), F32,
                       residue_major=(batch, seq))

    c = _fox_decay(f_logit, b_f_row, batch, seq)
    o_a = _fox_attention(u_a, c, batch, seq)
    o_b = _dilated_attention(u_b, band_bias, batch, seq)
    xs = _out_proj(xs, o_a, o_b, bf(w_o[i][:D_FOX]), bf(w_o[i][D_FOX:]), batch, seq)

    xs = _ffn(xs, row(norm_ffn2[i]), bf(ffn2_w_gate[i]), bf(ffn2_w_up[i]), bf(0.5 * ffn2_w_down[i]))
    last = i == depth - 1
    xs = _ple(xs, p[i].reshape(m, -1), row(norm_ple[i]), bf(w_ple_gate[i]), bf(w_ple_proj[i]),
              row(norm_final), final_norm=last)
  return xs.reshape(batch, seq, d).astype(x.dtype)
```

```python
import functools
import math
from typing import Any, Callable, NamedTuple

import jax
import jax.numpy as jnp
import numpy as np
from jax import lax
from jax.experimental import pallas as pl
from jax.experimental.pallas import tpu as pltpu

F32 = jnp.float32
BF16 = jnp.bfloat16

HEAD_DIM = 128
N_HEADS_FOX = 8
N_HEADS_DIL = 8
D_FOX = N_HEADS_FOX * HEAD_DIM
D_DIL = N_HEADS_DIL * HEAD_DIM
DILATED_PATTERNS = ((128, 1), (512, 4), (2048, 16))
WINDOW_KEYS = 128
N_REL_BUCKETS = 32
REL_MAX_DISTANCE = 2048
RMS_EPS = 1e-6
NEG_INF = -1e30
SCALE = HEAD_DIM ** -0.5
LOG2_E = math.log2(math.e)

V7X_LANES = 128
V7X_VMEM_LIMIT_BYTES = 56 * 1024 * 1024

RESIDUES = max(d for _, d in DILATED_PATTERNS)
PERM_ROWS = RESIDUES * RESIDUES

FFN_ROW_TILE = 1024
FFN_SUB_ROWS = 512
FFN_FF_TILE = 512
PROJ_ROW_TILE = 512
FOX_Q_TILE = 256
DIL_BLOCK_GROUP = 8


def _params(*semantics):
  return pltpu.CompilerParams(dimension_semantics=semantics,
                              vmem_limit_bytes=V7X_VMEM_LIMIT_BYTES)


def _rmsnorm(x, g):
  ms = jnp.mean(x * x, axis=-1, keepdims=True)
  return x * lax.rsqrt(ms + RMS_EPS) * g


def _dot(a, b):
  return jnp.dot(a, b, preferred_element_type=F32)


def _dot_nt(a, b):
  return lax.dot_general(a, b, (((1,), (1,)), ((), ())), preferred_element_type=F32)


class _SideJob(NamedTuple):
  arrays: tuple
  in_specs: tuple
  out_shape: Any
  out_spec: Any
  fn: Callable


def _cast_job(a, block, index, scale=1.0):
  spec = pl.BlockSpec(block, index)
  return _SideJob((a,), (spec,), jax.ShapeDtypeStruct(a.shape, BF16), spec,
                  lambda r: r[...] * scale)


def _run_side_jobs(jobs, in_refs, out_refs):
  in_refs = list(in_refs)
  for job, out_ref in zip(jobs, out_refs):
    refs = [in_refs.pop(0) for _ in job.arrays]
    out_ref[...] = job.fn(*refs).astype(out_ref.dtype)


def _side_args(jobs):
  arrays = [a for job in jobs for a in job.arrays]
  in_specs = [s for job in jobs for s in job.in_specs]
  return arrays, in_specs, [job.out_spec for job in jobs], [job.out_shape for job in jobs]


def _ffn_kernel(x_ref, g_ref, wg_ref, wu_ref, wd_ref, *rest, jobs):
  n_in = sum(len(job.arrays) for job in jobs)
  side_in, o_ref, side_out, h_ref = rest[:n_in], rest[n_in], rest[n_in + 1:-1], rest[-1]

  @pl.when(pl.program_id(1) == 0)
  def _():
    x = x_ref[...]
    h_ref[...] = _rmsnorm(x, g_ref[...]).astype(BF16)
    o_ref[...] = x

  _run_side_jobs(jobs, side_in, side_out)
  for r in range(h_ref.shape[0] // FFN_SUB_ROWS):
    rows = slice(r * FFN_SUB_ROWS, (r + 1) * FFN_SUB_ROWS)
    h = h_ref[rows, :]
    gate = _dot(h, wg_ref[...])
    up = _dot(h, wu_ref[...])
    act = (gate * jax.nn.sigmoid(gate)) * up
    o_ref[rows, :] += _dot(act.astype(BF16), wd_ref[...])


def _ffn(x, g, wg, wu, wd_half, jobs=()):
  m, d = x.shape
  dff = wg.shape[1]
  tm, tf = FFN_ROW_TILE, FFN_FF_TILE
  side_arrays, side_in_specs, side_out_specs, side_out_shapes = _side_args(jobs)
  return pl.pallas_call(
      functools.partial(_ffn_kernel, jobs=tuple(jobs)),
      grid=(m // tm, dff // tf),
      in_specs=[
          pl.BlockSpec((tm, d), lambda i, j: (i, 0)),
          pl.BlockSpec((1, d), lambda i, j: (0, 0)),
          pl.BlockSpec((d, tf), lambda i, j: (0, j)),
          pl.BlockSpec((d, tf), lambda i, j: (0, j)),
          pl.BlockSpec((tf, d), lambda i, j: (j, 0)),
      ] + side_in_specs,
      out_specs=[pl.BlockSpec((tm, d), lambda i, j: (i, 0))] + side_out_specs,
      out_shape=[jax.ShapeDtypeStruct((m, d), F32)] + side_out_shapes,
      scratch_shapes=[pltpu.VMEM((tm, d), BF16)],
      compiler_params=_params("parallel", "arbitrary"),
      name="ffn",
  )(x, g, wg, wu, wd_half, *side_arrays)


def _ffn_cast_jobs(m, w_gate, w_up, w_down):
  ni = m // FFN_ROW_TILE
  d, dff = w_gate.shape
  tf = FFN_FF_TILE
  assert d % ni == 0
  return [
      _cast_job(w_gate, (d // ni, tf), lambda i, j: (i, j)),
      _cast_job(w_up, (d // ni, tf), lambda i, j: (i, j)),
      _cast_job(w_down, (tf, d // ni), lambda i, j: (j, i), scale=0.5),
  ]


def _row_cast_jobs(m, *weights):
  ni = m // FFN_ROW_TILE
  return [_cast_job(w, (w.shape[0] // ni, w.shape[1]), lambda i, j: (i, 0)) for w in weights]


def _w_in_repack_jobs(m, w, n_a, n_f, n_b):
  ni = m // FFN_ROW_TILE
  d = w.shape[0]
  rows, tc = d // ni, FFN_FF_TILE
  shift = n_f % V7X_LANES
  assert n_a % tc == 0 and n_b % tc == 0 and n_f < V7X_LANES and rows % 16 == 0
  chunks = n_a // tc
  assert n_b // tc == chunks
  cj = lambda j: jnp.minimum(j, chunks - 1)
  lanes_per_chunk = tc // V7X_LANES

  def shifted(main_ref, tail_ref):
    both = jnp.concatenate([main_ref[...], tail_ref[...]], axis=1)
    return both[:, shift:shift + tc]

  def gate_columns(ref):
    lane = lax.broadcasted_iota(jnp.int32, ref.shape, 1)
    return jnp.where(lane < n_f, ref[...], 0.0)

  out = lambda n: jax.ShapeDtypeStruct((d, n), BF16)
  a_spec = pl.BlockSpec((rows, tc), lambda i, j: (i, cj(j)))
  return [
      _SideJob((w,), (a_spec,), out(n_a), a_spec, lambda r: r[...]),
      _SideJob((w, w),
               (pl.BlockSpec((rows, tc), lambda i, j: (i, chunks + cj(j))),
                pl.BlockSpec((rows, V7X_LANES),
                             lambda i, j: (i, (chunks + cj(j) + 1) * lanes_per_chunk))),
               out(n_b), a_spec, shifted),
      _SideJob((w,), (pl.BlockSpec((rows, V7X_LANES), lambda i, j: (i, n_a // V7X_LANES)),),
               out(V7X_LANES), pl.BlockSpec((rows, V7X_LANES), lambda i, j: (i, 0)),
               gate_columns),
  ]


def _swap_matrix():
  i = np.arange(PERM_ROWS)
  src = (i % RESIDUES) * RESIDUES + i // RESIDUES
  return jnp.asarray(np.eye(PERM_ROWS, dtype=np.float32)[src], BF16)


def _norm_matmul_kernel(x_ref, g_ref, w_ref, *rest, permute, narrow, jobs):
  rest = list(rest)
  swap_ref = rest.pop(0) if permute else None
  wn_ref = rest.pop(0) if narrow else None
  side_in = [rest.pop(0) for job in jobs for _ in job.arrays]
  o_ref = rest.pop(0)
  on_ref = rest.pop(0) if narrow else None
  side_out = [rest.pop(0) for _ in jobs]
  assert not rest, "unexpected extra refs"
  tm = x_ref.shape[0]
  _run_side_jobs(jobs, side_in, side_out)

  h = _rmsnorm(x_ref[...], g_ref[...]).astype(BF16)
  if permute:
    h = jnp.concatenate(
        [_dot(swap_ref[...], h[a * PERM_ROWS:(a + 1) * PERM_ROWS, :]).astype(BF16)
         for a in range(tm // PERM_ROWS)], axis=0)
  if narrow:
    on_ref[...] = _dot(h, wn_ref[...])
  res = _dot(h, w_ref[...]).astype(o_ref.dtype)
  if permute:
    per = PERM_ROWS // RESIDUES
    for a in range(tm // PERM_ROWS):
      for r in range(RESIDUES):
        start = a * PERM_ROWS + r * per
        o_ref[r, a * per:(a + 1) * per, :] = res[start:start + per, :]
  else:
    o_ref[...] = res


def _norm_matmul(x, g, w, out_dtype, residue_major=None, w_narrow=None, jobs=()):
  m, d = x.shape
  n = w.shape[1]
  tm = PROJ_ROW_TILE
  narrow = w_narrow is not None
  permute = residue_major is not None
  assert not (narrow and permute)
  in_specs = [
      pl.BlockSpec((tm, d), lambda i: (i, 0)),
      pl.BlockSpec((1, d), lambda i: (0, 0)),
      pl.BlockSpec((d, n), lambda i: (0, 0)),
  ]
  args = [x, g, w]
  if permute:
    batch, seq = residue_major
    tiles = seq // tm
    in_specs.append(pl.BlockSpec((PERM_ROWS, PERM_ROWS), lambda i: (0, 0)))
    args.append(_swap_matrix())
    out_specs = [pl.BlockSpec((RESIDUES, tm // RESIDUES, n), lambda i: (i // tiles, i % tiles, 0))]
    out_shape = [jax.ShapeDtypeStruct((batch * RESIDUES, seq // RESIDUES, n), out_dtype)]
  else:
    out_specs = [pl.BlockSpec((tm, n), lambda i: (i, 0))]
    out_shape = [jax.ShapeDtypeStruct((m, n), out_dtype)]
  if narrow:
    in_specs.append(pl.BlockSpec(w_narrow.shape, lambda i: (0, 0)))
    args.append(w_narrow)
    out_specs.append(pl.BlockSpec((tm, w_narrow.shape[1]), lambda i: (i, 0)))
    out_shape.append(jax.ShapeDtypeStruct((m, w_narrow.shape[1]), F32))
  side_arrays, side_in_specs, side_out_specs, side_out_shapes = _side_args(jobs)
  outs = pl.pallas_call(
      functools.partial(_norm_matmul_kernel, permute=permute, narrow=narrow, jobs=tuple(jobs)),
      grid=(m // tm,),
      in_specs=in_specs + side_in_specs,
      out_specs=out_specs + side_out_specs,
      out_shape=out_shape + side_out_shapes,
      compiler_params=_params("parallel"),
      name="norm_matmul",
  )(*args, *side_arrays)
  outs = list(outs)
  if permute:
    outs[0] = outs[0].reshape(m, n)
  return outs


def _cumsum_kernel(fl_ref, bf_ref, c_ref):
  s = fl_ref.shape[0]
  z = fl_ref[...] + bf_ref[...]
  logf = jnp.minimum(z, 0.0) - jnp.log1p(jnp.exp(-jnp.abs(z)))
  lt = logf.T[0:N_HEADS_FOX, :]
  row = lax.broadcasted_iota(jnp.int32, (V7X_LANES, V7X_LANES), 0)
  col = lax.broadcasted_iota(jnp.int32, (V7X_LANES, V7X_LANES), 1)
  upper = (row <= col).astype(F32)
  carry = jnp.zeros((N_HEADS_FOX, 1), F32)
  for j in range(s // V7X_LANES):
    blk = lt[:, j * V7X_LANES:(j + 1) * V7X_LANES]
    cs = jnp.dot(blk, upper, preferred_element_type=F32,
                 precision=lax.Precision.HIGHEST) + carry
    c_ref[:, j * V7X_LANES:(j + 1) * V7X_LANES] = cs
    carry = cs[:, V7X_LANES - 1:V7X_LANES]


def _fox_decay(f_logit, b_f_row, batch, seq):
  return pl.pallas_call(
      _cumsum_kernel,
      grid=(batch,),
      in_specs=[
          pl.BlockSpec((seq, V7X_LANES), lambda b: (b, 0)),
          pl.BlockSpec((1, V7X_LANES), lambda b: (0, 0)),
      ],
      out_specs=pl.BlockSpec((None, N_HEADS_FOX, seq), lambda b: (b, 0, 0)),
      out_shape=jax.ShapeDtypeStruct((batch, N_HEADS_FOX, seq), F32),
      compiler_params=_params("parallel"),
      name="fox_decay",
  )(f_logit, b_f_row)


def _fox_kernel(q_ref, k_ref, v_ref, c_ref, o_ref):
  seq = q_ref.shape[0]
  tq = FOX_Q_TILE
  h = pl.program_id(1)
  crow = c_ref[pl.ds(h, 1), :] * LOG2_E
  row = lax.broadcasted_iota(jnp.int32, (tq, tq), 0)
  col = lax.broadcasted_iota(jnp.int32, (tq, tq), 1)
  diag_mask = jnp.where(col > row, NEG_INF, 0.0).astype(F32)

  def scores(i):
    t0, t1 = i * tq, (i + 1) * tq
    q = q_ref[t0:t1, :]
    bias = crow[:, t1 - 1:t1] - crow[:, 0:t1]
    s_diag = _dot_nt(q, k_ref[t0:t1, :]) * (SCALE * LOG2_E) + bias[:, t0:t1] + diag_mask
    s_off = _dot_nt(q, k_ref[0:t0, :]) * (SCALE * LOG2_E) + bias[:, 0:t0] if i > 0 else None
    return s_diag, s_off

  def finish(i, s_diag, s_off):
    t0, t1 = i * tq, (i + 1) * tq
    m = jnp.max(s_diag, axis=-1, keepdims=True)
    if i > 0:
      m = jnp.maximum(m, jnp.max(s_off, axis=-1, keepdims=True))
    e_diag = jnp.exp2(s_diag - m)
    l = jnp.sum(e_diag, axis=-1, keepdims=True)
    o = _dot(e_diag.astype(BF16), v_ref[t0:t1, :])
    if i > 0:
      e_off = jnp.exp2(s_off - m)
      l = l + jnp.sum(e_off, axis=-1, keepdims=True)
      o = o + _dot(e_off.astype(BF16), v_ref[0:t0, :])
    o_ref[t0:t1, :] = (o / l).astype(o_ref.dtype)

  n_tiles = seq // tq
  pending = scores(0)
  for i in range(n_tiles):
    upcoming = scores(i + 1) if i + 1 < n_tiles else None
    finish(i, *pending)
    pending = upcoming


def _fox_attention(u_a, c, batch, seq):
  blk = lambda off: pl.BlockSpec((seq, HEAD_DIM), lambda b, h: (b, off + h))
  return pl.pallas_call(
      _fox_kernel,
      grid=(batch, N_HEADS_FOX),
      in_specs=[
          blk(0), blk(N_HEADS_FOX), blk(2 * N_HEADS_FOX),
          pl.BlockSpec((None, N_HEADS_FOX, seq), lambda b, h: (b, 0, 0)),
      ],
      out_specs=pl.BlockSpec((seq, HEAD_DIM), lambda b, h: (b, h)),
      out_shape=jax.ShapeDtypeStruct((batch * seq, D_FOX), BF16),
      compiler_params=_params("parallel", "arbitrary"),
      name="fox_attention",
  )(u_a, u_a, u_a, c)


def _t5_bucket_np(dist):
  max_exact = N_REL_BUCKETS // 2
  d = np.maximum(dist, 1).astype(np.float32)
  large = max_exact + (np.log(d / np.float32(max_exact))
                       / np.float32(math.log(REL_MAX_DISTANCE / max_exact))
                       * np.float32(N_REL_BUCKETS - max_exact)).astype(np.int32)
  large = np.minimum(large, N_REL_BUCKETS - 1)
  return np.where(dist < max_exact, dist, large).astype(np.int32)


def _block_positions(dilation):
  n = WINDOW_KEYS
  m = RESIDUES // dilation
  rows = n // m
  j = np.arange(m)[:, None]
  qpos = (n + m * np.arange(rows)[None, :] + j).reshape(-1)
  kpos = (m * np.arange(2 * rows)[None, :] + j).reshape(-1)
  return qpos, kpos


def _band_buckets():
  n = WINDOW_KEYS
  tiles = []
  for _, dilation in DILATED_PATTERNS:
    qpos, kpos = _block_positions(dilation)
    rel = qpos[:, None] - np.concatenate([kpos, qpos])[None, :]
    valid = (rel >= 0) & (rel <= n)
    bucket = _t5_bucket_np(np.maximum(rel, 0) * dilation)
    tiles.append(np.where(valid, bucket, -1))
  return np.stack(tiles).astype(np.int32)


def _bias_kernel(tab_ref, bkt_ref, o_ref):
  bkt = bkt_ref[...]
  for h in range(N_HEADS_DIL):
    acc = jnp.full(bkt.shape, NEG_INF, F32)
    for b in range(N_REL_BUCKETS):
      acc = jnp.where(bkt == b, tab_ref[b, h] * LOG2_E, acc)
    o_ref[h] = acc


def _band_bias(rel_table):
  buckets = jnp.asarray(_band_buckets())
  p, n, n2 = buckets.shape
  return pl.pallas_call(
      _bias_kernel,
      grid=(p,),
      in_specs=[
          pl.BlockSpec(memory_space=pltpu.SMEM),
          pl.BlockSpec((None, n, n2), lambda i: (i, 0, 0)),
      ],
      out_specs=pl.BlockSpec((None, N_HEADS_DIL, n, n2), lambda i: (i, 0, 0, 0)),
      out_shape=jax.ShapeDtypeStruct((p, N_HEADS_DIL, n, n2), F32),
      compiler_params=_params("parallel"),
      name="band_bias",
  )(rel_table, buckets)


def _dil_kernel(q_ref, k_ref, v_ref, bm_ref, o_ref,
                acc0, acc1, acc2, lse0, lse1, lse2):
  seq = q_ref.shape[0]
  n = WINDOW_KEYS
  seg = seq // RESIDUES
  accs = (acc0, acc1, acc2)
  lses = (lse0, lse1, lse2)

  def gather(ref, starts, size):
    return jnp.concatenate([ref[st:st + size, :] for st in starts], axis=0).astype(BF16)

  def block_rows(segments, nb):
    rows = n // len(segments)
    q_starts = [s * seg + nb * rows for s in segments]
    if nb == 0:
      return rows, q_starts, q_starts, rows
    return rows, q_starts, [st - rows for st in q_starts], 2 * rows

  def scores(p, segments, nb):
    rows, q_starts, k_starts, k_rows = block_rows(segments, nb)
    bm = bm_ref[p, :, 2 * n:3 * n] if nb == 0 else bm_ref[p, :, 0:2 * n]
    qk = _dot_nt(gather(q_ref, q_starts, rows), gather(k_ref, k_starts, k_rows))
    return qk * (SCALE * LOG2_E) + bm

  def finish(p, segments, nb, e, l, lse):
    rows, q_starts, k_starts, k_rows = block_rows(segments, nb)
    o = _dot(e.astype(BF16), gather(v_ref, k_starts, k_rows)) / l
    lse = jnp.broadcast_to(lse, (n, HEAD_DIM))
    for j, st in enumerate(q_starts):
      accs[p][st:st + rows, :] = o[j * rows:(j + 1) * rows, :]
      lses[p][st:st + rows, :] = lse[j * rows:(j + 1) * rows, :]

  blocks = [(p, list(range(r, RESIDUES, d)), nb)
            for p, (_, d) in enumerate(DILATED_PATTERNS)
            for r in range(d) for nb in range(seq // (n * d))]
  for g in range(0, len(blocks), DIL_BLOCK_GROUP):
    group = blocks[g:g + DIL_BLOCK_GROUP]
    ss = [scores(*blk) for blk in group]
    ms = [jnp.max(s, axis=-1, keepdims=True) for s in ss]
    es = [jnp.exp2(s - m) for s, m in zip(ss, ms)]
    ls = [jnp.sum(e, axis=-1, keepdims=True) for e in es]
    for blk, e, l, m in zip(group, es, ls, ms):
      finish(*blk, e, l, m + jnp.log2(l))

  def merge(c, carry):
    rows = pl.ds(pl.multiple_of(c * n, n), n)
    l0, l1, l2 = lse0[rows, :], lse1[rows, :], lse2[rows, :]
    mx = jnp.maximum(jnp.maximum(l0, l1), l2)
    w0, w1, w2 = jnp.exp2(l0 - mx), jnp.exp2(l1 - mx), jnp.exp2(l2 - mx)
    mixed = (w0 * acc0[rows, :] + w1 * acc1[rows, :] + w2 * acc2[rows, :]) / (w0 + w1 + w2)
    o_ref[rows, :] = mixed.astype(o_ref.dtype)
    return carry

  lax.fori_loop(0, seq // n, merge, 0)


def _dilated_attention(u_b, band_bias, batch, seq):
  assert seq // RESIDUES == WINDOW_KEYS
  blk = lambda off: pl.BlockSpec((seq, HEAD_DIM), lambda b, h: (b, off + h))
  n_pat = len(DILATED_PATTERNS)
  scratch = [pltpu.VMEM((seq, HEAD_DIM), F32) for _ in range(2 * n_pat)]
  return pl.pallas_call(
      _dil_kernel,
      grid=(batch, N_HEADS_DIL),
      in_specs=[
          blk(0), blk(N_HEADS_DIL), blk(2 * N_HEADS_DIL),
          pl.BlockSpec((n_pat, None, WINDOW_KEYS, 3 * WINDOW_KEYS), lambda b, h: (0, h, 0, 0)),
      ],
      out_specs=pl.BlockSpec((seq, HEAD_DIM), lambda b, h: (b, h)),
      out_shape=jax.ShapeDtypeStruct((batch * seq, D_DIL), BF16),
      scratch_shapes=scratch,
      compiler_params=_params("parallel", "arbitrary"),
      name="dilated_attention",
  )(u_b, u_b, u_b, band_bias)


def _out_proj_kernel(x_ref, a_ref, b_ref, swap_ref, wa_ref, wb_ref, o_ref):
  per = PERM_ROWS // RESIDUES
  for a in range(x_ref.shape[0] // PERM_ROWS):
    rows = slice(a * PERM_ROWS, (a + 1) * PERM_ROWS)
    slab = jnp.concatenate([b_ref[r, a * per:(a + 1) * per, :] for r in range(RESIDUES)], axis=0)
    o_b = _dot(swap_ref[...], slab).astype(BF16)
    o_ref[rows, :] = x_ref[rows, :] + _dot(a_ref[rows, :], wa_ref[...]) + _dot(o_b, wb_ref[...])


def _out_proj(x, o_a, o_b, wo, batch, seq):
  m, d = x.shape
  tm = PROJ_ROW_TILE
  tiles = seq // tm
  ca, cb = o_a.shape[1], o_b.shape[1]
  assert ca == cb and wo.shape[0] == ca + cb
  resident = pl.Buffered(1)
  return pl.pallas_call(
      _out_proj_kernel,
      grid=(m // tm,),
      in_specs=[
          pl.BlockSpec((tm, d), lambda i: (i, 0)),
          pl.BlockSpec((tm, ca), lambda i: (i, 0)),
          pl.BlockSpec((RESIDUES, tm // RESIDUES, cb), lambda i: (i // tiles, i % tiles, 0)),
          pl.BlockSpec((PERM_ROWS, PERM_ROWS), lambda i: (0, 0)),
          pl.BlockSpec((ca, d), lambda i: (0, 0), pipeline_mode=resident),
          pl.BlockSpec((cb, d), lambda i: (1, 0), pipeline_mode=resident),
      ],
      out_specs=pl.BlockSpec((tm, d), lambda i: (i, 0)),
      out_shape=jax.ShapeDtypeStruct((m, d), F32),
      compiler_params=_params("parallel"),
      name="out_proj",
  )(x, o_a, o_b.reshape(batch * RESIDUES, seq // RESIDUES, cb), _swap_matrix(), wo, wo)


def _ple_kernel(x_ref, p_ref, g_ref, wg_ref, wp_ref, gf_ref, o_ref, *, final_norm):
  x = x_ref[...]
  h = _rmsnorm(x, g_ref[...]).astype(BF16)
  gate = jax.nn.sigmoid(_dot(h, wg_ref[...]))
  y = x + gate * _dot(p_ref[...].astype(BF16), wp_ref[...])
  o_ref[...] = _rmsnorm(y, gf_ref[...]) if final_norm else y


def _ple(x, p, g, w_gate, w_proj, g_final, final_norm):
  m, d = x.shape
  tm = PROJ_ROW_TILE
  return pl.pallas_call(
      functools.partial(_ple_kernel, final_norm=final_norm),
      grid=(m // tm,),
      in_specs=[
          pl.BlockSpec((tm, d), lambda i: (i, 0)),
          pl.BlockSpec((tm, p.shape[1]), lambda i: (i, 0)),
          pl.BlockSpec((1, d), lambda i: (0, 0)),
          pl.BlockSpec(w_gate.shape, lambda i: (0, 0)),
          pl.BlockSpec(w_proj.shape, lambda i: (0, 0)),
          pl.BlockSpec((1, d), lambda i: (0, 0)),
      ],
      out_specs=pl.BlockSpec((tm, d), lambda i: (i, 0)),
      out_shape=jax.ShapeDtypeStruct((m, d), F32),
      compiler_params=_params("parallel"),
      name="ple",
  )(x, p, g, w_gate, w_proj, g_final)


def kernel(x, p, norm_ffn1, ffn1_w_gate, ffn1_w_up, ffn1_w_down, norm_mix, w_in, b_f, w_o,
           norm_ffn2, ffn2_w_gate, ffn2_w_up, ffn2_w_down, norm_ple, w_ple_gate, w_ple_proj,
           rel_table, norm_final):
  batch, seq, d = x.shape
  depth = p.shape[0]
  m = batch * seq
  bf = lambda w: w.astype(BF16)
  row = lambda g: g.reshape(1, -1).astype(F32)

  band_bias = _band_bias(rel_table.astype(F32))
  xs = x.reshape(m, d).astype(F32)
  for i in range(depth):
    xs, wg2, wu2, wd2, w_a, w_b, w_f = _ffn(
        xs, row(norm_ffn1[i]), bf(ffn1_w_gate[i]), bf(ffn1_w_up[i]), bf(0.5 * ffn1_w_down[i]),
        jobs=_ffn_cast_jobs(m, ffn2_w_gate[i], ffn2_w_up[i], ffn2_w_down[i])
        + _w_in_repack_jobs(m, w_in[i], 3 * D_FOX, N_HEADS_FOX, 3 * D_DIL))

    b_f_row = jnp.pad(b_f[i].astype(F32), (0, V7X_LANES - N_HEADS_FOX)).reshape(1, V7X_LANES)
    g_mix = row(norm_mix[i])
    wo_rows = w_o[i].shape[0] // (m // PROJ_ROW_TILE)
    u_a, f_logit, wo = _norm_matmul(
        xs, g_mix, w_a, BF16, w_narrow=w_f,
        jobs=[_cast_job(w_o[i], (wo_rows, w_o[i].shape[1]), lambda t: (t, 0))])
    u_b, = _norm_matmul(xs, g_mix, w_b, F32, residue_major=(batch, seq))

    c = _fox_decay(f_logit, b_f_row, batch, seq)
    o_a = _fox_attention(u_a, c, batch, seq)
    o_b = _dilated_attention(u_b, band_bias, batch, seq)
    xs = _out_proj(xs, o_a, o_b, wo, batch, seq)

    xs, w_gate, w_ple = _ffn(xs, row(norm_ffn2[i]), wg2, wu2, wd2,
                             jobs=_row_cast_jobs(m, w_ple_gate[i], w_ple_proj[i]))
    last = i == depth - 1
    xs = _ple(xs, p[i].reshape(m, -1), row(norm_ple[i]), w_gate, w_ple,
              row(norm_final), final_norm=last)
  return xs.reshape(batch, seq, d).astype(x.dtype)
```

```python
import functools
import math
from typing import Any, Callable, NamedTuple

import jax
import jax.numpy as jnp
import numpy as np
from jax import lax
from jax.experimental import pallas as pl
from jax.experimental.pallas import tpu as pltpu

F32 = jnp.float32
BF16 = jnp.bfloat16

HEAD_DIM = 128
N_HEADS_FOX = 8
N_HEADS_DIL = 8
D_FOX = N_HEADS_FOX * HEAD_DIM
D_DIL = N_HEADS_DIL * HEAD_DIM
DILATED_PATTERNS = ((128, 1), (512, 4), (2048, 16))
WINDOW_KEYS = 128
N_REL_BUCKETS = 32
REL_MAX_DISTANCE = 2048
RMS_EPS = 1e-6
NEG_INF = -1e30
SCALE = HEAD_DIM ** -0.5
LOG2_E = math.log2(math.e)

V7X_LANES = 128
V7X_VMEM_LIMIT_BYTES = 56 * 1024 * 1024

RESIDUES = max(d for _, d in DILATED_PATTERNS)
PERM_ROWS = RESIDUES * RESIDUES

FFN_ROW_TILE = 1024
FFN_SUB_ROWS = 512
FFN_FF_TILE = 512
PROJ_ROW_TILE = 512
FOX_Q_TILE = 256
DIL_BLOCK_GROUP = 8


def _params(*semantics):
  return pltpu.CompilerParams(dimension_semantics=semantics,
                              vmem_limit_bytes=V7X_VMEM_LIMIT_BYTES)


def _rmsnorm(x, g):
  ms = jnp.mean(x * x, axis=-1, keepdims=True)
  return x * lax.rsqrt(ms + RMS_EPS) * g


def _dot(a, b):
  return jnp.dot(a, b, preferred_element_type=F32)


def _dot_nt(a, b):
  return lax.dot_general(a, b, (((1,), (1,)), ((), ())), preferred_element_type=F32)


class _SideJob(NamedTuple):
  arrays: tuple
  in_specs: tuple
  out_shape: Any
  out_spec: Any
  fn: Callable


def _cast_job(a, block, index, scale=1.0):
  spec = pl.BlockSpec(block, index)
  return _SideJob((a,), (spec,), jax.ShapeDtypeStruct(a.shape, BF16), spec,
                  lambda r: r[...] * scale)


def _run_side_jobs(jobs, in_refs, out_refs):
  in_refs = list(in_refs)
  for job, out_ref in zip(jobs, out_refs):
    refs = [in_refs.pop(0) for _ in job.arrays]
    out_ref[...] = job.fn(*refs).astype(out_ref.dtype)


def _side_args(jobs):
  arrays = [a for job in jobs for a in job.arrays]
  in_specs = [s for job in jobs for s in job.in_specs]
  return arrays, in_specs, [job.out_spec for job in jobs], [job.out_shape for job in jobs]


def _ffn_kernel(x_ref, g_ref, wg_ref, wu_ref, wd_ref, *rest, jobs):
  n_in = sum(len(job.arrays) for job in jobs)
  side_in, o_ref, side_out, h_ref = rest[:n_in], rest[n_in], rest[n_in + 1:-1], rest[-1]

  def step(first):
    _run_side_jobs(jobs, side_in, side_out)
    for r in range(h_ref.shape[0] // FFN_SUB_ROWS):
      rows = slice(r * FFN_SUB_ROWS, (r + 1) * FFN_SUB_ROWS)
      if first:
        base = x_ref[rows, :]
        h = _rmsnorm(base, g_ref[...]).astype(BF16)
        h_ref[rows, :] = h
      else:
        base = o_ref[rows, :]
        h = h_ref[rows, :]
      gate = _dot(h, wg_ref[...])
      up = _dot(h, wu_ref[...])
      act = (gate * jax.nn.sigmoid(gate)) * up
      o_ref[rows, :] = base + _dot(act.astype(BF16), wd_ref[...])

  lax.cond(pl.program_id(1) == 0, lambda: step(True), lambda: step(False))


def _ffn(x, g, wg, wu, wd_half, jobs=()):
  m, d = x.shape
  dff = wg.shape[1]
  tm, tf = FFN_ROW_TILE, FFN_FF_TILE
  side_arrays, side_in_specs, side_out_specs, side_out_shapes = _side_args(jobs)
  return pl.pallas_call(
      functools.partial(_ffn_kernel, jobs=tuple(jobs)),
      grid=(m // tm, dff // tf),
      in_specs=[
          pl.BlockSpec((tm, d), lambda i, j: (i, 0)),
          pl.BlockSpec((1, d), lambda i, j: (0, 0)),
          pl.BlockSpec((d, tf), lambda i, j: (0, j)),
          pl.BlockSpec((d, tf), lambda i, j: (0, j)),
          pl.BlockSpec((tf, d), lambda i, j: (j, 0)),
      ] + side_in_specs,
      out_specs=[pl.BlockSpec((tm, d), lambda i, j: (i, 0))] + side_out_specs,
      out_shape=[jax.ShapeDtypeStruct((m, d), F32)] + side_out_shapes,
      scratch_shapes=[pltpu.VMEM((tm, d), BF16)],
      compiler_params=_params("parallel", "arbitrary"),
      name="ffn",
  )(x, g, wg, wu, wd_half, *side_arrays)


def _ffn_cast_jobs(m, w_gate, w_up, w_down):
  ni = m // FFN_ROW_TILE
  d, dff = w_gate.shape
  tf = FFN_FF_TILE
  assert d % ni == 0
  return [
      _cast_job(w_gate, (d // ni, tf), lambda i, j: (i, j)),
      _cast_job(w_up, (d // ni, tf), lambda i, j: (i, j)),
      _cast_job(w_down, (tf, d // ni), lambda i, j: (j, i), scale=0.5),
  ]


def _row_cast_jobs(m, *weights):
  ni = m // FFN_ROW_TILE
  return [_cast_job(w, (w.shape[0] // ni, w.shape[1]), lambda i, j: (i, 0)) for w in weights]


def _w_in_repack_jobs(m, w_t, n_a, n_f, n_b):
  ni = m // FFN_ROW_TILE
  d = w_t.shape[1]
  cols, tc = d // ni, FFN_FF_TILE
  assert n_a % tc == 0 and n_b % tc == 0 and n_f % 8 == 0 and n_f <= V7X_LANES
  assert cols % V7X_LANES == 0
  chunks = n_a // tc
  assert n_b // tc == chunks
  cj = lambda j: jnp.minimum(j, chunks - 1)

  def transposed(ref):
    return ref[...].T

  def shifted_transposed(main_ref, tail_ref):
    return jnp.concatenate([main_ref[n_f:, :], tail_ref[...]], axis=0).T

  def gate_rows_transposed(ref):
    row = lax.broadcasted_iota(jnp.int32, ref.shape, 0)
    return jnp.where(row < n_f, ref[...], 0.0).T

  out = lambda n: jax.ShapeDtypeStruct((d, n), BF16)
  out_spec = pl.BlockSpec((cols, tc), lambda i, j: (i, cj(j)))
  return [
      _SideJob((w_t,), (pl.BlockSpec((tc, cols), lambda i, j: (cj(j), i)),),
               out(n_a), out_spec, transposed),
      _SideJob((w_t, w_t),
               (pl.BlockSpec((tc, cols), lambda i, j: (chunks + cj(j), i)),
                pl.BlockSpec((n_f, cols), lambda i, j: ((n_a + (cj(j) + 1) * tc) // n_f, i))),
               out(n_b), out_spec, shifted_transposed),
      _SideJob((w_t,), (pl.BlockSpec((V7X_LANES, cols), lambda i, j: (n_a // V7X_LANES, i)),),
               out(V7X_LANES), pl.BlockSpec((cols, V7X_LANES), lambda i, j: (i, 0)),
               gate_rows_transposed),
  ]


def _swap_matrix():
  i = np.arange(PERM_ROWS)
  src = (i % RESIDUES) * RESIDUES + i // RESIDUES
  return jnp.asarray(np.eye(PERM_ROWS, dtype=np.float32)[src], BF16)


def _norm_matmul_kernel(x_ref, g_ref, w_ref, *rest, permute, narrow, jobs, key_cols):
  rest = list(rest)
  swap_ref = rest.pop(0) if permute else None
  wn_ref = rest.pop(0) if narrow else None
  side_in = [rest.pop(0) for job in jobs for _ in job.arrays]
  o_ref = rest.pop(0)
  on_ref = rest.pop(0) if narrow else None
  side_out = [rest.pop(0) for _ in jobs]
  assert not rest, "unexpected extra refs"
  tm = x_ref.shape[0]
  _run_side_jobs(jobs, side_in, side_out)

  h = _rmsnorm(x_ref[...], g_ref[...]).astype(BF16)
  if permute:
    h = jnp.concatenate(
        [_dot(swap_ref[...], h[a * PERM_ROWS:(a + 1) * PERM_ROWS, :]).astype(BF16)
         for a in range(tm // PERM_ROWS)], axis=0)
  if narrow:
    on_ref[...] = _dot(h, wn_ref[...])
  acc = _dot(h, w_ref[...])
  lo, hi = key_cols
  res = jnp.concatenate([acc[:, :lo], acc[:, lo:hi] * (SCALE * LOG2_E), acc[:, hi:]],
                        axis=1).astype(o_ref.dtype)
  if permute:
    per = PERM_ROWS // RESIDUES
    for a in range(tm // PERM_ROWS):
      for r in range(RESIDUES):
        start = a * PERM_ROWS + r * per
        o_ref[r, a * per:(a + 1) * per, :] = res[start:start + per, :]
  else:
    o_ref[...] = res


def _norm_matmul(x, g, w, out_dtype, key_cols, residue_major=None, w_narrow=None, jobs=()):
  m, d = x.shape
  n = w.shape[1]
  tm = PROJ_ROW_TILE
  narrow = w_narrow is not None
  permute = residue_major is not None
  assert not (narrow and permute)
  in_specs = [
      pl.BlockSpec((tm, d), lambda i: (i, 0)),
      pl.BlockSpec((1, d), lambda i: (0, 0)),
      pl.BlockSpec((d, n), lambda i: (0, 0)),
  ]
  args = [x, g, w]
  if permute:
    batch, seq = residue_major
    tiles = seq // tm
    in_specs.append(pl.BlockSpec((PERM_ROWS, PERM_ROWS), lambda i: (0, 0)))
    args.append(_swap_matrix())
    out_specs = [pl.BlockSpec((RESIDUES, tm // RESIDUES, n), lambda i: (i // tiles, i % tiles, 0))]
    out_shape = [jax.ShapeDtypeStruct((batch * RESIDUES, seq // RESIDUES, n), out_dtype)]
  else:
    out_specs = [pl.BlockSpec((tm, n), lambda i: (i, 0))]
    out_shape = [jax.ShapeDtypeStruct((m, n), out_dtype)]
  if narrow:
    in_specs.append(pl.BlockSpec(w_narrow.shape, lambda i: (0, 0)))
    args.append(w_narrow)
    out_specs.append(pl.BlockSpec((tm, w_narrow.shape[1]), lambda i: (i, 0)))
    out_shape.append(jax.ShapeDtypeStruct((m, w_narrow.shape[1]), F32))
  side_arrays, side_in_specs, side_out_specs, side_out_shapes = _side_args(jobs)
  outs = pl.pallas_call(
      functools.partial(_norm_matmul_kernel, permute=permute, narrow=narrow, jobs=tuple(jobs),
                        key_cols=key_cols),
      grid=(m // tm,),
      in_specs=in_specs + side_in_specs,
      out_specs=out_specs + side_out_specs,
      out_shape=out_shape + side_out_shapes,
      compiler_params=_params("parallel"),
      name="norm_matmul",
  )(*args, *side_arrays)
  outs = list(outs)
  if permute:
    outs[0] = outs[0].reshape(m, n)
  return outs


def _cumsum_kernel(fl_ref, bf_ref, c_ref):
  s = fl_ref.shape[0]
  z = fl_ref[...] + bf_ref[...]
  logf = jnp.minimum(z, 0.0) - jnp.log1p(jnp.exp(-jnp.abs(z)))
  lt = logf.T[0:N_HEADS_FOX, :]
  row = lax.broadcasted_iota(jnp.int32, (V7X_LANES, V7X_LANES), 0)
  col = lax.broadcasted_iota(jnp.int32, (V7X_LANES, V7X_LANES), 1)
  upper = (row <= col).astype(F32)
  carry = jnp.zeros((N_HEADS_FOX, 1), F32)
  for j in range(s // V7X_LANES):
    blk = lt[:, j * V7X_LANES:(j + 1) * V7X_LANES]
    cs = jnp.dot(blk, upper, preferred_element_type=F32,
                 precision=lax.Precision.HIGHEST) + carry
    c_ref[:, j * V7X_LANES:(j + 1) * V7X_LANES] = cs
    carry = cs[:, V7X_LANES - 1:V7X_LANES]


def _fox_decay(f_logit, b_f_row, batch, seq):
  return pl.pallas_call(
      _cumsum_kernel,
      grid=(batch,),
      in_specs=[
          pl.BlockSpec((seq, V7X_LANES), lambda b: (b, 0)),
          pl.BlockSpec((1, V7X_LANES), lambda b: (0, 0)),
      ],
      out_specs=pl.BlockSpec((None, N_HEADS_FOX, seq), lambda b: (b, 0, 0)),
      out_shape=jax.ShapeDtypeStruct((batch, N_HEADS_FOX, seq), F32),
      compiler_params=_params("parallel"),
      name="fox_decay",
  )(f_logit, b_f_row)


def _fox_kernel(q_ref, k_ref, v_ref, c_ref, o_ref):
  seq = q_ref.shape[0]
  tq = FOX_Q_TILE
  h = pl.program_id(1)
  crow = c_ref[pl.ds(h, 1), :] * LOG2_E
  row = lax.broadcasted_iota(jnp.int32, (tq, tq), 0)
  col = lax.broadcasted_iota(jnp.int32, (tq, tq), 1)
  diag_mask = jnp.where(col > row, NEG_INF, 0.0).astype(F32)

  def scores(i):
    t0, t1 = i * tq, (i + 1) * tq
    q = q_ref[t0:t1, :]
    bias = crow[:, t1 - 1:t1] - crow[:, 0:t1]
    s_diag = _dot_nt(q, k_ref[t0:t1, :]) + bias[:, t0:t1] + diag_mask
    s_off = _dot_nt(q, k_ref[0:t0, :]) + bias[:, 0:t0] if i > 0 else None
    return s_diag, s_off

  def finish(i, s_diag, s_off):
    t0, t1 = i * tq, (i + 1) * tq
    m = jnp.max(s_diag, axis=-1, keepdims=True)
    if i > 0:
      m = jnp.maximum(m, jnp.max(s_off, axis=-1, keepdims=True))
    e_diag = jnp.exp2(s_diag - m)
    l = jnp.sum(e_diag, axis=-1, keepdims=True)
    o = _dot(e_diag.astype(BF16), v_ref[t0:t1, :])
    if i > 0:
      e_off = jnp.exp2(s_off - m)
      l = l + jnp.sum(e_off, axis=-1, keepdims=True)
      o = o + _dot(e_off.astype(BF16), v_ref[0:t0, :])
    o_ref[t0:t1, :] = (o / l).astype(o_ref.dtype)

  n_tiles = seq // tq
  pending = scores(0)
  for i in range(n_tiles):
    upcoming = scores(i + 1) if i + 1 < n_tiles else None
    finish(i, *pending)
    pending = upcoming


def _fox_attention(u_a, c, batch, seq):
  blk = lambda off: pl.BlockSpec((seq, HEAD_DIM), lambda b, h: (b, off + h))
  return pl.pallas_call(
      _fox_kernel,
      grid=(batch, N_HEADS_FOX),
      in_specs=[
          blk(0), blk(N_HEADS_FOX), blk(2 * N_HEADS_FOX),
          pl.BlockSpec((None, N_HEADS_FOX, seq), lambda b, h: (b, 0, 0)),
      ],
      out_specs=pl.BlockSpec((seq, HEAD_DIM), lambda b, h: (b, h)),
      out_shape=jax.ShapeDtypeStruct((batch * seq, D_FOX), BF16),
      compiler_params=_params("parallel", "arbitrary"),
      name="fox_attention",
  )(u_a, u_a, u_a, c)


def _t5_bucket_np(dist):
  max_exact = N_REL_BUCKETS // 2
  d = np.maximum(dist, 1).astype(np.float32)
  large = max_exact + (np.log(d / np.float32(max_exact))
                       / np.float32(math.log(REL_MAX_DISTANCE / max_exact))
                       * np.float32(N_REL_BUCKETS - max_exact)).astype(np.int32)
  large = np.minimum(large, N_REL_BUCKETS - 1)
  return np.where(dist < max_exact, dist, large).astype(np.int32)


def _block_positions(dilation):
  n = WINDOW_KEYS
  m = RESIDUES // dilation
  rows = n // m
  j = np.arange(m)[:, None]
  qpos = (n + m * np.arange(rows)[None, :] + j).reshape(-1)
  kpos = (m * np.arange(2 * rows)[None, :] + j).reshape(-1)
  return qpos, kpos


def _band_buckets():
  n = WINDOW_KEYS
  tiles = []
  for _, dilation in DILATED_PATTERNS:
    qpos, kpos = _block_positions(dilation)
    rel = qpos[:, None] - np.concatenate([kpos, qpos])[None, :]
    valid = (rel >= 0) & (rel <= n)
    bucket = _t5_bucket_np(np.maximum(rel, 0) * dilation)
    tiles.append(np.where(valid, bucket, -1))
  return np.stack(tiles).astype(np.int32)


def _bias_kernel(tab_ref, bkt_ref, o_ref):
  bkt = bkt_ref[...]
  for h in range(N_HEADS_DIL):
    acc = jnp.full(bkt.shape, NEG_INF, F32)
    for b in range(N_REL_BUCKETS):
      acc = jnp.where(bkt == b, tab_ref[b, h] * LOG2_E, acc)
    o_ref[h] = acc


def _band_bias(rel_table):
  buckets = jnp.asarray(_band_buckets())
  p, n, n2 = buckets.shape
  return pl.pallas_call(
      _bias_kernel,
      grid=(p,),
      in_specs=[
          pl.BlockSpec(memory_space=pltpu.SMEM),
          pl.BlockSpec((None, n, n2), lambda i: (i, 0, 0)),
      ],
      out_specs=pl.BlockSpec((None, N_HEADS_DIL, n, n2), lambda i: (i, 0, 0, 0)),
      out_shape=jax.ShapeDtypeStruct((p, N_HEADS_DIL, n, n2), F32),
      compiler_params=_params("parallel"),
      name="band_bias",
  )(rel_table, buckets)


def _dil_kernel(q_ref, k_ref, v_ref, bm_ref, o_ref,
                acc0, acc1, acc2, lse0, lse1, lse2):
  seq = q_ref.shape[0]
  n = WINDOW_KEYS
  seg = seq // RESIDUES
  accs = (acc0, acc1, acc2)
  lses = (lse0, lse1, lse2)

  def gather(ref, starts, size):
    return jnp.concatenate([ref[st:st + size, :] for st in starts], axis=0).astype(BF16)

  def block_rows(segments, nb):
    rows = n // len(segments)
    q_starts = [s * seg + nb * rows for s in segments]
    if nb == 0:
      return rows, q_starts, q_starts, rows
    return rows, q_starts, [st - rows for st in q_starts], 2 * rows

  def scores(p, segments, nb):
    rows, q_starts, k_starts, k_rows = block_rows(segments, nb)
    bm = bm_ref[p, :, 2 * n:3 * n] if nb == 0 else bm_ref[p, :, 0:2 * n]
    return _dot_nt(gather(q_ref, q_starts, rows), gather(k_ref, k_starts, k_rows)) + bm

  def finish(p, segments, nb, e, l, lse):
    rows, q_starts, k_starts, k_rows = block_rows(segments, nb)
    o = _dot(e.astype(BF16), gather(v_ref, k_starts, k_rows)) / l
    lse = jnp.broadcast_to(lse, (n, HEAD_DIM))
    for j, st in enumerate(q_starts):
      accs[p][st:st + rows, :] = o[j * rows:(j + 1) * rows, :]
      lses[p][st:st + rows, :] = lse[j * rows:(j + 1) * rows, :]

  blocks = [(p, list(range(r, RESIDUES, d)), nb)
            for p, (_, d) in enumerate(DILATED_PATTERNS)
            for r in range(d) for nb in range(seq // (n * d))]
  for g in range(0, len(blocks), DIL_BLOCK_GROUP):
    group = blocks[g:g + DIL_BLOCK_GROUP]
    ss = [scores(*blk) for blk in group]
    ms = [jnp.max(s, axis=-1, keepdims=True) for s in ss]
    es = [jnp.exp2(s - m) for s, m in zip(ss, ms)]
    ls = [jnp.sum(e, axis=-1, keepdims=True) for e in es]
    for blk, e, l, m in zip(group, es, ls, ms):
      finish(*blk, e, l, m + jnp.log2(l))

  def merge(c, carry):
    rows = pl.ds(pl.multiple_of(c * n, n), n)
    l0, l1, l2 = lse0[rows, :], lse1[rows, :], lse2[rows, :]
    mx = jnp.maximum(jnp.maximum(l0, l1), l2)
    w0, w1, w2 = jnp.exp2(l0 - mx), jnp.exp2(l1 - mx), jnp.exp2(l2 - mx)
    mixed = (w0 * acc0[rows, :] + w1 * acc1[rows, :] + w2 * acc2[rows, :]) / (w0 + w1 + w2)
    o_ref[rows, :] = mixed.astype(o_ref.dtype)
    return carry

  lax.fori_loop(0, seq // n, merge, 0)


def _dilated_attention(u_b, band_bias, batch, seq):
  assert seq // RESIDUES == WINDOW_KEYS
  blk = lambda off: pl.BlockSpec((seq, HEAD_DIM), lambda b, h: (b, off + h))
  n_pat = len(DILATED_PATTERNS)
  scratch = [pltpu.VMEM((seq, HEAD_DIM), F32) for _ in range(2 * n_pat)]
  return pl.pallas_call(
      _dil_kernel,
      grid=(batch, N_HEADS_DIL),
      in_specs=[
          blk(0), blk(N_HEADS_DIL), blk(2 * N_HEADS_DIL),
          pl.BlockSpec((n_pat, None, WINDOW_KEYS, 3 * WINDOW_KEYS), lambda b, h: (0, h, 0, 0)),
      ],
      out_specs=pl.BlockSpec((seq, HEAD_DIM), lambda b, h: (b, h)),
      out_shape=jax.ShapeDtypeStruct((batch * seq, D_DIL), BF16),
      scratch_shapes=scratch,
      compiler_params=_params("parallel", "arbitrary"),
      name="dilated_attention",
  )(u_b, u_b, u_b, band_bias)


def _out_proj_kernel(x_ref, a_ref, b_ref, swap_ref, wa_ref, wb_ref, o_ref):
  per = PERM_ROWS // RESIDUES
  for a in range(x_ref.shape[0] // PERM_ROWS):
    rows = slice(a * PERM_ROWS, (a + 1) * PERM_ROWS)
    slab = jnp.concatenate([b_ref[r, a * per:(a + 1) * per, :] for r in range(RESIDUES)], axis=0)
    o_b = _dot(swap_ref[...], slab).astype(BF16)
    o_ref[rows, :] = x_ref[rows, :] + _dot(a_ref[rows, :], wa_ref[...]) + _dot(o_b, wb_ref[...])


def _out_proj(x, o_a, o_b, wo, batch, seq):
  m, d = x.shape
  tm = PROJ_ROW_TILE
  tiles = seq // tm
  ca, cb = o_a.shape[1], o_b.shape[1]
  assert ca == cb and wo.shape[0] == ca + cb
  resident = pl.Buffered(1)
  return pl.pallas_call(
      _out_proj_kernel,
      grid=(m // tm,),
      in_specs=[
          pl.BlockSpec((tm, d), lambda i: (i, 0)),
          pl.BlockSpec((tm, ca), lambda i: (i, 0)),
          pl.BlockSpec((RESIDUES, tm // RESIDUES, cb), lambda i: (i // tiles, i % tiles, 0)),
          pl.BlockSpec((PERM_ROWS, PERM_ROWS), lambda i: (0, 0)),
          pl.BlockSpec((ca, d), lambda i: (0, 0), pipeline_mode=resident),
          pl.BlockSpec((cb, d), lambda i: (1, 0), pipeline_mode=resident),
      ],
      out_specs=pl.BlockSpec((tm, d), lambda i: (i, 0)),
      out_shape=jax.ShapeDtypeStruct((m, d), F32),
      compiler_params=_params("parallel"),
      name="out_proj",
  )(x, o_a, o_b.reshape(batch * RESIDUES, seq // RESIDUES, cb), _swap_matrix(), wo, wo)


def _ple_kernel(x_ref, p_ref, g_ref, wg_ref, wp_ref, gf_ref, o_ref, *, final_norm):
  x = x_ref[...]
  h = _rmsnorm(x, g_ref[...]).astype(BF16)
  gate = jax.nn.sigmoid(_dot(h, wg_ref[...]))
  y = x + gate * _dot(p_ref[...].astype(BF16), wp_ref[...])
  o_ref[...] = _rmsnorm(y, gf_ref[...]) if final_norm else y


def _ple(x, p, g, w_gate, w_proj, g_final, final_norm):
  m, d = x.shape
  tm = PROJ_ROW_TILE
  return pl.pallas_call(
      functools.partial(_ple_kernel, final_norm=final_norm),
      grid=(m // tm,),
      in_specs=[
          pl.BlockSpec((tm, d), lambda i: (i, 0)),
          pl.BlockSpec((tm, p.shape[1]), lambda i: (i, 0)),
          pl.BlockSpec((1, d), lambda i: (0, 0)),
          pl.BlockSpec(w_gate.shape, lambda i: (0, 0)),
          pl.BlockSpec(w_proj.shape, lambda i: (0, 0)),
          pl.BlockSpec((1, d), lambda i: (0, 0)),
      ],
      out_specs=pl.BlockSpec((tm, d), lambda i: (i, 0)),
      out_shape=jax.ShapeDtypeStruct((m, d), F32),
      compiler_params=_params("parallel"),
      name="ple",
  )(x, p, g, w_gate, w_proj, g_final)


def kernel(x, p, norm_ffn1, ffn1_w_gate, ffn1_w_up, ffn1_w_down, norm_mix, w_in, b_f, w_o,
           norm_ffn2, ffn2_w_gate, ffn2_w_up, ffn2_w_down, norm_ple, w_ple_gate, w_ple_proj,
           rel_table, norm_final):
  batch, seq, d = x.shape
  depth = p.shape[0]
  m = batch * seq
  bf = lambda w: w.astype(BF16)
  row = lambda g: g.reshape(1, -1).astype(F32)

  band_bias = _band_bias(rel_table.astype(F32))
  xs = x.reshape(m, d).astype(F32)
  for i in range(depth):
    xs, wg2, wu2, wd2, w_a, w_b, w_f = _ffn(
        xs, row(norm_ffn1[i]), bf(ffn1_w_gate[i]), bf(ffn1_w_up[i]), bf(0.5 * ffn1_w_down[i]),
        jobs=_ffn_cast_jobs(m, ffn2_w_gate[i], ffn2_w_up[i], ffn2_w_down[i])
        + _w_in_repack_jobs(m, jnp.swapaxes(w_in[i], 0, 1), 3 * D_FOX, N_HEADS_FOX, 3 * D_DIL))

    b_f_row = jnp.pad(b_f[i].astype(F32), (0, V7X_LANES - N_HEADS_FOX)).reshape(1, V7X_LANES)
    g_mix = row(norm_mix[i])
    wo_rows = w_o[i].shape[0] // (m // PROJ_ROW_TILE)
    u_a, f_logit, wo = _norm_matmul(
        xs, g_mix, w_a, BF16, (D_FOX, 2 * D_FOX), w_narrow=w_f,
        jobs=[_cast_job(w_o[i], (wo_rows, w_o[i].shape[1]), lambda t: (t, 0))])
    u_b, = _norm_matmul(xs, g_mix, w_b, F32, (D_DIL, 2 * D_DIL), residue_major=(batch, seq))

    c = _fox_decay(f_logit, b_f_row, batch, seq)
    o_a = _fox_attention(u_a, c, batch, seq)
    o_b = _dilated_attention(u_b, band_bias, batch, seq)
    xs = _out_proj(xs, o_a, o_b, wo, batch, seq)

    xs, w_gate, w_ple = _ffn(xs, row(norm_ffn2[i]), wg2, wu2, wd2,
                             jobs=_row_cast_jobs(m, w_ple_gate[i], w_ple_proj[i]))
    last = i == depth - 1
    xs = _ple(xs, p[i].reshape(m, -1), row(norm_ple[i]), w_gate, w_ple,
              row(norm_final), final_norm=last)
  return xs.reshape(batch, seq, d).astype(x.dtype)
```

```python
import functools
import math
from typing import Any, Callable, NamedTuple

import jax
import jax.numpy as jnp
import numpy as np
from jax import lax
from jax.experimental import pallas as pl
from jax.experimental.pallas import tpu as pltpu

F32 = jnp.float32
BF16 = jnp.bfloat16

HEAD_DIM = 128
N_HEADS_FOX = 8
N_HEADS_DIL = 8
D_FOX = N_HEADS_FOX * HEAD_DIM
D_DIL = N_HEADS_DIL * HEAD_DIM
DILATED_PATTERNS = ((128, 1), (512, 4), (2048, 16))
WINDOW_KEYS = 128
N_REL_BUCKETS = 32
REL_MAX_DISTANCE = 2048
RMS_EPS = 1e-6
NEG_INF = -1e30
SCALE = HEAD_DIM ** -0.5
LOG2_E = math.log2(math.e)

V7X_LANES = 128
V7X_VMEM_LIMIT_BYTES = 56 * 1024 * 1024

RESIDUES = max(d for _, d in DILATED_PATTERNS)
PERM_ROWS = RESIDUES * RESIDUES

FFN_ROW_TILE = 1024
FFN_SUB_ROWS = 512
FFN_FF_TILE = 512
PROJ_ROW_TILE = 512
FOX_Q_TILE = 256
FOX_LOOKAHEAD = 2
DIL_BLOCK_GROUP = 12


def _params(*semantics):
  return pltpu.CompilerParams(dimension_semantics=semantics,
                              vmem_limit_bytes=V7X_VMEM_LIMIT_BYTES)


def _rmsnorm(x, g):
  ms = jnp.mean(x * x, axis=-1, keepdims=True)
  return x * lax.rsqrt(ms + RMS_EPS) * g


def _dot(a, b):
  return jnp.dot(a, b, preferred_element_type=F32)


def _dot_nt(a, b):
  return lax.dot_general(a, b, (((1,), (1,)), ((), ())), preferred_element_type=F32)


class _SideJob(NamedTuple):
  arrays: tuple
  in_specs: tuple
  out_shape: Any
  out_spec: Any
  fn: Callable


def _cast_job(a, block, index, scale=1.0):
  spec = pl.BlockSpec(block, index)
  return _SideJob((a,), (spec,), jax.ShapeDtypeStruct(a.shape, BF16), spec,
                  lambda r: r[...] * scale)


def _run_side_jobs(jobs, in_refs, out_refs):
  in_refs = list(in_refs)
  for job, out_ref in zip(jobs, out_refs):
    refs = [in_refs.pop(0) for _ in job.arrays]
    out_ref[...] = job.fn(*refs).astype(out_ref.dtype)


def _side_args(jobs):
  arrays = [a for job in jobs for a in job.arrays]
  in_specs = [s for job in jobs for s in job.in_specs]
  return arrays, in_specs, [job.out_spec for job in jobs], [job.out_shape for job in jobs]


def _ffn_kernel(x_ref, g_ref, wg_ref, wu_ref, wd_ref, *rest, jobs):
  n_in = sum(len(job.arrays) for job in jobs)
  side_in, o_ref, side_out, h_ref = rest[:n_in], rest[n_in], rest[n_in + 1:-1], rest[-1]

  def step(first):
    _run_side_jobs(jobs, side_in, side_out)
    for r in range(h_ref.shape[0] // FFN_SUB_ROWS):
      rows = slice(r * FFN_SUB_ROWS, (r + 1) * FFN_SUB_ROWS)
      if first:
        base = x_ref[rows, :]
        h = _rmsnorm(base, g_ref[...]).astype(BF16)
        h_ref[rows, :] = h
      else:
        base = o_ref[rows, :]
        h = h_ref[rows, :]
      gate = _dot(h, wg_ref[...])
      up = _dot(h, wu_ref[...])
      act = (gate * jax.nn.sigmoid(gate)) * up
      o_ref[rows, :] = base + _dot(act.astype(BF16), wd_ref[...])

  lax.cond(pl.program_id(1) == 0, lambda: step(True), lambda: step(False))


def _ffn(x, g, wg, wu, wd_half, jobs=()):
  m, d = x.shape
  dff = wg.shape[1]
  tm, tf = FFN_ROW_TILE, FFN_FF_TILE
  side_arrays, side_in_specs, side_out_specs, side_out_shapes = _side_args(jobs)
  return pl.pallas_call(
      functools.partial(_ffn_kernel, jobs=tuple(jobs)),
      grid=(m // tm, dff // tf),
      in_specs=[
          pl.BlockSpec((tm, d), lambda i, j: (i, 0)),
          pl.BlockSpec((1, d), lambda i, j: (0, 0)),
          pl.BlockSpec((d, tf), lambda i, j: (0, j)),
          pl.BlockSpec((d, tf), lambda i, j: (0, j)),
          pl.BlockSpec((tf, d), lambda i, j: (j, 0)),
      ] + side_in_specs,
      out_specs=[pl.BlockSpec((tm, d), lambda i, j: (i, 0))] + side_out_specs,
      out_shape=[jax.ShapeDtypeStruct((m, d), F32)] + side_out_shapes,
      scratch_shapes=[pltpu.VMEM((tm, d), BF16)],
      compiler_params=_params("parallel", "arbitrary"),
      name="ffn",
  )(x, g, wg, wu, wd_half, *side_arrays)


def _ffn_cast_jobs(m, w_gate, w_up, w_down):
  ni = m // FFN_ROW_TILE
  d, dff = w_gate.shape
  tf = FFN_FF_TILE
  assert d % ni == 0
  return [
      _cast_job(w_gate, (d // ni, tf), lambda i, j: (i, j)),
      _cast_job(w_up, (d // ni, tf), lambda i, j: (i, j)),
      _cast_job(w_down, (tf, d // ni), lambda i, j: (j, i), scale=0.5),
  ]


def _row_cast_jobs(m, *weights):
  ni = m // FFN_ROW_TILE
  return [_cast_job(w, (w.shape[0] // ni, w.shape[1]), lambda i, j: (i, 0)) for w in weights]


def _w_in_repack_jobs(m, w_t, n_a, n_f, n_b):
  ni = m // FFN_ROW_TILE
  d = w_t.shape[1]
  cols, tc = d // ni, FFN_FF_TILE
  assert n_a % tc == 0 and n_b % tc == 0 and n_f % 8 == 0 and n_f <= V7X_LANES
  assert cols % V7X_LANES == 0
  chunks = n_a // tc
  assert n_b // tc == chunks
  cj = lambda j: jnp.minimum(j, chunks - 1)

  def transposed(ref):
    return ref[...].T

  def shifted_transposed(main_ref, tail_ref):
    return jnp.concatenate([main_ref[n_f:, :], tail_ref[...]], axis=0).T

  def gate_rows_transposed(ref):
    row = lax.broadcasted_iota(jnp.int32, ref.shape, 0)
    return jnp.where(row < n_f, ref[...], 0.0).T

  out = lambda n: jax.ShapeDtypeStruct((d, n), BF16)
  out_spec = pl.BlockSpec((cols, tc), lambda i, j: (i, cj(j)))
  return [
      _SideJob((w_t,), (pl.BlockSpec((tc, cols), lambda i, j: (cj(j), i)),),
               out(n_a), out_spec, transposed),
      _SideJob((w_t, w_t),
               (pl.BlockSpec((tc, cols), lambda i, j: (chunks + cj(j), i)),
                pl.BlockSpec((n_f, cols), lambda i, j: ((n_a + (cj(j) + 1) * tc) // n_f, i))),
               out(n_b), out_spec, shifted_transposed),
      _SideJob((w_t,), (pl.BlockSpec((V7X_LANES, cols), lambda i, j: (n_a // V7X_LANES, i)),),
               out(V7X_LANES), pl.BlockSpec((cols, V7X_LANES), lambda i, j: (i, 0)),
               gate_rows_transposed),
  ]


def _swap_matrix():
  i = np.arange(PERM_ROWS)
  src = (i % RESIDUES) * RESIDUES + i // RESIDUES
  return jnp.asarray(np.eye(PERM_ROWS, dtype=np.float32)[src], BF16)


def _norm_matmul_kernel(x_ref, g_ref, w_ref, *rest, permute, narrow, jobs, key_cols):
  rest = list(rest)
  swap_ref = rest.pop(0) if permute else None
  wn_ref = rest.pop(0) if narrow else None
  side_in = [rest.pop(0) for job in jobs for _ in job.arrays]
  o_ref = rest.pop(0)
  on_ref = rest.pop(0) if narrow else None
  side_out = [rest.pop(0) for _ in jobs]
  assert not rest, "unexpected extra refs"
  tm = x_ref.shape[0]
  _run_side_jobs(jobs, side_in, side_out)

  h = _rmsnorm(x_ref[...], g_ref[...]).astype(BF16)
  if permute:
    h = jnp.concatenate(
        [_dot(swap_ref[...], h[a * PERM_ROWS:(a + 1) * PERM_ROWS, :]).astype(BF16)
         for a in range(tm // PERM_ROWS)], axis=0)
  if narrow:
    on_ref[...] = _dot(h, wn_ref[...])
  acc = _dot(h, w_ref[...])
  lo, hi = key_cols
  res = jnp.concatenate([acc[:, :lo], acc[:, lo:hi] * (SCALE * LOG2_E), acc[:, hi:]],
                        axis=1).astype(o_ref.dtype)
  if permute:
    per = PERM_ROWS // RESIDUES
    for a in range(tm // PERM_ROWS):
      for r in range(RESIDUES):
        start = a * PERM_ROWS + r * per
        o_ref[r, a * per:(a + 1) * per, :] = res[start:start + per, :]
  else:
    o_ref[...] = res


def _norm_matmul(x, g, w, out_dtype, key_cols, residue_major=None, w_narrow=None, jobs=()):
  m, d = x.shape
  n = w.shape[1]
  tm = PROJ_ROW_TILE
  narrow = w_narrow is not None
  permute = residue_major is not None
  assert not (narrow and permute)
  in_specs = [
      pl.BlockSpec((tm, d), lambda i: (i, 0)),
      pl.BlockSpec((1, d), lambda i: (0, 0)),
      pl.BlockSpec((d, n), lambda i: (0, 0)),
  ]
  args = [x, g, w]
  if permute:
    batch, seq = residue_major
    tiles = seq // tm
    in_specs.append(pl.BlockSpec((PERM_ROWS, PERM_ROWS), lambda i: (0, 0)))
    args.append(_swap_matrix())
    out_specs = [pl.BlockSpec((RESIDUES, tm // RESIDUES, n), lambda i: (i // tiles, i % tiles, 0))]
    out_shape = [jax.ShapeDtypeStruct((batch * RESIDUES, seq // RESIDUES, n), out_dtype)]
  else:
    out_specs = [pl.BlockSpec((tm, n), lambda i: (i, 0))]
    out_shape = [jax.ShapeDtypeStruct((m, n), out_dtype)]
  if narrow:
    in_specs.append(pl.BlockSpec(w_narrow.shape, lambda i: (0, 0)))
    args.append(w_narrow)
    out_specs.append(pl.BlockSpec((tm, w_narrow.shape[1]), lambda i: (i, 0)))
    out_shape.append(jax.ShapeDtypeStruct((m, w_narrow.shape[1]), F32))
  side_arrays, side_in_specs, side_out_specs, side_out_shapes = _side_args(jobs)
  outs = pl.pallas_call(
      functools.partial(_norm_matmul_kernel, permute=permute, narrow=narrow, jobs=tuple(jobs),
                        key_cols=key_cols),
      grid=(m // tm,),
      in_specs=in_specs + side_in_specs,
      out_specs=out_specs + side_out_specs,
      out_shape=out_shape + side_out_shapes,
      compiler_params=_params("parallel"),
      name="norm_matmul",
  )(*args, *side_arrays)
  outs = list(outs)
  if permute:
    outs[0] = outs[0].reshape(m, n)
  return outs


def _cumsum_kernel(fl_ref, bf_ref, c_ref):
  s = fl_ref.shape[0]
  z = fl_ref[...] + bf_ref[...]
  logf = jnp.minimum(z, 0.0) - jnp.log1p(jnp.exp(-jnp.abs(z)))
  lt = logf.T[0:N_HEADS_FOX, :]
  row = lax.broadcasted_iota(jnp.int32, (V7X_LANES, V7X_LANES), 0)
  col = lax.broadcasted_iota(jnp.int32, (V7X_LANES, V7X_LANES), 1)
  upper = (row <= col).astype(F32)
  carry = jnp.zeros((N_HEADS_FOX, 1), F32)
  for j in range(s // V7X_LANES):
    blk = lt[:, j * V7X_LANES:(j + 1) * V7X_LANES]
    cs = jnp.dot(blk, upper, preferred_element_type=F32,
                 precision=lax.Precision.HIGHEST) + carry
    c_ref[:, j * V7X_LANES:(j + 1) * V7X_LANES] = cs
    carry = cs[:, V7X_LANES - 1:V7X_LANES]


def _fox_decay(f_logit, b_f_row, batch, seq):
  return pl.pallas_call(
      _cumsum_kernel,
      grid=(batch,),
      in_specs=[
          pl.BlockSpec((seq, V7X_LANES), lambda b: (b, 0)),
          pl.BlockSpec((1, V7X_LANES), lambda b: (0, 0)),
      ],
      out_specs=pl.BlockSpec((None, N_HEADS_FOX, seq), lambda b: (b, 0, 0)),
      out_shape=jax.ShapeDtypeStruct((batch, N_HEADS_FOX, seq), F32),
      compiler_params=_params("parallel"),
      name="fox_decay",
  )(f_logit, b_f_row)


def _fox_kernel(q_ref, k_ref, v_ref, c_ref, o_ref):
  seq = q_ref.shape[0]
  tq = FOX_Q_TILE
  h = pl.program_id(1)
  crow = c_ref[pl.ds(h, 1), :] * LOG2_E
  row = lax.broadcasted_iota(jnp.int32, (tq, tq), 0)
  col = lax.broadcasted_iota(jnp.int32, (tq, tq), 1)
  diag_mask = jnp.where(col > row, NEG_INF, 0.0).astype(F32)

  def scores(i):
    t0, t1 = i * tq, (i + 1) * tq
    q = q_ref[t0:t1, :]
    bias = crow[:, t1 - 1:t1] - crow[:, 0:t1]
    s_diag = _dot_nt(q, k_ref[t0:t1, :]) + bias[:, t0:t1] + diag_mask
    s_off = _dot_nt(q, k_ref[0:t0, :]) + bias[:, 0:t0] if i > 0 else None
    return s_diag, s_off

  def finish(i, s_diag, s_off):
    t0, t1 = i * tq, (i + 1) * tq
    m = jnp.max(s_diag, axis=-1, keepdims=True)
    if i > 0:
      m = jnp.maximum(m, jnp.max(s_off, axis=-1, keepdims=True))
    e_diag = jnp.exp2(s_diag - m)
    l = jnp.sum(e_diag, axis=-1, keepdims=True)
    o = _dot(e_diag.astype(BF16), v_ref[t0:t1, :])
    if i > 0:
      e_off = jnp.exp2(s_off - m)
      l = l + jnp.sum(e_off, axis=-1, keepdims=True)
      o = o + _dot(e_off.astype(BF16), v_ref[0:t0, :])
    o_ref[t0:t1, :] = (o / l).astype(o_ref.dtype)

  n_tiles = seq // tq
  pending = [scores(i) for i in range(min(FOX_LOOKAHEAD, n_tiles))]
  for i in range(n_tiles):
    if i + FOX_LOOKAHEAD < n_tiles:
      pending.append(scores(i + FOX_LOOKAHEAD))
    finish(i, *pending.pop(0))


def _fox_attention(u_a, c, batch, seq):
  blk = lambda off: pl.BlockSpec((seq, HEAD_DIM), lambda b, h: (b, off + h))
  return pl.pallas_call(
      _fox_kernel,
      grid=(batch, N_HEADS_FOX),
      in_specs=[
          blk(0), blk(N_HEADS_FOX), blk(2 * N_HEADS_FOX),
          pl.BlockSpec((None, N_HEADS_FOX, seq), lambda b, h: (b, 0, 0)),
      ],
      out_specs=pl.BlockSpec((seq, HEAD_DIM), lambda b, h: (b, h)),
      out_shape=jax.ShapeDtypeStruct((batch * seq, D_FOX), BF16),
      compiler_params=_params("parallel", "arbitrary"),
      name="fox_attention",
  )(u_a, u_a, u_a, c)


def _t5_bucket_np(dist):
  max_exact = N_REL_BUCKETS // 2
  d = np.maximum(dist, 1).astype(np.float32)
  large = max_exact + (np.log(d / np.float32(max_exact))
                       / np.float32(math.log(REL_MAX_DISTANCE / max_exact))
                       * np.float32(N_REL_BUCKETS - max_exact)).astype(np.int32)
  large = np.minimum(large, N_REL_BUCKETS - 1)
  return np.where(dist < max_exact, dist, large).astype(np.int32)


def _block_positions(dilation):
  n = WINDOW_KEYS
  m = RESIDUES // dilation
  rows = n // m
  j = np.arange(m)[:, None]
  qpos = (n + m * np.arange(rows)[None, :] + j).reshape(-1)
  kpos = (m * np.arange(2 * rows)[None, :] + j).reshape(-1)
  return qpos, kpos


def _band_buckets():
  n = WINDOW_KEYS
  tiles = []
  for _, dilation in DILATED_PATTERNS:
    qpos, kpos = _block_positions(dilation)
    rel = qpos[:, None] - np.concatenate([kpos, qpos])[None, :]
    valid = (rel >= 0) & (rel <= n)
    bucket = _t5_bucket_np(np.maximum(rel, 0) * dilation)
    tiles.append(np.where(valid, bucket, -1))
  return np.stack(tiles).astype(np.int32)


def _bias_kernel(tab_ref, bkt_ref, o_ref):
  bkt = bkt_ref[...]
  for h in range(N_HEADS_DIL):
    acc = jnp.full(bkt.shape, NEG_INF, F32)
    for b in range(N_REL_BUCKETS):
      acc = jnp.where(bkt == b, tab_ref[b, h] * LOG2_E, acc)
    o_ref[h] = acc


def _band_bias(rel_table):
  buckets = jnp.asarray(_band_buckets())
  p, n, n2 = buckets.shape
  return pl.pallas_call(
      _bias_kernel,
      grid=(p,),
      in_specs=[
          pl.BlockSpec(memory_space=pltpu.SMEM),
          pl.BlockSpec((None, n, n2), lambda i: (i, 0, 0)),
      ],
      out_specs=pl.BlockSpec((None, N_HEADS_DIL, n, n2), lambda i: (i, 0, 0, 0)),
      out_shape=jax.ShapeDtypeStruct((p, N_HEADS_DIL, n, n2), F32),
      compiler_params=_params("parallel"),
      name="band_bias",
  )(rel_table, buckets)


def _dil_kernel(q_ref, k_ref, v_ref, bm_ref, o_ref, *scratch):
  seq = q_ref.shape[0]
  n = WINDOW_KEYS
  seg = seq // RESIDUES
  last = len(DILATED_PATTERNS) - 1
  assert DILATED_PATTERNS[last][1] == RESIDUES and n == seg
  accs, lses = scratch[:last], scratch[last:]

  def gather(ref, starts, size):
    return jnp.concatenate([ref[st:st + size, :] for st in starts], axis=0).astype(BF16)

  def block_rows(segments, nb):
    rows = n // len(segments)
    q_starts = [s * seg + nb * rows for s in segments]
    if nb == 0:
      return rows, q_starts, q_starts, rows
    return rows, q_starts, [st - rows for st in q_starts], 2 * rows

  def scores(p, segments, nb):
    rows, q_starts, k_starts, k_rows = block_rows(segments, nb)
    bm = bm_ref[p, :, 2 * n:3 * n] if nb == 0 else bm_ref[p, :, 0:2 * n]
    return _dot_nt(gather(q_ref, q_starts, rows), gather(k_ref, k_starts, k_rows)) + bm

  def finish(p, segments, nb, e, l, lse):
    rows, q_starts, k_starts, k_rows = block_rows(segments, nb)
    o = _dot(e.astype(BF16), gather(v_ref, k_starts, k_rows)) / l
    lse = jnp.broadcast_to(lse, (n, HEAD_DIM))
    if p < last:
      for j, st in enumerate(q_starts):
        accs[p][st:st + rows, :] = o[j * rows:(j + 1) * rows, :]
        lses[p][st:st + rows, :] = lse[j * rows:(j + 1) * rows, :]
      return
    seg_rows = slice(q_starts[0], q_starts[0] + n)
    all_lse = [ref[seg_rows, :] for ref in lses] + [lse]
    all_out = [ref[seg_rows, :] for ref in accs] + [o]
    top = functools.reduce(jnp.maximum, all_lse)
    weights = [jnp.exp2(x - top) for x in all_lse]
    add = lambda a, b: a + b
    mixed = (functools.reduce(add, [w * a for w, a in zip(weights, all_out)])
             / functools.reduce(add, weights))
    o_ref[seg_rows, :] = mixed.astype(o_ref.dtype)

  blocks = [(p, list(range(r, RESIDUES, d)), nb)
            for p, (_, d) in enumerate(DILATED_PATTERNS)
            for r in range(d) for nb in range(seq // (n * d))]
  for g in range(0, len(blocks), DIL_BLOCK_GROUP):
    group = blocks[g:g + DIL_BLOCK_GROUP]
    ss = [scores(*blk) for blk in group]
    ms = [jnp.max(s, axis=-1, keepdims=True) for s in ss]
    es = [jnp.exp2(s - m) for s, m in zip(ss, ms)]
    ls = [jnp.sum(e, axis=-1, keepdims=True) for e in es]
    for blk, e, l, m in zip(group, es, ls, ms):
      finish(*blk, e, l, m + jnp.log2(l))


def _dilated_attention(u_b, band_bias, batch, seq):
  assert seq // RESIDUES == WINDOW_KEYS
  blk = lambda off: pl.BlockSpec((seq, HEAD_DIM), lambda b, h: (b, off + h))
  n_pat = len(DILATED_PATTERNS)
  scratch = [pltpu.VMEM((seq, HEAD_DIM), F32) for _ in range(2 * (n_pat - 1))]
  return pl.pallas_call(
      _dil_kernel,
      grid=(batch, N_HEADS_DIL),
      in_specs=[
          blk(0), blk(N_HEADS_DIL), blk(2 * N_HEADS_DIL),
          pl.BlockSpec((n_pat, None, WINDOW_KEYS, 3 * WINDOW_KEYS), lambda b, h: (0, h, 0, 0)),
      ],
      out_specs=pl.BlockSpec((seq, HEAD_DIM), lambda b, h: (b, h)),
      out_shape=jax.ShapeDtypeStruct((batch * seq, D_DIL), BF16),
      scratch_shapes=scratch,
      compiler_params=_params("parallel", "arbitrary"),
      name="dilated_attention",
  )(u_b, u_b, u_b, band_bias)


def _out_proj_kernel(x_ref, a_ref, b_ref, swap_ref, wa_ref, wb_ref, o_ref):
  per = PERM_ROWS // RESIDUES
  for a in range(x_ref.shape[0] // PERM_ROWS):
    rows = slice(a * PERM_ROWS, (a + 1) * PERM_ROWS)
    slab = jnp.concatenate([b_ref[r, a * per:(a + 1) * per, :] for r in range(RESIDUES)], axis=0)
    o_b = _dot(swap_ref[...], slab).astype(BF16)
    o_ref[rows, :] = x_ref[rows, :] + _dot(a_ref[rows, :], wa_ref[...]) + _dot(o_b, wb_ref[...])


def _out_proj(x, o_a, o_b, wo, batch, seq):
  m, d = x.shape
  tm = PROJ_ROW_TILE
  tiles = seq // tm
  ca, cb = o_a.shape[1], o_b.shape[1]
  assert ca == cb and wo.shape[0] == ca + cb
  resident = pl.Buffered(1)
  return pl.pallas_call(
      _out_proj_kernel,
      grid=(m // tm,),
      in_specs=[
          pl.BlockSpec((tm, d), lambda i: (i, 0)),
          pl.BlockSpec((tm, ca), lambda i: (i, 0)),
          pl.BlockSpec((RESIDUES, tm // RESIDUES, cb), lambda i: (i // tiles, i % tiles, 0)),
          pl.BlockSpec((PERM_ROWS, PERM_ROWS), lambda i: (0, 0)),
          pl.BlockSpec((ca, d), lambda i: (0, 0), pipeline_mode=resident),
          pl.BlockSpec((cb, d), lambda i: (1, 0), pipeline_mode=resident),
      ],
      out_specs=pl.BlockSpec((tm, d), lambda i: (i, 0)),
      out_shape=jax.ShapeDtypeStruct((m, d), F32),
      compiler_params=_params("parallel"),
      name="out_proj",
  )(x, o_a, o_b.reshape(batch * RESIDUES, seq // RESIDUES, cb), _swap_matrix(), wo, wo)


def _ple_kernel(x_ref, p_ref, g_ref, wg_ref, wp_ref, gf_ref, o_ref, *, final_norm):
  x = x_ref[...]
  h = _rmsnorm(x, g_ref[...]).astype(BF16)
  gate = jax.nn.sigmoid(_dot(h, wg_ref[...]))
  y = x + gate * _dot(p_ref[...].astype(BF16), wp_ref[...])
  o_ref[...] = _rmsnorm(y, gf_ref[...]) if final_norm else y


def _ple(x, p, g, w_gate, w_proj, g_final, final_norm):
  m, d = x.shape
  tm = PROJ_ROW_TILE
  return pl.pallas_call(
      functools.partial(_ple_kernel, final_norm=final_norm),
      grid=(m // tm,),
      in_specs=[
          pl.BlockSpec((tm, d), lambda i: (i, 0)),
          pl.BlockSpec((tm, p.shape[1]), lambda i: (i, 0)),
          pl.BlockSpec((1, d), lambda i: (0, 0)),
          pl.BlockSpec(w_gate.shape, lambda i: (0, 0)),
          pl.BlockSpec(w_proj.shape, lambda i: (0, 0)),
          pl.BlockSpec((1, d), lambda i: (0, 0)),
      ],
      out_specs=pl.BlockSpec((tm, d), lambda i: (i, 0)),
      out_shape=jax.ShapeDtypeStruct((m, d), F32),
      compiler_params=_params("parallel"),
      name="ple",
  )(x, p, g, w_gate, w_proj, g_final)


def kernel(x, p, norm_ffn1, ffn1_w_gate, ffn1_w_up, ffn1_w_down, norm_mix, w_in, b_f, w_o,
           norm_ffn2, ffn2_w_gate, ffn2_w_up, ffn2_w_down, norm_ple, w_ple_gate, w_ple_proj,
           rel_table, norm_final):
  batch, seq, d = x.shape
  depth = p.shape[0]
  m = batch * seq
  bf = lambda w: w.astype(BF16)
  row = lambda g: g.reshape(1, -1).astype(F32)

  band_bias = _band_bias(rel_table.astype(F32))
  xs = x.reshape(m, d).astype(F32)
  for i in range(depth):
    xs, wg2, wu2, wd2, w_a, w_b, w_f = _ffn(
        xs, row(norm_ffn1[i]), bf(ffn1_w_gate[i]), bf(ffn1_w_up[i]), bf(0.5 * ffn1_w_down[i]),
        jobs=_ffn_cast_jobs(m, ffn2_w_gate[i], ffn2_w_up[i], ffn2_w_down[i])
        + _w_in_repack_jobs(m, jnp.swapaxes(w_in[i], 0, 1), 3 * D_FOX, N_HEADS_FOX, 3 * D_DIL))

    b_f_row = jnp.pad(b_f[i].astype(F32), (0, V7X_LANES - N_HEADS_FOX)).reshape(1, V7X_LANES)
    g_mix = row(norm_mix[i])
    wo_rows = w_o[i].shape[0] // (m // PROJ_ROW_TILE)
    u_a, f_logit, wo = _norm_matmul(
        xs, g_mix, w_a, BF16, (D_FOX, 2 * D_FOX), w_narrow=w_f,
        jobs=[_cast_job(w_o[i], (wo_rows, w_o[i].shape[1]), lambda t: (t, 0))])
    u_b, = _norm_matmul(xs, g_mix, w_b, F32, (D_DIL, 2 * D_DIL), residue_major=(batch, seq))

    c = _fox_decay(f_logit, b_f_row, batch, seq)
    o_a = _fox_attention(u_a, c, batch, seq)
    o_b = _dilated_attention(u_b, band_bias, batch, seq)
    xs = _out_proj(xs, o_a, o_b, wo, batch, seq)

    xs, w_gate, w_ple = _ffn(xs, row(norm_ffn2[i]), wg2, wu2, wd2,
                             jobs=_row_cast_jobs(m, w_ple_gate[i], w_ple_proj[i]))
    last = i == depth - 1
    xs = _ple(xs, p[i].reshape(m, -1), row(norm_ple[i]), w_gate, w_ple,
              row(norm_final), final_norm=last)
  return xs.reshape(batch, seq, d).astype(x.dtype)
```

```python
import functools
import math
from typing import Any, Callable, NamedTuple

import jax
import jax.numpy as jnp
import numpy as np
from jax import lax
from jax.experimental import pallas as pl
from jax.experimental.pallas import tpu as pltpu

F32 = jnp.float32
BF16 = jnp.bfloat16

HEAD_DIM = 128
N_HEADS_FOX = 8
N_HEADS_DIL = 8
D_FOX = N_HEADS_FOX * HEAD_DIM
D_DIL = N_HEADS_DIL * HEAD_DIM
DILATED_PATTERNS = ((128, 1), (512, 4), (2048, 16))
WINDOW_KEYS = 128
N_REL_BUCKETS = 32
REL_MAX_DISTANCE = 2048
RMS_EPS = 1e-6
NEG_INF = -1e30
SCALE = HEAD_DIM ** -0.5
LOG2_E = math.log2(math.e)

V7X_LANES = 128
V7X_VMEM_LIMIT_BYTES = 56 * 1024 * 1024

RESIDUES = max(d for _, d in DILATED_PATTERNS)
PERM_ROWS = RESIDUES * RESIDUES

FFN_ROW_TILE = 1024
FFN_SUB_ROWS = 512
FFN_FF_TILE = 512
PROJ_ROW_TILE = 512
FOX_Q_TILE = 256
FOX_LOOKAHEAD = 2
DIL_BLOCK_GROUP = 12


def _params(*semantics):
  return pltpu.CompilerParams(dimension_semantics=semantics,
                              vmem_limit_bytes=V7X_VMEM_LIMIT_BYTES)


def _rmsnorm(x, g):
  ms = jnp.mean(x * x, axis=-1, keepdims=True)
  return x * lax.rsqrt(ms + RMS_EPS) * g


def _dot(a, b):
  return jnp.dot(a, b, preferred_element_type=F32)


def _dot_nt(a, b):
  return lax.dot_general(a, b, (((1,), (1,)), ((), ())), preferred_element_type=F32)


def _weighted_values(e, v):
  both = _dot(e.astype(BF16), jnp.concatenate([v, jnp.ones_like(v)], axis=1))
  return both[:, :HEAD_DIM], both[:, HEAD_DIM:]


class _SideJob(NamedTuple):
  arrays: tuple
  in_specs: tuple
  out_shape: Any
  out_spec: Any
  fn: Callable


def _cast_job(a, block, index, scale=1.0):
  spec = pl.BlockSpec(block, index)
  return _SideJob((a,), (spec,), jax.ShapeDtypeStruct(a.shape, BF16), spec,
                  lambda r: r[...] * scale)


def _run_side_jobs(jobs, in_refs, out_refs):
  in_refs = list(in_refs)
  for job, out_ref in zip(jobs, out_refs):
    refs = [in_refs.pop(0) for _ in job.arrays]
    out_ref[...] = job.fn(*refs).astype(out_ref.dtype)


def _side_args(jobs):
  arrays = [a for job in jobs for a in job.arrays]
  in_specs = [s for job in jobs for s in job.in_specs]
  return arrays, in_specs, [job.out_spec for job in jobs], [job.out_shape for job in jobs]


def _ffn_kernel(x_ref, g_ref, wg_ref, wu_ref, wd_ref, *rest, jobs):
  n_in = sum(len(job.arrays) for job in jobs)
  side_in, o_ref, side_out, h_ref = rest[:n_in], rest[n_in], rest[n_in + 1:-1], rest[-1]

  def step(first):
    _run_side_jobs(jobs, side_in, side_out)
    for r in range(h_ref.shape[0] // FFN_SUB_ROWS):
      rows = slice(r * FFN_SUB_ROWS, (r + 1) * FFN_SUB_ROWS)
      if first:
        base = x_ref[rows, :]
        h = _rmsnorm(base, g_ref[...]).astype(BF16)
        h_ref[rows, :] = h
      else:
        base = o_ref[rows, :]
        h = h_ref[rows, :]
      gate = _dot(h, wg_ref[...])
      up = _dot(h, wu_ref[...])
      act = (gate * jax.nn.sigmoid(gate)) * up
      o_ref[rows, :] = base + _dot(act.astype(BF16), wd_ref[...])

  lax.cond(pl.program_id(1) == 0, lambda: step(True), lambda: step(False))


def _ffn(x, g, wg, wu, wd_half, jobs=()):
  m, d = x.shape
  dff = wg.shape[1]
  tm, tf = FFN_ROW_TILE, FFN_FF_TILE
  side_arrays, side_in_specs, side_out_specs, side_out_shapes = _side_args(jobs)
  return pl.pallas_call(
      functools.partial(_ffn_kernel, jobs=tuple(jobs)),
      grid=(m // tm, dff // tf),
      in_specs=[
          pl.BlockSpec((tm, d), lambda i, j: (i, 0)),
          pl.BlockSpec((1, d), lambda i, j: (0, 0)),
          pl.BlockSpec((d, tf), lambda i, j: (0, j)),
          pl.BlockSpec((d, tf), lambda i, j: (0, j)),
          pl.BlockSpec((tf, d), lambda i, j: (j, 0)),
      ] + side_in_specs,
      out_specs=[pl.BlockSpec((tm, d), lambda i, j: (i, 0))] + side_out_specs,
      out_shape=[jax.ShapeDtypeStruct((m, d), F32)] + side_out_shapes,
      scratch_shapes=[pltpu.VMEM((tm, d), BF16)],
      compiler_params=_params("parallel", "arbitrary"),
      name="ffn",
  )(x, g, wg, wu, wd_half, *side_arrays)


def _ffn_cast_jobs(m, w_gate, w_up, w_down):
  ni = m // FFN_ROW_TILE
  d, dff = w_gate.shape
  tf = FFN_FF_TILE
  assert d % ni == 0
  return [
      _cast_job(w_gate, (d // ni, tf), lambda i, j: (i, j)),
      _cast_job(w_up, (d // ni, tf), lambda i, j: (i, j)),
      _cast_job(w_down, (tf, d // ni), lambda i, j: (j, i), scale=0.5),
  ]


def _row_cast_jobs(m, *weights):
  ni = m // FFN_ROW_TILE
  return [_cast_job(w, (w.shape[0] // ni, w.shape[1]), lambda i, j: (i, 0)) for w in weights]


def _w_in_repack_jobs(m, w_t, n_a, n_f, n_b):
  ni = m // FFN_ROW_TILE
  d = w_t.shape[1]
  cols, tc = d // ni, FFN_FF_TILE
  assert n_a % tc == 0 and n_b % tc == 0 and n_f % 8 == 0 and n_f <= V7X_LANES
  assert cols % V7X_LANES == 0
  chunks = n_a // tc
  assert n_b // tc == chunks
  cj = lambda j: jnp.minimum(j, chunks - 1)

  def transposed(ref):
    return ref[...].T

  def shifted_transposed(main_ref, tail_ref):
    return jnp.concatenate([main_ref[n_f:, :], tail_ref[...]], axis=0).T

  def gate_rows_transposed(ref):
    row = lax.broadcasted_iota(jnp.int32, ref.shape, 0)
    return jnp.where(row < n_f, ref[...], 0.0).T

  out = lambda n: jax.ShapeDtypeStruct((d, n), BF16)
  out_spec = pl.BlockSpec((cols, tc), lambda i, j: (i, cj(j)))
  return [
      _SideJob((w_t,), (pl.BlockSpec((tc, cols), lambda i, j: (cj(j), i)),),
               out(n_a), out_spec, transposed),
      _SideJob((w_t, w_t),
               (pl.BlockSpec((tc, cols), lambda i, j: (chunks + cj(j), i)),
                pl.BlockSpec((n_f, cols), lambda i, j: ((n_a + (cj(j) + 1) * tc) // n_f, i))),
               out(n_b), out_spec, shifted_transposed),
      _SideJob((w_t,), (pl.BlockSpec((V7X_LANES, cols), lambda i, j: (n_a // V7X_LANES, i)),),
               out(V7X_LANES), pl.BlockSpec((cols, V7X_LANES), lambda i, j: (i, 0)),
               gate_rows_transposed),
  ]


def _swap_matrix():
  i = np.arange(PERM_ROWS)
  src = (i % RESIDUES) * RESIDUES + i // RESIDUES
  return jnp.asarray(np.eye(PERM_ROWS, dtype=np.float32)[src], BF16)


def _norm_matmul_kernel(x_ref, g_ref, w_ref, *rest, permute, narrow, jobs, key_cols):
  rest = list(rest)
  swap_ref = rest.pop(0) if permute else None
  wn_ref = rest.pop(0) if narrow else None
  side_in = [rest.pop(0) for job in jobs for _ in job.arrays]
  o_ref = rest.pop(0)
  on_ref = rest.pop(0) if narrow else None
  side_out = [rest.pop(0) for _ in jobs]
  assert not rest, "unexpected extra refs"
  tm = x_ref.shape[0]
  _run_side_jobs(jobs, side_in, side_out)

  h = _rmsnorm(x_ref[...], g_ref[...]).astype(BF16)
  if permute:
    h = jnp.concatenate(
        [_dot(swap_ref[...], h[a * PERM_ROWS:(a + 1) * PERM_ROWS, :]).astype(BF16)
         for a in range(tm // PERM_ROWS)], axis=0)
  if narrow:
    on_ref[...] = _dot(h, wn_ref[...])
  acc = _dot(h, w_ref[...])
  lo, hi = key_cols
  res = jnp.concatenate([acc[:, :lo], acc[:, lo:hi] * (SCALE * LOG2_E), acc[:, hi:]],
                        axis=1).astype(o_ref.dtype)
  if permute:
    per = PERM_ROWS // RESIDUES
    for a in range(tm // PERM_ROWS):
      for r in range(RESIDUES):
        start = a * PERM_ROWS + r * per
        o_ref[r, a * per:(a + 1) * per, :] = res[start:start + per, :]
  else:
    o_ref[...] = res


def _norm_matmul(x, g, w, out_dtype, key_cols, residue_major=None, w_narrow=None, jobs=()):
  m, d = x.shape
  n = w.shape[1]
  tm = PROJ_ROW_TILE
  narrow = w_narrow is not None
  permute = residue_major is not None
  assert not (narrow and permute)
  in_specs = [
      pl.BlockSpec((tm, d), lambda i: (i, 0)),
      pl.BlockSpec((1, d), lambda i: (0, 0)),
      pl.BlockSpec((d, n), lambda i: (0, 0)),
  ]
  args = [x, g, w]
  if permute:
    batch, seq = residue_major
    tiles = seq // tm
    in_specs.append(pl.BlockSpec((PERM_ROWS, PERM_ROWS), lambda i: (0, 0)))
    args.append(_swap_matrix())
    out_specs = [pl.BlockSpec((RESIDUES, tm // RESIDUES, n), lambda i: (i // tiles, i % tiles, 0))]
    out_shape = [jax.ShapeDtypeStruct((batch * RESIDUES, seq // RESIDUES, n), out_dtype)]
  else:
    out_specs = [pl.BlockSpec((tm, n), lambda i: (i, 0))]
    out_shape = [jax.ShapeDtypeStruct((m, n), out_dtype)]
  if narrow:
    in_specs.append(pl.BlockSpec(w_narrow.shape, lambda i: (0, 0)))
    args.append(w_narrow)
    out_specs.append(pl.BlockSpec((tm, w_narrow.shape[1]), lambda i: (i, 0)))
    out_shape.append(jax.ShapeDtypeStruct((m, w_narrow.shape[1]), F32))
  side_arrays, side_in_specs, side_out_specs, side_out_shapes = _side_args(jobs)
  outs = pl.pallas_call(
      functools.partial(_norm_matmul_kernel, permute=permute, narrow=narrow, jobs=tuple(jobs),
                        key_cols=key_cols),
      grid=(m // tm,),
      in_specs=in_specs + side_in_specs,
      out_specs=out_specs + side_out_specs,
      out_shape=out_shape + side_out_shapes,
      compiler_params=_params("parallel"),
      name="norm_matmul",
  )(*args, *side_arrays)
  outs = list(outs)
  if permute:
    outs[0] = outs[0].reshape(m, n)
  return outs


def _cumsum_kernel(fl_ref, bf_ref, c_ref):
  s = fl_ref.shape[0]
  z = fl_ref[...] + bf_ref[...]
  logf = jnp.minimum(z, 0.0) - jnp.log1p(jnp.exp(-jnp.abs(z)))
  lt = logf.T[0:N_HEADS_FOX, :]
  row = lax.broadcasted_iota(jnp.int32, (V7X_LANES, V7X_LANES), 0)
  col = lax.broadcasted_iota(jnp.int32, (V7X_LANES, V7X_LANES), 1)
  upper = (row <= col).astype(F32)
  carry = jnp.zeros((N_HEADS_FOX, 1), F32)
  for j in range(s // V7X_LANES):
    blk = lt[:, j * V7X_LANES:(j + 1) * V7X_LANES]
    cs = jnp.dot(blk, upper, preferred_element_type=F32,
                 precision=lax.Precision.HIGHEST) + carry
    c_ref[:, j * V7X_LANES:(j + 1) * V7X_LANES] = cs
    carry = cs[:, V7X_LANES - 1:V7X_LANES]


def _fox_decay(f_logit, b_f_row, batch, seq):
  return pl.pallas_call(
      _cumsum_kernel,
      grid=(batch,),
      in_specs=[
          pl.BlockSpec((seq, V7X_LANES), lambda b: (b, 0)),
          pl.BlockSpec((1, V7X_LANES), lambda b: (0, 0)),
      ],
      out_specs=pl.BlockSpec((None, N_HEADS_FOX, seq), lambda b: (b, 0, 0)),
      out_shape=jax.ShapeDtypeStruct((batch, N_HEADS_FOX, seq), F32),
      compiler_params=_params("parallel"),
      name="fox_decay",
  )(f_logit, b_f_row)


def _fox_kernel(q_ref, k_ref, v_ref, c_ref, o_ref):
  seq = q_ref.shape[0]
  tq = FOX_Q_TILE
  h = pl.program_id(1)
  crow = c_ref[pl.ds(h, 1), :] * LOG2_E
  row = lax.broadcasted_iota(jnp.int32, (tq, tq), 0)
  col = lax.broadcasted_iota(jnp.int32, (tq, tq), 1)
  diag_mask = jnp.where(col > row, NEG_INF, 0.0).astype(F32)

  def scores(i):
    t0, t1 = i * tq, (i + 1) * tq
    q = q_ref[t0:t1, :]
    bias = crow[:, t1 - 1:t1] - crow[:, 0:t1]
    s_diag = _dot_nt(q, k_ref[t0:t1, :]) + bias[:, t0:t1] + diag_mask
    s_off = _dot_nt(q, k_ref[0:t0, :]) + bias[:, 0:t0] if i > 0 else None
    return s_diag, s_off

  def finish(i, s_diag, s_off):
    t0, t1 = i * tq, (i + 1) * tq
    m = jnp.max(s_diag, axis=-1, keepdims=True)
    if i > 0:
      m = jnp.maximum(m, jnp.max(s_off, axis=-1, keepdims=True))
    o, l = _weighted_values(jnp.exp2(s_diag - m), v_ref[t0:t1, :])
    if i > 0:
      o_off, l_off = _weighted_values(jnp.exp2(s_off - m), v_ref[0:t0, :])
      o, l = o + o_off, l + l_off
    o_ref[t0:t1, :] = (o / l).astype(o_ref.dtype)

  n_tiles = seq // tq
  pending = [scores(i) for i in range(min(FOX_LOOKAHEAD, n_tiles))]
  for i in range(n_tiles):
    if i + FOX_LOOKAHEAD < n_tiles:
      pending.append(scores(i + FOX_LOOKAHEAD))
    finish(i, *pending.pop(0))


def _fox_attention(u_a, c, batch, seq):
  blk = lambda off: pl.BlockSpec((seq, HEAD_DIM), lambda b, h: (b, off + h))
  return pl.pallas_call(
      _fox_kernel,
      grid=(batch, N_HEADS_FOX),
      in_specs=[
          blk(0), blk(N_HEADS_FOX), blk(2 * N_HEADS_FOX),
          pl.BlockSpec((None, N_HEADS_FOX, seq), lambda b, h: (b, 0, 0)),
      ],
      out_specs=pl.BlockSpec((seq, HEAD_DIM), lambda b, h: (b, h)),
      out_shape=jax.ShapeDtypeStruct((batch * seq, D_FOX), BF16),
      compiler_params=_params("parallel", "arbitrary"),
      name="fox_attention",
  )(u_a, u_a, u_a, c)


def _t5_bucket_np(dist):
  max_exact = N_REL_BUCKETS // 2
  d = np.maximum(dist, 1).astype(np.float32)
  large = max_exact + (np.log(d / np.float32(max_exact))
                       / np.float32(math.log(REL_MAX_DISTANCE / max_exact))
                       * np.float32(N_REL_BUCKETS - max_exact)).astype(np.int32)
  large = np.minimum(large, N_REL_BUCKETS - 1)
  return np.where(dist < max_exact, dist, large).astype(np.int32)


def _block_positions(dilation):
  n = WINDOW_KEYS
  m = RESIDUES // dilation
  rows = n // m
  j = np.arange(m)[:, None]
  qpos = (n + m * np.arange(rows)[None, :] + j).reshape(-1)
  kpos = (m * np.arange(2 * rows)[None, :] + j).reshape(-1)
  return qpos, kpos


def _band_buckets():
  n = WINDOW_KEYS
  tiles = []
  for _, dilation in DILATED_PATTERNS:
    qpos, kpos = _block_positions(dilation)
    rel = qpos[:, None] - np.concatenate([kpos, qpos])[None, :]
    valid = (rel >= 0) & (rel <= n)
    bucket = _t5_bucket_np(np.maximum(rel, 0) * dilation)
    tiles.append(np.where(valid, bucket, -1))
  return np.stack(tiles).astype(np.int32)


def _bias_kernel(tab_ref, bkt_ref, o_ref):
  bkt = bkt_ref[...]
  for h in range(N_HEADS_DIL):
    acc = jnp.full(bkt.shape, NEG_INF, F32)
    for b in range(N_REL_BUCKETS):
      acc = jnp.where(bkt == b, tab_ref[b, h] * LOG2_E, acc)
    o_ref[h] = acc


def _band_bias(rel_table):
  buckets = jnp.asarray(_band_buckets())
  p, n, n2 = buckets.shape
  return pl.pallas_call(
      _bias_kernel,
      grid=(p,),
      in_specs=[
          pl.BlockSpec(memory_space=pltpu.SMEM),
          pl.BlockSpec((None, n, n2), lambda i: (i, 0, 0)),
      ],
      out_specs=pl.BlockSpec((None, N_HEADS_DIL, n, n2), lambda i: (i, 0, 0, 0)),
      out_shape=jax.ShapeDtypeStruct((p, N_HEADS_DIL, n, n2), F32),
      compiler_params=_params("parallel"),
      name="band_bias",
  )(rel_table, buckets)


def _dil_kernel(q_ref, k_ref, v_ref, bm_ref, o_ref, *scratch):
  seq = q_ref.shape[0]
  n = WINDOW_KEYS
  seg = seq // RESIDUES
  last = len(DILATED_PATTERNS) - 1
  assert DILATED_PATTERNS[last][1] == RESIDUES and n == seg
  accs, lses = scratch[:last], scratch[last:]

  def gather(ref, starts, size):
    return jnp.concatenate([ref[st:st + size, :] for st in starts], axis=0).astype(BF16)

  def block_rows(segments, nb):
    rows = n // len(segments)
    q_starts = [s * seg + nb * rows for s in segments]
    if nb == 0:
      return rows, q_starts, q_starts, rows
    return rows, q_starts, [st - rows for st in q_starts], 2 * rows

  def scores(p, segments, nb):
    rows, q_starts, k_starts, k_rows = block_rows(segments, nb)
    bm = bm_ref[p, :, 2 * n:3 * n] if nb == 0 else bm_ref[p, :, 0:2 * n]
    return _dot_nt(gather(q_ref, q_starts, rows), gather(k_ref, k_starts, k_rows)) + bm

  def finish(p, segments, nb, e, m):
    rows, q_starts, k_starts, k_rows = block_rows(segments, nb)
    o, l = _weighted_values(e, gather(v_ref, k_starts, k_rows))
    o = o / l
    lse = m + jnp.log2(l)
    if p < last:
      for j, st in enumerate(q_starts):
        accs[p][st:st + rows, :] = o[j * rows:(j + 1) * rows, :]
        lses[p][st:st + rows, :] = lse[j * rows:(j + 1) * rows, :]
      return
    seg_rows = slice(q_starts[0], q_starts[0] + n)
    all_lse = [ref[seg_rows, :] for ref in lses] + [lse]
    all_out = [ref[seg_rows, :] for ref in accs] + [o]
    top = functools.reduce(jnp.maximum, all_lse)
    weights = [jnp.exp2(x - top) for x in all_lse]
    add = lambda a, b: a + b
    mixed = (functools.reduce(add, [w * a for w, a in zip(weights, all_out)])
             / functools.reduce(add, weights))
    o_ref[seg_rows, :] = mixed.astype(o_ref.dtype)

  blocks = [(p, list(range(r, RESIDUES, d)), nb)
            for p, (_, d) in enumerate(DILATED_PATTERNS)
            for r in range(d) for nb in range(seq // (n * d))]
  for g in range(0, len(blocks), DIL_BLOCK_GROUP):
    group = blocks[g:g + DIL_BLOCK_GROUP]
    ss = [scores(*blk) for blk in group]
    ms = [jnp.max(s, axis=-1, keepdims=True) for s in ss]
    es = [jnp.exp2(s - m) for s, m in zip(ss, ms)]
    for blk, e, m in zip(group, es, ms):
      finish(*blk, e, m)


def _dilated_attention(u_b, band_bias, batch, seq):
  assert seq // RESIDUES == WINDOW_KEYS
  blk = lambda off: pl.BlockSpec((seq, HEAD_DIM), lambda b, h: (b, off + h))
  n_pat = len(DILATED_PATTERNS)
  scratch = [pltpu.VMEM((seq, HEAD_DIM), F32) for _ in range(2 * (n_pat - 1))]
  return pl.pallas_call(
      _dil_kernel,
      grid=(batch, N_HEADS_DIL),
      in_specs=[
          blk(0), blk(N_HEADS_DIL), blk(2 * N_HEADS_DIL),
          pl.BlockSpec((n_pat, None, WINDOW_KEYS, 3 * WINDOW_KEYS), lambda b, h: (0, h, 0, 0)),
      ],
      out_specs=pl.BlockSpec((seq, HEAD_DIM), lambda b, h: (b, h)),
      out_shape=jax.ShapeDtypeStruct((batch * seq, D_DIL), BF16),
      scratch_shapes=scratch,
      compiler_params=_params("parallel", "arbitrary"),
      name="dilated_attention",
  )(u_b, u_b, u_b, band_bias)


def _out_proj_kernel(x_ref, a_ref, b_ref, swap_ref, wa_ref, wb_ref, o_ref):
  per = PERM_ROWS // RESIDUES
  for a in range(x_ref.shape[0] // PERM_ROWS):
    rows = slice(a * PERM_ROWS, (a + 1) * PERM_ROWS)
    slab = jnp.concatenate([b_ref[r, a * per:(a + 1) * per, :] for r in range(RESIDUES)], axis=0)
    o_b = _dot(swap_ref[...], slab).astype(BF16)
    o_ref[rows, :] = x_ref[rows, :] + _dot(a_ref[rows, :], wa_ref[...]) + _dot(o_b, wb_ref[...])


def _out_proj(x, o_a, o_b, wo, batch, seq):
  m, d = x.shape
  tm = PROJ_ROW_TILE
  tiles = seq // tm
  ca, cb = o_a.shape[1], o_b.shape[1]
  assert ca == cb and wo.shape[0] == ca + cb
  resident = pl.Buffered(1)
  return pl.pallas_call(
      _out_proj_kernel,
      grid=(m // tm,),
      in_specs=[
          pl.BlockSpec((tm, d), lambda i: (i, 0)),
          pl.BlockSpec((tm, ca), lambda i: (i, 0)),
          pl.BlockSpec((RESIDUES, tm // RESIDUES, cb), lambda i: (i // tiles, i % tiles, 0)),
          pl.BlockSpec((PERM_ROWS, PERM_ROWS), lambda i: (0, 0)),
          pl.BlockSpec((ca, d), lambda i: (0, 0), pipeline_mode=resident),
          pl.BlockSpec((cb, d), lambda i: (1, 0), pipeline_mode=resident),
      ],
      out_specs=pl.BlockSpec((tm, d), lambda i: (i, 0)),
      out_shape=jax.ShapeDtypeStruct((m, d), F32),
      compiler_params=_params("parallel"),
      name="out_proj",
  )(x, o_a, o_b.reshape(batch * RESIDUES, seq // RESIDUES, cb), _swap_matrix(), wo, wo)


def _ple_kernel(x_ref, p_ref, g_ref, wg_ref, wp_ref, gf_ref, o_ref, *, final_norm):
  x = x_ref[...]
  h = _rmsnorm(x, g_ref[...]).astype(BF16)
  gate = jax.nn.sigmoid(_dot(h, wg_ref[...]))
  y = x + gate * _dot(p_ref[...].astype(BF16), wp_ref[...])
  o_ref[...] = _rmsnorm(y, gf_ref[...]) if final_norm else y


def _ple(x, p, g, w_gate, w_proj, g_final, final_norm):
  m, d = x.shape
  tm = PROJ_ROW_TILE
  return pl.pallas_call(
      functools.partial(_ple_kernel, final_norm=final_norm),
      grid=(m // tm,),
      in_specs=[
          pl.BlockSpec((tm, d), lambda i: (i, 0)),
          pl.BlockSpec((tm, p.shape[1]), lambda i: (i, 0)),
          pl.BlockSpec((1, d), lambda i: (0, 0)),
          pl.BlockSpec(w_gate.shape, lambda i: (0, 0)),
          pl.BlockSpec(w_proj.shape, lambda i: (0, 0)),
          pl.BlockSpec((1, d), lambda i: (0, 0)),
      ],
      out_specs=pl.BlockSpec((tm, d), lambda i: (i, 0)),
      out_shape=jax.ShapeDtypeStruct((m, d), F32),
      compiler_params=_params("parallel"),
      name="ple",
  )(x, p, g, w_gate, w_proj, g_final)


def kernel(x, p, norm_ffn1, ffn1_w_gate, ffn1_w_up, ffn1_w_down, norm_mix, w_in, b_f, w_o,
           norm_ffn2, ffn2_w_gate, ffn2_w_up, ffn2_w_down, norm_ple, w_ple_gate, w_ple_proj,
           rel_table, norm_final):
  batch, seq, d = x.shape
  depth = p.shape[0]
  m = batch * seq
  bf = lambda w: w.astype(BF16)
  row = lambda g: g.reshape(1, -1).astype(F32)

  band_bias = _band_bias(rel_table.astype(F32))
  xs = x.reshape(m, d).astype(F32)
  for i in range(depth):
    xs, wg2, wu2, wd2, w_a, w_b, w_f = _ffn(
        xs, row(norm_ffn1[i]), bf(ffn1_w_gate[i]), bf(ffn1_w_up[i]), bf(0.5 * ffn1_w_down[i]),
        jobs=_ffn_cast_jobs(m, ffn2_w_gate[i], ffn2_w_up[i], ffn2_w_down[i])
        + _w_in_repack_jobs(m, jnp.swapaxes(w_in[i], 0, 1), 3 * D_FOX, N_HEADS_FOX, 3 * D_DIL))

    b_f_row = jnp.pad(b_f[i].astype(F32), (0, V7X_LANES - N_HEADS_FOX)).reshape(1, V7X_LANES)
    g_mix = row(norm_mix[i])
    wo_rows = w_o[i].shape[0] // (m // PROJ_ROW_TILE)
    u_a, f_logit, wo = _norm_matmul(
        xs, g_mix, w_a, BF16, (D_FOX, 2 * D_FOX), w_narrow=w_f,
        jobs=[_cast_job(w_o[i], (wo_rows, w_o[i].shape[1]), lambda t: (t, 0))])
    u_b, = _norm_matmul(xs, g_mix, w_b, F32, (D_DIL, 2 * D_DIL), residue_major=(batch, seq))

    c = _fox_decay(f_logit, b_f_row, batch, seq)
    o_a = _fox_attention(u_a, c, batch, seq)
    o_b = _dilated_attention(u_b, band_bias, batch, seq)
    xs = _out_proj(xs, o_a, o_b, wo, batch, seq)

    xs, w_gate, w_ple = _ffn(xs, row(norm_ffn2[i]), wg2, wu2, wd2,
                             jobs=_row_cast_jobs(m, w_ple_gate[i], w_ple_proj[i]))
    last = i == depth - 1
    xs = _ple(xs, p[i].reshape(m, -1), row(norm_ple[i]), w_gate, w_ple,
              row(norm_final), final_norm=last)
  return xs.reshape(batch, seq, d).astype(x.dtype)
```

```python
import functools
import math
from typing import Any, Callable, NamedTuple

import jax
import jax.numpy as jnp
import numpy as np
from jax import lax
from jax.experimental import pallas as pl
from jax.experimental.pallas import tpu as pltpu

F32 = jnp.float32
BF16 = jnp.bfloat16

HEAD_DIM = 128
N_HEADS_FOX = 8
N_HEADS_DIL = 8
D_FOX = N_HEADS_FOX * HEAD_DIM
D_DIL = N_HEADS_DIL * HEAD_DIM
DILATED_PATTERNS = ((128, 1), (512, 4), (2048, 16))
WINDOW_KEYS = 128
N_REL_BUCKETS = 32
REL_MAX_DISTANCE = 2048
RMS_EPS = 1e-6
NEG_INF = -1e30
SCALE = HEAD_DIM ** -0.5
LOG2_E = math.log2(math.e)

V7X_LANES = 128
V7X_VMEM_LIMIT_BYTES = 56 * 1024 * 1024

RESIDUES = max(d for _, d in DILATED_PATTERNS)
PERM_ROWS = RESIDUES * RESIDUES

FFN_ROW_TILE = 1024
FFN_SUB_ROWS = 512
FFN_FF_TILE = 512
FFN_HEAD_FF_TILE = 256
PROJ_ROW_TILE = 512
FOX_Q_TILE = 256
FOX_LOOKAHEAD = 2
DIL_BLOCK_GROUP = 12


def _params(*semantics):
  return pltpu.CompilerParams(dimension_semantics=semantics,
                              vmem_limit_bytes=V7X_VMEM_LIMIT_BYTES)


def _rmsnorm(x, g):
  ms = jnp.mean(x * x, axis=-1, keepdims=True)
  return x * lax.rsqrt(ms + RMS_EPS) * g


def _dot(a, b):
  return jnp.dot(a, b, preferred_element_type=F32)


def _dot_nt(a, b):
  return lax.dot_general(a, b, (((1,), (1,)), ((), ())), preferred_element_type=F32)


def _weighted_values(e, v):
  both = _dot(e.astype(BF16), jnp.concatenate([v, jnp.ones_like(v)], axis=1))
  return both[:, :HEAD_DIM], both[:, HEAD_DIM:]


class _SideJob(NamedTuple):
  arrays: tuple
  in_specs: tuple
  out_shape: Any
  out_spec: Any
  fn: Callable


def _cast_job(a, block, index, scale=1.0):
  spec = pl.BlockSpec(block, index)
  return _SideJob((a,), (spec,), jax.ShapeDtypeStruct(a.shape, BF16), spec,
                  lambda r: r[...] * scale)


def _run_side_jobs(jobs, in_refs, out_refs):
  in_refs = list(in_refs)
  for job, out_ref in zip(jobs, out_refs):
    refs = [in_refs.pop(0) for _ in job.arrays]
    out_ref[...] = job.fn(*refs).astype(out_ref.dtype)


def _side_args(jobs):
  arrays = [a for job in jobs for a in job.arrays]
  in_specs = [s for job in jobs for s in job.in_specs]
  return arrays, in_specs, [job.out_spec for job in jobs], [job.out_shape for job in jobs]


def _ffn_head_kernel(x_ref, g_ref, wg_ref, wu_ref, wd_ref, o_ref, og_ref, ou_ref, od_ref, h_ref):
  @pl.when(pl.program_id(0) == 0)
  def _():
    x = x_ref[...]
    h_ref[...] = _rmsnorm(x, g_ref[...]).astype(BF16)
    o_ref[...] = x

  wg = wg_ref[...].astype(BF16)
  wu = wu_ref[...].astype(BF16)
  wd = (wd_ref[...] * 0.5).astype(BF16)
  og_ref[...] = wg
  ou_ref[...] = wu
  od_ref[...] = wd
  for r in range(h_ref.shape[0] // FFN_SUB_ROWS):
    rows = slice(r * FFN_SUB_ROWS, (r + 1) * FFN_SUB_ROWS)
    h = h_ref[rows, :]
    gate = _dot(h, wg)
    up = _dot(h, wu)
    act = (gate * jax.nn.sigmoid(gate)) * up
    o_ref[rows, :] += _dot(act.astype(BF16), wd)


def _ffn_head(x, g, w_gate, w_up, w_down):
  d = x.shape[1]
  dff = w_gate.shape[1]
  tm, tf = FFN_ROW_TILE, FFN_HEAD_FF_TILE
  return pl.pallas_call(
      _ffn_head_kernel,
      grid=(dff // tf,),
      in_specs=[
          pl.BlockSpec((tm, d), lambda j: (0, 0), pipeline_mode=pl.Buffered(1)),
          pl.BlockSpec((1, d), lambda j: (0, 0)),
          pl.BlockSpec((d, tf), lambda j: (0, j)),
          pl.BlockSpec((d, tf), lambda j: (0, j)),
          pl.BlockSpec((tf, d), lambda j: (j, 0)),
      ],
      out_specs=[
          pl.BlockSpec((tm, d), lambda j: (0, 0)),
          pl.BlockSpec((d, tf), lambda j: (0, j)),
          pl.BlockSpec((d, tf), lambda j: (0, j)),
          pl.BlockSpec((tf, d), lambda j: (j, 0)),
      ],
      out_shape=[
          jax.ShapeDtypeStruct((tm, d), F32),
          jax.ShapeDtypeStruct(w_gate.shape, BF16),
          jax.ShapeDtypeStruct(w_up.shape, BF16),
          jax.ShapeDtypeStruct(w_down.shape, BF16),
      ],
      scratch_shapes=[pltpu.VMEM((tm, d), BF16)],
      compiler_params=_params("arbitrary"),
      name="ffn_head",
  )(x, g, w_gate, w_up, w_down)


def _ffn_kernel(x_ref, g_ref, wg_ref, wu_ref, wd_ref, *rest, jobs, has_head):
  rest = list(rest)
  head_ref = rest.pop(0) if has_head else None
  copy_sem = rest.pop() if has_head else None
  n_in = sum(len(job.arrays) for job in jobs)
  side_in, o_ref, side_out, h_ref = rest[:n_in], rest[n_in], rest[n_in + 1:-1], rest[-1]

  def step(first):
    _run_side_jobs(jobs, side_in, side_out)
    for r in range(h_ref.shape[0] // FFN_SUB_ROWS):
      rows = slice(r * FFN_SUB_ROWS, (r + 1) * FFN_SUB_ROWS)
      if first:
        base = x_ref[rows, :]
        h = _rmsnorm(base, g_ref[...]).astype(BF16)
        h_ref[rows, :] = h
      else:
        base = o_ref[rows, :]
        h = h_ref[rows, :]
      gate = _dot(h, wg_ref[...])
      up = _dot(h, wu_ref[...])
      act = (gate * jax.nn.sigmoid(gate)) * up
      o_ref[rows, :] = base + _dot(act.astype(BF16), wd_ref[...])

  def compute_tile():
    lax.cond(pl.program_id(1) == 0, lambda: step(True), lambda: step(False))

  def copy_head_tile():
    _run_side_jobs(jobs, side_in, side_out)

    @pl.when(pl.program_id(1) == 0)
    def _():
      copy = pltpu.make_async_copy(head_ref, o_ref, copy_sem)
      copy.start()
      copy.wait()

  if has_head:
    lax.cond(pl.program_id(0) == 0, copy_head_tile, compute_tile)
  else:
    compute_tile()


def _ffn(x, g, wg, wu, wd_half, jobs=(), head=None):
  m, d = x.shape
  dff = wg.shape[1]
  tm, tf = FFN_ROW_TILE, FFN_FF_TILE
  has_head = head is not None
  chunk = (lambda i, j: jnp.where(i == 0, 0, j)) if has_head else (lambda i, j: j)
  side_arrays, side_in_specs, side_out_specs, side_out_shapes = _side_args(jobs)
  head_args = [head] if has_head else []
  head_specs = [pl.BlockSpec(memory_space=pl.ANY)] if has_head else []
  head_scratch = [pltpu.SemaphoreType.DMA(())] if has_head else []
  return pl.pallas_call(
      functools.partial(_ffn_kernel, jobs=tuple(jobs), has_head=has_head),
      grid=(m // tm, dff // tf),
      in_specs=[
          pl.BlockSpec((tm, d), lambda i, j: (i, 0)),
          pl.BlockSpec((1, d), lambda i, j: (0, 0)),
          pl.BlockSpec((d, tf), lambda i, j: (0, chunk(i, j))),
          pl.BlockSpec((d, tf), lambda i, j: (0, chunk(i, j))),
          pl.BlockSpec((tf, d), lambda i, j: (chunk(i, j), 0)),
      ] + head_specs + side_in_specs,
      out_specs=[pl.BlockSpec((tm, d), lambda i, j: (i, 0))] + side_out_specs,
      out_shape=[jax.ShapeDtypeStruct((m, d), F32)] + side_out_shapes,
      scratch_shapes=[pltpu.VMEM((tm, d), BF16)] + head_scratch,
      compiler_params=_params("arbitrary" if has_head else "parallel", "arbitrary"),
      name="ffn",
  )(x, g, wg, wu, wd_half, *head_args, *side_arrays)


def _ffn_cast_jobs(m, w_gate, w_up, w_down):
  ni = m // FFN_ROW_TILE
  d, dff = w_gate.shape
  tf = FFN_FF_TILE
  assert d % ni == 0
  return [
      _cast_job(w_gate, (d // ni, tf), lambda i, j: (i, j)),
      _cast_job(w_up, (d // ni, tf), lambda i, j: (i, j)),
      _cast_job(w_down, (tf, d // ni), lambda i, j: (j, i), scale=0.5),
  ]


def _row_cast_jobs(m, *weights):
  ni = m // FFN_ROW_TILE
  return [_cast_job(w, (w.shape[0] // ni, w.shape[1]), lambda i, j: (i, 0)) for w in weights]


def _w_in_repack_jobs(m, w_t, n_a, n_f, n_b):
  ni = m // FFN_ROW_TILE
  d = w_t.shape[1]
  cols, tc = d // ni, FFN_FF_TILE
  assert n_a % tc == 0 and n_b % tc == 0 and n_f % 8 == 0 and n_f <= V7X_LANES
  assert cols % V7X_LANES == 0
  chunks = n_a // tc
  assert n_b // tc == chunks
  cj = lambda j: jnp.minimum(j, chunks - 1)

  def transposed(ref):
    return ref[...].T

  def shifted_transposed(main_ref, tail_ref):
    return jnp.concatenate([main_ref[n_f:, :], tail_ref[...]], axis=0).T

  def gate_rows_transposed(ref):
    row = lax.broadcasted_iota(jnp.int32, ref.shape, 0)
    return jnp.where(row < n_f, ref[...], 0.0).T

  out = lambda n: jax.ShapeDtypeStruct((d, n), BF16)
  out_spec = pl.BlockSpec((cols, tc), lambda i, j: (i, cj(j)))
  return [
      _SideJob((w_t,), (pl.BlockSpec((tc, cols), lambda i, j: (cj(j), i)),),
               out(n_a), out_spec, transposed),
      _SideJob((w_t, w_t),
               (pl.BlockSpec((tc, cols), lambda i, j: (chunks + cj(j), i)),
                pl.BlockSpec((n_f, cols), lambda i, j: ((n_a + (cj(j) + 1) * tc) // n_f, i))),
               out(n_b), out_spec, shifted_transposed),
      _SideJob((w_t,), (pl.BlockSpec((V7X_LANES, cols), lambda i, j: (n_a // V7X_LANES, i)),),
               out(V7X_LANES), pl.BlockSpec((cols, V7X_LANES), lambda i, j: (i, 0)),
               gate_rows_transposed),
  ]


def _swap_matrix():
  i = np.arange(PERM_ROWS)
  src = (i % RESIDUES) * RESIDUES + i // RESIDUES
  return jnp.asarray(np.eye(PERM_ROWS, dtype=np.float32)[src], BF16)


def _norm_matmul_kernel(x_ref, g_ref, w_ref, *rest, permute, narrow, jobs, key_cols):
  rest = list(rest)
  swap_ref = rest.pop(0) if permute else None
  wn_ref = rest.pop(0) if narrow else None
  side_in = [rest.pop(0) for job in jobs for _ in job.arrays]
  o_ref = rest.pop(0)
  on_ref = rest.pop(0) if narrow else None
  side_out = [rest.pop(0) for _ in jobs]
  assert not rest, "unexpected extra refs"
  tm = x_ref.shape[0]
  _run_side_jobs(jobs, side_in, side_out)

  h = _rmsnorm(x_ref[...], g_ref[...]).astype(BF16)
  if permute:
    h = jnp.concatenate(
        [_dot(swap_ref[...], h[a * PERM_ROWS:(a + 1) * PERM_ROWS, :]).astype(BF16)
         for a in range(tm // PERM_ROWS)], axis=0)
  if narrow:
    on_ref[...] = _dot(h, wn_ref[...])
  acc = _dot(h, w_ref[...])
  lo, hi = key_cols
  res = jnp.concatenate([acc[:, :lo], acc[:, lo:hi] * (SCALE * LOG2_E), acc[:, hi:]],
                        axis=1).astype(o_ref.dtype)
  if permute:
    per = PERM_ROWS // RESIDUES
    for a in range(tm // PERM_ROWS):
      for r in range(RESIDUES):
        start = a * PERM_ROWS + r * per
        o_ref[r, a * per:(a + 1) * per, :] = res[start:start + per, :]
  else:
    o_ref[...] = res


def _norm_matmul(x, g, w, out_dtype, key_cols, residue_major=None, w_narrow=None, jobs=()):
  m, d = x.shape
  n = w.shape[1]
  tm = PROJ_ROW_TILE
  narrow = w_narrow is not None
  permute = residue_major is not None
  assert not (narrow and permute)
  in_specs = [
      pl.BlockSpec((tm, d), lambda i: (i, 0)),
      pl.BlockSpec((1, d), lambda i: (0, 0)),
      pl.BlockSpec((d, n), lambda i: (0, 0)),
  ]
  args = [x, g, w]
  if permute:
    batch, seq = residue_major
    tiles = seq // tm
    in_specs.append(pl.BlockSpec((PERM_ROWS, PERM_ROWS), lambda i: (0, 0)))
    args.append(_swap_matrix())
    out_specs = [pl.BlockSpec((RESIDUES, tm // RESIDUES, n), lambda i: (i // tiles, i % tiles, 0))]
    out_shape = [jax.ShapeDtypeStruct((batch * RESIDUES, seq // RESIDUES, n), out_dtype)]
  else:
    out_specs = [pl.BlockSpec((tm, n), lambda i: (i, 0))]
    out_shape = [jax.ShapeDtypeStruct((m, n), out_dtype)]
  if narrow:
    in_specs.append(pl.BlockSpec(w_narrow.shape, lambda i: (0, 0)))
    args.append(w_narrow)
    out_specs.append(pl.BlockSpec((tm, w_narrow.shape[1]), lambda i: (i, 0)))
    out_shape.append(jax.ShapeDtypeStruct((m, w_narrow.shape[1]), F32))
  side_arrays, side_in_specs, side_out_specs, side_out_shapes = _side_args(jobs)
  outs = pl.pallas_call(
      functools.partial(_norm_matmul_kernel, permute=permute, narrow=narrow, jobs=tuple(jobs),
                        key_cols=key_cols),
      grid=(m // tm,),
      in_specs=in_specs + side_in_specs,
      out_specs=out_specs + side_out_specs,
      out_shape=out_shape + side_out_shapes,
      compiler_params=_params("parallel"),
      name="norm_matmul",
  )(*args, *side_arrays)
  outs = list(outs)
  if permute:
    outs[0] = outs[0].reshape(m, n)
  return outs


def _cumsum_kernel(fl_ref, bf_ref, c_ref):
  s = fl_ref.shape[0]
  z = fl_ref[...] + bf_ref[...]
  logf = jnp.minimum(z, 0.0) - jnp.log1p(jnp.exp(-jnp.abs(z)))
  lt = logf.T[0:N_HEADS_FOX, :]
  row = lax.broadcasted_iota(jnp.int32, (V7X_LANES, V7X_LANES), 0)
  col = lax.broadcasted_iota(jnp.int32, (V7X_LANES, V7X_LANES), 1)
  upper = (row <= col).astype(F32)
  carry = jnp.zeros((N_HEADS_FOX, 1), F32)
  for j in range(s // V7X_LANES):
    blk = lt[:, j * V7X_LANES:(j + 1) * V7X_LANES]
    cs = jnp.dot(blk, upper, preferred_element_type=F32,
                 precision=lax.Precision.HIGHEST) + carry
    c_ref[:, j * V7X_LANES:(j + 1) * V7X_LANES] = cs
    carry = cs[:, V7X_LANES - 1:V7X_LANES]


def _fox_decay(f_logit, b_f_row, batch, seq):
  return pl.pallas_call(
      _cumsum_kernel,
      grid=(batch,),
      in_specs=[
          pl.BlockSpec((seq, V7X_LANES), lambda b: (b, 0)),
          pl.BlockSpec((1, V7X_LANES), lambda b: (0, 0)),
      ],
      out_specs=pl.BlockSpec((None, N_HEADS_FOX, seq), lambda b: (b, 0, 0)),
      out_shape=jax.ShapeDtypeStruct((batch, N_HEADS_FOX, seq), F32),
      compiler_params=_params("parallel"),
      name="fox_decay",
  )(f_logit, b_f_row)


def _fox_kernel(q_ref, k_ref, v_ref, c_ref, o_ref):
  seq = q_ref.shape[0]
  tq = FOX_Q_TILE
  h = pl.program_id(1)
  crow = c_ref[pl.ds(h, 1), :] * LOG2_E
  row = lax.broadcasted_iota(jnp.int32, (tq, tq), 0)
  col = lax.broadcasted_iota(jnp.int32, (tq, tq), 1)
  diag_mask = jnp.where(col > row, NEG_INF, 0.0).astype(F32)

  def scores(i):
    t0, t1 = i * tq, (i + 1) * tq
    q = q_ref[t0:t1, :]
    bias = crow[:, t1 - 1:t1] - crow[:, 0:t1]
    s_diag = _dot_nt(q, k_ref[t0:t1, :]) + bias[:, t0:t1] + diag_mask
    s_off = _dot_nt(q, k_ref[0:t0, :]) + bias[:, 0:t0] if i > 0 else None
    return s_diag, s_off

  def finish(i, s_diag, s_off):
    t0, t1 = i * tq, (i + 1) * tq
    m = jnp.max(s_diag, axis=-1, keepdims=True)
    if i > 0:
      m = jnp.maximum(m, jnp.max(s_off, axis=-1, keepdims=True))
    o, l = _weighted_values(jnp.exp2(s_diag - m), v_ref[t0:t1, :])
    if i > 0:
      o_off, l_off = _weighted_values(jnp.exp2(s_off - m), v_ref[0:t0, :])
      o, l = o + o_off, l + l_off
    o_ref[t0:t1, :] = (o / l).astype(o_ref.dtype)

  n_tiles = seq // tq
  pending = [scores(i) for i in range(min(FOX_LOOKAHEAD, n_tiles))]
  for i in range(n_tiles):
    if i + FOX_LOOKAHEAD < n_tiles:
      pending.append(scores(i + FOX_LOOKAHEAD))
    finish(i, *pending.pop(0))


def _fox_attention(u_a, c, batch, seq):
  blk = lambda off: pl.BlockSpec((seq, HEAD_DIM), lambda b, h: (b, off + h))
  return pl.pallas_call(
      _fox_kernel,
      grid=(batch, N_HEADS_FOX),
      in_specs=[
          blk(0), blk(N_HEADS_FOX), blk(2 * N_HEADS_FOX),
          pl.BlockSpec((None, N_HEADS_FOX, seq), lambda b, h: (b, 0, 0)),
      ],
      out_specs=pl.BlockSpec((seq, HEAD_DIM), lambda b, h: (b, h)),
      out_shape=jax.ShapeDtypeStruct((batch * seq, D_FOX), BF16),
      compiler_params=_params("parallel", "arbitrary"),
      name="fox_attention",
  )(u_a, u_a, u_a, c)


def _t5_bucket_np(dist):
  max_exact = N_REL_BUCKETS // 2
  d = np.maximum(dist, 1).astype(np.float32)
  large = max_exact + (np.log(d / np.float32(max_exact))
                       / np.float32(math.log(REL_MAX_DISTANCE / max_exact))
                       * np.float32(N_REL_BUCKETS - max_exact)).astype(np.int32)
  large = np.minimum(large, N_REL_BUCKETS - 1)
  return np.where(dist < max_exact, dist, large).astype(np.int32)


def _block_positions(dilation):
  n = WINDOW_KEYS
  m = RESIDUES // dilation
  rows = n // m
  j = np.arange(m)[:, None]
  qpos = (n + m * np.arange(rows)[None, :] + j).reshape(-1)
  kpos = (m * np.arange(2 * rows)[None, :] + j).reshape(-1)
  return qpos, kpos


def _band_buckets():
  n = WINDOW_KEYS
  tiles = []
  for _, dilation in DILATED_PATTERNS:
    qpos, kpos = _block_positions(dilation)
    rel = qpos[:, None] - np.concatenate([kpos, qpos])[None, :]
    valid = (rel >= 0) & (rel <= n)
    bucket = _t5_bucket_np(np.maximum(rel, 0) * dilation)
    tiles.append(np.where(valid, bucket, -1))
  return np.stack(tiles).astype(np.int32)


def _bias_kernel(tab_ref, bkt_ref, o_ref):
  bkt = bkt_ref[...]
  for h in range(N_HEADS_DIL):
    acc = jnp.full(bkt.shape, NEG_INF, F32)
    for b in range(N_REL_BUCKETS):
      acc = jnp.where(bkt == b, tab_ref[b, h] * LOG2_E, acc)
    o_ref[h] = acc


def _band_bias(rel_table):
  buckets = jnp.asarray(_band_buckets())
  p, n, n2 = buckets.shape
  return pl.pallas_call(
      _bias_kernel,
      grid=(p,),
      in_specs=[
          pl.BlockSpec(memory_space=pltpu.SMEM),
          pl.BlockSpec((None, n, n2), lambda i: (i, 0, 0)),
      ],
      out_specs=pl.BlockSpec((None, N_HEADS_DIL, n, n2), lambda i: (i, 0, 0, 0)),
      out_shape=jax.ShapeDtypeStruct((p, N_HEADS_DIL, n, n2), F32),
      compiler_params=_params("parallel"),
      name="band_bias",
  )(rel_table, buckets)


def _dil_kernel(q_ref, k_ref, v_ref, bm_ref, o_ref, *scratch):
  seq = q_ref.shape[0]
  n = WINDOW_KEYS
  seg = seq // RESIDUES
  last = len(DILATED_PATTERNS) - 1
  assert DILATED_PATTERNS[last][1] == RESIDUES and n == seg
  accs, lses = scratch[:last], scratch[last:]

  def gather(ref, starts, size):
    return jnp.concatenate([ref[st:st + size, :] for st in starts], axis=0).astype(BF16)

  def block_rows(segments, nb):
    rows = n // len(segments)
    q_starts = [s * seg + nb * rows for s in segments]
    if nb == 0:
      return rows, q_starts, q_starts, rows
    return rows, q_starts, [st - rows for st in q_starts], 2 * rows

  def scores(p, segments, nb):
    rows, q_starts, k_starts, k_rows = block_rows(segments, nb)
    bm = bm_ref[p, :, 2 * n:3 * n] if nb == 0 else bm_ref[p, :, 0:2 * n]
    return _dot_nt(gather(q_ref, q_starts, rows), gather(k_ref, k_starts, k_rows)) + bm

  def finish(p, segments, nb, e, m):
    rows, q_starts, k_starts, k_rows = block_rows(segments, nb)
    o, l = _weighted_values(e, gather(v_ref, k_starts, k_rows))
    o = o / l
    lse = m + jnp.log2(l)
    if p < last:
      for j, st in enumerate(q_starts):
        accs[p][st:st + rows, :] = o[j * rows:(j + 1) * rows, :]
        lses[p][st:st + rows, :] = lse[j * rows:(j + 1) * rows, :]
      return
    seg_rows = slice(q_starts[0], q_starts[0] + n)
    all_lse = [ref[seg_rows, :] for ref in lses] + [lse]
    all_out = [ref[seg_rows, :] for ref in accs] + [o]
    top = functools.reduce(jnp.maximum, all_lse)
    weights = [jnp.exp2(x - top) for x in all_lse]
    add = lambda a, b: a + b
    mixed = (functools.reduce(add, [w * a for w, a in zip(weights, all_out)])
             / functools.reduce(add, weights))
    o_ref[seg_rows, :] = mixed.astype(o_ref.dtype)

  blocks = [(p, list(range(r, RESIDUES, d)), nb)
            for p, (_, d) in enumerate(DILATED_PATTERNS)
            for r in range(d) for nb in range(seq // (n * d))]
  for g in range(0, len(blocks), DIL_BLOCK_GROUP):
    group = blocks[g:g + DIL_BLOCK_GROUP]
    ss = [scores(*blk) for blk in group]
    ms = [jnp.max(s, axis=-1, keepdims=True) for s in ss]
    es = [jnp.exp2(s - m) for s, m in zip(ss, ms)]
    for blk, e, m in zip(group, es, ms):
      finish(*blk, e, m)


def _dilated_attention(u_b, band_bias, batch, seq):
  assert seq // RESIDUES == WINDOW_KEYS
  blk = lambda off: pl.BlockSpec((seq, HEAD_DIM), lambda b, h: (b, off + h))
  n_pat = len(DILATED_PATTERNS)
  scratch = [pltpu.VMEM((seq, HEAD_DIM), F32) for _ in range(2 * (n_pat - 1))]
  return pl.pallas_call(
      _dil_kernel,
      grid=(batch, N_HEADS_DIL),
      in_specs=[
          blk(0), blk(N_HEADS_DIL), blk(2 * N_HEADS_DIL),
          pl.BlockSpec((n_pat, None, WINDOW_KEYS, 3 * WINDOW_KEYS), lambda b, h: (0, h, 0, 0)),
      ],
      out_specs=pl.BlockSpec((seq, HEAD_DIM), lambda b, h: (b, h)),
      out_shape=jax.ShapeDtypeStruct((batch * seq, D_DIL), BF16),
      scratch_shapes=scratch,
      compiler_params=_params("parallel", "arbitrary"),
      name="dilated_attention",
  )(u_b, u_b, u_b, band_bias)


def _out_proj_kernel(x_ref, a_ref, b_ref, swap_ref, wa_ref, wb_ref, o_ref):
  per = PERM_ROWS // RESIDUES
  for a in range(x_ref.shape[0] // PERM_ROWS):
    rows = slice(a * PERM_ROWS, (a + 1) * PERM_ROWS)
    slab = jnp.concatenate([b_ref[r, a * per:(a + 1) * per, :] for r in range(RESIDUES)], axis=0)
    o_b = _dot(swap_ref[...], slab).astype(BF16)
    o_ref[rows, :] = x_ref[rows, :] + _dot(a_ref[rows, :], wa_ref[...]) + _dot(o_b, wb_ref[...])


def _out_proj(x, o_a, o_b, wo, batch, seq):
  m, d = x.shape
  tm = PROJ_ROW_TILE
  tiles = seq // tm
  ca, cb = o_a.shape[1], o_b.shape[1]
  assert ca == cb and wo.shape[0] == ca + cb
  resident = pl.Buffered(1)
  return pl.pallas_call(
      _out_proj_kernel,
      grid=(m // tm,),
      in_specs=[
          pl.BlockSpec((tm, d), lambda i: (i, 0)),
          pl.BlockSpec((tm, ca), lambda i: (i, 0)),
          pl.BlockSpec((RESIDUES, tm // RESIDUES, cb), lambda i: (i // tiles, i % tiles, 0)),
          pl.BlockSpec((PERM_ROWS, PERM_ROWS), lambda i: (0, 0)),
          pl.BlockSpec((ca, d), lambda i: (0, 0), pipeline_mode=resident),
          pl.BlockSpec((cb, d), lambda i: (1, 0), pipeline_mode=resident),
      ],
      out_specs=pl.BlockSpec((tm, d), lambda i: (i, 0)),
      out_shape=jax.ShapeDtypeStruct((m, d), F32),
      compiler_params=_params("parallel"),
      name="out_proj",
  )(x, o_a, o_b.reshape(batch * RESIDUES, seq // RESIDUES, cb), _swap_matrix(), wo, wo)


def _ple_kernel(x_ref, p_ref, g_ref, wg_ref, wp_ref, gf_ref, o_ref, *, final_norm):
  x = x_ref[...]
  h = _rmsnorm(x, g_ref[...]).astype(BF16)
  gate = jax.nn.sigmoid(_dot(h, wg_ref[...]))
  y = x + gate * _dot(p_ref[...].astype(BF16), wp_ref[...])
  o_ref[...] = _rmsnorm(y, gf_ref[...]) if final_norm else y


def _ple(x, p, g, w_gate, w_proj, g_final, final_norm):
  m, d = x.shape
  tm = PROJ_ROW_TILE
  return pl.pallas_call(
      functools.partial(_ple_kernel, final_norm=final_norm),
      grid=(m // tm,),
      in_specs=[
          pl.BlockSpec((tm, d), lambda i: (i, 0)),
          pl.BlockSpec((tm, p.shape[1]), lambda i: (i, 0)),
          pl.BlockSpec((1, d), lambda i: (0, 0)),
          pl.BlockSpec(w_gate.shape, lambda i: (0, 0)),
          pl.BlockSpec(w_proj.shape, lambda i: (0, 0)),
          pl.BlockSpec((1, d), lambda i: (0, 0)),
      ],
      out_specs=pl.BlockSpec((tm, d), lambda i: (i, 0)),
      out_shape=jax.ShapeDtypeStruct((m, d), F32),
      compiler_params=_params("parallel"),
      name="ple",
  )(x, p, g, w_gate, w_proj, g_final)


def kernel(x, p, norm_ffn1, ffn1_w_gate, ffn1_w_up, ffn1_w_down, norm_mix, w_in, b_f, w_o,
           norm_ffn2, ffn2_w_gate, ffn2_w_up, ffn2_w_down, norm_ple, w_ple_gate, w_ple_proj,
           rel_table, norm_final):
  batch, seq, d = x.shape
  depth = p.shape[0]
  m = batch * seq
  bf = lambda w: w.astype(BF16)
  row = lambda g: g.reshape(1, -1).astype(F32)

  band_bias = _band_bias(rel_table.astype(F32))
  xs = x.reshape(m, d).astype(F32)
  for i in range(depth):
    g_ffn1 = row(norm_ffn1[i])
    head, wg1, wu1, wd1 = _ffn_head(xs, g_ffn1, ffn1_w_gate[i], ffn1_w_up[i], ffn1_w_down[i])
    xs, wg2, wu2, wd2, w_a, w_b, w_f = _ffn(
        xs, g_ffn1, wg1, wu1, wd1, head=head,
        jobs=_ffn_cast_jobs(m, ffn2_w_gate[i], ffn2_w_up[i], ffn2_w_down[i])
        + _w_in_repack_jobs(m, jnp.swapaxes(w_in[i], 0, 1), 3 * D_FOX, N_HEADS_FOX, 3 * D_DIL))

    b_f_row = jnp.pad(b_f[i].astype(F32), (0, V7X_LANES - N_HEADS_FOX)).reshape(1, V7X_LANES)
    g_mix = row(norm_mix[i])
    wo_rows = w_o[i].shape[0] // (m // PROJ_ROW_TILE)
    u_a, f_logit, wo = _norm_matmul(
        xs, g_mix, w_a, BF16, (D_FOX, 2 * D_FOX), w_narrow=w_f,
        jobs=[_cast_job(w_o[i], (wo_rows, w_o[i].shape[1]), lambda t: (t, 0))])
    u_b, = _norm_matmul(xs, g_mix, w_b, F32, (D_DIL, 2 * D_DIL), residue_major=(batch, seq))

    c = _fox_decay(f_logit, b_f_row, batch, seq)
    o_a = _fox_attention(u_a, c, batch, seq)
    o_b = _dilated_attention(u_b, band_bias, batch, seq)
    xs = _out_proj(xs, o_a, o_b, wo, batch, seq)

    xs, w_gate, w_ple = _ffn(xs, row(norm_ffn2[i]), wg2, wu2, wd2,
                             jobs=_row_cast_jobs(m, w_ple_gate[i], w_ple_proj[i]))
    last = i == depth - 1
    xs = _ple(xs, p[i].reshape(m, -1), row(norm_ple[i]), w_gate, w_ple,
              row(norm_final), final_norm=last)
  return xs.reshape(batch, seq, d).astype(x.dtype)
```

```python
import functools
import math
from typing import Any, Callable, NamedTuple

import jax
import jax.numpy as jnp
import numpy as np
from jax import lax
from jax.experimental import pallas as pl
from jax.experimental.pallas import tpu as pltpu

F32 = jnp.float32
BF16 = jnp.bfloat16

HEAD_DIM = 128
N_HEADS_FOX = 8
N_HEADS_DIL = 8
D_FOX = N_HEADS_FOX * HEAD_DIM
D_DIL = N_HEADS_DIL * HEAD_DIM
DILATED_PATTERNS = ((128, 1), (512, 4), (2048, 16))
WINDOW_KEYS = 128
N_REL_BUCKETS = 32
REL_MAX_DISTANCE = 2048
RMS_EPS = 1e-6
NEG_INF = -1e30
SCALE = HEAD_DIM ** -0.5
LOG2_E = math.log2(math.e)

V7X_LANES = 128
V7X_VMEM_LIMIT_BYTES = 56 * 1024 * 1024

RESIDUES = max(d for _, d in DILATED_PATTERNS)
PERM_ROWS = RESIDUES * RESIDUES

FFN_ROW_TILE = 1024
FFN_SUB_ROWS = 512
FFN_FF_TILE = 512
FFN_HEAD_FF_TILE = 256
PROJ_ROW_TILE = 512
FOX_Q_TILE = 256
FOX_LOOKAHEAD = 2
DIL_BLOCK_GROUP = 12


def _params(*semantics):
  return pltpu.CompilerParams(dimension_semantics=semantics,
                              vmem_limit_bytes=V7X_VMEM_LIMIT_BYTES)


def _rmsnorm(x, g):
  ms = jnp.mean(x * x, axis=-1, keepdims=True)
  return x * lax.rsqrt(ms + RMS_EPS) * g


def _dot(a, b):
  return jnp.dot(a, b, preferred_element_type=F32)


def _dot_nt(a, b):
  return lax.dot_general(a, b, (((1,), (1,)), ((), ())), preferred_element_type=F32)


def _weighted_values(e, v):
  both = _dot(e.astype(BF16), jnp.concatenate([v, jnp.ones_like(v)], axis=1))
  return both[:, :HEAD_DIM], both[:, HEAD_DIM:]


class _SideJob(NamedTuple):
  arrays: tuple
  in_specs: tuple
  out_shape: Any
  out_spec: Any
  fn: Callable


def _cast_job(a, block, index, scale=1.0):
  spec = pl.BlockSpec(block, index)
  return _SideJob((a,), (spec,), jax.ShapeDtypeStruct(a.shape, BF16), spec,
                  lambda r: r[...] * scale)


def _run_side_jobs(jobs, in_refs, out_refs):
  in_refs = list(in_refs)
  for job, out_ref in zip(jobs, out_refs):
    refs = [in_refs.pop(0) for _ in job.arrays]
    out_ref[...] = job.fn(*refs).astype(out_ref.dtype)


def _side_args(jobs):
  arrays = [a for job in jobs for a in job.arrays]
  in_specs = [s for job in jobs for s in job.in_specs]
  return arrays, in_specs, [job.out_spec for job in jobs], [job.out_shape for job in jobs]


def _ffn_head_kernel(x_ref, g_ref, wg_ref, wu_ref, wd_ref, o_ref, og_ref, ou_ref, od_ref, h_ref):
  @pl.when(pl.program_id(0) == 0)
  def _():
    x = x_ref[...]
    h_ref[...] = _rmsnorm(x, g_ref[...]).astype(BF16)
    o_ref[...] = x

  wg = wg_ref[...].astype(BF16)
  wu = wu_ref[...].astype(BF16)
  wd = (wd_ref[...] * 0.5).astype(BF16)
  og_ref[...] = wg
  ou_ref[...] = wu
  od_ref[...] = wd
  for r in range(h_ref.shape[0] // FFN_SUB_ROWS):
    rows = slice(r * FFN_SUB_ROWS, (r + 1) * FFN_SUB_ROWS)
    h = h_ref[rows, :]
    gate = _dot(h, wg)
    up = _dot(h, wu)
    act = (gate * jax.nn.sigmoid(gate)) * up
    o_ref[rows, :] += _dot(act.astype(BF16), wd)


def _ffn_head(x, g, w_gate, w_up, w_down):
  d = x.shape[1]
  dff = w_gate.shape[1]
  tm, tf = FFN_ROW_TILE, FFN_HEAD_FF_TILE
  return pl.pallas_call(
      _ffn_head_kernel,
      grid=(dff // tf,),
      in_specs=[
          pl.BlockSpec((tm, d), lambda j: (0, 0), pipeline_mode=pl.Buffered(1)),
          pl.BlockSpec((1, d), lambda j: (0, 0)),
          pl.BlockSpec((d, tf), lambda j: (0, j)),
          pl.BlockSpec((d, tf), lambda j: (0, j)),
          pl.BlockSpec((tf, d), lambda j: (j, 0)),
      ],
      out_specs=[
          pl.BlockSpec((tm, d), lambda j: (0, 0)),
          pl.BlockSpec((d, tf), lambda j: (0, j)),
          pl.BlockSpec((d, tf), lambda j: (0, j)),
          pl.BlockSpec((tf, d), lambda j: (j, 0)),
      ],
      out_shape=[
          jax.ShapeDtypeStruct((tm, d), F32),
          jax.ShapeDtypeStruct(w_gate.shape, BF16),
          jax.ShapeDtypeStruct(w_up.shape, BF16),
          jax.ShapeDtypeStruct(w_down.shape, BF16),
      ],
      scratch_shapes=[pltpu.VMEM((tm, d), BF16)],
      compiler_params=_params("arbitrary"),
      name="ffn_head",
  )(x, g, w_gate, w_up, w_down)


def _ffn_kernel(x_ref, g_ref, wg_ref, wu_ref, wd_ref, *rest, jobs, has_head):
  rest = list(rest)
  head_ref = rest.pop(0) if has_head else None
  copy_sem = rest.pop() if has_head else None
  n_in = sum(len(job.arrays) for job in jobs)
  side_in, o_ref, side_out, h_ref = rest[:n_in], rest[n_in], rest[n_in + 1:-1], rest[-1]

  def step(first):
    _run_side_jobs(jobs, side_in, side_out)
    for r in range(h_ref.shape[0] // FFN_SUB_ROWS):
      rows = slice(r * FFN_SUB_ROWS, (r + 1) * FFN_SUB_ROWS)
      if first:
        base = x_ref[rows, :]
        h = _rmsnorm(base, g_ref[...]).astype(BF16)
        h_ref[rows, :] = h
      else:
        base = o_ref[rows, :]
        h = h_ref[rows, :]
      gate = _dot(h, wg_ref[...])
      up = _dot(h, wu_ref[...])
      act = (gate * jax.nn.sigmoid(gate)) * up
      o_ref[rows, :] = base + _dot(act.astype(BF16), wd_ref[...])

  def compute_tile():
    lax.cond(pl.program_id(1) == 0, lambda: step(True), lambda: step(False))

  def copy_head_tile():
    _run_side_jobs(jobs, side_in, side_out)

    @pl.when(pl.program_id(1) == 0)
    def _():
      copy = pltpu.make_async_copy(head_ref, o_ref, copy_sem)
      copy.start()
      copy.wait()

  if has_head:
    lax.cond(pl.program_id(0) == 0, copy_head_tile, compute_tile)
  else:
    compute_tile()


def _ffn(x, g, wg, wu, wd_half, jobs=(), head=None):
  m, d = x.shape
  dff = wg.shape[1]
  tm, tf = FFN_ROW_TILE, FFN_FF_TILE
  has_head = head is not None
  chunk = (lambda i, j: jnp.where(i == 0, 0, j)) if has_head else (lambda i, j: j)
  side_arrays, side_in_specs, side_out_specs, side_out_shapes = _side_args(jobs)
  head_args = [head] if has_head else []
  head_specs = [pl.BlockSpec(memory_space=pl.ANY)] if has_head else []
  head_scratch = [pltpu.SemaphoreType.DMA(())] if has_head else []
  return pl.pallas_call(
      functools.partial(_ffn_kernel, jobs=tuple(jobs), has_head=has_head),
      grid=(m // tm, dff // tf),
      in_specs=[
          pl.BlockSpec((tm, d), lambda i, j: (i, 0)),
          pl.BlockSpec((1, d), lambda i, j: (0, 0)),
          pl.BlockSpec((d, tf), lambda i, j: (0, chunk(i, j))),
          pl.BlockSpec((d, tf), lambda i, j: (0, chunk(i, j))),
          pl.BlockSpec((tf, d), lambda i, j: (chunk(i, j), 0)),
      ] + head_specs + side_in_specs,
      out_specs=[pl.BlockSpec((tm, d), lambda i, j: (i, 0))] + side_out_specs,
      out_shape=[jax.ShapeDtypeStruct((m, d), F32)] + side_out_shapes,
      scratch_shapes=[pltpu.VMEM((tm, d), BF16)] + head_scratch,
      compiler_params=_params("arbitrary" if has_head else "parallel", "arbitrary"),
      name="ffn",
  )(x, g, wg, wu, wd_half, *head_args, *side_arrays)


def _ffn_cast_jobs(m, w_gate, w_up, w_down):
  ni = m // FFN_ROW_TILE
  d, dff = w_gate.shape
  tf = FFN_FF_TILE
  assert d % ni == 0
  return [
      _cast_job(w_gate, (d // ni, tf), lambda i, j: (i, j)),
      _cast_job(w_up, (d // ni, tf), lambda i, j: (i, j)),
      _cast_job(w_down, (tf, d // ni), lambda i, j: (j, i), scale=0.5),
  ]


def _row_cast_jobs(m, *weights):
  ni = m // FFN_ROW_TILE
  return [_cast_job(w, (w.shape[0] // ni, w.shape[1]), lambda i, j: (i, 0)) for w in weights]


def _w_in_repack_jobs(m, w_t, n_a, n_f, n_b):
  ni = m // FFN_ROW_TILE
  d = w_t.shape[1]
  cols, tc = d // ni, FFN_FF_TILE
  assert n_a % tc == 0 and n_b % tc == 0 and n_f % 8 == 0 and n_f <= V7X_LANES
  assert cols % V7X_LANES == 0
  chunks = n_a // tc
  assert n_b // tc == chunks
  cj = lambda j: jnp.minimum(j, chunks - 1)

  def transposed(ref):
    return ref[...].T

  def shifted_transposed(main_ref, tail_ref):
    return jnp.concatenate([main_ref[n_f:, :], tail_ref[...]], axis=0).T

  def gate_rows_transposed(ref):
    row = lax.broadcasted_iota(jnp.int32, ref.shape, 0)
    return jnp.where(row < n_f, ref[...], 0.0).T

  out = lambda n: jax.ShapeDtypeStruct((d, n), BF16)
  out_spec = pl.BlockSpec((cols, tc), lambda i, j: (i, cj(j)))
  return [
      _SideJob((w_t,), (pl.BlockSpec((tc, cols), lambda i, j: (cj(j), i)),),
               out(n_a), out_spec, transposed),
      _SideJob((w_t, w_t),
               (pl.BlockSpec((tc, cols), lambda i, j: (chunks + cj(j), i)),
                pl.BlockSpec((n_f, cols), lambda i, j: ((n_a + (cj(j) + 1) * tc) // n_f, i))),
               out(n_b), out_spec, shifted_transposed),
      _SideJob((w_t,), (pl.BlockSpec((V7X_LANES, cols), lambda i, j: (n_a // V7X_LANES, i)),),
               out(V7X_LANES), pl.BlockSpec((cols, V7X_LANES), lambda i, j: (i, 0)),
               gate_rows_transposed),
  ]


def _swap_matrix():
  i = np.arange(PERM_ROWS)
  src = (i % RESIDUES) * RESIDUES + i // RESIDUES
  return jnp.asarray(np.eye(PERM_ROWS, dtype=np.float32)[src], BF16)


def _norm_matmul_kernel(x_ref, g_ref, w_ref, *rest, permute, narrow, jobs, key_cols):
  rest = list(rest)
  swap_ref = rest.pop(0) if permute else None
  wn_ref = rest.pop(0) if narrow else None
  side_in = [rest.pop(0) for job in jobs for _ in job.arrays]
  o_ref = rest.pop(0)
  on_ref = rest.pop(0) if narrow else None
  side_out = [rest.pop(0) for _ in jobs]
  assert not rest, "unexpected extra refs"
  tm = x_ref.shape[0]
  _run_side_jobs(jobs, side_in, side_out)

  h = _rmsnorm(x_ref[...], g_ref[...]).astype(BF16)
  if permute:
    h = jnp.concatenate(
        [_dot(swap_ref[...], h[a * PERM_ROWS:(a + 1) * PERM_ROWS, :]).astype(BF16)
         for a in range(tm // PERM_ROWS)], axis=0)
  if narrow:
    on_ref[...] = _dot(h, wn_ref[...])
  acc = _dot(h, w_ref[...])
  lo, hi = key_cols
  res = jnp.concatenate([acc[:, :lo], acc[:, lo:hi] * (SCALE * LOG2_E), acc[:, hi:]],
                        axis=1).astype(o_ref.dtype)
  if permute:
    per = PERM_ROWS // RESIDUES
    for a in range(tm // PERM_ROWS):
      for r in range(RESIDUES):
        start = a * PERM_ROWS + r * per
        o_ref[r, a * per:(a + 1) * per, :] = res[start:start + per, :]
  else:
    o_ref[...] = res


def _norm_matmul(x, g, w, out_dtype, key_cols, residue_major=None, w_narrow=None, jobs=()):
  m, d = x.shape
  n = w.shape[1]
  tm = PROJ_ROW_TILE
  narrow = w_narrow is not None
  permute = residue_major is not None
  assert not (narrow and permute)
  in_specs = [
      pl.BlockSpec((tm, d), lambda i: (i, 0)),
      pl.BlockSpec((1, d), lambda i: (0, 0)),
      pl.BlockSpec((d, n), lambda i: (0, 0)),
  ]
  args = [x, g, w]
  if permute:
    batch, seq = residue_major
    tiles = seq // tm
    in_specs.append(pl.BlockSpec((PERM_ROWS, PERM_ROWS), lambda i: (0, 0)))
    args.append(_swap_matrix())
    out_specs = [pl.BlockSpec((RESIDUES, tm // RESIDUES, n), lambda i: (i // tiles, i % tiles, 0))]
    out_shape = [jax.ShapeDtypeStruct((batch * RESIDUES, seq // RESIDUES, n), out_dtype)]
  else:
    out_specs = [pl.BlockSpec((tm, n), lambda i: (i, 0))]
    out_shape = [jax.ShapeDtypeStruct((m, n), out_dtype)]
  if narrow:
    in_specs.append(pl.BlockSpec(w_narrow.shape, lambda i: (0, 0)))
    args.append(w_narrow)
    out_specs.append(pl.BlockSpec((tm, w_narrow.shape[1]), lambda i: (i, 0)))
    out_shape.append(jax.ShapeDtypeStruct((m, w_narrow.shape[1]), F32))
  side_arrays, side_in_specs, side_out_specs, side_out_shapes = _side_args(jobs)
  outs = pl.pallas_call(
      functools.partial(_norm_matmul_kernel, permute=permute, narrow=narrow, jobs=tuple(jobs),
                        key_cols=key_cols),
      grid=(m // tm,),
      in_specs=in_specs + side_in_specs,
      out_specs=out_specs + side_out_specs,
      out_shape=out_shape + side_out_shapes,
      compiler_params=_params("parallel"),
      name="norm_matmul",
  )(*args, *side_arrays)
  outs = list(outs)
  if permute:
    outs[0] = outs[0].reshape(m, n)
  return outs


def _cumsum_kernel(fl_ref, bf_ref, c_ref):
  s = fl_ref.shape[0]
  z = fl_ref[...] + bf_ref[...]
  logf = jnp.minimum(z, 0.0) - jnp.log1p(jnp.exp(-jnp.abs(z)))
  lt = logf.T[0:N_HEADS_FOX, :]
  row = lax.broadcasted_iota(jnp.int32, (V7X_LANES, V7X_LANES), 0)
  col = lax.broadcasted_iota(jnp.int32, (V7X_LANES, V7X_LANES), 1)
  upper = (row <= col).astype(F32)
  carry = jnp.zeros((N_HEADS_FOX, 1), F32)
  for j in range(s // V7X_LANES):
    blk = lt[:, j * V7X_LANES:(j + 1) * V7X_LANES]
    cs = jnp.dot(blk, upper, preferred_element_type=F32,
                 precision=lax.Precision.HIGHEST) + carry
    c_ref[:, j * V7X_LANES:(j + 1) * V7X_LANES] = cs
    carry = cs[:, V7X_LANES - 1:V7X_LANES]


def _fox_decay(f_logit, b_f_row, batch, seq):
  return pl.pallas_call(
      _cumsum_kernel,
      grid=(batch,),
      in_specs=[
          pl.BlockSpec((seq, V7X_LANES), lambda b: (b, 0)),
          pl.BlockSpec((1, V7X_LANES), lambda b: (0, 0)),
      ],
      out_specs=pl.BlockSpec((None, N_HEADS_FOX, seq), lambda b: (b, 0, 0)),
      out_shape=jax.ShapeDtypeStruct((batch, N_HEADS_FOX, seq), F32),
      compiler_params=_params("parallel"),
      name="fox_decay",
  )(f_logit, b_f_row)


def _fox_phases(q_ref, k_ref, v_ref, c_ref, o_ref):
  seq = q_ref.shape[0]
  tq = FOX_Q_TILE
  h = pl.program_id(1)
  crow = c_ref[pl.ds(h, 1), :] * LOG2_E
  row = lax.broadcasted_iota(jnp.int32, (tq, tq), 0)
  col = lax.broadcasted_iota(jnp.int32, (tq, tq), 1)
  diag_mask = jnp.where(col > row, NEG_INF, 0.0).astype(F32)

  def scores(i):
    t0, t1 = i * tq, (i + 1) * tq
    q = q_ref[t0:t1, :]
    bias = crow[:, t1 - 1:t1] - crow[:, 0:t1]
    s_diag = _dot_nt(q, k_ref[t0:t1, :]) + bias[:, t0:t1] + diag_mask
    s_off = _dot_nt(q, k_ref[0:t0, :]) + bias[:, 0:t0] if i > 0 else None
    return s_diag, s_off

  def finish(i, s_diag, s_off):
    t0, t1 = i * tq, (i + 1) * tq
    m = jnp.max(s_diag, axis=-1, keepdims=True)
    if i > 0:
      m = jnp.maximum(m, jnp.max(s_off, axis=-1, keepdims=True))
    o, l = _weighted_values(jnp.exp2(s_diag - m), v_ref[t0:t1, :])
    if i > 0:
      o_off, l_off = _weighted_values(jnp.exp2(s_off - m), v_ref[0:t0, :])
      o, l = o + o_off, l + l_off
    o_ref[t0:t1, :] = (o / l).astype(o_ref.dtype)

  n_tiles = seq // tq
  pending = [scores(i) for i in range(min(FOX_LOOKAHEAD, n_tiles))]
  yield
  for i in range(n_tiles):
    if i + FOX_LOOKAHEAD < n_tiles:
      pending.append(scores(i + FOX_LOOKAHEAD))
      yield
    finish(i, *pending.pop(0))
    yield


def _t5_bucket_np(dist):
  max_exact = N_REL_BUCKETS // 2
  d = np.maximum(dist, 1).astype(np.float32)
  large = max_exact + (np.log(d / np.float32(max_exact))
                       / np.float32(math.log(REL_MAX_DISTANCE / max_exact))
                       * np.float32(N_REL_BUCKETS - max_exact)).astype(np.int32)
  large = np.minimum(large, N_REL_BUCKETS - 1)
  return np.where(dist < max_exact, dist, large).astype(np.int32)


def _block_positions(dilation):
  n = WINDOW_KEYS
  m = RESIDUES // dilation
  rows = n // m
  j = np.arange(m)[:, None]
  qpos = (n + m * np.arange(rows)[None, :] + j).reshape(-1)
  kpos = (m * np.arange(2 * rows)[None, :] + j).reshape(-1)
  return qpos, kpos


def _band_buckets():
  n = WINDOW_KEYS
  tiles = []
  for _, dilation in DILATED_PATTERNS:
    qpos, kpos = _block_positions(dilation)
    rel = qpos[:, None] - np.concatenate([kpos, qpos])[None, :]
    valid = (rel >= 0) & (rel <= n)
    bucket = _t5_bucket_np(np.maximum(rel, 0) * dilation)
    tiles.append(np.where(valid, bucket, -1))
  return np.stack(tiles).astype(np.int32)


def _bias_kernel(tab_ref, bkt_ref, o_ref):
  bkt = bkt_ref[...]
  for h in range(N_HEADS_DIL):
    acc = jnp.full(bkt.shape, NEG_INF, F32)
    for b in range(N_REL_BUCKETS):
      acc = jnp.where(bkt == b, tab_ref[b, h] * LOG2_E, acc)
    o_ref[h] = acc


def _band_bias(rel_table):
  buckets = jnp.asarray(_band_buckets())
  p, n, n2 = buckets.shape
  return pl.pallas_call(
      _bias_kernel,
      grid=(p,),
      in_specs=[
          pl.BlockSpec(memory_space=pltpu.SMEM),
          pl.BlockSpec((None, n, n2), lambda i: (i, 0, 0)),
      ],
      out_specs=pl.BlockSpec((None, N_HEADS_DIL, n, n2), lambda i: (i, 0, 0, 0)),
      out_shape=jax.ShapeDtypeStruct((p, N_HEADS_DIL, n, n2), F32),
      compiler_params=_params("parallel"),
      name="band_bias",
  )(rel_table, buckets)


def _dil_phases(q_ref, k_ref, v_ref, bm_ref, o_ref, *scratch):
  seq = q_ref.shape[0]
  n = WINDOW_KEYS
  seg = seq // RESIDUES
  last = len(DILATED_PATTERNS) - 1
  assert DILATED_PATTERNS[last][1] == RESIDUES and n == seg
  accs, lses = scratch[:last], scratch[last:]

  def gather(ref, starts, size):
    return jnp.concatenate([ref[st:st + size, :] for st in starts], axis=0).astype(BF16)

  def block_rows(segments, nb):
    rows = n // len(segments)
    q_starts = [s * seg + nb * rows for s in segments]
    if nb == 0:
      return rows, q_starts, q_starts, rows
    return rows, q_starts, [st - rows for st in q_starts], 2 * rows

  def scores(p, segments, nb):
    rows, q_starts, k_starts, k_rows = block_rows(segments, nb)
    bm = bm_ref[p, :, 2 * n:3 * n] if nb == 0 else bm_ref[p, :, 0:2 * n]
    return _dot_nt(gather(q_ref, q_starts, rows), gather(k_ref, k_starts, k_rows)) + bm

  def finish(p, segments, nb, e, m):
    rows, q_starts, k_starts, k_rows = block_rows(segments, nb)
    o, l = _weighted_values(e, gather(v_ref, k_starts, k_rows))
    o = o / l
    lse = m + jnp.log2(l)
    if p < last:
      for j, st in enumerate(q_starts):
        accs[p][st:st + rows, :] = o[j * rows:(j + 1) * rows, :]
        lses[p][st:st + rows, :] = lse[j * rows:(j + 1) * rows, :]
      return
    seg_rows = slice(q_starts[0], q_starts[0] + n)
    all_lse = [ref[seg_rows, :] for ref in lses] + [lse]
    all_out = [ref[seg_rows, :] for ref in accs] + [o]
    top = functools.reduce(jnp.maximum, all_lse)
    weights = [jnp.exp2(x - top) for x in all_lse]
    add = lambda a, b: a + b
    mixed = (functools.reduce(add, [w * a for w, a in zip(weights, all_out)])
             / functools.reduce(add, weights))
    o_ref[seg_rows, :] = mixed.astype(o_ref.dtype)

  blocks = [(p, list(range(r, RESIDUES, d)), nb)
            for p, (_, d) in enumerate(DILATED_PATTERNS)
            for r in range(d) for nb in range(seq // (n * d))]
  for g in range(0, len(blocks), DIL_BLOCK_GROUP):
    group = blocks[g:g + DIL_BLOCK_GROUP]
    ss = [scores(*blk) for blk in group]
    yield
    ms = [jnp.max(s, axis=-1, keepdims=True) for s in ss]
    es = [jnp.exp2(s - m) for s, m in zip(ss, ms)]
    yield
    for blk, e, m in zip(group, es, ms):
      finish(*blk, e, m)
    yield


def _mixer_kernel(qa_ref, ka_ref, va_ref, c_ref, qb_ref, kb_ref, vb_ref, bm_ref,
                  oa_ref, ob_ref, *scratch):
  streams = [_fox_phases(qa_ref, ka_ref, va_ref, c_ref, oa_ref),
             _dil_phases(qb_ref, kb_ref, vb_ref, bm_ref, ob_ref, *scratch)]
  while streams:
    for stream in list(streams):
      try:
        next(stream)
      except StopIteration:
        streams.remove(stream)


def _token_mixers(u_a, c, u_b, band_bias, batch, seq):
  assert seq // RESIDUES == WINDOW_KEYS and N_HEADS_FOX == N_HEADS_DIL
  blk = lambda off: pl.BlockSpec((seq, HEAD_DIM), lambda b, h: (b, off + h))
  n_pat = len(DILATED_PATTERNS)
  scratch = [pltpu.VMEM((seq, HEAD_DIM), F32) for _ in range(2 * (n_pat - 1))]
  out_blk = pl.BlockSpec((seq, HEAD_DIM), lambda b, h: (b, h))
  return pl.pallas_call(
      _mixer_kernel,
      grid=(batch, N_HEADS_FOX),
      in_specs=[
          blk(0), blk(N_HEADS_FOX), blk(2 * N_HEADS_FOX),
          pl.BlockSpec((None, N_HEADS_FOX, seq), lambda b, h: (b, 0, 0)),
          blk(0), blk(N_HEADS_DIL), blk(2 * N_HEADS_DIL),
          pl.BlockSpec((n_pat, None, WINDOW_KEYS, 3 * WINDOW_KEYS), lambda b, h: (0, h, 0, 0)),
      ],
      out_specs=[out_blk, out_blk],
      out_shape=[jax.ShapeDtypeStruct((batch * seq, D_FOX), BF16),
                 jax.ShapeDtypeStruct((batch * seq, D_DIL), BF16)],
      scratch_shapes=scratch,
      compiler_params=_params("parallel", "arbitrary"),
      name="token_mixers",
  )(u_a, u_a, u_a, c, u_b, u_b, u_b, band_bias)


def _out_proj_kernel(x_ref, a_ref, b_ref, swap_ref, wa_ref, wb_ref, o_ref):
  per = PERM_ROWS // RESIDUES
  for a in range(x_ref.shape[0] // PERM_ROWS):
    rows = slice(a * PERM_ROWS, (a + 1) * PERM_ROWS)
    slab = jnp.concatenate([b_ref[r, a * per:(a + 1) * per, :] for r in range(RESIDUES)], axis=0)
    o_b = _dot(swap_ref[...], slab).astype(BF16)
    o_ref[rows, :] = x_ref[rows, :] + _dot(a_ref[rows, :], wa_ref[...]) + _dot(o_b, wb_ref[...])


def _out_proj(x, o_a, o_b, wo, batch, seq):
  m, d = x.shape
  tm = PROJ_ROW_TILE
  tiles = seq // tm
  ca, cb = o_a.shape[1], o_b.shape[1]
  assert ca == cb and wo.shape[0] == ca + cb
  resident = pl.Buffered(1)
  return pl.pallas_call(
      _out_proj_kernel,
      grid=(m // tm,),
      in_specs=[
          pl.BlockSpec((tm, d), lambda i: (i, 0)),
          pl.BlockSpec((tm, ca), lambda i: (i, 0)),
          pl.BlockSpec((RESIDUES, tm // RESIDUES, cb), lambda i: (i // tiles, i % tiles, 0)),
          pl.BlockSpec((PERM_ROWS, PERM_ROWS), lambda i: (0, 0)),
          pl.BlockSpec((ca, d), lambda i: (0, 0), pipeline_mode=resident),
          pl.BlockSpec((cb, d), lambda i: (1, 0), pipeline_mode=resident),
      ],
      out_specs=pl.BlockSpec((tm, d), lambda i: (i, 0)),
      out_shape=jax.ShapeDtypeStruct((m, d), F32),
      compiler_params=_params("parallel"),
      name="out_proj",
  )(x, o_a, o_b.reshape(batch * RESIDUES, seq // RESIDUES, cb), _swap_matrix(), wo, wo)


def _ple_kernel(x_ref, p_ref, g_ref, wg_ref, wp_ref, gf_ref, o_ref, *, final_norm):
  x = x_ref[...]
  h = _rmsnorm(x, g_ref[...]).astype(BF16)
  gate = jax.nn.sigmoid(_dot(h, wg_ref[...]))
  y = x + gate * _dot(p_ref[...].astype(BF16), wp_ref[...])
  o_ref[...] = _rmsnorm(y, gf_ref[...]) if final_norm else y


def _ple(x, p, g, w_gate, w_proj, g_final, final_norm):
  m, d = x.shape
  tm = PROJ_ROW_TILE
  return pl.pallas_call(
      functools.partial(_ple_kernel, final_norm=final_norm),
      grid=(m // tm,),
      in_specs=[
          pl.BlockSpec((tm, d), lambda i: (i, 0)),
          pl.BlockSpec((tm, p.shape[1]), lambda i: (i, 0)),
          pl.BlockSpec((1, d), lambda i: (0, 0)),
          pl.BlockSpec(w_gate.shape, lambda i: (0, 0)),
          pl.BlockSpec(w_proj.shape, lambda i: (0, 0)),
          pl.BlockSpec((1, d), lambda i: (0, 0)),
      ],
      out_specs=pl.BlockSpec((tm, d), lambda i: (i, 0)),
      out_shape=jax.ShapeDtypeStruct((m, d), F32),
      compiler_params=_params("parallel"),
      name="ple",
  )(x, p, g, w_gate, w_proj, g_final)


def kernel(x, p, norm_ffn1, ffn1_w_gate, ffn1_w_up, ffn1_w_down, norm_mix, w_in, b_f, w_o,
           norm_ffn2, ffn2_w_gate, ffn2_w_up, ffn2_w_down, norm_ple, w_ple_gate, w_ple_proj,
           rel_table, norm_final):
  batch, seq, d = x.shape
  depth = p.shape[0]
  m = batch * seq
  bf = lambda w: w.astype(BF16)
  row = lambda g: g.reshape(1, -1).astype(F32)

  band_bias = _band_bias(rel_table.astype(F32))
  xs = x.reshape(m, d).astype(F32)
  for i in range(depth):
    g_ffn1 = row(norm_ffn1[i])
    head, wg1, wu1, wd1 = _ffn_head(xs, g_ffn1, ffn1_w_gate[i], ffn1_w_up[i], ffn1_w_down[i])
    xs, wg2, wu2, wd2, w_a, w_b, w_f = _ffn(
        xs, g_ffn1, wg1, wu1, wd1, head=head,
        jobs=_ffn_cast_jobs(m, ffn2_w_gate[i], ffn2_w_up[i], ffn2_w_down[i])
        + _w_in_repack_jobs(m, jnp.swapaxes(w_in[i], 0, 1), 3 * D_FOX, N_HEADS_FOX, 3 * D_DIL))

    b_f_row = jnp.pad(b_f[i].astype(F32), (0, V7X_LANES - N_HEADS_FOX)).reshape(1, V7X_LANES)
    g_mix = row(norm_mix[i])
    wo_rows = w_o[i].shape[0] // (m // PROJ_ROW_TILE)
    u_a, f_logit, wo = _norm_matmul(
        xs, g_mix, w_a, BF16, (D_FOX, 2 * D_FOX), w_narrow=w_f,
        jobs=[_cast_job(w_o[i], (wo_rows, w_o[i].shape[1]), lambda t: (t, 0))])
    u_b, = _norm_matmul(xs, g_mix, w_b, F32, (D_DIL, 2 * D_DIL), residue_major=(batch, seq))

    c = _fox_decay(f_logit, b_f_row, batch, seq)
    o_a, o_b = _token_mixers(u_a, c, u_b, band_bias, batch, seq)
    xs = _out_proj(xs, o_a, o_b, wo, batch, seq)

    xs, w_gate, w_ple = _ffn(xs, row(norm_ffn2[i]), wg2, wu2, wd2,
                             jobs=_row_cast_jobs(m, w_ple_gate[i], w_ple_proj[i]))
    last = i == depth - 1
    xs = _ple(xs, p[i].reshape(m, -1), row(norm_ple[i]), w_gate, w_ple,
              row(norm_final), final_norm=last)
  return xs.reshape(batch, seq, d).astype(x.dtype)
```

```python
import functools
import math
from typing import Any, Callable, NamedTuple

import jax
import jax.numpy as jnp
import numpy as np
from jax import lax
from jax.experimental import pallas as pl
from jax.experimental.pallas import tpu as pltpu

F32 = jnp.float32
BF16 = jnp.bfloat16

HEAD_DIM = 128
N_HEADS_FOX = 8
N_HEADS_DIL = 8
D_FOX = N_HEADS_FOX * HEAD_DIM
D_DIL = N_HEADS_DIL * HEAD_DIM
DILATED_PATTERNS = ((128, 1), (512, 4), (2048, 16))
WINDOW_KEYS = 128
N_REL_BUCKETS = 32
REL_MAX_DISTANCE = 2048
RMS_EPS = 1e-6
NEG_INF = -1e30
SCALE = HEAD_DIM ** -0.5
LOG2_E = math.log2(math.e)

V7X_LANES = 128
BF16_TILE_ROWS = 16
V7X_VMEM_LIMIT_BYTES = 56 * 1024 * 1024

RESIDUES = max(d for _, d in DILATED_PATTERNS)
PERM_ROWS = RESIDUES * RESIDUES

FFN_ROW_TILE = 1024
FFN_SUB_ROWS = 512
FFN_FF_TILE = 512
FFN_HEAD_FF_TILE = 256
PROJ_ROW_TILE = 512
FOX_Q_TILE = 256
FOX_LOOKAHEAD = 2
DIL_BLOCK_GROUP = 12


def _params(*semantics):
  return pltpu.CompilerParams(dimension_semantics=semantics,
                              vmem_limit_bytes=V7X_VMEM_LIMIT_BYTES)


def _rmsnorm(x, g):
  ms = jnp.mean(x * x, axis=-1, keepdims=True)
  return x * lax.rsqrt(ms + RMS_EPS) * g


def _dot(a, b):
  return jnp.dot(a, b, preferred_element_type=F32)


def _dot_nt(a, b):
  return lax.dot_general(a, b, (((1,), (1,)), ((), ())), preferred_element_type=F32)


def _weighted_values(e, v):
  both = _dot(e.astype(BF16), jnp.concatenate([v, jnp.ones_like(v)], axis=1))
  return both[:, :HEAD_DIM], both[:, HEAD_DIM:]


class _SideJob(NamedTuple):
  arrays: tuple
  in_specs: tuple
  out_shape: Any
  out_spec: Any
  fn: Callable


def _cast_job(a, block, index, scale=1.0):
  spec = pl.BlockSpec(block, index)
  return _SideJob((a,), (spec,), jax.ShapeDtypeStruct(a.shape, BF16), spec,
                  lambda r: r[...] * scale)


def _run_side_jobs(jobs, in_refs, out_refs):
  in_refs = list(in_refs)
  for job, out_ref in zip(jobs, out_refs):
    refs = [in_refs.pop(0) for _ in job.arrays]
    out_ref[...] = job.fn(*refs).astype(out_ref.dtype)


def _side_args(jobs):
  arrays = [a for job in jobs for a in job.arrays]
  in_specs = [s for job in jobs for s in job.in_specs]
  return arrays, in_specs, [job.out_spec for job in jobs], [job.out_shape for job in jobs]


def _ffn_head_kernel(x_ref, g_ref, wg_ref, wu_ref, wd_ref, o_ref, og_ref, ou_ref, od_ref, h_ref):
  @pl.when(pl.program_id(0) == 0)
  def _():
    x = x_ref[...]
    h_ref[...] = _rmsnorm(x, g_ref[...]).astype(BF16)
    o_ref[...] = x

  wg = wg_ref[...].astype(BF16)
  wu = wu_ref[...].astype(BF16)
  wd = (wd_ref[...] * 0.5).astype(BF16)
  og_ref[...] = wg
  ou_ref[...] = wu
  od_ref[...] = wd
  for r in range(h_ref.shape[0] // FFN_SUB_ROWS):
    rows = slice(r * FFN_SUB_ROWS, (r + 1) * FFN_SUB_ROWS)
    h = h_ref[rows, :]
    gate = _dot(h, wg)
    up = _dot(h, wu)
    act = (gate * jax.nn.sigmoid(gate)) * up
    o_ref[rows, :] += _dot(act.astype(BF16), wd)


def _ffn_head(x, g, w_gate, w_up, w_down):
  d = x.shape[1]
  dff = w_gate.shape[1]
  tm, tf = FFN_ROW_TILE, FFN_HEAD_FF_TILE
  return pl.pallas_call(
      _ffn_head_kernel,
      grid=(dff // tf,),
      in_specs=[
          pl.BlockSpec((tm, d), lambda j: (0, 0), pipeline_mode=pl.Buffered(1)),
          pl.BlockSpec((1, d), lambda j: (0, 0)),
          pl.BlockSpec((d, tf), lambda j: (0, j)),
          pl.BlockSpec((d, tf), lambda j: (0, j)),
          pl.BlockSpec((tf, d), lambda j: (j, 0)),
      ],
      out_specs=[
          pl.BlockSpec((tm, d), lambda j: (0, 0)),
          pl.BlockSpec((d, tf), lambda j: (0, j)),
          pl.BlockSpec((d, tf), lambda j: (0, j)),
          pl.BlockSpec((tf, d), lambda j: (j, 0)),
      ],
      out_shape=[
          jax.ShapeDtypeStruct((tm, d), F32),
          jax.ShapeDtypeStruct(w_gate.shape, BF16),
          jax.ShapeDtypeStruct(w_up.shape, BF16),
          jax.ShapeDtypeStruct(w_down.shape, BF16),
      ],
      scratch_shapes=[pltpu.VMEM((tm, d), BF16)],
      compiler_params=_params("arbitrary"),
      name="ffn_head",
  )(x, g, w_gate, w_up, w_down)


def _ffn_kernel(x_ref, g_ref, wg_ref, wu_ref, wd_ref, *rest, jobs, has_head):
  rest = list(rest)
  head_ref = rest.pop(0) if has_head else None
  copy_sem = rest.pop() if has_head else None
  n_in = sum(len(job.arrays) for job in jobs)
  side_in, o_ref, side_out, h_ref = rest[:n_in], rest[n_in], rest[n_in + 1:-1], rest[-1]

  def step(first):
    _run_side_jobs(jobs, side_in, side_out)
    for r in range(h_ref.shape[0] // FFN_SUB_ROWS):
      rows = slice(r * FFN_SUB_ROWS, (r + 1) * FFN_SUB_ROWS)
      if first:
        base = x_ref[rows, :]
        h = _rmsnorm(base, g_ref[...]).astype(BF16)
        h_ref[rows, :] = h
      else:
        base = o_ref[rows, :]
        h = h_ref[rows, :]
      gate = _dot(h, wg_ref[...])
      up = _dot(h, wu_ref[...])
      act = (gate * jax.nn.sigmoid(gate)) * up
      o_ref[rows, :] = base + _dot(act.astype(BF16), wd_ref[...])

  def compute_tile():
    lax.cond(pl.program_id(1) == 0, lambda: step(True), lambda: step(False))

  def copy_head_tile():
    _run_side_jobs(jobs, side_in, side_out)

    @pl.when(pl.program_id(1) == 0)
    def _():
      copy = pltpu.make_async_copy(head_ref, o_ref, copy_sem)
      copy.start()
      copy.wait()

  if has_head:
    lax.cond(pl.program_id(0) == 0, copy_head_tile, compute_tile)
  else:
    compute_tile()


def _ffn(x, g, wg, wu, wd_half, jobs=(), head=None):
  m, d = x.shape
  dff = wg.shape[1]
  tm, tf = FFN_ROW_TILE, FFN_FF_TILE
  has_head = head is not None
  chunk = (lambda i, j: jnp.where(i == 0, 0, j)) if has_head else (lambda i, j: j)
  side_arrays, side_in_specs, side_out_specs, side_out_shapes = _side_args(jobs)
  head_args = [head] if has_head else []
  head_specs = [pl.BlockSpec(memory_space=pl.ANY)] if has_head else []
  head_scratch = [pltpu.SemaphoreType.DMA(())] if has_head else []
  return pl.pallas_call(
      functools.partial(_ffn_kernel, jobs=tuple(jobs), has_head=has_head),
      grid=(m // tm, dff // tf),
      in_specs=[
          pl.BlockSpec((tm, d), lambda i, j: (i, 0)),
          pl.BlockSpec((1, d), lambda i, j: (0, 0)),
          pl.BlockSpec((d, tf), lambda i, j: (0, chunk(i, j))),
          pl.BlockSpec((d, tf), lambda i, j: (0, chunk(i, j))),
          pl.BlockSpec((tf, d), lambda i, j: (chunk(i, j), 0)),
      ] + head_specs + side_in_specs,
      out_specs=[pl.BlockSpec((tm, d), lambda i, j: (i, 0))] + side_out_specs,
      out_shape=[jax.ShapeDtypeStruct((m, d), F32)] + side_out_shapes,
      scratch_shapes=[pltpu.VMEM((tm, d), BF16)] + head_scratch,
      compiler_params=_params("arbitrary" if has_head else "parallel", "arbitrary"),
      name="ffn",
  )(x, g, wg, wu, wd_half, *head_args, *side_arrays)


def _slab_cast_jobs(steps, *weights, scale=1.0):
  jobs = []
  for w in weights:
    rows = max(BF16_TILE_ROWS, w.shape[0] // steps)
    assert w.shape[0] % rows == 0 and rows % BF16_TILE_ROWS == 0
    last = w.shape[0] // rows - 1
    jobs.append(_cast_job(w, (rows, w.shape[1]), lambda t, last=last: (jnp.minimum(t, last), 0),
                          scale=scale))
  return jobs


def _w_in_repack_jobs(m, w_t, n_a, n_f, n_b):
  ni = m // FFN_ROW_TILE
  d = w_t.shape[1]
  cols, tc = d // ni, FFN_FF_TILE
  assert n_a % tc == 0 and n_b % tc == 0 and n_f % 8 == 0 and n_f <= V7X_LANES
  assert cols % V7X_LANES == 0
  chunks = n_a // tc
  assert n_b // tc == chunks
  cj = lambda j: jnp.minimum(j, chunks - 1)

  def transposed(ref):
    return ref[...].T

  def shifted_transposed(main_ref, tail_ref):
    return jnp.concatenate([main_ref[n_f:, :], tail_ref[...]], axis=0).T

  def gate_rows_transposed(ref):
    row = lax.broadcasted_iota(jnp.int32, ref.shape, 0)
    return jnp.where(row < n_f, ref[...], 0.0).T

  out = lambda n: jax.ShapeDtypeStruct((d, n), BF16)
  out_spec = pl.BlockSpec((cols, tc), lambda i, j: (i, cj(j)))
  return [
      _SideJob((w_t,), (pl.BlockSpec((tc, cols), lambda i, j: (cj(j), i)),),
               out(n_a), out_spec, transposed),
      _SideJob((w_t, w_t),
               (pl.BlockSpec((tc, cols), lambda i, j: (chunks + cj(j), i)),
                pl.BlockSpec((n_f, cols), lambda i, j: ((n_a + (cj(j) + 1) * tc) // n_f, i))),
               out(n_b), out_spec, shifted_transposed),
      _SideJob((w_t,), (pl.BlockSpec((V7X_LANES, cols), lambda i, j: (n_a // V7X_LANES, i)),),
               out(V7X_LANES), pl.BlockSpec((cols, V7X_LANES), lambda i, j: (i, 0)),
               gate_rows_transposed),
  ]


def _swap_matrix():
  i = np.arange(PERM_ROWS)
  src = (i % RESIDUES) * RESIDUES + i // RESIDUES
  return jnp.asarray(np.eye(PERM_ROWS, dtype=np.float32)[src], BF16)


def _norm_matmul_kernel(x_ref, g_ref, w_ref, *rest, permute, narrow, jobs, key_cols):
  rest = list(rest)
  swap_ref = rest.pop(0) if permute else None
  wn_ref = rest.pop(0) if narrow else None
  side_in = [rest.pop(0) for job in jobs for _ in job.arrays]
  o_ref = rest.pop(0)
  on_ref = rest.pop(0) if narrow else None
  side_out = [rest.pop(0) for _ in jobs]
  assert not rest, "unexpected extra refs"
  tm = x_ref.shape[0]
  _run_side_jobs(jobs, side_in, side_out)

  h = _rmsnorm(x_ref[...], g_ref[...]).astype(BF16)
  if permute:
    h = jnp.concatenate(
        [_dot(swap_ref[...], h[a * PERM_ROWS:(a + 1) * PERM_ROWS, :]).astype(BF16)
         for a in range(tm // PERM_ROWS)], axis=0)
  if narrow:
    on_ref[...] = _dot(h, wn_ref[...])
  acc = _dot(h, w_ref[...])
  lo, hi = key_cols
  res = jnp.concatenate([acc[:, :lo], acc[:, lo:hi] * (SCALE * LOG2_E), acc[:, hi:]],
                        axis=1).astype(o_ref.dtype)
  if permute:
    per = PERM_ROWS // RESIDUES
    for a in range(tm // PERM_ROWS):
      for r in range(RESIDUES):
        start = a * PERM_ROWS + r * per
        o_ref[r, a * per:(a + 1) * per, :] = res[start:start + per, :]
  else:
    o_ref[...] = res


def _norm_matmul(x, g, w, out_dtype, key_cols, residue_major=None, w_narrow=None, jobs=()):
  m, d = x.shape
  n = w.shape[1]
  tm = PROJ_ROW_TILE
  narrow = w_narrow is not None
  permute = residue_major is not None
  assert not (narrow and permute)
  in_specs = [
      pl.BlockSpec((tm, d), lambda i: (i, 0)),
      pl.BlockSpec((1, d), lambda i: (0, 0)),
      pl.BlockSpec((d, n), lambda i: (0, 0)),
  ]
  args = [x, g, w]
  if permute:
    batch, seq = residue_major
    tiles = seq // tm
    in_specs.append(pl.BlockSpec((PERM_ROWS, PERM_ROWS), lambda i: (0, 0)))
    args.append(_swap_matrix())
    out_specs = [pl.BlockSpec((RESIDUES, tm // RESIDUES, n), lambda i: (i // tiles, i % tiles, 0))]
    out_shape = [jax.ShapeDtypeStruct((batch * RESIDUES, seq // RESIDUES, n), out_dtype)]
  else:
    out_specs = [pl.BlockSpec((tm, n), lambda i: (i, 0))]
    out_shape = [jax.ShapeDtypeStruct((m, n), out_dtype)]
  if narrow:
    in_specs.append(pl.BlockSpec(w_narrow.shape, lambda i: (0, 0)))
    args.append(w_narrow)
    out_specs.append(pl.BlockSpec((tm, w_narrow.shape[1]), lambda i: (i, 0)))
    out_shape.append(jax.ShapeDtypeStruct((m, w_narrow.shape[1]), F32))
  side_arrays, side_in_specs, side_out_specs, side_out_shapes = _side_args(jobs)
  outs = pl.pallas_call(
      functools.partial(_norm_matmul_kernel, permute=permute, narrow=narrow, jobs=tuple(jobs),
                        key_cols=key_cols),
      grid=(m // tm,),
      in_specs=in_specs + side_in_specs,
      out_specs=out_specs + side_out_specs,
      out_shape=out_shape + side_out_shapes,
      compiler_params=_params("parallel"),
      name="norm_matmul",
  )(*args, *side_arrays)
  outs = list(outs)
  if permute:
    outs[0] = outs[0].reshape(m, n)
  return outs


def _cumsum_kernel(fl_ref, bf_ref, c_ref):
  s = fl_ref.shape[0]
  z = fl_ref[...] + bf_ref[...]
  logf = jnp.minimum(z, 0.0) - jnp.log1p(jnp.exp(-jnp.abs(z)))
  lt = logf.T[0:N_HEADS_FOX, :]
  row = lax.broadcasted_iota(jnp.int32, (V7X_LANES, V7X_LANES), 0)
  col = lax.broadcasted_iota(jnp.int32, (V7X_LANES, V7X_LANES), 1)
  upper = (row <= col).astype(F32)
  carry = jnp.zeros((N_HEADS_FOX, 1), F32)
  for j in range(s // V7X_LANES):
    blk = lt[:, j * V7X_LANES:(j + 1) * V7X_LANES]
    cs = jnp.dot(blk, upper, preferred_element_type=F32,
                 precision=lax.Precision.HIGHEST) + carry
    c_ref[:, j * V7X_LANES:(j + 1) * V7X_LANES] = cs
    carry = cs[:, V7X_LANES - 1:V7X_LANES]


def _fox_decay(f_logit, b_f_row, batch, seq):
  return pl.pallas_call(
      _cumsum_kernel,
      grid=(batch,),
      in_specs=[
          pl.BlockSpec((seq, V7X_LANES), lambda b: (b, 0)),
          pl.BlockSpec((1, V7X_LANES), lambda b: (0, 0)),
      ],
      out_specs=pl.BlockSpec((None, N_HEADS_FOX, seq), lambda b: (b, 0, 0)),
      out_shape=jax.ShapeDtypeStruct((batch, N_HEADS_FOX, seq), F32),
      compiler_params=_params("parallel"),
      name="fox_decay",
  )(f_logit, b_f_row)


def _fox_phases(q_ref, k_ref, v_ref, c_ref, o_ref):
  seq = q_ref.shape[0]
  tq = FOX_Q_TILE
  h = pl.program_id(1)
  crow = c_ref[pl.ds(h, 1), :] * LOG2_E
  row = lax.broadcasted_iota(jnp.int32, (tq, tq), 0)
  col = lax.broadcasted_iota(jnp.int32, (tq, tq), 1)
  diag_mask = jnp.where(col > row, NEG_INF, 0.0).astype(F32)

  def scores(i):
    t0, t1 = i * tq, (i + 1) * tq
    q = q_ref[t0:t1, :]
    bias = crow[:, t1 - 1:t1] - crow[:, 0:t1]
    s_diag = _dot_nt(q, k_ref[t0:t1, :]) + bias[:, t0:t1] + diag_mask
    s_off = _dot_nt(q, k_ref[0:t0, :]) + bias[:, 0:t0] if i > 0 else None
    return s_diag, s_off

  def finish(i, s_diag, s_off):
    t0, t1 = i * tq, (i + 1) * tq
    m = jnp.max(s_diag, axis=-1, keepdims=True)
    if i > 0:
      m = jnp.maximum(m, jnp.max(s_off, axis=-1, keepdims=True))
    o, l = _weighted_values(jnp.exp2(s_diag - m), v_ref[t0:t1, :])
    if i > 0:
      o_off, l_off = _weighted_values(jnp.exp2(s_off - m), v_ref[0:t0, :])
      o, l = o + o_off, l + l_off
    o_ref[t0:t1, :] = (o / l).astype(o_ref.dtype)

  n_tiles = seq // tq
  pending = [scores(i) for i in range(min(FOX_LOOKAHEAD, n_tiles))]
  yield
  for i in range(n_tiles):
    if i + FOX_LOOKAHEAD < n_tiles:
      pending.append(scores(i + FOX_LOOKAHEAD))
      yield
    finish(i, *pending.pop(0))
    yield


def _t5_bucket_np(dist):
  max_exact = N_REL_BUCKETS // 2
  d = np.maximum(dist, 1).astype(np.float32)
  large = max_exact + (np.log(d / np.float32(max_exact))
                       / np.float32(math.log(REL_MAX_DISTANCE / max_exact))
                       * np.float32(N_REL_BUCKETS - max_exact)).astype(np.int32)
  large = np.minimum(large, N_REL_BUCKETS - 1)
  return np.where(dist < max_exact, dist, large).astype(np.int32)


def _block_positions(dilation):
  n = WINDOW_KEYS
  m = RESIDUES // dilation
  rows = n // m
  j = np.arange(m)[:, None]
  qpos = (n + m * np.arange(rows)[None, :] + j).reshape(-1)
  kpos = (m * np.arange(2 * rows)[None, :] + j).reshape(-1)
  return qpos, kpos


def _band_buckets():
  n = WINDOW_KEYS
  tiles = []
  for _, dilation in DILATED_PATTERNS:
    qpos, kpos = _block_positions(dilation)
    rel = qpos[:, None] - np.concatenate([kpos, qpos])[None, :]
    valid = (rel >= 0) & (rel <= n)
    bucket = _t5_bucket_np(np.maximum(rel, 0) * dilation)
    tiles.append(np.where(valid, bucket, -1))
  return np.stack(tiles).astype(np.int32)


def _bias_kernel(tab_ref, bkt_ref, o_ref):
  bkt = bkt_ref[...]
  for h in range(N_HEADS_DIL):
    acc = jnp.full(bkt.shape, NEG_INF, F32)
    for b in range(N_REL_BUCKETS):
      acc = jnp.where(bkt == b, tab_ref[b, h] * LOG2_E, acc)
    o_ref[h] = acc


def _band_bias(rel_table):
  buckets = jnp.asarray(_band_buckets())
  p, n, n2 = buckets.shape
  return pl.pallas_call(
      _bias_kernel,
      grid=(p,),
      in_specs=[
          pl.BlockSpec(memory_space=pltpu.SMEM),
          pl.BlockSpec((None, n, n2), lambda i: (i, 0, 0)),
      ],
      out_specs=pl.BlockSpec((None, N_HEADS_DIL, n, n2), lambda i: (i, 0, 0, 0)),
      out_shape=jax.ShapeDtypeStruct((p, N_HEADS_DIL, n, n2), F32),
      compiler_params=_params("parallel"),
      name="band_bias",
  )(rel_table, buckets)


def _dil_phases(q_ref, k_ref, v_ref, bm_ref, o_ref, *scratch):
  seq = q_ref.shape[0]
  n = WINDOW_KEYS
  seg = seq // RESIDUES
  last = len(DILATED_PATTERNS) - 1
  assert DILATED_PATTERNS[last][1] == RESIDUES and n == seg
  accs, lses = scratch[:last], scratch[last:]

  def gather(ref, starts, size):
    return jnp.concatenate([ref[st:st + size, :] for st in starts], axis=0).astype(BF16)

  def block_rows(segments, nb):
    rows = n // len(segments)
    q_starts = [s * seg + nb * rows for s in segments]
    if nb == 0:
      return rows, q_starts, q_starts, rows
    return rows, q_starts, [st - rows for st in q_starts], 2 * rows

  def scores(p, segments, nb):
    rows, q_starts, k_starts, k_rows = block_rows(segments, nb)
    bm = bm_ref[p, :, 2 * n:3 * n] if nb == 0 else bm_ref[p, :, 0:2 * n]
    return _dot_nt(gather(q_ref, q_starts, rows), gather(k_ref, k_starts, k_rows)) + bm

  def finish(p, segments, nb, e, m):
    rows, q_starts, k_starts, k_rows = block_rows(segments, nb)
    o, l = _weighted_values(e, gather(v_ref, k_starts, k_rows))
    o = o / l
    lse = m + jnp.log2(l)
    if p < last:
      for j, st in enumerate(q_starts):
        accs[p][st:st + rows, :] = o[j * rows:(j + 1) * rows, :]
        lses[p][st:st + rows, :] = lse[j * rows:(j + 1) * rows, :]
      return
    seg_rows = slice(q_starts[0], q_starts[0] + n)
    all_lse = [ref[seg_rows, :] for ref in lses] + [lse]
    all_out = [ref[seg_rows, :] for ref in accs] + [o]
    top = functools.reduce(jnp.maximum, all_lse)
    weights = [jnp.exp2(x - top) for x in all_lse]
    add = lambda a, b: a + b
    mixed = (functools.reduce(add, [w * a for w, a in zip(weights, all_out)])
             / functools.reduce(add, weights))
    o_ref[seg_rows, :] = mixed.astype(o_ref.dtype)

  blocks = [(p, list(range(r, RESIDUES, d)), nb)
            for p, (_, d) in enumerate(DILATED_PATTERNS)
            for r in range(d) for nb in range(seq // (n * d))]
  for g in range(0, len(blocks), DIL_BLOCK_GROUP):
    group = blocks[g:g + DIL_BLOCK_GROUP]
    ss = [scores(*blk) for blk in group]
    yield
    ms = [jnp.max(s, axis=-1, keepdims=True) for s in ss]
    es = [jnp.exp2(s - m) for s, m in zip(ss, ms)]
    yield
    for blk, e, m in zip(group, es, ms):
      finish(*blk, e, m)
    yield


def _mixer_kernel(qa_ref, ka_ref, va_ref, c_ref, qb_ref, kb_ref, vb_ref, bm_ref,
                  oa_ref, ob_ref, *scratch):
  streams = [_fox_phases(qa_ref, ka_ref, va_ref, c_ref, oa_ref),
             _dil_phases(qb_ref, kb_ref, vb_ref, bm_ref, ob_ref, *scratch)]
  while streams:
    for stream in list(streams):
      try:
        next(stream)
      except StopIteration:
        streams.remove(stream)


def _token_mixers(u_a, c, u_b, band_bias, batch, seq):
  assert seq // RESIDUES == WINDOW_KEYS and N_HEADS_FOX == N_HEADS_DIL
  blk = lambda off: pl.BlockSpec((seq, HEAD_DIM), lambda b, h: (b, off + h))
  n_pat = len(DILATED_PATTERNS)
  scratch = [pltpu.VMEM((seq, HEAD_DIM), F32) for _ in range(2 * (n_pat - 1))]
  out_blk = pl.BlockSpec((seq, HEAD_DIM), lambda b, h: (b, h))
  return pl.pallas_call(
      _mixer_kernel,
      grid=(batch, N_HEADS_FOX),
      in_specs=[
          blk(0), blk(N_HEADS_FOX), blk(2 * N_HEADS_FOX),
          pl.BlockSpec((None, N_HEADS_FOX, seq), lambda b, h: (b, 0, 0)),
          blk(0), blk(N_HEADS_DIL), blk(2 * N_HEADS_DIL),
          pl.BlockSpec((n_pat, None, WINDOW_KEYS, 3 * WINDOW_KEYS), lambda b, h: (0, h, 0, 0)),
      ],
      out_specs=[out_blk, out_blk],
      out_shape=[jax.ShapeDtypeStruct((batch * seq, D_FOX), BF16),
                 jax.ShapeDtypeStruct((batch * seq, D_DIL), BF16)],
      scratch_shapes=scratch,
      compiler_params=_params("parallel", "arbitrary"),
      name="token_mixers",
  )(u_a, u_a, u_a, c, u_b, u_b, u_b, band_bias)


def _out_proj_kernel(x_ref, a_ref, b_ref, swap_ref, wa_ref, wb_ref, *rest, jobs):
  n_in = sum(len(job.arrays) for job in jobs)
  side_in, o_ref, side_out = rest[:n_in], rest[n_in], rest[n_in + 1:]
  _run_side_jobs(jobs, side_in, side_out)
  per = PERM_ROWS // RESIDUES
  for a in range(x_ref.shape[0] // PERM_ROWS):
    rows = slice(a * PERM_ROWS, (a + 1) * PERM_ROWS)
    slab = jnp.concatenate([b_ref[r, a * per:(a + 1) * per, :] for r in range(RESIDUES)], axis=0)
    o_b = _dot(swap_ref[...], slab).astype(BF16)
    o_ref[rows, :] = x_ref[rows, :] + _dot(a_ref[rows, :], wa_ref[...]) + _dot(o_b, wb_ref[...])


def _out_proj(x, o_a, o_b, wo, batch, seq, jobs=()):
  m, d = x.shape
  tm = PROJ_ROW_TILE
  tiles = seq // tm
  ca, cb = o_a.shape[1], o_b.shape[1]
  assert ca == cb and wo.shape[0] == ca + cb
  resident = pl.Buffered(1)
  side_arrays, side_in_specs, side_out_specs, side_out_shapes = _side_args(jobs)
  return pl.pallas_call(
      functools.partial(_out_proj_kernel, jobs=tuple(jobs)),
      grid=(m // tm,),
      in_specs=[
          pl.BlockSpec((tm, d), lambda i: (i, 0)),
          pl.BlockSpec((tm, ca), lambda i: (i, 0)),
          pl.BlockSpec((RESIDUES, tm // RESIDUES, cb), lambda i: (i // tiles, i % tiles, 0)),
          pl.BlockSpec((PERM_ROWS, PERM_ROWS), lambda i: (0, 0)),
          pl.BlockSpec((ca, d), lambda i: (0, 0), pipeline_mode=resident),
          pl.BlockSpec((cb, d), lambda i: (1, 0), pipeline_mode=resident),
      ] + side_in_specs,
      out_specs=[pl.BlockSpec((tm, d), lambda i: (i, 0))] + side_out_specs,
      out_shape=[jax.ShapeDtypeStruct((m, d), F32)] + side_out_shapes,
      compiler_params=_params("parallel"),
      name="out_proj",
  )(x, o_a, o_b.reshape(batch * RESIDUES, seq // RESIDUES, cb), _swap_matrix(), wo, wo,
    *side_arrays)


def _ple_kernel(x_ref, p_ref, g_ref, wg_ref, wp_ref, gf_ref, o_ref, *, final_norm):
  x = x_ref[...]
  h = _rmsnorm(x, g_ref[...]).astype(BF16)
  gate = jax.nn.sigmoid(_dot(h, wg_ref[...]))
  y = x + gate * _dot(p_ref[...].astype(BF16), wp_ref[...])
  o_ref[...] = _rmsnorm(y, gf_ref[...]) if final_norm else y


def _ple(x, p, g, w_gate, w_proj, g_final, final_norm):
  m, d = x.shape
  tm = PROJ_ROW_TILE
  return pl.pallas_call(
      functools.partial(_ple_kernel, final_norm=final_norm),
      grid=(m // tm,),
      in_specs=[
          pl.BlockSpec((tm, d), lambda i: (i, 0)),
          pl.BlockSpec((tm, p.shape[1]), lambda i: (i, 0)),
          pl.BlockSpec((1, d), lambda i: (0, 0)),
          pl.BlockSpec(w_gate.shape, lambda i: (0, 0)),
          pl.BlockSpec(w_proj.shape, lambda i: (0, 0)),
          pl.BlockSpec((1, d), lambda i: (0, 0)),
      ],
      out_specs=pl.BlockSpec((tm, d), lambda i: (i, 0)),
      out_shape=jax.ShapeDtypeStruct((m, d), F32),
      compiler_params=_params("parallel"),
      name="ple",
  )(x, p, g, w_gate, w_proj, g_final)


def kernel(x, p, norm_ffn1, ffn1_w_gate, ffn1_w_up, ffn1_w_down, norm_mix, w_in, b_f, w_o,
           norm_ffn2, ffn2_w_gate, ffn2_w_up, ffn2_w_down, norm_ple, w_ple_gate, w_ple_proj,
           rel_table, norm_final):
  batch, seq, d = x.shape
  depth = p.shape[0]
  m = batch * seq
  bf = lambda w: w.astype(BF16)
  row = lambda g: g.reshape(1, -1).astype(F32)

  band_bias = _band_bias(rel_table.astype(F32))
  xs = x.reshape(m, d).astype(F32)
  for i in range(depth):
    g_ffn1 = row(norm_ffn1[i])
    head, wg1, wu1, wd1 = _ffn_head(xs, g_ffn1, ffn1_w_gate[i], ffn1_w_up[i], ffn1_w_down[i])
    xs, w_a, w_b, w_f = _ffn(
        xs, g_ffn1, wg1, wu1, wd1, head=head,
        jobs=_w_in_repack_jobs(m, jnp.swapaxes(w_in[i], 0, 1), 3 * D_FOX, N_HEADS_FOX, 3 * D_DIL))

    b_f_row = jnp.pad(b_f[i].astype(F32), (0, V7X_LANES - N_HEADS_FOX)).reshape(1, V7X_LANES)
    g_mix = row(norm_mix[i])
    proj_steps = m // PROJ_ROW_TILE
    u_a, f_logit, wo = _norm_matmul(
        xs, g_mix, w_a, BF16, (D_FOX, 2 * D_FOX), w_narrow=w_f,
        jobs=_slab_cast_jobs(proj_steps, w_o[i]))
    u_b, wg2, wu2, wd2 = _norm_matmul(
        xs, g_mix, w_b, F32, (D_DIL, 2 * D_DIL), residue_major=(batch, seq),
        jobs=_slab_cast_jobs(proj_steps, ffn2_w_gate[i], ffn2_w_up[i])
        + _slab_cast_jobs(proj_steps, ffn2_w_down[i], scale=0.5))

    c = _fox_decay(f_logit, b_f_row, batch, seq)
    o_a, o_b = _token_mixers(u_a, c, u_b, band_bias, batch, seq)
    xs, w_gate, w_ple = _out_proj(xs, o_a, o_b, wo, batch, seq,
                                  jobs=_slab_cast_jobs(proj_steps, w_ple_gate[i], w_ple_proj[i]))

    xs, = _ffn(xs, row(norm_ffn2[i]), wg2, wu2, wd2)
    last = i == depth - 1
    xs = _ple(xs, p[i].reshape(m, -1), row(norm_ple[i]), w_gate, w_ple,
              row(norm_final), final_norm=last)
  return xs.reshape(batch, seq, d).astype(x.dtype)
```

```python
import functools
import math
from typing import Any, Callable, NamedTuple

import jax
import jax.numpy as jnp
import numpy as np
from jax import lax
from jax.experimental import pallas as pl
from jax.experimental.pallas import tpu as pltpu

F32 = jnp.float32
BF16 = jnp.bfloat16

HEAD_DIM = 128
N_HEADS_FOX = 8
N_HEADS_DIL = 8
D_FOX = N_HEADS_FOX * HEAD_DIM
D_DIL = N_HEADS_DIL * HEAD_DIM
DILATED_PATTERNS = ((128, 1), (512, 4), (2048, 16))
WINDOW_KEYS = 128
N_REL_BUCKETS = 32
REL_MAX_DISTANCE = 2048
RMS_EPS = 1e-6
NEG_INF = -1e30
SCALE = HEAD_DIM ** -0.5
LOG2_E = math.log2(math.e)

V7X_LANES = 128
BF16_TILE_ROWS = 16
V7X_VMEM_LIMIT_BYTES = 56 * 1024 * 1024

RESIDUES = max(d for _, d in DILATED_PATTERNS)
PERM_ROWS = RESIDUES * RESIDUES

FFN_ROW_TILE = 1024
FFN_SUB_ROWS = 512
FFN_FF_TILE = 512
FFN_HEAD_FF_TILE = 256
PROJ_ROW_TILE = 512
FOX_Q_TILE = 256
FOX_LOOKAHEAD = 2
DIL_BLOCK_GROUP = 12


def _params(*semantics):
  return pltpu.CompilerParams(dimension_semantics=semantics,
                              vmem_limit_bytes=V7X_VMEM_LIMIT_BYTES)


def _rmsnorm(x, g):
  ms = jnp.mean(x * x, axis=-1, keepdims=True)
  return x * lax.rsqrt(ms + RMS_EPS) * g


def _dot(a, b):
  return jnp.dot(a, b, preferred_element_type=F32)


def _dot_nt(a, b):
  return lax.dot_general(a, b, (((1,), (1,)), ((), ())), preferred_element_type=F32)


def _weighted_values(e, v):
  both = _dot(e.astype(BF16), jnp.concatenate([v, jnp.ones_like(v)], axis=1))
  return both[:, :HEAD_DIM], both[:, HEAD_DIM:]


class _SideJob(NamedTuple):
  arrays: tuple
  in_specs: tuple
  out_shape: Any
  out_spec: Any
  fn: Callable


def _cast_job(a, block, index, scale=1.0):
  spec = pl.BlockSpec(block, index)
  return _SideJob((a,), (spec,), jax.ShapeDtypeStruct(a.shape, BF16), spec,
                  lambda r: r[...] * scale)


def _run_side_jobs(jobs, in_refs, out_refs):
  in_refs = list(in_refs)
  for job, out_ref in zip(jobs, out_refs):
    refs = [in_refs.pop(0) for _ in job.arrays]
    out_ref[...] = job.fn(*refs).astype(out_ref.dtype)


def _side_args(jobs):
  arrays = [a for job in jobs for a in job.arrays]
  in_specs = [s for job in jobs for s in job.in_specs]
  return arrays, in_specs, [job.out_spec for job in jobs], [job.out_shape for job in jobs]


def _ffn_head_kernel(x_ref, g_ref, wg_ref, wu_ref, wd_ref, *rest, jobs):
  n_in = sum(len(job.arrays) for job in jobs)
  side_in, (o_ref, og_ref, ou_ref, od_ref) = rest[:n_in], rest[n_in:n_in + 4]
  side_out, h_ref = rest[n_in + 4:-1], rest[-1]
  _run_side_jobs(jobs, side_in, side_out)

  @pl.when(pl.program_id(0) == 0)
  def _():
    x = x_ref[...]
    h_ref[...] = _rmsnorm(x, g_ref[...]).astype(BF16)
    o_ref[...] = x

  wg = wg_ref[...].astype(BF16)
  wu = wu_ref[...].astype(BF16)
  wd = (wd_ref[...] * 0.5).astype(BF16)
  og_ref[...] = wg
  ou_ref[...] = wu
  od_ref[...] = wd
  for r in range(h_ref.shape[0] // FFN_SUB_ROWS):
    rows = slice(r * FFN_SUB_ROWS, (r + 1) * FFN_SUB_ROWS)
    h = h_ref[rows, :]
    gate = _dot(h, wg)
    up = _dot(h, wu)
    act = (gate * jax.nn.sigmoid(gate)) * up
    o_ref[rows, :] += _dot(act.astype(BF16), wd)


def _ffn_head_steps(w_gate):
  return w_gate.shape[1] // FFN_HEAD_FF_TILE


def _ffn_head(x, g, w_gate, w_up, w_down, jobs=()):
  d = x.shape[1]
  tm, tf = FFN_ROW_TILE, FFN_HEAD_FF_TILE
  side_arrays, side_in_specs, side_out_specs, side_out_shapes = _side_args(jobs)
  return pl.pallas_call(
      functools.partial(_ffn_head_kernel, jobs=tuple(jobs)),
      grid=(_ffn_head_steps(w_gate),),
      in_specs=[
          pl.BlockSpec((tm, d), lambda j: (0, 0), pipeline_mode=pl.Buffered(1)),
          pl.BlockSpec((1, d), lambda j: (0, 0)),
          pl.BlockSpec((d, tf), lambda j: (0, j)),
          pl.BlockSpec((d, tf), lambda j: (0, j)),
          pl.BlockSpec((tf, d), lambda j: (j, 0)),
      ] + side_in_specs,
      out_specs=[
          pl.BlockSpec((tm, d), lambda j: (0, 0)),
          pl.BlockSpec((d, tf), lambda j: (0, j)),
          pl.BlockSpec((d, tf), lambda j: (0, j)),
          pl.BlockSpec((tf, d), lambda j: (j, 0)),
      ] + side_out_specs,
      out_shape=[
          jax.ShapeDtypeStruct((tm, d), F32),
          jax.ShapeDtypeStruct(w_gate.shape, BF16),
          jax.ShapeDtypeStruct(w_up.shape, BF16),
          jax.ShapeDtypeStruct(w_down.shape, BF16),
      ] + side_out_shapes,
      scratch_shapes=[pltpu.VMEM((tm, d), BF16)],
      compiler_params=_params("arbitrary"),
      name="ffn_head",
  )(x, g, w_gate, w_up, w_down, *side_arrays)


def _ffn_kernel(x_ref, g_ref, wg_ref, wu_ref, wd_ref, *rest, jobs, has_head):
  rest = list(rest)
  head_ref = rest.pop(0) if has_head else None
  copy_sem = rest.pop() if has_head else None
  n_in = sum(len(job.arrays) for job in jobs)
  side_in, o_ref, side_out, h_ref = rest[:n_in], rest[n_in], rest[n_in + 1:-1], rest[-1]

  def step(first):
    _run_side_jobs(jobs, side_in, side_out)
    for r in range(h_ref.shape[0] // FFN_SUB_ROWS):
      rows = slice(r * FFN_SUB_ROWS, (r + 1) * FFN_SUB_ROWS)
      if first:
        base = x_ref[rows, :]
        h = _rmsnorm(base, g_ref[...]).astype(BF16)
        h_ref[rows, :] = h
      else:
        base = o_ref[rows, :]
        h = h_ref[rows, :]
      gate = _dot(h, wg_ref[...])
      up = _dot(h, wu_ref[...])
      act = (gate * jax.nn.sigmoid(gate)) * up
      o_ref[rows, :] = base + _dot(act.astype(BF16), wd_ref[...])

  def compute_tile():
    lax.cond(pl.program_id(1) == 0, lambda: step(True), lambda: step(False))

  def copy_head_tile():
    _run_side_jobs(jobs, side_in, side_out)

    @pl.when(pl.program_id(1) == 0)
    def _():
      copy = pltpu.make_async_copy(head_ref, o_ref, copy_sem)
      copy.start()
      copy.wait()

  if has_head:
    lax.cond(pl.program_id(0) == 0, copy_head_tile, compute_tile)
  else:
    compute_tile()


def _ffn(x, g, wg, wu, wd_half, jobs=(), head=None):
  m, d = x.shape
  dff = wg.shape[1]
  tm, tf = FFN_ROW_TILE, FFN_FF_TILE
  has_head = head is not None
  chunk = (lambda i, j: jnp.where(i == 0, 0, j)) if has_head else (lambda i, j: j)
  side_arrays, side_in_specs, side_out_specs, side_out_shapes = _side_args(jobs)
  head_args = [head] if has_head else []
  head_specs = [pl.BlockSpec(memory_space=pl.ANY)] if has_head else []
  head_scratch = [pltpu.SemaphoreType.DMA(())] if has_head else []
  return pl.pallas_call(
      functools.partial(_ffn_kernel, jobs=tuple(jobs), has_head=has_head),
      grid=(m // tm, dff // tf),
      in_specs=[
          pl.BlockSpec((tm, d), lambda i, j: (i, 0)),
          pl.BlockSpec((1, d), lambda i, j: (0, 0)),
          pl.BlockSpec((d, tf), lambda i, j: (0, chunk(i, j))),
          pl.BlockSpec((d, tf), lambda i, j: (0, chunk(i, j))),
          pl.BlockSpec((tf, d), lambda i, j: (chunk(i, j), 0)),
      ] + head_specs + side_in_specs,
      out_specs=[pl.BlockSpec((tm, d), lambda i, j: (i, 0))] + side_out_specs,
      out_shape=[jax.ShapeDtypeStruct((m, d), F32)] + side_out_shapes,
      scratch_shapes=[pltpu.VMEM((tm, d), BF16)] + head_scratch,
      compiler_params=_params("arbitrary" if has_head else "parallel", "arbitrary"),
      name="ffn",
  )(x, g, wg, wu, wd_half, *head_args, *side_arrays)


def _slab_cast_jobs(steps, *weights, scale=1.0):
  jobs = []
  for w in weights:
    rows = max(BF16_TILE_ROWS, w.shape[0] // steps)
    assert w.shape[0] % rows == 0 and rows % BF16_TILE_ROWS == 0
    last = w.shape[0] // rows - 1
    jobs.append(_cast_job(w, (rows, w.shape[1]), lambda t, last=last: (jnp.minimum(t, last), 0),
                          scale=scale))
  return jobs


def _w_in_repack_jobs(steps, w_t, n_a, n_f, n_b):
  d = w_t.shape[1]
  lanes = V7X_LANES
  slabs = d // lanes
  assert d % lanes == 0 and slabs <= steps and n_f % 8 == 0 and n_f <= lanes
  assert n_a % n_f == 0 and n_b <= n_a + n_f
  slab = lambda t: jnp.minimum(t, slabs - 1)

  def gate_rows(ref):
    return jnp.concatenate([ref[...], jnp.zeros((lanes - n_f, lanes), F32)], axis=0).T

  out = lambda n: jax.ShapeDtypeStruct((d, n), BF16)
  out_spec = lambda n: pl.BlockSpec((lanes, n), lambda t: (slab(t), 0))
  return [
      _SideJob((w_t,), (pl.BlockSpec((n_a, lanes), lambda t: (0, slab(t))),),
               out(n_a), out_spec(n_a), lambda ref: ref[...].T),
      _SideJob((w_t,), (pl.BlockSpec((n_a + n_f, lanes), lambda t: (1, slab(t))),),
               out(n_b), out_spec(n_b), lambda ref: ref[:n_b, :].T),
      _SideJob((w_t,), (pl.BlockSpec((n_f, lanes), lambda t: (n_a // n_f, slab(t))),),
               out(lanes), out_spec(lanes), gate_rows),
  ]


def _swap_matrix():
  i = np.arange(PERM_ROWS)
  src = (i % RESIDUES) * RESIDUES + i // RESIDUES
  return jnp.asarray(np.eye(PERM_ROWS, dtype=np.float32)[src], BF16)


def _norm_matmul_kernel(x_ref, g_ref, w_ref, *rest, permute, narrow, jobs, key_cols):
  rest = list(rest)
  swap_ref = rest.pop(0) if permute else None
  wn_ref = rest.pop(0) if narrow else None
  side_in = [rest.pop(0) for job in jobs for _ in job.arrays]
  o_ref = rest.pop(0)
  on_ref = rest.pop(0) if narrow else None
  side_out = [rest.pop(0) for _ in jobs]
  assert not rest, "unexpected extra refs"
  tm = x_ref.shape[0]
  _run_side_jobs(jobs, side_in, side_out)

  h = _rmsnorm(x_ref[...], g_ref[...]).astype(BF16)
  if permute:
    h = jnp.concatenate(
        [_dot(swap_ref[...], h[a * PERM_ROWS:(a + 1) * PERM_ROWS, :]).astype(BF16)
         for a in range(tm // PERM_ROWS)], axis=0)
  if narrow:
    on_ref[...] = _dot(h, wn_ref[...])
  acc = _dot(h, w_ref[...])
  lo, hi = key_cols
  res = jnp.concatenate([acc[:, :lo], acc[:, lo:hi] * (SCALE * LOG2_E), acc[:, hi:]],
                        axis=1).astype(o_ref.dtype)
  if permute:
    per = PERM_ROWS // RESIDUES
    for a in range(tm // PERM_ROWS):
      for r in range(RESIDUES):
        start = a * PERM_ROWS + r * per
        o_ref[r, a * per:(a + 1) * per, :] = res[start:start + per, :]
  else:
    o_ref[...] = res


def _norm_matmul(x, g, w, out_dtype, key_cols, residue_major=None, w_narrow=None, jobs=()):
  m, d = x.shape
  n = w.shape[1]
  tm = PROJ_ROW_TILE
  narrow = w_narrow is not None
  permute = residue_major is not None
  assert not (narrow and permute)
  in_specs = [
      pl.BlockSpec((tm, d), lambda i: (i, 0)),
      pl.BlockSpec((1, d), lambda i: (0, 0)),
      pl.BlockSpec((d, n), lambda i: (0, 0)),
  ]
  args = [x, g, w]
  if permute:
    batch, seq = residue_major
    tiles = seq // tm
    in_specs.append(pl.BlockSpec((PERM_ROWS, PERM_ROWS), lambda i: (0, 0)))
    args.append(_swap_matrix())
    out_specs = [pl.BlockSpec((RESIDUES, tm // RESIDUES, n), lambda i: (i // tiles, i % tiles, 0))]
    out_shape = [jax.ShapeDtypeStruct((batch * RESIDUES, seq // RESIDUES, n), out_dtype)]
  else:
    out_specs = [pl.BlockSpec((tm, n), lambda i: (i, 0))]
    out_shape = [jax.ShapeDtypeStruct((m, n), out_dtype)]
  if narrow:
    in_specs.append(pl.BlockSpec(w_narrow.shape, lambda i: (0, 0)))
    args.append(w_narrow)
    out_specs.append(pl.BlockSpec((tm, w_narrow.shape[1]), lambda i: (i, 0)))
    out_shape.append(jax.ShapeDtypeStruct((m, w_narrow.shape[1]), F32))
  side_arrays, side_in_specs, side_out_specs, side_out_shapes = _side_args(jobs)
  outs = pl.pallas_call(
      functools.partial(_norm_matmul_kernel, permute=permute, narrow=narrow, jobs=tuple(jobs),
                        key_cols=key_cols),
      grid=(m // tm,),
      in_specs=in_specs + side_in_specs,
      out_specs=out_specs + side_out_specs,
      out_shape=out_shape + side_out_shapes,
      compiler_params=_params("parallel"),
      name="norm_matmul",
  )(*args, *side_arrays)
  outs = list(outs)
  if permute:
    outs[0] = outs[0].reshape(m, n)
  return outs


def _cumsum_kernel(fl_ref, bf_ref, c_ref):
  s = fl_ref.shape[0]
  z = fl_ref[...] + bf_ref[...]
  logf = jnp.minimum(z, 0.0) - jnp.log1p(jnp.exp(-jnp.abs(z)))
  lt = logf.T[0:N_HEADS_FOX, :]
  row = lax.broadcasted_iota(jnp.int32, (V7X_LANES, V7X_LANES), 0)
  col = lax.broadcasted_iota(jnp.int32, (V7X_LANES, V7X_LANES), 1)
  upper = (row <= col).astype(F32)
  carry = jnp.zeros((N_HEADS_FOX, 1), F32)
  for j in range(s // V7X_LANES):
    blk = lt[:, j * V7X_LANES:(j + 1) * V7X_LANES]
    cs = jnp.dot(blk, upper, preferred_element_type=F32,
                 precision=lax.Precision.HIGHEST) + carry
    c_ref[:, j * V7X_LANES:(j + 1) * V7X_LANES] = cs
    carry = cs[:, V7X_LANES - 1:V7X_LANES]


def _fox_decay(f_logit, b_f_row, batch, seq):
  return pl.pallas_call(
      _cumsum_kernel,
      grid=(batch,),
      in_specs=[
          pl.BlockSpec((seq, V7X_LANES), lambda b: (b, 0)),
          pl.BlockSpec((1, V7X_LANES), lambda b: (0, 0)),
      ],
      out_specs=pl.BlockSpec((None, N_HEADS_FOX, seq), lambda b: (b, 0, 0)),
      out_shape=jax.ShapeDtypeStruct((batch, N_HEADS_FOX, seq), F32),
      compiler_params=_params("parallel"),
      name="fox_decay",
  )(f_logit, b_f_row)


def _fox_phases(q_ref, k_ref, v_ref, c_ref, o_ref):
  seq = q_ref.shape[0]
  tq = FOX_Q_TILE
  h = pl.program_id(1)
  crow = c_ref[pl.ds(h, 1), :] * LOG2_E
  row = lax.broadcasted_iota(jnp.int32, (tq, tq), 0)
  col = lax.broadcasted_iota(jnp.int32, (tq, tq), 1)
  diag_mask = jnp.where(col > row, NEG_INF, 0.0).astype(F32)

  def scores(i):
    t0, t1 = i * tq, (i + 1) * tq
    q = q_ref[t0:t1, :]
    bias = crow[:, t1 - 1:t1] - crow[:, 0:t1]
    s_diag = _dot_nt(q, k_ref[t0:t1, :]) + bias[:, t0:t1] + diag_mask
    s_off = _dot_nt(q, k_ref[0:t0, :]) + bias[:, 0:t0] if i > 0 else None
    return s_diag, s_off

  def finish(i, s_diag, s_off):
    t0, t1 = i * tq, (i + 1) * tq
    m = jnp.max(s_diag, axis=-1, keepdims=True)
    if i > 0:
      m = jnp.maximum(m, jnp.max(s_off, axis=-1, keepdims=True))
    o, l = _weighted_values(jnp.exp2(s_diag - m), v_ref[t0:t1, :])
    if i > 0:
      o_off, l_off = _weighted_values(jnp.exp2(s_off - m), v_ref[0:t0, :])
      o, l = o + o_off, l + l_off
    o_ref[t0:t1, :] = (o / l).astype(o_ref.dtype)

  n_tiles = seq // tq
  pending = [scores(i) for i in range(min(FOX_LOOKAHEAD, n_tiles))]
  yield
  for i in range(n_tiles):
    if i + FOX_LOOKAHEAD < n_tiles:
      pending.append(scores(i + FOX_LOOKAHEAD))
      yield
    finish(i, *pending.pop(0))
    yield


def _t5_bucket_np(dist):
  max_exact = N_REL_BUCKETS // 2
  d = np.maximum(dist, 1).astype(np.float32)
  large = max_exact + (np.log(d / np.float32(max_exact))
                       / np.float32(math.log(REL_MAX_DISTANCE / max_exact))
                       * np.float32(N_REL_BUCKETS - max_exact)).astype(np.int32)
  large = np.minimum(large, N_REL_BUCKETS - 1)
  return np.where(dist < max_exact, dist, large).astype(np.int32)


def _block_positions(dilation):
  n = WINDOW_KEYS
  m = RESIDUES // dilation
  rows = n // m
  j = np.arange(m)[:, None]
  qpos = (n + m * np.arange(rows)[None, :] + j).reshape(-1)
  kpos = (m * np.arange(2 * rows)[None, :] + j).reshape(-1)
  return qpos, kpos


def _band_buckets():
  n = WINDOW_KEYS
  tiles = []
  for _, dilation in DILATED_PATTERNS:
    qpos, kpos = _block_positions(dilation)
    rel = qpos[:, None] - np.concatenate([kpos, qpos])[None, :]
    valid = (rel >= 0) & (rel <= n)
    bucket = _t5_bucket_np(np.maximum(rel, 0) * dilation)
    tiles.append(np.where(valid, bucket, -1))
  return np.stack(tiles).astype(np.int32)


def _bias_kernel(tab_ref, bkt_ref, o_ref):
  bkt = bkt_ref[...]
  for h in range(N_HEADS_DIL):
    acc = jnp.full(bkt.shape, NEG_INF, F32)
    for b in range(N_REL_BUCKETS):
      acc = jnp.where(bkt == b, tab_ref[b, h] * LOG2_E, acc)
    o_ref[h] = acc


def _band_bias(rel_table):
  buckets = jnp.asarray(_band_buckets())
  p, n, n2 = buckets.shape
  return pl.pallas_call(
      _bias_kernel,
      grid=(p,),
      in_specs=[
          pl.BlockSpec(memory_space=pltpu.SMEM),
          pl.BlockSpec((None, n, n2), lambda i: (i, 0, 0)),
      ],
      out_specs=pl.BlockSpec((None, N_HEADS_DIL, n, n2), lambda i: (i, 0, 0, 0)),
      out_shape=jax.ShapeDtypeStruct((p, N_HEADS_DIL, n, n2), F32),
      compiler_params=_params("parallel"),
      name="band_bias",
  )(rel_table, buckets)


def _dil_phases(q_ref, k_ref, v_ref, bm_ref, o_ref, *scratch):
  seq = q_ref.shape[0]
  n = WINDOW_KEYS
  seg = seq // RESIDUES
  last = len(DILATED_PATTERNS) - 1
  assert DILATED_PATTERNS[last][1] == RESIDUES and n == seg
  accs, lses = scratch[:last], scratch[last:]

  def gather(ref, starts, size):
    return jnp.concatenate([ref[st:st + size, :] for st in starts], axis=0).astype(BF16)

  def block_rows(segments, nb):
    rows = n // len(segments)
    q_starts = [s * seg + nb * rows for s in segments]
    if nb == 0:
      return rows, q_starts, q_starts, rows
    return rows, q_starts, [st - rows for st in q_starts], 2 * rows

  def scores(p, segments, nb):
    rows, q_starts, k_starts, k_rows = block_rows(segments, nb)
    bm = bm_ref[p, :, 2 * n:3 * n] if nb == 0 else bm_ref[p, :, 0:2 * n]
    return _dot_nt(gather(q_ref, q_starts, rows), gather(k_ref, k_starts, k_rows)) + bm

  def finish(p, segments, nb, e, m):
    rows, q_starts, k_starts, k_rows = block_rows(segments, nb)
    o, l = _weighted_values(e, gather(v_ref, k_starts, k_rows))
    o = o / l
    lse = m + jnp.log2(l)
    if p < last:
      for j, st in enumerate(q_starts):
        accs[p][st:st + rows, :] = o[j * rows:(j + 1) * rows, :]
        lses[p][st:st + rows, :] = lse[j * rows:(j + 1) * rows, :]
      return
    seg_rows = slice(q_starts[0], q_starts[0] + n)
    all_lse = [ref[seg_rows, :] for ref in lses] + [lse]
    all_out = [ref[seg_rows, :] for ref in accs] + [o]
    top = functools.reduce(jnp.maximum, all_lse)
    weights = [jnp.exp2(x - top) for x in all_lse]
    add = lambda a, b: a + b
    mixed = (functools.reduce(add, [w * a for w, a in zip(weights, all_out)])
             / functools.reduce(add, weights))
    o_ref[seg_rows, :] = mixed.astype(o_ref.dtype)

  blocks = [(p, list(range(r, RESIDUES, d)), nb)
            for p, (_, d) in enumerate(DILATED_PATTERNS)
            for r in range(d) for nb in range(seq // (n * d))]
  for g in range(0, len(blocks), DIL_BLOCK_GROUP):
    group = blocks[g:g + DIL_BLOCK_GROUP]
    ss = [scores(*blk) for blk in group]
    yield
    ms = [jnp.max(s, axis=-1, keepdims=True) for s in ss]
    es = [jnp.exp2(s - m) for s, m in zip(ss, ms)]
    yield
    for blk, e, m in zip(group, es, ms):
      finish(*blk, e, m)
    yield


def _mixer_kernel(qa_ref, ka_ref, va_ref, c_ref, qb_ref, kb_ref, vb_ref, bm_ref,
                  oa_ref, ob_ref, *scratch):
  streams = [_fox_phases(qa_ref, ka_ref, va_ref, c_ref, oa_ref),
             _dil_phases(qb_ref, kb_ref, vb_ref, bm_ref, ob_ref, *scratch)]
  while streams:
    for stream in list(streams):
      try:
        next(stream)
      except StopIteration:
        streams.remove(stream)


def _token_mixers(u_a, c, u_b, band_bias, batch, seq):
  assert seq // RESIDUES == WINDOW_KEYS and N_HEADS_FOX == N_HEADS_DIL
  blk = lambda off: pl.BlockSpec((seq, HEAD_DIM), lambda b, h: (b, off + h))
  n_pat = len(DILATED_PATTERNS)
  scratch = [pltpu.VMEM((seq, HEAD_DIM), F32) for _ in range(2 * (n_pat - 1))]
  out_blk = pl.BlockSpec((seq, HEAD_DIM), lambda b, h: (b, h))
  return pl.pallas_call(
      _mixer_kernel,
      grid=(batch, N_HEADS_FOX),
      in_specs=[
          blk(0), blk(N_HEADS_FOX), blk(2 * N_HEADS_FOX),
          pl.BlockSpec((None, N_HEADS_FOX, seq), lambda b, h: (b, 0, 0)),
          blk(0), blk(N_HEADS_DIL), blk(2 * N_HEADS_DIL),
          pl.BlockSpec((n_pat, None, WINDOW_KEYS, 3 * WINDOW_KEYS), lambda b, h: (0, h, 0, 0)),
      ],
      out_specs=[out_blk, out_blk],
      out_shape=[jax.ShapeDtypeStruct((batch * seq, D_FOX), BF16),
                 jax.ShapeDtypeStruct((batch * seq, D_DIL), BF16)],
      scratch_shapes=scratch,
      compiler_params=_params("parallel", "arbitrary"),
      name="token_mixers",
  )(u_a, u_a, u_a, c, u_b, u_b, u_b, band_bias)


def _out_proj_kernel(x_ref, a_ref, b_ref, swap_ref, wa_ref, wb_ref, *rest, jobs):
  n_in = sum(len(job.arrays) for job in jobs)
  side_in, o_ref, side_out = rest[:n_in], rest[n_in], rest[n_in + 1:]
  _run_side_jobs(jobs, side_in, side_out)
  per = PERM_ROWS // RESIDUES
  for a in range(x_ref.shape[0] // PERM_ROWS):
    rows = slice(a * PERM_ROWS, (a + 1) * PERM_ROWS)
    slab = jnp.concatenate([b_ref[r, a * per:(a + 1) * per, :] for r in range(RESIDUES)], axis=0)
    o_b = _dot(swap_ref[...], slab).astype(BF16)
    o_ref[rows, :] = x_ref[rows, :] + _dot(a_ref[rows, :], wa_ref[...]) + _dot(o_b, wb_ref[...])


def _out_proj(x, o_a, o_b, wo, batch, seq, jobs=()):
  m, d = x.shape
  tm = PROJ_ROW_TILE
  tiles = seq // tm
  ca, cb = o_a.shape[1], o_b.shape[1]
  assert ca == cb and wo.shape[0] == ca + cb
  resident = pl.Buffered(1)
  side_arrays, side_in_specs, side_out_specs, side_out_shapes = _side_args(jobs)
  return pl.pallas_call(
      functools.partial(_out_proj_kernel, jobs=tuple(jobs)),
      grid=(m // tm,),
      in_specs=[
          pl.BlockSpec((tm, d), lambda i: (i, 0)),
          pl.BlockSpec((tm, ca), lambda i: (i, 0)),
          pl.BlockSpec((RESIDUES, tm // RESIDUES, cb), lambda i: (i // tiles, i % tiles, 0)),
          pl.BlockSpec((PERM_ROWS, PERM_ROWS), lambda i: (0, 0)),
          pl.BlockSpec((ca, d), lambda i: (0, 0), pipeline_mode=resident),
          pl.BlockSpec((cb, d), lambda i: (1, 0), pipeline_mode=resident),
      ] + side_in_specs,
      out_specs=[pl.BlockSpec((tm, d), lambda i: (i, 0))] + side_out_specs,
      out_shape=[jax.ShapeDtypeStruct((m, d), F32)] + side_out_shapes,
      compiler_params=_params("parallel"),
      name="out_proj",
  )(x, o_a, o_b.reshape(batch * RESIDUES, seq // RESIDUES, cb), _swap_matrix(), wo, wo,
    *side_arrays)


def _ple_kernel(x_ref, p_ref, g_ref, wg_ref, wp_ref, gf_ref, o_ref, *, final_norm):
  x = x_ref[...]
  h = _rmsnorm(x, g_ref[...]).astype(BF16)
  gate = jax.nn.sigmoid(_dot(h, wg_ref[...]))
  y = x + gate * _dot(p_ref[...].astype(BF16), wp_ref[...])
  o_ref[...] = _rmsnorm(y, gf_ref[...]) if final_norm else y


def _ple(x, p, g, w_gate, w_proj, g_final, final_norm):
  m, d = x.shape
  tm = PROJ_ROW_TILE
  return pl.pallas_call(
      functools.partial(_ple_kernel, final_norm=final_norm),
      grid=(m // tm,),
      in_specs=[
          pl.BlockSpec((tm, d), lambda i: (i, 0)),
          pl.BlockSpec((tm, p.shape[1]), lambda i: (i, 0)),
          pl.BlockSpec((1, d), lambda i: (0, 0)),
          pl.BlockSpec(w_gate.shape, lambda i: (0, 0)),
          pl.BlockSpec(w_proj.shape, lambda i: (0, 0)),
          pl.BlockSpec((1, d), lambda i: (0, 0)),
      ],
      out_specs=pl.BlockSpec((tm, d), lambda i: (i, 0)),
      out_shape=jax.ShapeDtypeStruct((m, d), F32),
      compiler_params=_params("parallel"),
      name="ple",
  )(x, p, g, w_gate, w_proj, g_final)


def kernel(x, p, norm_ffn1, ffn1_w_gate, ffn1_w_up, ffn1_w_down, norm_mix, w_in, b_f, w_o,
           norm_ffn2, ffn2_w_gate, ffn2_w_up, ffn2_w_down, norm_ple, w_ple_gate, w_ple_proj,
           rel_table, norm_final):
  batch, seq, d = x.shape
  depth = p.shape[0]
  m = batch * seq
  bf = lambda w: w.astype(BF16)
  row = lambda g: g.reshape(1, -1).astype(F32)

  band_bias = _band_bias(rel_table.astype(F32))
  xs = x.reshape(m, d).astype(F32)
  for i in range(depth):
    g_ffn1 = row(norm_ffn1[i])
    head, wg1, wu1, wd1, w_a, w_b, w_f = _ffn_head(
        xs, g_ffn1, ffn1_w_gate[i], ffn1_w_up[i], ffn1_w_down[i],
        jobs=_w_in_repack_jobs(_ffn_head_steps(ffn1_w_gate[i]), jnp.swapaxes(w_in[i], 0, 1),
                               3 * D_FOX, N_HEADS_FOX, 3 * D_DIL))
    xs, = _ffn(xs, g_ffn1, wg1, wu1, wd1, head=head)

    b_f_row = jnp.pad(b_f[i].astype(F32), (0, V7X_LANES - N_HEADS_FOX)).reshape(1, V7X_LANES)
    g_mix = row(norm_mix[i])
    proj_steps = m // PROJ_ROW_TILE
    u_a, f_logit, wo = _norm_matmul(
        xs, g_mix, w_a, BF16, (D_FOX, 2 * D_FOX), w_narrow=w_f,
        jobs=_slab_cast_jobs(proj_steps, w_o[i]))
    u_b, wg2, wu2, wd2 = _norm_matmul(
        xs, g_mix, w_b, F32, (D_DIL, 2 * D_DIL), residue_major=(batch, seq),
        jobs=_slab_cast_jobs(proj_steps, ffn2_w_gate[i], ffn2_w_up[i])
        + _slab_cast_jobs(proj_steps, ffn2_w_down[i], scale=0.5))

    c = _fox_decay(f_logit, b_f_row, batch, seq)
    o_a, o_b = _token_mixers(u_a, c, u_b, band_bias, batch, seq)
    xs, w_gate, w_ple = _out_proj(xs, o_a, o_b, wo, batch, seq,
                                  jobs=_slab_cast_jobs(proj_steps, w_ple_gate[i], w_ple_proj[i]))

    xs, = _ffn(xs, row(norm_ffn2[i]), wg2, wu2, wd2)
    last = i == depth - 1
    xs = _ple(xs, p[i].reshape(m, -1), row(norm_ple[i]), w_gate, w_ple,
              row(norm_final), final_norm=last)
  return xs.reshape(batch, seq, d).astype(x.dtype)
```

```python
import functools
import math
from typing import Any, Callable, NamedTuple

import jax
import jax.numpy as jnp
import numpy as np
from jax import lax
from jax.experimental import pallas as pl
from jax.experimental.pallas import tpu as pltpu

F32 = jnp.float32
BF16 = jnp.bfloat16

HEAD_DIM = 128
N_HEADS_FOX = 8
N_HEADS_DIL = 8
D_FOX = N_HEADS_FOX * HEAD_DIM
D_DIL = N_HEADS_DIL * HEAD_DIM
DILATED_PATTERNS = ((128, 1), (512, 4), (2048, 16))
WINDOW_KEYS = 128
N_REL_BUCKETS = 32
REL_MAX_DISTANCE = 2048
RMS_EPS = 1e-6
NEG_INF = -1e30
SCALE = HEAD_DIM ** -0.5
LOG2_E = math.log2(math.e)

V7X_LANES = 128
BF16_TILE_ROWS = 16
V7X_VMEM_LIMIT_BYTES = 56 * 1024 * 1024

RESIDUES = max(d for _, d in DILATED_PATTERNS)
PERM_ROWS = RESIDUES * RESIDUES

FFN_ROW_TILE = 1024
FFN_SUB_ROWS = 512
FFN_FF_TILE = 512
FFN_HEAD_FF_TILE = 256
PROJ_ROW_TILE = 512
FOX_Q_TILE = 256
FOX_LOOKAHEAD = 2
DIL_BLOCK_GROUP = 12


def _params(*semantics):
  return pltpu.CompilerParams(dimension_semantics=semantics,
                              vmem_limit_bytes=V7X_VMEM_LIMIT_BYTES)


def _rmsnorm(x, g):
  ms = jnp.mean(x * x, axis=-1, keepdims=True)
  return x * lax.rsqrt(ms + RMS_EPS) * g


def _dot(a, b):
  return jnp.dot(a, b, preferred_element_type=F32)


def _dot_nt(a, b):
  return lax.dot_general(a, b, (((1,), (1,)), ((), ())), preferred_element_type=F32)


def _weighted_values(e, v):
  both = _dot(e.astype(BF16), jnp.concatenate([v, jnp.ones_like(v)], axis=1))
  return both[:, :HEAD_DIM], both[:, HEAD_DIM:]


class _SideJob(NamedTuple):
  arrays: tuple
  in_specs: tuple
  out_shape: Any
  out_spec: Any
  fn: Callable


def _cast_job(a, block, index, scale=1.0):
  spec = pl.BlockSpec(block, index)
  return _SideJob((a,), (spec,), jax.ShapeDtypeStruct(a.shape, BF16), spec,
                  lambda r: r[...] * scale)


def _run_side_jobs(jobs, in_refs, out_refs):
  in_refs = list(in_refs)
  for job, out_ref in zip(jobs, out_refs):
    refs = [in_refs.pop(0) for _ in job.arrays]
    out_ref[...] = job.fn(*refs).astype(out_ref.dtype)


def _side_args(jobs):
  arrays = [a for job in jobs for a in job.arrays]
  in_specs = [s for job in jobs for s in job.in_specs]
  return arrays, in_specs, [job.out_spec for job in jobs], [job.out_shape for job in jobs]


def _ffn_head_kernel(x_ref, g_ref, wg_ref, wu_ref, wd_ref, *rest, jobs):
  n_in = sum(len(job.arrays) for job in jobs)
  side_in, (o_ref, og_ref, ou_ref, od_ref) = rest[:n_in], rest[n_in:n_in + 4]
  side_out, h_ref = rest[n_in + 4:-1], rest[-1]
  _run_side_jobs(jobs, side_in, side_out)

  @pl.when(pl.program_id(0) == 0)
  def _():
    x = x_ref[...]
    h_ref[...] = _rmsnorm(x, g_ref[...]).astype(BF16)
    o_ref[...] = x

  wg = wg_ref[...].astype(BF16)
  wu = wu_ref[...].astype(BF16)
  wd = (wd_ref[...] * 0.5).astype(BF16)
  og_ref[...] = wg
  ou_ref[...] = wu
  od_ref[...] = wd
  for r in range(h_ref.shape[0] // FFN_SUB_ROWS):
    rows = slice(r * FFN_SUB_ROWS, (r + 1) * FFN_SUB_ROWS)
    h = h_ref[rows, :]
    gate = _dot(h, wg)
    up = _dot(h, wu)
    act = (gate * jax.nn.sigmoid(gate)) * up
    o_ref[rows, :] += _dot(act.astype(BF16), wd)


def _ffn_head_steps(w_gate):
  return w_gate.shape[1] // FFN_HEAD_FF_TILE


def _ffn_head(x, g, w_gate, w_up, w_down, jobs=()):
  d = x.shape[1]
  tm, tf = FFN_ROW_TILE, FFN_HEAD_FF_TILE
  side_arrays, side_in_specs, side_out_specs, side_out_shapes = _side_args(jobs)
  return pl.pallas_call(
      functools.partial(_ffn_head_kernel, jobs=tuple(jobs)),
      grid=(_ffn_head_steps(w_gate),),
      in_specs=[
          pl.BlockSpec((tm, d), lambda j: (0, 0), pipeline_mode=pl.Buffered(1)),
          pl.BlockSpec((1, d), lambda j: (0, 0)),
          pl.BlockSpec((d, tf), lambda j: (0, j)),
          pl.BlockSpec((d, tf), lambda j: (0, j)),
          pl.BlockSpec((tf, d), lambda j: (j, 0)),
      ] + side_in_specs,
      out_specs=[
          pl.BlockSpec((tm, d), lambda j: (0, 0)),
          pl.BlockSpec((d, tf), lambda j: (0, j)),
          pl.BlockSpec((d, tf), lambda j: (0, j)),
          pl.BlockSpec((tf, d), lambda j: (j, 0)),
      ] + side_out_specs,
      out_shape=[
          jax.ShapeDtypeStruct((tm, d), F32),
          jax.ShapeDtypeStruct(w_gate.shape, BF16),
          jax.ShapeDtypeStruct(w_up.shape, BF16),
          jax.ShapeDtypeStruct(w_down.shape, BF16),
      ] + side_out_shapes,
      scratch_shapes=[pltpu.VMEM((tm, d), BF16)],
      compiler_params=_params("arbitrary"),
      name="ffn_head",
  )(x, g, w_gate, w_up, w_down, *side_arrays)


def _ffn_kernel(x_ref, g_ref, wg_hbm, wu_hbm, wd_hbm, *rest, has_head):
  rest = list(rest)
  head_hbm = rest.pop(0) if has_head else None
  o_ref, h_ref, wg_buf, wu_buf, wd_buf, w_sems = rest[:6]
  copy_sem = rest[6] if has_head else None
  tf = wg_buf.shape[2]
  n_chunks = wg_hbm.shape[1] // tf
  first_tile = 1 if has_head else 0
  tile = pl.program_id(0) - first_tile
  total = (pl.num_programs(0) - first_tile) * n_chunks

  def chunk_copies(j, slot):
    span = pl.ds(j * tf, tf)
    return (pltpu.make_async_copy(wg_hbm.at[:, span], wg_buf.at[slot], w_sems.at[slot, 0]),
            pltpu.make_async_copy(wu_hbm.at[:, span], wu_buf.at[slot], w_sems.at[slot, 1]),
            pltpu.make_async_copy(wd_hbm.at[span, :], wd_buf.at[slot], w_sems.at[slot, 2]))

  def chunk_step(j, first):
    s = tile * n_chunks + j
    slot = lax.rem(s, 2)

    @pl.when(s == 0)
    def _():
      for copy in chunk_copies(0, 0):
        copy.start()

    for copy in chunk_copies(j, slot):
      copy.wait()

    @pl.when(s + 1 < total)
    def _():
      for copy in chunk_copies(lax.rem(j + 1, n_chunks), 1 - slot):
        copy.start()

    for r in range(h_ref.shape[0] // FFN_SUB_ROWS):
      rows = slice(r * FFN_SUB_ROWS, (r + 1) * FFN_SUB_ROWS)
      if first:
        base = x_ref[rows, :]
        h = _rmsnorm(base, g_ref[...]).astype(BF16)
        h_ref[rows, :] = h
      else:
        base = o_ref[rows, :]
        h = h_ref[rows, :]
      gate = _dot(h, wg_buf[slot])
      up = _dot(h, wu_buf[slot])
      act = (gate * jax.nn.sigmoid(gate)) * up
      o_ref[rows, :] = base + _dot(act.astype(BF16), wd_buf[slot])

  def compute_tile():
    chunk_step(0, True)

    def body(j, carry):
      chunk_step(j, False)
      return carry

    lax.fori_loop(1, n_chunks, body, 0)

  def copy_head_tile():
    copy = pltpu.make_async_copy(head_hbm, o_ref, copy_sem)
    copy.start()
    copy.wait()

  if has_head:
    lax.cond(pl.program_id(0) == 0, copy_head_tile, compute_tile)
  else:
    compute_tile()


def _ffn(x, g, wg, wu, wd_half, head=None):
  m, d = x.shape
  dff = wg.shape[1]
  tm, tf = FFN_ROW_TILE, FFN_FF_TILE
  assert dff % tf == 0
  has_head = head is not None
  in_hbm = pl.BlockSpec(memory_space=pl.ANY)
  return pl.pallas_call(
      functools.partial(_ffn_kernel, has_head=has_head),
      grid=(m // tm,),
      in_specs=[
          pl.BlockSpec((tm, d), lambda i: (i, 0)),
          pl.BlockSpec((1, d), lambda i: (0, 0)),
          in_hbm, in_hbm, in_hbm,
      ] + ([in_hbm] if has_head else []),
      out_specs=pl.BlockSpec((tm, d), lambda i: (i, 0)),
      out_shape=jax.ShapeDtypeStruct((m, d), F32),
      scratch_shapes=[
          pltpu.VMEM((tm, d), BF16),
          pltpu.VMEM((2, d, tf), BF16),
          pltpu.VMEM((2, d, tf), BF16),
          pltpu.VMEM((2, tf, d), BF16),
          pltpu.SemaphoreType.DMA((2, 3)),
      ] + ([pltpu.SemaphoreType.DMA(())] if has_head else []),
      compiler_params=_params("arbitrary"),
      name="ffn",
  )(x, g, wg, wu, wd_half, *([head] if has_head else []))


def _slab_cast_jobs(steps, *weights, scale=1.0):
  jobs = []
  for w in weights:
    rows = max(BF16_TILE_ROWS, w.shape[0] // steps)
    assert w.shape[0] % rows == 0 and rows % BF16_TILE_ROWS == 0
    last = w.shape[0] // rows - 1
    jobs.append(_cast_job(w, (rows, w.shape[1]), lambda t, last=last: (jnp.minimum(t, last), 0),
                          scale=scale))
  return jobs


def _w_in_repack_jobs(steps, w_t, n_a, n_f, n_b):
  d = w_t.shape[1]
  lanes = V7X_LANES
  slabs = d // lanes
  assert d % lanes == 0 and slabs <= steps and n_f % 8 == 0 and n_f <= lanes
  assert n_a % n_f == 0 and n_b <= n_a + n_f
  slab = lambda t: jnp.minimum(t, slabs - 1)

  def gate_rows(ref):
    return jnp.concatenate([ref[...], jnp.zeros((lanes - n_f, lanes), F32)], axis=0).T

  out = lambda n: jax.ShapeDtypeStruct((d, n), BF16)
  out_spec = lambda n: pl.BlockSpec((lanes, n), lambda t: (slab(t), 0))
  return [
      _SideJob((w_t,), (pl.BlockSpec((n_a, lanes), lambda t: (0, slab(t))),),
               out(n_a), out_spec(n_a), lambda ref: ref[...].T),
      _SideJob((w_t,), (pl.BlockSpec((n_a + n_f, lanes), lambda t: (1, slab(t))),),
               out(n_b), out_spec(n_b), lambda ref: ref[:n_b, :].T),
      _SideJob((w_t,), (pl.BlockSpec((n_f, lanes), lambda t: (n_a // n_f, slab(t))),),
               out(lanes), out_spec(lanes), gate_rows),
  ]


def _swap_matrix():
  i = np.arange(PERM_ROWS)
  src = (i % RESIDUES) * RESIDUES + i // RESIDUES
  return jnp.asarray(np.eye(PERM_ROWS, dtype=np.float32)[src], BF16)


def _norm_matmul_kernel(x_ref, g_ref, w_ref, *rest, permute, narrow, jobs, key_cols):
  rest = list(rest)
  swap_ref = rest.pop(0) if permute else None
  wn_ref = rest.pop(0) if narrow else None
  side_in = [rest.pop(0) for job in jobs for _ in job.arrays]
  o_ref = rest.pop(0)
  on_ref = rest.pop(0) if narrow else None
  side_out = [rest.pop(0) for _ in jobs]
  assert not rest, "unexpected extra refs"
  tm = x_ref.shape[0]
  _run_side_jobs(jobs, side_in, side_out)

  h = _rmsnorm(x_ref[...], g_ref[...]).astype(BF16)
  if permute:
    h = jnp.concatenate(
        [_dot(swap_ref[...], h[a * PERM_ROWS:(a + 1) * PERM_ROWS, :]).astype(BF16)
         for a in range(tm // PERM_ROWS)], axis=0)
  if narrow:
    on_ref[...] = _dot(h, wn_ref[...])
  acc = _dot(h, w_ref[...])
  lo, hi = key_cols
  res = jnp.concatenate([acc[:, :lo], acc[:, lo:hi] * (SCALE * LOG2_E), acc[:, hi:]],
                        axis=1).astype(o_ref.dtype)
  if permute:
    per = PERM_ROWS // RESIDUES
    for a in range(tm // PERM_ROWS):
      for r in range(RESIDUES):
        start = a * PERM_ROWS + r * per
        o_ref[r, a * per:(a + 1) * per, :] = res[start:start + per, :]
  else:
    o_ref[...] = res


def _norm_matmul(x, g, w, out_dtype, key_cols, residue_major=None, w_narrow=None, jobs=()):
  m, d = x.shape
  n = w.shape[1]
  tm = PROJ_ROW_TILE
  narrow = w_narrow is not None
  permute = residue_major is not None
  assert not (narrow and permute)
  in_specs = [
      pl.BlockSpec((tm, d), lambda i: (i, 0)),
      pl.BlockSpec((1, d), lambda i: (0, 0)),
      pl.BlockSpec((d, n), lambda i: (0, 0)),
  ]
  args = [x, g, w]
  if permute:
    batch, seq = residue_major
    tiles = seq // tm
    in_specs.append(pl.BlockSpec((PERM_ROWS, PERM_ROWS), lambda i: (0, 0)))
    args.append(_swap_matrix())
    out_specs = [pl.BlockSpec((RESIDUES, tm // RESIDUES, n), lambda i: (i // tiles, i % tiles, 0))]
    out_shape = [jax.ShapeDtypeStruct((batch * RESIDUES, seq // RESIDUES, n), out_dtype)]
  else:
    out_specs = [pl.BlockSpec((tm, n), lambda i: (i, 0))]
    out_shape = [jax.ShapeDtypeStruct((m, n), out_dtype)]
  if narrow:
    in_specs.append(pl.BlockSpec(w_narrow.shape, lambda i: (0, 0)))
    args.append(w_narrow)
    out_specs.append(pl.BlockSpec((tm, w_narrow.shape[1]), lambda i: (i, 0)))
    out_shape.append(jax.ShapeDtypeStruct((m, w_narrow.shape[1]), F32))
  side_arrays, side_in_specs, side_out_specs, side_out_shapes = _side_args(jobs)
  outs = pl.pallas_call(
      functools.partial(_norm_matmul_kernel, permute=permute, narrow=narrow, jobs=tuple(jobs),
                        key_cols=key_cols),
      grid=(m // tm,),
      in_specs=in_specs + side_in_specs,
      out_specs=out_specs + side_out_specs,
      out_shape=out_shape + side_out_shapes,
      compiler_params=_params("parallel"),
      name="norm_matmul",
  )(*args, *side_arrays)
  outs = list(outs)
  if permute:
    outs[0] = outs[0].reshape(m, n)
  return outs


def _cumsum_kernel(fl_ref, bf_ref, c_ref):
  s = fl_ref.shape[0]
  z = fl_ref[...] + bf_ref[...]
  logf = jnp.minimum(z, 0.0) - jnp.log1p(jnp.exp(-jnp.abs(z)))
  lt = logf.T[0:N_HEADS_FOX, :]
  row = lax.broadcasted_iota(jnp.int32, (V7X_LANES, V7X_LANES), 0)
  col = lax.broadcasted_iota(jnp.int32, (V7X_LANES, V7X_LANES), 1)
  upper = (row <= col).astype(F32)
  carry = jnp.zeros((N_HEADS_FOX, 1), F32)
  for j in range(s // V7X_LANES):
    blk = lt[:, j * V7X_LANES:(j + 1) * V7X_LANES]
    cs = jnp.dot(blk, upper, preferred_element_type=F32,
                 precision=lax.Precision.HIGHEST) + carry
    c_ref[:, j * V7X_LANES:(j + 1) * V7X_LANES] = cs
    carry = cs[:, V7X_LANES - 1:V7X_LANES]


def _fox_decay(f_logit, b_f_row, batch, seq):
  return pl.pallas_call(
      _cumsum_kernel,
      grid=(batch,),
      in_specs=[
          pl.BlockSpec((seq, V7X_LANES), lambda b: (b, 0)),
          pl.BlockSpec((1, V7X_LANES), lambda b: (0, 0)),
      ],
      out_specs=pl.BlockSpec((None, N_HEADS_FOX, seq), lambda b: (b, 0, 0)),
      out_shape=jax.ShapeDtypeStruct((batch, N_HEADS_FOX, seq), F32),
      compiler_params=_params("parallel"),
      name="fox_decay",
  )(f_logit, b_f_row)


def _fox_phases(q_ref, k_ref, v_ref, c_ref, o_ref):
  seq = q_ref.shape[0]
  tq = FOX_Q_TILE
  h = pl.program_id(1)
  crow = c_ref[pl.ds(h, 1), :] * LOG2_E
  row = lax.broadcasted_iota(jnp.int32, (tq, tq), 0)
  col = lax.broadcasted_iota(jnp.int32, (tq, tq), 1)
  diag_mask = jnp.where(col > row, NEG_INF, 0.0).astype(F32)

  def scores(i):
    t0, t1 = i * tq, (i + 1) * tq
    q = q_ref[t0:t1, :]
    bias = crow[:, t1 - 1:t1] - crow[:, 0:t1]
    s_diag = _dot_nt(q, k_ref[t0:t1, :]) + bias[:, t0:t1] + diag_mask
    s_off = _dot_nt(q, k_ref[0:t0, :]) + bias[:, 0:t0] if i > 0 else None
    return s_diag, s_off

  def finish(i, s_diag, s_off):
    t0, t1 = i * tq, (i + 1) * tq
    m = jnp.max(s_diag, axis=-1, keepdims=True)
    if i > 0:
      m = jnp.maximum(m, jnp.max(s_off, axis=-1, keepdims=True))
    o, l = _weighted_values(jnp.exp2(s_diag - m), v_ref[t0:t1, :])
    if i > 0:
      o_off, l_off = _weighted_values(jnp.exp2(s_off - m), v_ref[0:t0, :])
      o, l = o + o_off, l + l_off
    o_ref[t0:t1, :] = (o / l).astype(o_ref.dtype)

  n_tiles = seq // tq
  pending = [scores(i) for i in range(min(FOX_LOOKAHEAD, n_tiles))]
  yield
  for i in range(n_tiles):
    if i + FOX_LOOKAHEAD < n_tiles:
      pending.append(scores(i + FOX_LOOKAHEAD))
      yield
    finish(i, *pending.pop(0))
    yield


def _t5_bucket_np(dist):
  max_exact = N_REL_BUCKETS // 2
  d = np.maximum(dist, 1).astype(np.float32)
  large = max_exact + (np.log(d / np.float32(max_exact))
                       / np.float32(math.log(REL_MAX_DISTANCE / max_exact))
                       * np.float32(N_REL_BUCKETS - max_exact)).astype(np.int32)
  large = np.minimum(large, N_REL_BUCKETS - 1)
  return np.where(dist < max_exact, dist, large).astype(np.int32)


def _block_positions(dilation):
  n = WINDOW_KEYS
  m = RESIDUES // dilation
  rows = n // m
  j = np.arange(m)[:, None]
  qpos = (n + m * np.arange(rows)[None, :] + j).reshape(-1)
  kpos = (m * np.arange(2 * rows)[None, :] + j).reshape(-1)
  return qpos, kpos


def _band_buckets():
  n = WINDOW_KEYS
  tiles = []
  for _, dilation in DILATED_PATTERNS:
    qpos, kpos = _block_positions(dilation)
    rel = qpos[:, None] - np.concatenate([kpos, qpos])[None, :]
    valid = (rel >= 0) & (rel <= n)
    bucket = _t5_bucket_np(np.maximum(rel, 0) * dilation)
    tiles.append(np.where(valid, bucket, -1))
  return np.stack(tiles).astype(np.int32)


def _bias_kernel(tab_ref, bkt_ref, o_ref):
  bkt = bkt_ref[...]
  for h in range(N_HEADS_DIL):
    acc = jnp.full(bkt.shape, NEG_INF, F32)
    for b in range(N_REL_BUCKETS):
      acc = jnp.where(bkt == b, tab_ref[b, h] * LOG2_E, acc)
    o_ref[h] = acc


def _band_bias(rel_table):
  buckets = jnp.asarray(_band_buckets())
  p, n, n2 = buckets.shape
  return pl.pallas_call(
      _bias_kernel,
      grid=(p,),
      in_specs=[
          pl.BlockSpec(memory_space=pltpu.SMEM),
          pl.BlockSpec((None, n, n2), lambda i: (i, 0, 0)),
      ],
      out_specs=pl.BlockSpec((None, N_HEADS_DIL, n, n2), lambda i: (i, 0, 0, 0)),
      out_shape=jax.ShapeDtypeStruct((p, N_HEADS_DIL, n, n2), F32),
      compiler_params=_params("parallel"),
      name="band_bias",
  )(rel_table, buckets)


def _dil_phases(q_ref, k_ref, v_ref, bm_ref, o_ref, *scratch):
  seq = q_ref.shape[0]
  n = WINDOW_KEYS
  seg = seq // RESIDUES
  last = len(DILATED_PATTERNS) - 1
  assert DILATED_PATTERNS[last][1] == RESIDUES and n == seg
  accs, lses = scratch[:last], scratch[last:]

  def gather(ref, starts, size):
    return jnp.concatenate([ref[st:st + size, :] for st in starts], axis=0).astype(BF16)

  def block_rows(segments, nb):
    rows = n // len(segments)
    q_starts = [s * seg + nb * rows for s in segments]
    if nb == 0:
      return rows, q_starts, q_starts, rows
    return rows, q_starts, [st - rows for st in q_starts], 2 * rows

  def scores(p, segments, nb):
    rows, q_starts, k_starts, k_rows = block_rows(segments, nb)
    bm = bm_ref[p, :, 2 * n:3 * n] if nb == 0 else bm_ref[p, :, 0:2 * n]
    return _dot_nt(gather(q_ref, q_starts, rows), gather(k_ref, k_starts, k_rows)) + bm

  def finish(p, segments, nb, e, m):
    rows, q_starts, k_starts, k_rows = block_rows(segments, nb)
    o, l = _weighted_values(e, gather(v_ref, k_starts, k_rows))
    o = o / l
    lse = m + jnp.log2(l)
    if p < last:
      for j, st in enumerate(q_starts):
        accs[p][st:st + rows, :] = o[j * rows:(j + 1) * rows, :]
        lses[p][st:st + rows, :] = lse[j * rows:(j + 1) * rows, :]
      return
    seg_rows = slice(q_starts[0], q_starts[0] + n)
    all_lse = [ref[seg_rows, :] for ref in lses] + [lse]
    all_out = [ref[seg_rows, :] for ref in accs] + [o]
    top = functools.reduce(jnp.maximum, all_lse)
    weights = [jnp.exp2(x - top) for x in all_lse]
    add = lambda a, b: a + b
    mixed = (functools.reduce(add, [w * a for w, a in zip(weights, all_out)])
             / functools.reduce(add, weights))
    o_ref[seg_rows, :] = mixed.astype(o_ref.dtype)

  blocks = [(p, list(range(r, RESIDUES, d)), nb)
            for p, (_, d) in enumerate(DILATED_PATTERNS)
            for r in range(d) for nb in range(seq // (n * d))]
  for g in range(0, len(blocks), DIL_BLOCK_GROUP):
    group = blocks[g:g + DIL_BLOCK_GROUP]
    ss = [scores(*blk) for blk in group]
    yield
    ms = [jnp.max(s, axis=-1, keepdims=True) for s in ss]
    es = [jnp.exp2(s - m) for s, m in zip(ss, ms)]
    yield
    for blk, e, m in zip(group, es, ms):
      finish(*blk, e, m)
    yield


def _mixer_kernel(qa_ref, ka_ref, va_ref, c_ref, qb_ref, kb_ref, vb_ref, bm_ref,
                  oa_ref, ob_ref, *scratch):
  streams = [_fox_phases(qa_ref, ka_ref, va_ref, c_ref, oa_ref),
             _dil_phases(qb_ref, kb_ref, vb_ref, bm_ref, ob_ref, *scratch)]
  while streams:
    for stream in list(streams):
      try:
        next(stream)
      except StopIteration:
        streams.remove(stream)


def _token_mixers(u_a, c, u_b, band_bias, batch, seq):
  assert seq // RESIDUES == WINDOW_KEYS and N_HEADS_FOX == N_HEADS_DIL
  blk = lambda off: pl.BlockSpec((seq, HEAD_DIM), lambda b, h: (b, off + h))
  n_pat = len(DILATED_PATTERNS)
  scratch = [pltpu.VMEM((seq, HEAD_DIM), F32) for _ in range(2 * (n_pat - 1))]
  out_blk = pl.BlockSpec((seq, HEAD_DIM), lambda b, h: (b, h))
  return pl.pallas_call(
      _mixer_kernel,
      grid=(batch, N_HEADS_FOX),
      in_specs=[
          blk(0), blk(N_HEADS_FOX), blk(2 * N_HEADS_FOX),
          pl.BlockSpec((None, N_HEADS_FOX, seq), lambda b, h: (b, 0, 0)),
          blk(0), blk(N_HEADS_DIL), blk(2 * N_HEADS_DIL),
          pl.BlockSpec((n_pat, None, WINDOW_KEYS, 3 * WINDOW_KEYS), lambda b, h: (0, h, 0, 0)),
      ],
      out_specs=[out_blk, out_blk],
      out_shape=[jax.ShapeDtypeStruct((batch * seq, D_FOX), BF16),
                 jax.ShapeDtypeStruct((batch * seq, D_DIL), BF16)],
      scratch_shapes=scratch,
      compiler_params=_params("parallel", "arbitrary"),
      name="token_mixers",
  )(u_a, u_a, u_a, c, u_b, u_b, u_b, band_bias)


def _out_proj_kernel(x_ref, a_ref, b_ref, swap_ref, wa_ref, wb_ref, *rest, jobs):
  n_in = sum(len(job.arrays) for job in jobs)
  side_in, o_ref, side_out = rest[:n_in], rest[n_in], rest[n_in + 1:]
  _run_side_jobs(jobs, side_in, side_out)
  per = PERM_ROWS // RESIDUES
  for a in range(x_ref.shape[0] // PERM_ROWS):
    rows = slice(a * PERM_ROWS, (a + 1) * PERM_ROWS)
    slab = jnp.concatenate([b_ref[r, a * per:(a + 1) * per, :] for r in range(RESIDUES)], axis=0)
    o_b = _dot(swap_ref[...], slab).astype(BF16)
    o_ref[rows, :] = x_ref[rows, :] + _dot(a_ref[rows, :], wa_ref[...]) + _dot(o_b, wb_ref[...])


def _out_proj(x, o_a, o_b, wo, batch, seq, jobs=()):
  m, d = x.shape
  tm = PROJ_ROW_TILE
  tiles = seq // tm
  ca, cb = o_a.shape[1], o_b.shape[1]
  assert ca == cb and wo.shape[0] == ca + cb
  resident = pl.Buffered(1)
  side_arrays, side_in_specs, side_out_specs, side_out_shapes = _side_args(jobs)
  return pl.pallas_call(
      functools.partial(_out_proj_kernel, jobs=tuple(jobs)),
      grid=(m // tm,),
      in_specs=[
          pl.BlockSpec((tm, d), lambda i: (i, 0)),
          pl.BlockSpec((tm, ca), lambda i: (i, 0)),
          pl.BlockSpec((RESIDUES, tm // RESIDUES, cb), lambda i: (i // tiles, i % tiles, 0)),
          pl.BlockSpec((PERM_ROWS, PERM_ROWS), lambda i: (0, 0)),
          pl.BlockSpec((ca, d), lambda i: (0, 0), pipeline_mode=resident),
          pl.BlockSpec((cb, d), lambda i: (1, 0), pipeline_mode=resident),
      ] + side_in_specs,
      out_specs=[pl.BlockSpec((tm, d), lambda i: (i, 0))] + side_out_specs,
      out_shape=[jax.ShapeDtypeStruct((m, d), F32)] + side_out_shapes,
      compiler_params=_params("parallel"),
      name="out_proj",
  )(x, o_a, o_b.reshape(batch * RESIDUES, seq // RESIDUES, cb), _swap_matrix(), wo, wo,
    *side_arrays)


def _ple_kernel(x_ref, p_ref, g_ref, wg_ref, wp_ref, gf_ref, o_ref, *, final_norm):
  x = x_ref[...]
  h = _rmsnorm(x, g_ref[...]).astype(BF16)
  gate = jax.nn.sigmoid(_dot(h, wg_ref[...]))
  y = x + gate * _dot(p_ref[...].astype(BF16), wp_ref[...])
  o_ref[...] = _rmsnorm(y, gf_ref[...]) if final_norm else y


def _ple(x, p, g, w_gate, w_proj, g_final, final_norm):
  m, d = x.shape
  tm = PROJ_ROW_TILE
  return pl.pallas_call(
      functools.partial(_ple_kernel, final_norm=final_norm),
      grid=(m // tm,),
      in_specs=[
          pl.BlockSpec((tm, d), lambda i: (i, 0)),
          pl.BlockSpec((tm, p.shape[1]), lambda i: (i, 0)),
          pl.BlockSpec((1, d), lambda i: (0, 0)),
          pl.BlockSpec(w_gate.shape, lambda i: (0, 0)),
          pl.BlockSpec(w_proj.shape, lambda i: (0, 0)),
          pl.BlockSpec((1, d), lambda i: (0, 0)),
      ],
      out_specs=pl.BlockSpec((tm, d), lambda i: (i, 0)),
      out_shape=jax.ShapeDtypeStruct((m, d), F32),
      compiler_params=_params("parallel"),
      name="ple",
  )(x, p, g, w_gate, w_proj, g_final)


def kernel(x, p, norm_ffn1, ffn1_w_gate, ffn1_w_up, ffn1_w_down, norm_mix, w_in, b_f, w_o,
           norm_ffn2, ffn2_w_gate, ffn2_w_up, ffn2_w_down, norm_ple, w_ple_gate, w_ple_proj,
           rel_table, norm_final):
  batch, seq, d = x.shape
  depth = p.shape[0]
  m = batch * seq
  bf = lambda w: w.astype(BF16)
  row = lambda g: g.reshape(1, -1).astype(F32)

  band_bias = _band_bias(rel_table.astype(F32))
  xs = x.reshape(m, d).astype(F32)
  for i in range(depth):
    g_ffn1 = row(norm_ffn1[i])
    head, wg1, wu1, wd1, w_a, w_b, w_f = _ffn_head(
        xs, g_ffn1, ffn1_w_gate[i], ffn1_w_up[i], ffn1_w_down[i],
        jobs=_w_in_repack_jobs(_ffn_head_steps(ffn1_w_gate[i]), jnp.swapaxes(w_in[i], 0, 1),
                               3 * D_FOX, N_HEADS_FOX, 3 * D_DIL))
    xs = _ffn(xs, g_ffn1, wg1, wu1, wd1, head=head)

    b_f_row = jnp.pad(b_f[i].astype(F32), (0, V7X_LANES - N_HEADS_FOX)).reshape(1, V7X_LANES)
    g_mix = row(norm_mix[i])
    proj_steps = m // PROJ_ROW_TILE
    u_a, f_logit, wo = _norm_matmul(
        xs, g_mix, w_a, BF16, (D_FOX, 2 * D_FOX), w_narrow=w_f,
        jobs=_slab_cast_jobs(proj_steps, w_o[i]))
    u_b, wg2, wu2, wd2 = _norm_matmul(
        xs, g_mix, w_b, F32, (D_DIL, 2 * D_DIL), residue_major=(batch, seq),
        jobs=_slab_cast_jobs(proj_steps, ffn2_w_gate[i], ffn2_w_up[i])
        + _slab_cast_jobs(proj_steps, ffn2_w_down[i], scale=0.5))

    c = _fox_decay(f_logit, b_f_row, batch, seq)
    o_a, o_b = _token_mixers(u_a, c, u_b, band_bias, batch, seq)
    xs, w_gate, w_ple = _out_proj(xs, o_a, o_b, wo, batch, seq,
                                  jobs=_slab_cast_jobs(proj_steps, w_ple_gate[i], w_ple_proj[i]))

    xs = _ffn(xs, row(norm_ffn2[i]), wg2, wu2, wd2)
    last = i == depth - 1
    xs = _ple(xs, p[i].reshape(m, -1), row(norm_ple[i]), w_gate, w_ple,
              row(norm_final), final_norm=last)
  return xs.reshape(batch, seq, d).astype(x.dtype)
```

```python
import functools
import math
from typing import Any, Callable, NamedTuple

import jax
import jax.numpy as jnp
import numpy as np
from jax import lax
from jax.experimental import pallas as pl
from jax.experimental.pallas import tpu as pltpu

F32 = jnp.float32
BF16 = jnp.bfloat16

HEAD_DIM = 128
N_HEADS_FOX = 8
N_HEADS_DIL = 8
D_FOX = N_HEADS_FOX * HEAD_DIM
D_DIL = N_HEADS_DIL * HEAD_DIM
DILATED_PATTERNS = ((128, 1), (512, 4), (2048, 16))
WINDOW_KEYS = 128
N_REL_BUCKETS = 32
REL_MAX_DISTANCE = 2048
RMS_EPS = 1e-6
NEG_INF = -1e30
SCALE = HEAD_DIM ** -0.5
LOG2_E = math.log2(math.e)

V7X_LANES = 128
BF16_TILE_ROWS = 16
V7X_VMEM_LIMIT_BYTES = 56 * 1024 * 1024

RESIDUES = max(d for _, d in DILATED_PATTERNS)
PERM_ROWS = RESIDUES * RESIDUES

FFN_ROW_TILE = 1024
FFN_SUB_ROWS = 512
FFN_FF_TILE = 512
FFN_HEAD_FF_TILE = 256
PROJ_ROW_TILE = 512
FOX_Q_TILE = 256
FOX_LOOKAHEAD = 2
DIL_BLOCK_GROUP = 12


def _params(*semantics):
  return pltpu.CompilerParams(dimension_semantics=semantics,
                              vmem_limit_bytes=V7X_VMEM_LIMIT_BYTES)


def _rmsnorm(x, g):
  ms = jnp.mean(x * x, axis=-1, keepdims=True)
  return x * lax.rsqrt(ms + RMS_EPS) * g


def _dot(a, b):
  return jnp.dot(a, b, preferred_element_type=F32)


def _dot_nt(a, b):
  return lax.dot_general(a, b, (((1,), (1,)), ((), ())), preferred_element_type=F32)


def _weighted_values(e, v):
  both = _dot(e.astype(BF16), jnp.concatenate([v, jnp.ones_like(v)], axis=1))
  return both[:, :HEAD_DIM], both[:, HEAD_DIM:]


class _SideJob(NamedTuple):
  arrays: tuple
  in_specs: tuple
  out_shape: Any
  out_spec: Any
  fn: Callable


def _cast_job(a, block, index, scale=1.0):
  spec = pl.BlockSpec(block, index)
  return _SideJob((a,), (spec,), jax.ShapeDtypeStruct(a.shape, BF16), spec,
                  lambda r: r[...] * scale)


def _run_side_jobs(jobs, in_refs, out_refs):
  in_refs = list(in_refs)
  for job, out_ref in zip(jobs, out_refs):
    refs = [in_refs.pop(0) for _ in job.arrays]
    out_ref[...] = job.fn(*refs).astype(out_ref.dtype)


def _side_args(jobs):
  arrays = [a for job in jobs for a in job.arrays]
  in_specs = [s for job in jobs for s in job.in_specs]
  return arrays, in_specs, [job.out_spec for job in jobs], [job.out_shape for job in jobs]


def _ffn_head_kernel(x_ref, g_ref, wg_ref, wu_ref, wd_ref, *rest, jobs):
  n_in = sum(len(job.arrays) for job in jobs)
  side_in, (o_ref, og_ref, ou_ref, od_ref) = rest[:n_in], rest[n_in:n_in + 4]
  side_out, h_ref = rest[n_in + 4:-1], rest[-1]
  _run_side_jobs(jobs, side_in, side_out)

  @pl.when(pl.program_id(0) == 0)
  def _():
    x = x_ref[...]
    h_ref[...] = _rmsnorm(x, g_ref[...]).astype(BF16)
    o_ref[...] = x

  wg = wg_ref[...].astype(BF16)
  wu = wu_ref[...].astype(BF16)
  wd = (wd_ref[...] * 0.5).astype(BF16)
  og_ref[...] = wg
  ou_ref[...] = wu
  od_ref[...] = wd
  for r in range(h_ref.shape[0] // FFN_SUB_ROWS):
    rows = slice(r * FFN_SUB_ROWS, (r + 1) * FFN_SUB_ROWS)
    h = h_ref[rows, :]
    gate = _dot(h, wg)
    up = _dot(h, wu)
    act = (gate * jax.nn.sigmoid(gate)) * up
    o_ref[rows, :] += _dot(act.astype(BF16), wd)


def _ffn_head_steps(w_gate):
  return w_gate.shape[1] // FFN_HEAD_FF_TILE


def _ffn_head(x, g, w_gate, w_up, w_down, jobs=()):
  d = x.shape[1]
  tm, tf = FFN_ROW_TILE, FFN_HEAD_FF_TILE
  side_arrays, side_in_specs, side_out_specs, side_out_shapes = _side_args(jobs)
  return pl.pallas_call(
      functools.partial(_ffn_head_kernel, jobs=tuple(jobs)),
      grid=(_ffn_head_steps(w_gate),),
      in_specs=[
          pl.BlockSpec((tm, d), lambda j: (0, 0), pipeline_mode=pl.Buffered(1)),
          pl.BlockSpec((1, d), lambda j: (0, 0)),
          pl.BlockSpec((d, tf), lambda j: (0, j)),
          pl.BlockSpec((d, tf), lambda j: (0, j)),
          pl.BlockSpec((tf, d), lambda j: (j, 0)),
      ] + side_in_specs,
      out_specs=[
          pl.BlockSpec((tm, d), lambda j: (0, 0)),
          pl.BlockSpec((d, tf), lambda j: (0, j)),
          pl.BlockSpec((d, tf), lambda j: (0, j)),
          pl.BlockSpec((tf, d), lambda j: (j, 0)),
      ] + side_out_specs,
      out_shape=[
          jax.ShapeDtypeStruct((tm, d), F32),
          jax.ShapeDtypeStruct(w_gate.shape, BF16),
          jax.ShapeDtypeStruct(w_up.shape, BF16),
          jax.ShapeDtypeStruct(w_down.shape, BF16),
      ] + side_out_shapes,
      scratch_shapes=[pltpu.VMEM((tm, d), BF16)],
      compiler_params=_params("arbitrary"),
      name="ffn_head",
  )(x, g, w_gate, w_up, w_down, *side_arrays)


def _ffn_kernel(x_ref, g_ref, wg_ref, wu_ref, wd_ref, *rest, has_head):
  if has_head:
    head_hbm, o_ref, h_ref, copy_sem = rest
  else:
    o_ref, h_ref = rest

  def step(first):
    for r in range(h_ref.shape[0] // FFN_SUB_ROWS):
      rows = slice(r * FFN_SUB_ROWS, (r + 1) * FFN_SUB_ROWS)
      if first:
        base = x_ref[rows, :]
        h = _rmsnorm(base, g_ref[...]).astype(BF16)
        h_ref[rows, :] = h
      else:
        base = o_ref[rows, :]
        h = h_ref[rows, :]
      gate = _dot(h, wg_ref[...])
      up = _dot(h, wu_ref[...])
      act = (gate * jax.nn.sigmoid(gate)) * up
      o_ref[rows, :] = base + _dot(act.astype(BF16), wd_ref[...])

  def compute_tile():
    lax.cond(pl.program_id(1) == 0, lambda: step(True), lambda: step(False))

  def copy_head_tile():
    @pl.when(pl.program_id(1) == 0)
    def _():
      copy = pltpu.make_async_copy(head_hbm, o_ref, copy_sem)
      copy.start()
      copy.wait()

  if has_head:
    lax.cond(pl.program_id(0) == 0, copy_head_tile, compute_tile)
  else:
    compute_tile()


def _ffn(x, g, wg, wu, wd_half, head=None):
  m, d = x.shape
  dff = wg.shape[1]
  tm, tf = FFN_ROW_TILE, FFN_FF_TILE
  has_head = head is not None
  chunk = (lambda i, j: jnp.where(i == 0, 0, j)) if has_head else (lambda i, j: j)
  return pl.pallas_call(
      functools.partial(_ffn_kernel, has_head=has_head),
      grid=(m // tm, dff // tf),
      in_specs=[
          pl.BlockSpec((tm, d), lambda i, j: (i, 0)),
          pl.BlockSpec((1, d), lambda i, j: (0, 0)),
          pl.BlockSpec((d, tf), lambda i, j: (0, chunk(i, j))),
          pl.BlockSpec((d, tf), lambda i, j: (0, chunk(i, j))),
          pl.BlockSpec((tf, d), lambda i, j: (chunk(i, j), 0)),
      ] + ([pl.BlockSpec(memory_space=pl.ANY)] if has_head else []),
      out_specs=pl.BlockSpec((tm, d), lambda i, j: (i, 0)),
      out_shape=jax.ShapeDtypeStruct((m, d), F32),
      scratch_shapes=([pltpu.VMEM((tm, d), BF16)]
                      + ([pltpu.SemaphoreType.DMA(())] if has_head else [])),
      compiler_params=_params("arbitrary" if has_head else "parallel", "arbitrary"),
      name="ffn",
  )(x, g, wg, wu, wd_half, *([head] if has_head else []))


def _slab_cast_jobs(steps, *weights, scale=1.0):
  jobs = []
  for w in weights:
    rows = max(BF16_TILE_ROWS, w.shape[0] // steps)
    assert w.shape[0] % rows == 0 and rows % BF16_TILE_ROWS == 0
    last = w.shape[0] // rows - 1
    jobs.append(_cast_job(w, (rows, w.shape[1]), lambda t, last=last: (jnp.minimum(t, last), 0),
                          scale=scale))
  return jobs


def _w_in_repack_jobs(steps, w_t, n_a, n_f, n_b):
  d = w_t.shape[1]
  lanes = V7X_LANES
  slabs = d // lanes
  assert d % lanes == 0 and slabs <= steps and n_f % 8 == 0 and n_f <= lanes
  assert n_a % n_f == 0 and n_b <= n_a + n_f
  slab = lambda t: jnp.minimum(t, slabs - 1)

  def gate_rows(ref):
    return jnp.concatenate([ref[...], jnp.zeros((lanes - n_f, lanes), F32)], axis=0).T

  out = lambda n: jax.ShapeDtypeStruct((d, n), BF16)
  out_spec = lambda n: pl.BlockSpec((lanes, n), lambda t: (slab(t), 0))
  return [
      _SideJob((w_t,), (pl.BlockSpec((n_a, lanes), lambda t: (0, slab(t))),),
               out(n_a), out_spec(n_a), lambda ref: ref[...].T),
      _SideJob((w_t,), (pl.BlockSpec((n_a + n_f, lanes), lambda t: (1, slab(t))),),
               out(n_b), out_spec(n_b), lambda ref: ref[:n_b, :].T),
      _SideJob((w_t,), (pl.BlockSpec((n_f, lanes), lambda t: (n_a // n_f, slab(t))),),
               out(lanes), out_spec(lanes), gate_rows),
  ]


def _swap_matrix():
  i = np.arange(PERM_ROWS)
  src = (i % RESIDUES) * RESIDUES + i // RESIDUES
  return jnp.asarray(np.eye(PERM_ROWS, dtype=np.float32)[src], BF16)


def _norm_matmul_kernel(x_ref, g_ref, w_ref, *rest, permute, narrow, jobs, key_cols):
  rest = list(rest)
  swap_ref = rest.pop(0) if permute else None
  wn_ref = rest.pop(0) if narrow else None
  side_in = [rest.pop(0) for job in jobs for _ in job.arrays]
  o_ref = rest.pop(0)
  on_ref = rest.pop(0) if narrow else None
  side_out = [rest.pop(0) for _ in jobs]
  assert not rest, "unexpected extra refs"
  tm = x_ref.shape[0]
  _run_side_jobs(jobs, side_in, side_out)

  h = _rmsnorm(x_ref[...], g_ref[...]).astype(BF16)
  if permute:
    h = jnp.concatenate(
        [_dot(swap_ref[...], h[a * PERM_ROWS:(a + 1) * PERM_ROWS, :]).astype(BF16)
         for a in range(tm // PERM_ROWS)], axis=0)
  if narrow:
    on_ref[...] = _dot(h, wn_ref[...])
  acc = _dot(h, w_ref[...])
  lo, hi = key_cols
  res = jnp.concatenate([acc[:, :lo], acc[:, lo:hi] * (SCALE * LOG2_E), acc[:, hi:]],
                        axis=1).astype(o_ref.dtype)
  if permute:
    per = PERM_ROWS // RESIDUES
    for a in range(tm // PERM_ROWS):
      for r in range(RESIDUES):
        start = a * PERM_ROWS + r * per
        o_ref[r, a * per:(a + 1) * per, :] = res[start:start + per, :]
  else:
    o_ref[...] = res


def _norm_matmul(x, g, w, out_dtype, key_cols, residue_major=None, w_narrow=None, jobs=()):
  m, d = x.shape
  n = w.shape[1]
  tm = PROJ_ROW_TILE
  narrow = w_narrow is not None
  permute = residue_major is not None
  assert not (narrow and permute)
  in_specs = [
      pl.BlockSpec((tm, d), lambda i: (i, 0)),
      pl.BlockSpec((1, d), lambda i: (0, 0)),
      pl.BlockSpec((d, n), lambda i: (0, 0)),
  ]
  args = [x, g, w]
  if permute:
    batch, seq = residue_major
    tiles = seq // tm
    in_specs.append(pl.BlockSpec((PERM_ROWS, PERM_ROWS), lambda i: (0, 0)))
    args.append(_swap_matrix())
    out_specs = [pl.BlockSpec((RESIDUES, tm // RESIDUES, n), lambda i: (i // tiles, i % tiles, 0))]
    out_shape = [jax.ShapeDtypeStruct((batch * RESIDUES, seq // RESIDUES, n), out_dtype)]
  else:
    out_specs = [pl.BlockSpec((tm, n), lambda i: (i, 0))]
    out_shape = [jax.ShapeDtypeStruct((m, n), out_dtype)]
  if narrow:
    in_specs.append(pl.BlockSpec(w_narrow.shape, lambda i: (0, 0)))
    args.append(w_narrow)
    out_specs.append(pl.BlockSpec((tm, w_narrow.shape[1]), lambda i: (i, 0)))
    out_shape.append(jax.ShapeDtypeStruct((m, w_narrow.shape[1]), F32))
  side_arrays, side_in_specs, side_out_specs, side_out_shapes = _side_args(jobs)
  outs = pl.pallas_call(
      functools.partial(_norm_matmul_kernel, permute=permute, narrow=narrow, jobs=tuple(jobs),
                        key_cols=key_cols),
      grid=(m // tm,),
      in_specs=in_specs + side_in_specs,
      out_specs=out_specs + side_out_specs,
      out_shape=out_shape + side_out_shapes,
      compiler_params=_params("parallel"),
      name="norm_matmul",
  )(*args, *side_arrays)
  outs = list(outs)
  if permute:
    outs[0] = outs[0].reshape(m, n)
  return outs


def _cumsum_kernel(fl_ref, bf_ref, c_ref):
  s = fl_ref.shape[0]
  z = fl_ref[...] + bf_ref[...]
  logf = jnp.minimum(z, 0.0) - jnp.log1p(jnp.exp(-jnp.abs(z)))
  lt = logf.T[0:N_HEADS_FOX, :]
  row = lax.broadcasted_iota(jnp.int32, (V7X_LANES, V7X_LANES), 0)
  col = lax.broadcasted_iota(jnp.int32, (V7X_LANES, V7X_LANES), 1)
  upper = (row <= col).astype(F32)
  carry = jnp.zeros((N_HEADS_FOX, 1), F32)
  for j in range(s // V7X_LANES):
    blk = lt[:, j * V7X_LANES:(j + 1) * V7X_LANES]
    cs = jnp.dot(blk, upper, preferred_element_type=F32,
                 precision=lax.Precision.HIGHEST) + carry
    c_ref[:, j * V7X_LANES:(j + 1) * V7X_LANES] = cs
    carry = cs[:, V7X_LANES - 1:V7X_LANES]


def _fox_decay(f_logit, b_f_row, batch, seq):
  return pl.pallas_call(
      _cumsum_kernel,
      grid=(batch,),
      in_specs=[
          pl.BlockSpec((seq, V7X_LANES), lambda b: (b, 0)),
          pl.BlockSpec((1, V7X_LANES), lambda b: (0, 0)),
      ],
      out_specs=pl.BlockSpec((None, N_HEADS_FOX, seq), lambda b: (b, 0, 0)),
      out_shape=jax.ShapeDtypeStruct((batch, N_HEADS_FOX, seq), F32),
      compiler_params=_params("parallel"),
      name="fox_decay",
  )(f_logit, b_f_row)


def _fox_phases(q_ref, k_ref, v_ref, c_ref, o_ref):
  seq = q_ref.shape[0]
  tq = FOX_Q_TILE
  h = pl.program_id(1)
  crow = c_ref[pl.ds(h, 1), :] * LOG2_E
  row = lax.broadcasted_iota(jnp.int32, (tq, tq), 0)
  col = lax.broadcasted_iota(jnp.int32, (tq, tq), 1)
  diag_mask = jnp.where(col > row, NEG_INF, 0.0).astype(F32)

  def scores(i):
    t0, t1 = i * tq, (i + 1) * tq
    q = q_ref[t0:t1, :]
    bias = crow[:, t1 - 1:t1] - crow[:, 0:t1]
    s_diag = _dot_nt(q, k_ref[t0:t1, :]) + bias[:, t0:t1] + diag_mask
    s_off = _dot_nt(q, k_ref[0:t0, :]) + bias[:, 0:t0] if i > 0 else None
    return s_diag, s_off

  def finish(i, s_diag, s_off):
    t0, t1 = i * tq, (i + 1) * tq
    m = jnp.max(s_diag, axis=-1, keepdims=True)
    if i > 0:
      m = jnp.maximum(m, jnp.max(s_off, axis=-1, keepdims=True))
    o, l = _weighted_values(jnp.exp2(s_diag - m), v_ref[t0:t1, :])
    if i > 0:
      o_off, l_off = _weighted_values(jnp.exp2(s_off - m), v_ref[0:t0, :])
      o, l = o + o_off, l + l_off
    o_ref[t0:t1, :] = (o / l).astype(o_ref.dtype)

  n_tiles = seq // tq
  pending = [scores(i) for i in range(min(FOX_LOOKAHEAD, n_tiles))]
  yield
  for i in range(n_tiles):
    if i + FOX_LOOKAHEAD < n_tiles:
      pending.append(scores(i + FOX_LOOKAHEAD))
      yield
    finish(i, *pending.pop(0))
    yield


def _t5_bucket_np(dist):
  max_exact = N_REL_BUCKETS // 2
  d = np.maximum(dist, 1).astype(np.float32)
  large = max_exact + (np.log(d / np.float32(max_exact))
                       / np.float32(math.log(REL_MAX_DISTANCE / max_exact))
                       * np.float32(N_REL_BUCKETS - max_exact)).astype(np.int32)
  large = np.minimum(large, N_REL_BUCKETS - 1)
  return np.where(dist < max_exact, dist, large).astype(np.int32)


def _block_positions(dilation):
  n = WINDOW_KEYS
  m = RESIDUES // dilation
  rows = n // m
  j = np.arange(m)[:, None]
  qpos = (n + m * np.arange(rows)[None, :] + j).reshape(-1)
  kpos = (m * np.arange(2 * rows)[None, :] + j).reshape(-1)
  return qpos, kpos


def _band_buckets():
  n = WINDOW_KEYS
  tiles = []
  for _, dilation in DILATED_PATTERNS:
    qpos, kpos = _block_positions(dilation)
    rel = qpos[:, None] - np.concatenate([kpos, qpos])[None, :]
    valid = (rel >= 0) & (rel <= n)
    bucket = _t5_bucket_np(np.maximum(rel, 0) * dilation)
    tiles.append(np.where(valid, bucket, -1))
  return np.stack(tiles).astype(np.int32)


def _bias_kernel(tab_ref, bkt_ref, o_ref):
  bkt = bkt_ref[...]
  for h in range(N_HEADS_DIL):
    acc = jnp.full(bkt.shape, NEG_INF, F32)
    for b in range(N_REL_BUCKETS):
      acc = jnp.where(bkt == b, tab_ref[b, h] * LOG2_E, acc)
    o_ref[h] = acc


def _band_bias(rel_table):
  buckets = jnp.asarray(_band_buckets())
  p, n, n2 = buckets.shape
  return pl.pallas_call(
      _bias_kernel,
      grid=(p,),
      in_specs=[
          pl.BlockSpec(memory_space=pltpu.SMEM),
          pl.BlockSpec((None, n, n2), lambda i: (i, 0, 0)),
      ],
      out_specs=pl.BlockSpec((None, N_HEADS_DIL, n, n2), lambda i: (i, 0, 0, 0)),
      out_shape=jax.ShapeDtypeStruct((p, N_HEADS_DIL, n, n2), F32),
      compiler_params=_params("parallel"),
      name="band_bias",
  )(rel_table, buckets)


def _dil_phases(q_ref, k_ref, v_ref, bm_ref, o_ref, *scratch):
  seq = q_ref.shape[0]
  n = WINDOW_KEYS
  seg = seq // RESIDUES
  last = len(DILATED_PATTERNS) - 1
  assert DILATED_PATTERNS[last][1] == RESIDUES and n == seg
  accs, lses = scratch[:last], scratch[last:]

  def gather(ref, starts, size):
    return jnp.concatenate([ref[st:st + size, :] for st in starts], axis=0).astype(BF16)

  def block_rows(segments, nb):
    rows = n // len(segments)
    q_starts = [s * seg + nb * rows for s in segments]
    if nb == 0:
      return rows, q_starts, q_starts, rows
    return rows, q_starts, [st - rows for st in q_starts], 2 * rows

  def scores(p, segments, nb):
    rows, q_starts, k_starts, k_rows = block_rows(segments, nb)
    bm = bm_ref[p, :, 2 * n:3 * n] if nb == 0 else bm_ref[p, :, 0:2 * n]
    return _dot_nt(gather(q_ref, q_starts, rows), gather(k_ref, k_starts, k_rows)) + bm

  def finish(p, segments, nb, e, m):
    rows, q_starts, k_starts, k_rows = block_rows(segments, nb)
    o, l = _weighted_values(e, gather(v_ref, k_starts, k_rows))
    o = o / l
    lse = m + jnp.log2(l)
    if p < last:
      for j, st in enumerate(q_starts):
        accs[p][st:st + rows, :] = o[j * rows:(j + 1) * rows, :]
        lses[p][st:st + rows, :] = lse[j * rows:(j + 1) * rows, :]
      return
    seg_rows = slice(q_starts[0], q_starts[0] + n)
    all_lse = [ref[seg_rows, :] for ref in lses] + [lse]
    all_out = [ref[seg_rows, :] for ref in accs] + [o]
    top = functools.reduce(jnp.maximum, all_lse)
    weights = [jnp.exp2(x - top) for x in all_lse]
    add = lambda a, b: a + b
    mixed = (functools.reduce(add, [w * a for w, a in zip(weights, all_out)])
             / functools.reduce(add, weights))
    o_ref[seg_rows, :] = mixed.astype(o_ref.dtype)

  blocks = [(p, list(range(r, RESIDUES, d)), nb)
            for p, (_, d) in enumerate(DILATED_PATTERNS)
            for r in range(d) for nb in range(seq // (n * d))]
  for g in range(0, len(blocks), DIL_BLOCK_GROUP):
    group = blocks[g:g + DIL_BLOCK_GROUP]
    ss = [scores(*blk) for blk in group]
    yield
    ms = [jnp.max(s, axis=-1, keepdims=True) for s in ss]
    es = [jnp.exp2(s - m) for s, m in zip(ss, ms)]
    yield
    for blk, e, m in zip(group, es, ms):
      finish(*blk, e, m)
    yield


def _mixer_kernel(qa_ref, ka_ref, va_ref, c_ref, qb_ref, kb_ref, vb_ref, bm_ref,
                  oa_ref, ob_ref, *scratch):
  streams = [_fox_phases(qa_ref, ka_ref, va_ref, c_ref, oa_ref),
             _dil_phases(qb_ref, kb_ref, vb_ref, bm_ref, ob_ref, *scratch)]
  while streams:
    for stream in list(streams):
      try:
        next(stream)
      except StopIteration:
        streams.remove(stream)


def _token_mixers(u_a, c, u_b, band_bias, batch, seq):
  assert seq // RESIDUES == WINDOW_KEYS and N_HEADS_FOX == N_HEADS_DIL
  blk = lambda off: pl.BlockSpec((seq, HEAD_DIM), lambda b, h: (b, off + h))
  n_pat = len(DILATED_PATTERNS)
  scratch = [pltpu.VMEM((seq, HEAD_DIM), F32) for _ in range(2 * (n_pat - 1))]
  out_blk = pl.BlockSpec((seq, HEAD_DIM), lambda b, h: (b, h))
  return pl.pallas_call(
      _mixer_kernel,
      grid=(batch, N_HEADS_FOX),
      in_specs=[
          blk(0), blk(N_HEADS_FOX), blk(2 * N_HEADS_FOX),
          pl.BlockSpec((None, N_HEADS_FOX, seq), lambda b, h: (b, 0, 0)),
          blk(0), blk(N_HEADS_DIL), blk(2 * N_HEADS_DIL),
          pl.BlockSpec((n_pat, None, WINDOW_KEYS, 3 * WINDOW_KEYS), lambda b, h: (0, h, 0, 0)),
      ],
      out_specs=[out_blk, out_blk],
      out_shape=[jax.ShapeDtypeStruct((batch * seq, D_FOX), BF16),
                 jax.ShapeDtypeStruct((batch * seq, D_DIL), BF16)],
      scratch_shapes=scratch,
      compiler_params=_params("parallel", "arbitrary"),
      name="token_mixers",
  )(u_a, u_a, u_a, c, u_b, u_b, u_b, band_bias)


def _out_proj_kernel(x_ref, a_ref, b_ref, swap_ref, wa_ref, wb_ref, *rest, jobs):
  n_in = sum(len(job.arrays) for job in jobs)
  side_in, o_ref, side_out = rest[:n_in], rest[n_in], rest[n_in + 1:]
  _run_side_jobs(jobs, side_in, side_out)
  per = PERM_ROWS // RESIDUES
  for a in range(x_ref.shape[0] // PERM_ROWS):
    rows = slice(a * PERM_ROWS, (a + 1) * PERM_ROWS)
    slab = jnp.concatenate([b_ref[r, a * per:(a + 1) * per, :] for r in range(RESIDUES)], axis=0)
    o_b = _dot(swap_ref[...], slab).astype(BF16)
    o_ref[rows, :] = x_ref[rows, :] + _dot(a_ref[rows, :], wa_ref[...]) + _dot(o_b, wb_ref[...])


def _out_proj(x, o_a, o_b, wo, batch, seq, jobs=()):
  m, d = x.shape
  tm = PROJ_ROW_TILE
  tiles = seq // tm
  ca, cb = o_a.shape[1], o_b.shape[1]
  assert ca == cb and wo.shape[0] == ca + cb
  resident = pl.Buffered(1)
  side_arrays, side_in_specs, side_out_specs, side_out_shapes = _side_args(jobs)
  return pl.pallas_call(
      functools.partial(_out_proj_kernel, jobs=tuple(jobs)),
      grid=(m // tm,),
      in_specs=[
          pl.BlockSpec((tm, d), lambda i: (i, 0)),
          pl.BlockSpec((tm, ca), lambda i: (i, 0)),
          pl.BlockSpec((RESIDUES, tm // RESIDUES, cb), lambda i: (i // tiles, i % tiles, 0)),
          pl.BlockSpec((PERM_ROWS, PERM_ROWS), lambda i: (0, 0)),
          pl.BlockSpec((ca, d), lambda i: (0, 0), pipeline_mode=resident),
          pl.BlockSpec((cb, d), lambda i: (1, 0), pipeline_mode=resident),
      ] + side_in_specs,
      out_specs=[pl.BlockSpec((tm, d), lambda i: (i, 0))] + side_out_specs,
      out_shape=[jax.ShapeDtypeStruct((m, d), F32)] + side_out_shapes,
      compiler_params=_params("parallel"),
      name="out_proj",
  )(x, o_a, o_b.reshape(batch * RESIDUES, seq // RESIDUES, cb), _swap_matrix(), wo, wo,
    *side_arrays)


def _ple_kernel(x_ref, p_ref, g_ref, wg_ref, wp_ref, gf_ref, o_ref, *, final_norm):
  x = x_ref[...]
  h = _rmsnorm(x, g_ref[...]).astype(BF16)
  gate = jax.nn.sigmoid(_dot(h, wg_ref[...]))
  y = x + gate * _dot(p_ref[...].astype(BF16), wp_ref[...])
  o_ref[...] = _rmsnorm(y, gf_ref[...]) if final_norm else y


def _ple(x, p, g, w_gate, w_proj, g_final, final_norm):
  m, d = x.shape
  tm = PROJ_ROW_TILE
  return pl.pallas_call(
      functools.partial(_ple_kernel, final_norm=final_norm),
      grid=(m // tm,),
      in_specs=[
          pl.BlockSpec((tm, d), lambda i: (i, 0)),
          pl.BlockSpec((tm, p.shape[1]), lambda i: (i, 0)),
          pl.BlockSpec((1, d), lambda i: (0, 0)),
          pl.BlockSpec(w_gate.shape, lambda i: (0, 0)),
          pl.BlockSpec(w_proj.shape, lambda i: (0, 0)),
          pl.BlockSpec((1, d), lambda i: (0, 0)),
      ],
      out_specs=pl.BlockSpec((tm, d), lambda i: (i, 0)),
      out_shape=jax.ShapeDtypeStruct((m, d), F32),
      compiler_params=_params("parallel"),
      name="ple",
  )(x, p, g, w_gate, w_proj, g_final)


def kernel(x, p, norm_ffn1, ffn1_w_gate, ffn1_w_up, ffn1_w_down, norm_mix, w_in, b_f, w_o,
           norm_ffn2, ffn2_w_gate, ffn2_w_up, ffn2_w_down, norm_ple, w_ple_gate, w_ple_proj,
           rel_table, norm_final):
  batch, seq, d = x.shape
  depth = p.shape[0]
  m = batch * seq
  bf = lambda w: w.astype(BF16)
  row = lambda g: g.reshape(1, -1).astype(F32)

  band_bias = _band_bias(rel_table.astype(F32))
  xs = x.reshape(m, d).astype(F32)
  for i in range(depth):
    g_ffn1 = row(norm_ffn1[i])
    proj_steps = m // PROJ_ROW_TILE
    head_steps = _ffn_head_steps(ffn1_w_gate[i])
    repack = lambda steps: _w_in_repack_jobs(steps, jnp.swapaxes(w_in[i], 0, 1),
                                             3 * D_FOX, N_HEADS_FOX, 3 * D_DIL)
    a_job, _, f_job = repack(head_steps)
    _, b_job, _ = repack(proj_steps)
    head, wg1, wu1, wd1, w_a, w_f = _ffn_head(
        xs, g_ffn1, ffn1_w_gate[i], ffn1_w_up[i], ffn1_w_down[i], jobs=[a_job, f_job])
    xs = _ffn(xs, g_ffn1, wg1, wu1, wd1, head=head)

    b_f_row = jnp.pad(b_f[i].astype(F32), (0, V7X_LANES - N_HEADS_FOX)).reshape(1, V7X_LANES)
    g_mix = row(norm_mix[i])
    u_a, f_logit, wo, w_b = _norm_matmul(
        xs, g_mix, w_a, BF16, (D_FOX, 2 * D_FOX), w_narrow=w_f,
        jobs=_slab_cast_jobs(proj_steps, w_o[i]) + [b_job])
    u_b, wg2, wu2, wd2 = _norm_matmul(
        xs, g_mix, w_b, F32, (D_DIL, 2 * D_DIL), residue_major=(batch, seq),
        jobs=_slab_cast_jobs(proj_steps, ffn2_w_gate[i], ffn2_w_up[i])
        + _slab_cast_jobs(proj_steps, ffn2_w_down[i], scale=0.5))

    c = _fox_decay(f_logit, b_f_row, batch, seq)
    o_a, o_b = _token_mixers(u_a, c, u_b, band_bias, batch, seq)
    xs, w_gate, w_ple = _out_proj(xs, o_a, o_b, wo, batch, seq,
                                  jobs=_slab_cast_jobs(proj_steps, w_ple_gate[i], w_ple_proj[i]))

    xs = _ffn(xs, row(norm_ffn2[i]), wg2, wu2, wd2)
    last = i == depth - 1
    xs = _ple(xs, p[i].reshape(m, -1), row(norm_ple[i]), w_gate, w_ple,
              row(norm_final), final_norm=last)
  return xs.reshape(batch, seq, d).astype(x.dtype)
```

```python
import functools
import math
from typing import Any, Callable, NamedTuple

import jax
import jax.numpy as jnp
import numpy as np
from jax import lax
from jax.experimental import pallas as pl
from jax.experimental.pallas import tpu as pltpu

F32 = jnp.float32
BF16 = jnp.bfloat16

HEAD_DIM = 128
N_HEADS_FOX = 8
N_HEADS_DIL = 8
D_FOX = N_HEADS_FOX * HEAD_DIM
D_DIL = N_HEADS_DIL * HEAD_DIM
DILATED_PATTERNS = ((128, 1), (512, 4), (2048, 16))
WINDOW_KEYS = 128
N_REL_BUCKETS = 32
REL_MAX_DISTANCE = 2048
RMS_EPS = 1e-6
NEG_INF = -1e30
SCALE = HEAD_DIM ** -0.5
LOG2_E = math.log2(math.e)

V7X_LANES = 128
BF16_TILE_ROWS = 16
V7X_VMEM_LIMIT_BYTES = 56 * 1024 * 1024

RESIDUES = max(d for _, d in DILATED_PATTERNS)
PERM_ROWS = RESIDUES * RESIDUES

FFN_ROW_TILE = 1024
FFN_SUB_ROWS = 512
FFN_FF_TILE = 512
FFN_HEAD_FF_TILE = 256
PROJ_ROW_TILE = 512
FOX_Q_TILE = 256
FOX_LOOKAHEAD = 2
DIL_BLOCK_GROUP = 12


def _params(*semantics):
  return pltpu.CompilerParams(dimension_semantics=semantics,
                              vmem_limit_bytes=V7X_VMEM_LIMIT_BYTES)


def _rmsnorm(x, g):
  ms = jnp.mean(x * x, axis=-1, keepdims=True)
  return x * lax.rsqrt(ms + RMS_EPS) * g


def _dot(a, b):
  return jnp.dot(a, b, preferred_element_type=F32)


def _dot_nt(a, b):
  return lax.dot_general(a, b, (((1,), (1,)), ((), ())), preferred_element_type=F32)


def _weighted_values(e, v):
  both = _dot(e.astype(BF16), jnp.concatenate([v, jnp.ones_like(v)], axis=1))
  return both[:, :HEAD_DIM], both[:, HEAD_DIM:]


class _SideJob(NamedTuple):
  arrays: tuple
  in_specs: tuple
  out_shape: Any
  out_spec: Any
  fn: Callable


def _cast_job(a, block, index, scale=1.0):
  spec = pl.BlockSpec(block, index)
  return _SideJob((a,), (spec,), jax.ShapeDtypeStruct(a.shape, BF16), spec,
                  lambda r: r[...] * scale)


def _run_side_jobs(jobs, in_refs, out_refs):
  in_refs = list(in_refs)
  for job, out_ref in zip(jobs, out_refs):
    refs = [in_refs.pop(0) for _ in job.arrays]
    out_ref[...] = job.fn(*refs).astype(out_ref.dtype)


def _side_args(jobs):
  arrays = [a for job in jobs for a in job.arrays]
  in_specs = [s for job in jobs for s in job.in_specs]
  return arrays, in_specs, [job.out_spec for job in jobs], [job.out_shape for job in jobs]


def _ffn_head_kernel(x_ref, g_ref, wg_ref, wu_ref, wd_ref, *rest, jobs):
  n_in = sum(len(job.arrays) for job in jobs)
  side_in, (o_ref, og_ref, ou_ref, od_ref) = rest[:n_in], rest[n_in:n_in + 4]
  side_out, h_ref = rest[n_in + 4:-1], rest[-1]
  _run_side_jobs(jobs, side_in, side_out)

  @pl.when(pl.program_id(0) == 0)
  def _():
    x = x_ref[...]
    h_ref[...] = _rmsnorm(x, g_ref[...]).astype(BF16)
    o_ref[...] = x

  wg = wg_ref[...].astype(BF16)
  wu = wu_ref[...].astype(BF16)
  wd = (wd_ref[...] * 0.5).astype(BF16)
  og_ref[...] = wg
  ou_ref[...] = wu
  od_ref[...] = wd
  for r in range(h_ref.shape[0] // FFN_SUB_ROWS):
    rows = slice(r * FFN_SUB_ROWS, (r + 1) * FFN_SUB_ROWS)
    h = h_ref[rows, :]
    gate = _dot(h, wg)
    up = _dot(h, wu)
    act = (gate * jax.nn.sigmoid(gate)) * up
    o_ref[rows, :] += _dot(act.astype(BF16), wd)


def _ffn_head_steps(w_gate):
  return w_gate.shape[1] // FFN_HEAD_FF_TILE


def _ffn_head(x, g, w_gate, w_up, w_down, jobs=()):
  d = x.shape[1]
  tm, tf = FFN_ROW_TILE, FFN_HEAD_FF_TILE
  side_arrays, side_in_specs, side_out_specs, side_out_shapes = _side_args(jobs)
  return pl.pallas_call(
      functools.partial(_ffn_head_kernel, jobs=tuple(jobs)),
      grid=(_ffn_head_steps(w_gate),),
      in_specs=[
          pl.BlockSpec((tm, d), lambda j: (0, 0), pipeline_mode=pl.Buffered(1)),
          pl.BlockSpec((1, d), lambda j: (0, 0)),
          pl.BlockSpec((d, tf), lambda j: (0, j)),
          pl.BlockSpec((d, tf), lambda j: (0, j)),
          pl.BlockSpec((tf, d), lambda j: (j, 0)),
      ] + side_in_specs,
      out_specs=[
          pl.BlockSpec((tm, d), lambda j: (0, 0)),
          pl.BlockSpec((d, tf), lambda j: (0, j)),
          pl.BlockSpec((d, tf), lambda j: (0, j)),
          pl.BlockSpec((tf, d), lambda j: (j, 0)),
      ] + side_out_specs,
      out_shape=[
          jax.ShapeDtypeStruct((tm, d), F32),
          jax.ShapeDtypeStruct(w_gate.shape, BF16),
          jax.ShapeDtypeStruct(w_up.shape, BF16),
          jax.ShapeDtypeStruct(w_down.shape, BF16),
      ] + side_out_shapes,
      scratch_shapes=[pltpu.VMEM((tm, d), BF16)],
      compiler_params=_params("arbitrary"),
      name="ffn_head",
  )(x, g, w_gate, w_up, w_down, *side_arrays)


def _ffn_kernel(x_ref, g_ref, wg_ref, wu_ref, wd_ref, *rest, has_head):
  if has_head:
    head_hbm, o_ref, h_ref, copy_sem = rest
  else:
    o_ref, h_ref = rest

  def step(first):
    for r in range(h_ref.shape[0] // FFN_SUB_ROWS):
      rows = slice(r * FFN_SUB_ROWS, (r + 1) * FFN_SUB_ROWS)
      if first:
        base = x_ref[rows, :]
        h = _rmsnorm(base, g_ref[...]).astype(BF16)
        h_ref[rows, :] = h
      else:
        base = o_ref[rows, :]
        h = h_ref[rows, :]
      gate = _dot(h, wg_ref[...])
      up = _dot(h, wu_ref[...])
      act = (gate * jax.nn.sigmoid(gate)) * up
      o_ref[rows, :] = base + _dot(act.astype(BF16), wd_ref[...])

  def compute_tile():
    lax.cond(pl.program_id(1) == 0, lambda: step(True), lambda: step(False))

  def copy_head_tile():
    @pl.when(pl.program_id(1) == 0)
    def _():
      copy = pltpu.make_async_copy(head_hbm, o_ref, copy_sem)
      copy.start()
      copy.wait()

  if has_head:
    lax.cond(pl.program_id(0) == 0, copy_head_tile, compute_tile)
  else:
    compute_tile()


def _ffn(x, g, wg, wu, wd_half, head=None):
  m, d = x.shape
  dff = wg.shape[1]
  tm, tf = FFN_ROW_TILE, FFN_FF_TILE
  has_head = head is not None
  chunk = (lambda i, j: jnp.where(i == 0, 0, j)) if has_head else (lambda i, j: j)
  return pl.pallas_call(
      functools.partial(_ffn_kernel, has_head=has_head),
      grid=(m // tm, dff // tf),
      in_specs=[
          pl.BlockSpec((tm, d), lambda i, j: (i, 0)),
          pl.BlockSpec((1, d), lambda i, j: (0, 0)),
          pl.BlockSpec((d, tf), lambda i, j: (0, chunk(i, j))),
          pl.BlockSpec((d, tf), lambda i, j: (0, chunk(i, j))),
          pl.BlockSpec((tf, d), lambda i, j: (chunk(i, j), 0)),
      ] + ([pl.BlockSpec(memory_space=pl.ANY)] if has_head else []),
      out_specs=pl.BlockSpec((tm, d), lambda i, j: (i, 0)),
      out_shape=jax.ShapeDtypeStruct((m, d), F32),
      scratch_shapes=([pltpu.VMEM((tm, d), BF16)]
                      + ([pltpu.SemaphoreType.DMA(())] if has_head else [])),
      compiler_params=_params("arbitrary" if has_head else "parallel", "arbitrary"),
      name="ffn",
  )(x, g, wg, wu, wd_half, *([head] if has_head else []))


def _slab_cast_jobs(steps, *weights, scale=1.0):
  jobs = []
  for w in weights:
    rows = max(BF16_TILE_ROWS, w.shape[0] // steps)
    assert w.shape[0] % rows == 0 and rows % BF16_TILE_ROWS == 0
    last = w.shape[0] // rows - 1
    jobs.append(_cast_job(w, (rows, w.shape[1]), lambda t, last=last: (jnp.minimum(t, last), 0),
                          scale=scale))
  return jobs


def _w_in_repack_jobs(steps, w_t, n_a, n_f, n_b):
  d = w_t.shape[1]
  lanes = V7X_LANES
  slabs = d // lanes
  assert d % lanes == 0 and slabs <= steps and n_f % 8 == 0 and n_f <= lanes
  assert n_a % n_f == 0 and n_b <= n_a + n_f
  slab = lambda t: jnp.minimum(t, slabs - 1)

  def gate_rows(ref):
    return jnp.concatenate([ref[...], jnp.zeros((lanes - n_f, lanes), F32)], axis=0).T

  out = lambda n: jax.ShapeDtypeStruct((d, n), BF16)
  out_spec = lambda n: pl.BlockSpec((lanes, n), lambda t: (slab(t), 0))
  return [
      _SideJob((w_t,), (pl.BlockSpec((n_a, lanes), lambda t: (0, slab(t))),),
               out(n_a), out_spec(n_a), lambda ref: ref[...].T),
      _SideJob((w_t,), (pl.BlockSpec((n_a + n_f, lanes), lambda t: (1, slab(t))),),
               out(n_b), out_spec(n_b), lambda ref: ref[:n_b, :].T),
      _SideJob((w_t,), (pl.BlockSpec((n_f, lanes), lambda t: (n_a // n_f, slab(t))),),
               out(lanes), out_spec(lanes), gate_rows),
  ]


def _swap_matrix():
  i = np.arange(PERM_ROWS)
  src = (i % RESIDUES) * RESIDUES + i // RESIDUES
  return jnp.asarray(np.eye(PERM_ROWS, dtype=np.float32)[src], BF16)


def _norm_matmul_kernel(x_ref, g_ref, w_ref, *rest, permute, narrow, jobs, key_cols):
  rest = list(rest)
  swap_ref = rest.pop(0) if permute else None
  wn_ref = rest.pop(0) if narrow else None
  side_in = [rest.pop(0) for job in jobs for _ in job.arrays]
  o_ref = rest.pop(0)
  on_ref = rest.pop(0) if narrow else None
  side_out = [rest.pop(0) for _ in jobs]
  assert not rest, "unexpected extra refs"
  tm = x_ref.shape[0]
  _run_side_jobs(jobs, side_in, side_out)

  h = _rmsnorm(x_ref[...], g_ref[...]).astype(BF16)
  if permute:
    h = jnp.concatenate(
        [_dot(swap_ref[...], h[a * PERM_ROWS:(a + 1) * PERM_ROWS, :]).astype(BF16)
         for a in range(tm // PERM_ROWS)], axis=0)
  if narrow:
    on_ref[...] = _dot(h, wn_ref[...])
  acc = _dot(h, w_ref[...])
  lo, hi = key_cols
  res = jnp.concatenate([acc[:, :lo], acc[:, lo:hi] * (SCALE * LOG2_E), acc[:, hi:]],
                        axis=1).astype(o_ref.dtype)
  if permute:
    per = PERM_ROWS // RESIDUES
    for a in range(tm // PERM_ROWS):
      for r in range(RESIDUES):
        start = a * PERM_ROWS + r * per
        o_ref[r, a * per:(a + 1) * per, :] = res[start:start + per, :]
  else:
    o_ref[...] = res


def _norm_matmul(x, g, w, out_dtype, key_cols, residue_major=None, w_narrow=None, jobs=()):
  m, d = x.shape
  n = w.shape[1]
  tm = PROJ_ROW_TILE
  narrow = w_narrow is not None
  permute = residue_major is not None
  assert not (narrow and permute)
  in_specs = [
      pl.BlockSpec((tm, d), lambda i: (i, 0)),
      pl.BlockSpec((1, d), lambda i: (0, 0)),
      pl.BlockSpec((d, n), lambda i: (0, 0)),
  ]
  args = [x, g, w]
  if permute:
    batch, seq = residue_major
    tiles = seq // tm
    in_specs.append(pl.BlockSpec((PERM_ROWS, PERM_ROWS), lambda i: (0, 0)))
    args.append(_swap_matrix())
    out_specs = [pl.BlockSpec((RESIDUES, tm // RESIDUES, n), lambda i: (i // tiles, i % tiles, 0))]
    out_shape = [jax.ShapeDtypeStruct((batch * RESIDUES, seq // RESIDUES, n), out_dtype)]
  else:
    out_specs = [pl.BlockSpec((tm, n), lambda i: (i, 0))]
    out_shape = [jax.ShapeDtypeStruct((m, n), out_dtype)]
  if narrow:
    in_specs.append(pl.BlockSpec(w_narrow.shape, lambda i: (0, 0)))
    args.append(w_narrow)
    out_specs.append(pl.BlockSpec((tm, w_narrow.shape[1]), lambda i: (i, 0)))
    out_shape.append(jax.ShapeDtypeStruct((m, w_narrow.shape[1]), F32))
  side_arrays, side_in_specs, side_out_specs, side_out_shapes = _side_args(jobs)
  outs = pl.pallas_call(
      functools.partial(_norm_matmul_kernel, permute=permute, narrow=narrow, jobs=tuple(jobs),
                        key_cols=key_cols),
      grid=(m // tm,),
      in_specs=in_specs + side_in_specs,
      out_specs=out_specs + side_out_specs,
      out_shape=out_shape + side_out_shapes,
      compiler_params=_params("parallel"),
      name="norm_matmul",
  )(*args, *side_arrays)
  outs = list(outs)
  if permute:
    outs[0] = outs[0].reshape(m, n)
  return outs


def _cumsum_kernel(fl_ref, bf_ref, c_ref):
  s = fl_ref.shape[0]
  z = fl_ref[...].T[0:N_HEADS_FOX, :] + bf_ref[...]
  lt = jnp.minimum(z, 0.0) - jnp.log1p(jnp.exp(-jnp.abs(z)))
  row = lax.broadcasted_iota(jnp.int32, (V7X_LANES, V7X_LANES), 0)
  col = lax.broadcasted_iota(jnp.int32, (V7X_LANES, V7X_LANES), 1)
  upper = (row <= col).astype(F32)
  chunks = [jnp.dot(lt[:, j * V7X_LANES:(j + 1) * V7X_LANES], upper, preferred_element_type=F32,
                    precision=lax.Precision.HIGHEST) for j in range(s // V7X_LANES)]
  offset = jnp.zeros((N_HEADS_FOX, 1), F32)
  for j, cs in enumerate(chunks):
    c_ref[:, j * V7X_LANES:(j + 1) * V7X_LANES] = cs + offset
    offset = offset + cs[:, V7X_LANES - 1:V7X_LANES]


def _fox_decay(f_logit, b_f_col, batch, seq):
  return pl.pallas_call(
      _cumsum_kernel,
      grid=(batch,),
      in_specs=[
          pl.BlockSpec((seq, V7X_LANES), lambda b: (b, 0)),
          pl.BlockSpec((N_HEADS_FOX, 1), lambda b: (0, 0)),
      ],
      out_specs=pl.BlockSpec((None, N_HEADS_FOX, seq), lambda b: (b, 0, 0)),
      out_shape=jax.ShapeDtypeStruct((batch, N_HEADS_FOX, seq), F32),
      compiler_params=_params("parallel"),
      name="fox_decay",
  )(f_logit, b_f_col)


def _fox_phases(q_ref, k_ref, v_ref, c_ref, o_ref):
  seq = q_ref.shape[0]
  tq = FOX_Q_TILE
  h = pl.program_id(1)
  crow = c_ref[pl.ds(h, 1), :] * LOG2_E
  row = lax.broadcasted_iota(jnp.int32, (tq, tq), 0)
  col = lax.broadcasted_iota(jnp.int32, (tq, tq), 1)
  diag_mask = jnp.where(col > row, NEG_INF, 0.0).astype(F32)

  def scores(i):
    t0, t1 = i * tq, (i + 1) * tq
    q = q_ref[t0:t1, :]
    bias = crow[:, t1 - 1:t1] - crow[:, 0:t1]
    s_diag = _dot_nt(q, k_ref[t0:t1, :]) + bias[:, t0:t1] + diag_mask
    s_off = _dot_nt(q, k_ref[0:t0, :]) + bias[:, 0:t0] if i > 0 else None
    return s_diag, s_off

  def finish(i, s_diag, s_off):
    t0, t1 = i * tq, (i + 1) * tq
    m = jnp.max(s_diag, axis=-1, keepdims=True)
    if i > 0:
      m = jnp.maximum(m, jnp.max(s_off, axis=-1, keepdims=True))
    o, l = _weighted_values(jnp.exp2(s_diag - m), v_ref[t0:t1, :])
    if i > 0:
      o_off, l_off = _weighted_values(jnp.exp2(s_off - m), v_ref[0:t0, :])
      o, l = o + o_off, l + l_off
    o_ref[t0:t1, :] = (o / l).astype(o_ref.dtype)

  n_tiles = seq // tq
  pending = [scores(i) for i in range(min(FOX_LOOKAHEAD, n_tiles))]
  yield
  for i in range(n_tiles):
    if i + FOX_LOOKAHEAD < n_tiles:
      pending.append(scores(i + FOX_LOOKAHEAD))
      yield
    finish(i, *pending.pop(0))
    yield


def _t5_bucket_np(dist):
  max_exact = N_REL_BUCKETS // 2
  d = np.maximum(dist, 1).astype(np.float32)
  large = max_exact + (np.log(d / np.float32(max_exact))
                       / np.float32(math.log(REL_MAX_DISTANCE / max_exact))
                       * np.float32(N_REL_BUCKETS - max_exact)).astype(np.int32)
  large = np.minimum(large, N_REL_BUCKETS - 1)
  return np.where(dist < max_exact, dist, large).astype(np.int32)


def _block_positions(dilation):
  n = WINDOW_KEYS
  m = RESIDUES // dilation
  rows = n // m
  j = np.arange(m)[:, None]
  qpos = (n + m * np.arange(rows)[None, :] + j).reshape(-1)
  kpos = (m * np.arange(2 * rows)[None, :] + j).reshape(-1)
  return qpos, kpos


def _band_buckets():
  n = WINDOW_KEYS
  tiles = []
  for _, dilation in DILATED_PATTERNS:
    qpos, kpos = _block_positions(dilation)
    rel = qpos[:, None] - np.concatenate([kpos, qpos])[None, :]
    valid = (rel >= 0) & (rel <= n)
    bucket = _t5_bucket_np(np.maximum(rel, 0) * dilation)
    tiles.append(np.where(valid, bucket, -1))
  return np.stack(tiles).astype(np.int32)


def _bias_kernel(tab_ref, bkt_ref, o_ref):
  table = [[tab_ref[b, h] * LOG2_E for h in range(N_HEADS_DIL)] for b in range(N_REL_BUCKETS)]
  for r0 in range(0, bkt_ref.shape[0], BF16_TILE_ROWS):
    rows = slice(r0, r0 + BF16_TILE_ROWS)
    bkt = bkt_ref[rows, :]
    accs = [jnp.full(bkt.shape, NEG_INF, F32)] * N_HEADS_DIL
    for b in range(N_REL_BUCKETS):
      hit = bkt == b
      accs = [jnp.where(hit, table[b][h], acc) for h, acc in enumerate(accs)]
    for h, acc in enumerate(accs):
      o_ref[h, rows, :] = acc


def _band_bias(rel_table):
  buckets = jnp.asarray(_band_buckets())
  p, n, n2 = buckets.shape
  return pl.pallas_call(
      _bias_kernel,
      grid=(p,),
      in_specs=[
          pl.BlockSpec(memory_space=pltpu.SMEM),
          pl.BlockSpec((None, n, n2), lambda i: (i, 0, 0)),
      ],
      out_specs=pl.BlockSpec((None, N_HEADS_DIL, n, n2), lambda i: (i, 0, 0, 0)),
      out_shape=jax.ShapeDtypeStruct((p, N_HEADS_DIL, n, n2), F32),
      compiler_params=_params("parallel"),
      name="band_bias",
  )(rel_table, buckets)


def _dil_phases(q_ref, k_ref, v_ref, bm_ref, o_ref, *scratch):
  seq = q_ref.shape[0]
  n = WINDOW_KEYS
  seg = seq // RESIDUES
  last = len(DILATED_PATTERNS) - 1
  assert DILATED_PATTERNS[last][1] == RESIDUES and n == seg
  accs, lses = scratch[:last], scratch[last:]

  def gather(ref, starts, size):
    return jnp.concatenate([ref[st:st + size, :] for st in starts], axis=0).astype(BF16)

  def block_rows(segments, nb):
    rows = n // len(segments)
    q_starts = [s * seg + nb * rows for s in segments]
    if nb == 0:
      return rows, q_starts, q_starts, rows
    return rows, q_starts, [st - rows for st in q_starts], 2 * rows

  def scores(p, segments, nb):
    rows, q_starts, k_starts, k_rows = block_rows(segments, nb)
    bm = bm_ref[p, :, 2 * n:3 * n] if nb == 0 else bm_ref[p, :, 0:2 * n]
    return _dot_nt(gather(q_ref, q_starts, rows), gather(k_ref, k_starts, k_rows)) + bm

  def finish(p, segments, nb, e, m):
    rows, q_starts, k_starts, k_rows = block_rows(segments, nb)
    o, l = _weighted_values(e, gather(v_ref, k_starts, k_rows))
    o = o / l
    lse = m + jnp.log2(l)
    if p < last:
      for j, st in enumerate(q_starts):
        accs[p][st:st + rows, :] = o[j * rows:(j + 1) * rows, :]
        lses[p][st:st + rows, :] = lse[j * rows:(j + 1) * rows, :]
      return
    seg_rows = slice(q_starts[0], q_starts[0] + n)
    all_lse = [ref[seg_rows, :] for ref in lses] + [lse]
    all_out = [ref[seg_rows, :] for ref in accs] + [o]
    top = functools.reduce(jnp.maximum, all_lse)
    weights = [jnp.exp2(x - top) for x in all_lse]
    add = lambda a, b: a + b
    mixed = (functools.reduce(add, [w * a for w, a in zip(weights, all_out)])
             / functools.reduce(add, weights))
    o_ref[seg_rows, :] = mixed.astype(o_ref.dtype)

  blocks = [(p, list(range(r, RESIDUES, d)), nb)
            for p, (_, d) in enumerate(DILATED_PATTERNS)
            for r in range(d) for nb in range(seq // (n * d))]
  for g in range(0, len(blocks), DIL_BLOCK_GROUP):
    group = blocks[g:g + DIL_BLOCK_GROUP]
    ss = [scores(*blk) for blk in group]
    yield
    ms = [jnp.max(s, axis=-1, keepdims=True) for s in ss]
    es = [jnp.exp2(s - m) for s, m in zip(ss, ms)]
    yield
    for blk, e, m in zip(group, es, ms):
      finish(*blk, e, m)
    yield


def _mixer_kernel(qa_ref, ka_ref, va_ref, c_ref, qb_ref, kb_ref, vb_ref, bm_ref,
                  oa_ref, ob_ref, *scratch):
  streams = [_dil_phases(qb_ref, kb_ref, vb_ref, bm_ref, ob_ref, *scratch),
             _fox_phases(qa_ref, ka_ref, va_ref, c_ref, oa_ref)]
  while streams:
    for stream in list(streams):
      try:
        next(stream)
      except StopIteration:
        streams.remove(stream)


def _token_mixers(u_a, c, u_b, band_bias, batch, seq):
  assert seq // RESIDUES == WINDOW_KEYS and N_HEADS_FOX == N_HEADS_DIL
  blk = lambda off: pl.BlockSpec((seq, HEAD_DIM), lambda b, h: (b, off + h))
  n_pat = len(DILATED_PATTERNS)
  scratch = [pltpu.VMEM((seq, HEAD_DIM), F32) for _ in range(2 * (n_pat - 1))]
  out_blk = pl.BlockSpec((seq, HEAD_DIM), lambda b, h: (b, h))
  return pl.pallas_call(
      _mixer_kernel,
      grid=(batch, N_HEADS_FOX),
      in_specs=[
          blk(0), blk(N_HEADS_FOX), blk(2 * N_HEADS_FOX),
          pl.BlockSpec((None, N_HEADS_FOX, seq), lambda b, h: (b, 0, 0)),
          blk(0), blk(N_HEADS_DIL), blk(2 * N_HEADS_DIL),
          pl.BlockSpec((n_pat, None, WINDOW_KEYS, 3 * WINDOW_KEYS), lambda b, h: (0, h, 0, 0)),
      ],
      out_specs=[out_blk, out_blk],
      out_shape=[jax.ShapeDtypeStruct((batch * seq, D_FOX), BF16),
                 jax.ShapeDtypeStruct((batch * seq, D_DIL), BF16)],
      scratch_shapes=scratch,
      compiler_params=_params("parallel", "arbitrary"),
      name="token_mixers",
  )(u_a, u_a, u_a, c, u_b, u_b, u_b, band_bias)


def _out_proj_kernel(x_ref, a_ref, b_ref, swap_ref, wa_ref, wb_ref, *rest, jobs):
  n_in = sum(len(job.arrays) for job in jobs)
  side_in, o_ref, side_out = rest[:n_in], rest[n_in], rest[n_in + 1:]
  _run_side_jobs(jobs, side_in, side_out)
  per = PERM_ROWS // RESIDUES
  for a in range(x_ref.shape[0] // PERM_ROWS):
    rows = slice(a * PERM_ROWS, (a + 1) * PERM_ROWS)
    slab = jnp.concatenate([b_ref[r, a * per:(a + 1) * per, :] for r in range(RESIDUES)], axis=0)
    o_b = _dot(swap_ref[...], slab).astype(BF16)
    o_ref[rows, :] = x_ref[rows, :] + _dot(a_ref[rows, :], wa_ref[...]) + _dot(o_b, wb_ref[...])


def _out_proj(x, o_a, o_b, wo, batch, seq, jobs=()):
  m, d = x.shape
  tm = PROJ_ROW_TILE
  tiles = seq // tm
  ca, cb = o_a.shape[1], o_b.shape[1]
  assert ca == cb and wo.shape[0] == ca + cb
  resident = pl.Buffered(1)
  side_arrays, side_in_specs, side_out_specs, side_out_shapes = _side_args(jobs)
  return pl.pallas_call(
      functools.partial(_out_proj_kernel, jobs=tuple(jobs)),
      grid=(m // tm,),
      in_specs=[
          pl.BlockSpec((tm, d), lambda i: (i, 0)),
          pl.BlockSpec((tm, ca), lambda i: (i, 0)),
          pl.BlockSpec((RESIDUES, tm // RESIDUES, cb), lambda i: (i // tiles, i % tiles, 0)),
          pl.BlockSpec((PERM_ROWS, PERM_ROWS), lambda i: (0, 0)),
          pl.BlockSpec((ca, d), lambda i: (0, 0), pipeline_mode=resident),
          pl.BlockSpec((cb, d), lambda i: (1, 0), pipeline_mode=resident),
      ] + side_in_specs,
      out_specs=[pl.BlockSpec((tm, d), lambda i: (i, 0))] + side_out_specs,
      out_shape=[jax.ShapeDtypeStruct((m, d), F32)] + side_out_shapes,
      compiler_params=_params("parallel"),
      name="out_proj",
  )(x, o_a, o_b.reshape(batch * RESIDUES, seq // RESIDUES, cb), _swap_matrix(), wo, wo,
    *side_arrays)


def _ple_kernel(x_ref, p_ref, g_ref, wg_ref, wp_ref, gf_ref, o_ref, *, final_norm):
  x = x_ref[...]
  h = _rmsnorm(x, g_ref[...]).astype(BF16)
  gate = jax.nn.sigmoid(_dot(h, wg_ref[...]))
  y = x + gate * _dot(p_ref[...].astype(BF16), wp_ref[...])
  o_ref[...] = _rmsnorm(y, gf_ref[...]) if final_norm else y


def _ple(x, p, g, w_gate, w_proj, g_final, final_norm):
  m, d = x.shape
  tm = PROJ_ROW_TILE
  return pl.pallas_call(
      functools.partial(_ple_kernel, final_norm=final_norm),
      grid=(m // tm,),
      in_specs=[
          pl.BlockSpec((tm, d), lambda i: (i, 0)),
          pl.BlockSpec((tm, p.shape[1]), lambda i: (i, 0)),
          pl.BlockSpec((1, d), lambda i: (0, 0)),
          pl.BlockSpec(w_gate.shape, lambda i: (0, 0)),
          pl.BlockSpec(w_proj.shape, lambda i: (0, 0)),
          pl.BlockSpec((1, d), lambda i: (0, 0)),
      ],
      out_specs=pl.BlockSpec((tm, d), lambda i: (i, 0)),
      out_shape=jax.ShapeDtypeStruct((m, d), F32),
      compiler_params=_params("parallel"),
      name="ple",
  )(x, p, g, w_gate, w_proj, g_final)


def kernel(x, p, norm_ffn1, ffn1_w_gate, ffn1_w_up, ffn1_w_down, norm_mix, w_in, b_f, w_o,
           norm_ffn2, ffn2_w_gate, ffn2_w_up, ffn2_w_down, norm_ple, w_ple_gate, w_ple_proj,
           rel_table, norm_final):
  batch, seq, d = x.shape
  depth = p.shape[0]
  m = batch * seq
  bf = lambda w: w.astype(BF16)
  row = lambda g: g.reshape(1, -1).astype(F32)

  band_bias = _band_bias(rel_table.astype(F32))
  xs = x.reshape(m, d).astype(F32)
  for i in range(depth):
    g_ffn1 = row(norm_ffn1[i])
    proj_steps = m // PROJ_ROW_TILE
    head_steps = _ffn_head_steps(ffn1_w_gate[i])
    repack = lambda steps: _w_in_repack_jobs(steps, jnp.swapaxes(w_in[i], 0, 1),
                                             3 * D_FOX, N_HEADS_FOX, 3 * D_DIL)
    a_job, _, f_job = repack(head_steps)
    _, b_job, _ = repack(proj_steps)
    head, wg1, wu1, wd1, w_a, w_f = _ffn_head(
        xs, g_ffn1, ffn1_w_gate[i], ffn1_w_up[i], ffn1_w_down[i], jobs=[a_job, f_job])
    xs = _ffn(xs, g_ffn1, wg1, wu1, wd1, head=head)

    b_f_col = b_f[i].astype(F32).reshape(N_HEADS_FOX, 1)
    g_mix = row(norm_mix[i])
    u_a, f_logit, wo, w_b = _norm_matmul(
        xs, g_mix, w_a, BF16, (D_FOX, 2 * D_FOX), w_narrow=w_f,
        jobs=_slab_cast_jobs(proj_steps, w_o[i]) + [b_job])
    u_b, wg2, wu2, wd2 = _norm_matmul(
        xs, g_mix, w_b, F32, (D_DIL, 2 * D_DIL), residue_major=(batch, seq),
        jobs=_slab_cast_jobs(proj_steps, ffn2_w_gate[i], ffn2_w_up[i])
        + _slab_cast_jobs(proj_steps, ffn2_w_down[i], scale=0.5))

    c = _fox_decay(f_logit, b_f_col, batch, seq)
    o_a, o_b = _token_mixers(u_a, c, u_b, band_bias, batch, seq)
    xs, w_gate, w_ple = _out_proj(xs, o_a, o_b, wo, batch, seq,
                                  jobs=_slab_cast_jobs(proj_steps, w_ple_gate[i], w_ple_proj[i]))

    xs = _ffn(xs, row(norm_ffn2[i]), wg2, wu2, wd2)
    last = i == depth - 1
    xs = _ple(xs, p[i].reshape(m, -1), row(norm_ple[i]), w_gate, w_ple,
              row(norm_final), final_norm=last)
  return xs.reshape(batch, seq, d).astype(x.dtype)
```

```python
import functools
import math
from typing import Any, Callable, NamedTuple

import jax
import jax.numpy as jnp
import numpy as np
from jax import lax
from jax.experimental import pallas as pl
from jax.experimental.pallas import tpu as pltpu

F32 = jnp.float32
BF16 = jnp.bfloat16

HEAD_DIM = 128
N_HEADS_FOX = 8
N_HEADS_DIL = 8
D_FOX = N_HEADS_FOX * HEAD_DIM
D_DIL = N_HEADS_DIL * HEAD_DIM
DILATED_PATTERNS = ((128, 1), (512, 4), (2048, 16))
WINDOW_KEYS = 128
N_REL_BUCKETS = 32
REL_MAX_DISTANCE = 2048
RMS_EPS = 1e-6
NEG_INF = -1e30
SCALE = HEAD_DIM ** -0.5
LOG2_E = math.log2(math.e)

V7X_LANES = 128
BF16_TILE_ROWS = 16
V7X_VMEM_LIMIT_BYTES = 56 * 1024 * 1024

RESIDUES = max(d for _, d in DILATED_PATTERNS)
PERM_ROWS = RESIDUES * RESIDUES

FFN_ROW_TILE = 1024
FFN_SUB_ROWS = 512
FFN_FF_TILE = 512
FFN_HEAD_FF_TILE = 256
PROJ_ROW_TILE = 512
OUT_PROJ_ROW_TILE = 1024
FOX_Q_TILE = 256
FOX_LOOKAHEAD = 2
DIL_BLOCK_GROUP = 12


def _params(*semantics):
  return pltpu.CompilerParams(dimension_semantics=semantics,
                              vmem_limit_bytes=V7X_VMEM_LIMIT_BYTES)


def _rmsnorm(x, g):
  ms = jnp.mean(x * x, axis=-1, keepdims=True)
  return x * lax.rsqrt(ms + RMS_EPS) * g


def _dot(a, b):
  return jnp.dot(a, b, preferred_element_type=F32)


def _dot_nt(a, b):
  return lax.dot_general(a, b, (((1,), (1,)), ((), ())), preferred_element_type=F32)


def _weighted_values(e, v):
  both = _dot(e.astype(BF16), jnp.concatenate([v, jnp.ones_like(v)], axis=1))
  return both[:, :HEAD_DIM], both[:, HEAD_DIM:]


class _SideJob(NamedTuple):
  arrays: tuple
  in_specs: tuple
  out_shape: Any
  out_spec: Any
  fn: Callable


def _cast_job(a, block, index, scale=1.0):
  spec = pl.BlockSpec(block, index)
  return _SideJob((a,), (spec,), jax.ShapeDtypeStruct(a.shape, BF16), spec,
                  lambda r: r[...] * scale)


def _run_side_jobs(jobs, in_refs, out_refs):
  in_refs = list(in_refs)
  for job, out_ref in zip(jobs, out_refs):
    refs = [in_refs.pop(0) for _ in job.arrays]
    out_ref[...] = job.fn(*refs).astype(out_ref.dtype)


def _side_args(jobs):
  arrays = [a for job in jobs for a in job.arrays]
  in_specs = [s for job in jobs for s in job.in_specs]
  return arrays, in_specs, [job.out_spec for job in jobs], [job.out_shape for job in jobs]


def _ffn_head_kernel(x_ref, g_ref, wg_ref, wu_ref, wd_ref, *rest, jobs):
  n_in = sum(len(job.arrays) for job in jobs)
  side_in, (o_ref, og_ref, ou_ref, od_ref) = rest[:n_in], rest[n_in:n_in + 4]
  side_out, h_ref = rest[n_in + 4:-1], rest[-1]
  _run_side_jobs(jobs, side_in, side_out)

  @pl.when(pl.program_id(0) == 0)
  def _():
    x = x_ref[...]
    h_ref[...] = _rmsnorm(x, g_ref[...]).astype(BF16)
    o_ref[...] = x

  wg = wg_ref[...].astype(BF16)
  wu = wu_ref[...].astype(BF16)
  wd = (wd_ref[...] * 0.5).astype(BF16)
  og_ref[...] = wg
  ou_ref[...] = wu
  od_ref[...] = wd
  for r in range(h_ref.shape[0] // FFN_SUB_ROWS):
    rows = slice(r * FFN_SUB_ROWS, (r + 1) * FFN_SUB_ROWS)
    h = h_ref[rows, :]
    gate = _dot(h, wg)
    up = _dot(h, wu)
    act = (gate * jax.nn.sigmoid(gate)) * up
    o_ref[rows, :] += _dot(act.astype(BF16), wd)


def _ffn_head_steps(w_gate):
  return w_gate.shape[1] // FFN_HEAD_FF_TILE


def _ffn_head(x, g, w_gate, w_up, w_down, jobs=()):
  d = x.shape[1]
  tm, tf = FFN_ROW_TILE, FFN_HEAD_FF_TILE
  side_arrays, side_in_specs, side_out_specs, side_out_shapes = _side_args(jobs)
  return pl.pallas_call(
      functools.partial(_ffn_head_kernel, jobs=tuple(jobs)),
      grid=(_ffn_head_steps(w_gate),),
      in_specs=[
          pl.BlockSpec((tm, d), lambda j: (0, 0), pipeline_mode=pl.Buffered(1)),
          pl.BlockSpec((1, d), lambda j: (0, 0)),
          pl.BlockSpec((d, tf), lambda j: (0, j)),
          pl.BlockSpec((d, tf), lambda j: (0, j)),
          pl.BlockSpec((tf, d), lambda j: (j, 0)),
      ] + side_in_specs,
      out_specs=[
          pl.BlockSpec((tm, d), lambda j: (0, 0)),
          pl.BlockSpec((d, tf), lambda j: (0, j)),
          pl.BlockSpec((d, tf), lambda j: (0, j)),
          pl.BlockSpec((tf, d), lambda j: (j, 0)),
      ] + side_out_specs,
      out_shape=[
          jax.ShapeDtypeStruct((tm, d), F32),
          jax.ShapeDtypeStruct(w_gate.shape, BF16),
          jax.ShapeDtypeStruct(w_up.shape, BF16),
          jax.ShapeDtypeStruct(w_down.shape, BF16),
      ] + side_out_shapes,
      scratch_shapes=[pltpu.VMEM((tm, d), BF16)],
      compiler_params=_params("arbitrary"),
      name="ffn_head",
  )(x, g, w_gate, w_up, w_down, *side_arrays)


def _ffn_kernel(x_ref, g_ref, wg_ref, wu_ref, wd_ref, *rest, has_head):
  if has_head:
    head_hbm, o_ref, h_ref, copy_sem = rest
  else:
    o_ref, h_ref = rest

  def step(first):
    for r in range(h_ref.shape[0] // FFN_SUB_ROWS):
      rows = slice(r * FFN_SUB_ROWS, (r + 1) * FFN_SUB_ROWS)
      if first:
        base = x_ref[rows, :]
        h = _rmsnorm(base, g_ref[...]).astype(BF16)
        h_ref[rows, :] = h
      else:
        base = o_ref[rows, :]
        h = h_ref[rows, :]
      gate = _dot(h, wg_ref[...])
      up = _dot(h, wu_ref[...])
      act = (gate * jax.nn.sigmoid(gate)) * up
      o_ref[rows, :] = base + _dot(act.astype(BF16), wd_ref[...])

  def compute_tile():
    lax.cond(pl.program_id(1) == 0, lambda: step(True), lambda: step(False))

  def copy_head_tile():
    @pl.when(pl.program_id(1) == 0)
    def _():
      copy = pltpu.make_async_copy(head_hbm, o_ref, copy_sem)
      copy.start()
      copy.wait()

  if has_head:
    lax.cond(pl.program_id(0) == 0, copy_head_tile, compute_tile)
  else:
    compute_tile()


def _ffn(x, g, wg, wu, wd_half, head=None):
  m, d = x.shape
  dff = wg.shape[1]
  tm, tf = FFN_ROW_TILE, FFN_FF_TILE
  has_head = head is not None
  chunk = (lambda i, j: jnp.where(i == 0, 0, j)) if has_head else (lambda i, j: j)
  return pl.pallas_call(
      functools.partial(_ffn_kernel, has_head=has_head),
      grid=(m // tm, dff // tf),
      in_specs=[
          pl.BlockSpec((tm, d), lambda i, j: (i, 0)),
          pl.BlockSpec((1, d), lambda i, j: (0, 0)),
          pl.BlockSpec((d, tf), lambda i, j: (0, chunk(i, j))),
          pl.BlockSpec((d, tf), lambda i, j: (0, chunk(i, j))),
          pl.BlockSpec((tf, d), lambda i, j: (chunk(i, j), 0)),
      ] + ([pl.BlockSpec(memory_space=pl.ANY)] if has_head else []),
      out_specs=pl.BlockSpec((tm, d), lambda i, j: (i, 0)),
      out_shape=jax.ShapeDtypeStruct((m, d), F32),
      scratch_shapes=([pltpu.VMEM((tm, d), BF16)]
                      + ([pltpu.SemaphoreType.DMA(())] if has_head else [])),
      compiler_params=_params("arbitrary" if has_head else "parallel", "arbitrary"),
      name="ffn",
  )(x, g, wg, wu, wd_half, *([head] if has_head else []))


def _slab_cast_jobs(steps, *weights, scale=1.0):
  jobs = []
  for w in weights:
    rows = max(BF16_TILE_ROWS, w.shape[0] // steps)
    assert w.shape[0] % rows == 0 and rows % BF16_TILE_ROWS == 0
    last = w.shape[0] // rows - 1
    jobs.append(_cast_job(w, (rows, w.shape[1]), lambda t, last=last: (jnp.minimum(t, last), 0),
                          scale=scale))
  return jobs


def _w_in_repack_jobs(steps, w_t, n_a, n_f, n_b):
  d = w_t.shape[1]
  lanes = V7X_LANES
  slabs = d // lanes
  assert d % lanes == 0 and slabs <= steps and n_f % 8 == 0 and n_f <= lanes
  assert n_a % n_f == 0 and n_b <= n_a + n_f
  slab = lambda t: jnp.minimum(t, slabs - 1)

  def gate_rows(ref):
    return jnp.concatenate([ref[...], jnp.zeros((lanes - n_f, lanes), F32)], axis=0).T

  out = lambda n: jax.ShapeDtypeStruct((d, n), BF16)
  out_spec = lambda n: pl.BlockSpec((lanes, n), lambda t: (slab(t), 0))
  return [
      _SideJob((w_t,), (pl.BlockSpec((n_a, lanes), lambda t: (0, slab(t))),),
               out(n_a), out_spec(n_a), lambda ref: ref[...].T),
      _SideJob((w_t,), (pl.BlockSpec((n_a + n_f, lanes), lambda t: (1, slab(t))),),
               out(n_b), out_spec(n_b), lambda ref: ref[:n_b, :].T),
      _SideJob((w_t,), (pl.BlockSpec((n_f, lanes), lambda t: (n_a // n_f, slab(t))),),
               out(lanes), out_spec(lanes), gate_rows),
  ]


def _swap_matrix():
  i = np.arange(PERM_ROWS)
  src = (i % RESIDUES) * RESIDUES + i // RESIDUES
  return jnp.asarray(np.eye(PERM_ROWS, dtype=np.float32)[src], BF16)


def _norm_matmul_kernel(x_ref, g_ref, w_ref, *rest, permute, narrow, jobs, key_cols):
  rest = list(rest)
  swap_ref = rest.pop(0) if permute else None
  wn_ref = rest.pop(0) if narrow else None
  side_in = [rest.pop(0) for job in jobs for _ in job.arrays]
  o_ref = rest.pop(0)
  on_ref = rest.pop(0) if narrow else None
  side_out = [rest.pop(0) for _ in jobs]
  assert not rest, "unexpected extra refs"
  tm = x_ref.shape[0]
  _run_side_jobs(jobs, side_in, side_out)

  h = _rmsnorm(x_ref[...], g_ref[...]).astype(BF16)
  if permute:
    h = jnp.concatenate(
        [_dot(swap_ref[...], h[a * PERM_ROWS:(a + 1) * PERM_ROWS, :]).astype(BF16)
         for a in range(tm // PERM_ROWS)], axis=0)
  if narrow:
    on_ref[...] = _dot(h, wn_ref[...])
  acc = _dot(h, w_ref[...])
  lo, hi = key_cols
  res = jnp.concatenate([acc[:, :lo], acc[:, lo:hi] * (SCALE * LOG2_E), acc[:, hi:]],
                        axis=1).astype(o_ref.dtype)
  if permute:
    per = PERM_ROWS // RESIDUES
    for a in range(tm // PERM_ROWS):
      for r in range(RESIDUES):
        start = a * PERM_ROWS + r * per
        o_ref[r, a * per:(a + 1) * per, :] = res[start:start + per, :]
  else:
    o_ref[...] = res


def _norm_matmul(x, g, w, out_dtype, key_cols, residue_major=None, w_narrow=None, jobs=()):
  m, d = x.shape
  n = w.shape[1]
  tm = PROJ_ROW_TILE
  narrow = w_narrow is not None
  permute = residue_major is not None
  assert not (narrow and permute)
  in_specs = [
      pl.BlockSpec((tm, d), lambda i: (i, 0)),
      pl.BlockSpec((1, d), lambda i: (0, 0)),
      pl.BlockSpec((d, n), lambda i: (0, 0)),
  ]
  args = [x, g, w]
  if permute:
    batch, seq = residue_major
    tiles = seq // tm
    in_specs.append(pl.BlockSpec((PERM_ROWS, PERM_ROWS), lambda i: (0, 0)))
    args.append(_swap_matrix())
    out_specs = [pl.BlockSpec((RESIDUES, tm // RESIDUES, n), lambda i: (i // tiles, i % tiles, 0))]
    out_shape = [jax.ShapeDtypeStruct((batch * RESIDUES, seq // RESIDUES, n), out_dtype)]
  else:
    out_specs = [pl.BlockSpec((tm, n), lambda i: (i, 0))]
    out_shape = [jax.ShapeDtypeStruct((m, n), out_dtype)]
  if narrow:
    in_specs.append(pl.BlockSpec(w_narrow.shape, lambda i: (0, 0)))
    args.append(w_narrow)
    out_specs.append(pl.BlockSpec((tm, w_narrow.shape[1]), lambda i: (i, 0)))
    out_shape.append(jax.ShapeDtypeStruct((m, w_narrow.shape[1]), F32))
  side_arrays, side_in_specs, side_out_specs, side_out_shapes = _side_args(jobs)
  outs = pl.pallas_call(
      functools.partial(_norm_matmul_kernel, permute=permute, narrow=narrow, jobs=tuple(jobs),
                        key_cols=key_cols),
      grid=(m // tm,),
      in_specs=in_specs + side_in_specs,
      out_specs=out_specs + side_out_specs,
      out_shape=out_shape + side_out_shapes,
      compiler_params=_params("parallel"),
      name="norm_matmul",
  )(*args, *side_arrays)
  outs = list(outs)
  if permute:
    outs[0] = outs[0].reshape(m, n)
  return outs


def _cumsum_kernel(fl_ref, bf_ref, c_ref):
  s = fl_ref.shape[0]
  z = fl_ref[...].T[0:N_HEADS_FOX, :] + bf_ref[...]
  lt = jnp.minimum(z, 0.0) - jnp.log1p(jnp.exp(-jnp.abs(z)))
  row = lax.broadcasted_iota(jnp.int32, (V7X_LANES, V7X_LANES), 0)
  col = lax.broadcasted_iota(jnp.int32, (V7X_LANES, V7X_LANES), 1)
  upper = (row <= col).astype(F32)
  chunks = [jnp.dot(lt[:, j * V7X_LANES:(j + 1) * V7X_LANES], upper, preferred_element_type=F32,
                    precision=lax.Precision.HIGHEST) for j in range(s // V7X_LANES)]
  offset = jnp.zeros((N_HEADS_FOX, 1), F32)
  for j, cs in enumerate(chunks):
    c_ref[:, j * V7X_LANES:(j + 1) * V7X_LANES] = cs + offset
    offset = offset + cs[:, V7X_LANES - 1:V7X_LANES]


def _fox_decay(f_logit, b_f_col, batch, seq):
  return pl.pallas_call(
      _cumsum_kernel,
      grid=(batch,),
      in_specs=[
          pl.BlockSpec((seq, V7X_LANES), lambda b: (b, 0)),
          pl.BlockSpec((N_HEADS_FOX, 1), lambda b: (0, 0)),
      ],
      out_specs=pl.BlockSpec((None, N_HEADS_FOX, seq), lambda b: (b, 0, 0)),
      out_shape=jax.ShapeDtypeStruct((batch, N_HEADS_FOX, seq), F32),
      compiler_params=_params("parallel"),
      name="fox_decay",
  )(f_logit, b_f_col)


def _fox_phases(q_ref, k_ref, v_ref, c_ref, o_ref):
  seq = q_ref.shape[0]
  tq = FOX_Q_TILE
  h = pl.program_id(1)
  crow = c_ref[pl.ds(h, 1), :] * LOG2_E
  row = lax.broadcasted_iota(jnp.int32, (tq, tq), 0)
  col = lax.broadcasted_iota(jnp.int32, (tq, tq), 1)
  diag_mask = jnp.where(col > row, NEG_INF, 0.0).astype(F32)

  def scores(i):
    t0, t1 = i * tq, (i + 1) * tq
    q = q_ref[t0:t1, :]
    bias = crow[:, t1 - 1:t1] - crow[:, 0:t1]
    s_diag = _dot_nt(q, k_ref[t0:t1, :]) + bias[:, t0:t1] + diag_mask
    s_off = _dot_nt(q, k_ref[0:t0, :]) + bias[:, 0:t0] if i > 0 else None
    return s_diag, s_off

  def finish(i, s_diag, s_off):
    t0, t1 = i * tq, (i + 1) * tq
    m = jnp.max(s_diag, axis=-1, keepdims=True)
    if i > 0:
      m = jnp.maximum(m, jnp.max(s_off, axis=-1, keepdims=True))
    o, l = _weighted_values(jnp.exp2(s_diag - m), v_ref[t0:t1, :])
    if i > 0:
      o_off, l_off = _weighted_values(jnp.exp2(s_off - m), v_ref[0:t0, :])
      o, l = o + o_off, l + l_off
    o_ref[t0:t1, :] = (o / l).astype(o_ref.dtype)

  n_tiles = seq // tq
  pending = [scores(i) for i in range(min(FOX_LOOKAHEAD, n_tiles))]
  yield
  for i in range(n_tiles):
    if i + FOX_LOOKAHEAD < n_tiles:
      pending.append(scores(i + FOX_LOOKAHEAD))
      yield
    finish(i, *pending.pop(0))
    yield


def _t5_bucket_np(dist):
  max_exact = N_REL_BUCKETS // 2
  d = np.maximum(dist, 1).astype(np.float32)
  large = max_exact + (np.log(d / np.float32(max_exact))
                       / np.float32(math.log(REL_MAX_DISTANCE / max_exact))
                       * np.float32(N_REL_BUCKETS - max_exact)).astype(np.int32)
  large = np.minimum(large, N_REL_BUCKETS - 1)
  return np.where(dist < max_exact, dist, large).astype(np.int32)


def _block_positions(dilation):
  n = WINDOW_KEYS
  m = RESIDUES // dilation
  rows = n // m
  j = np.arange(m)[:, None]
  qpos = (n + m * np.arange(rows)[None, :] + j).reshape(-1)
  kpos = (m * np.arange(2 * rows)[None, :] + j).reshape(-1)
  return qpos, kpos


def _band_buckets():
  n = WINDOW_KEYS
  tiles = []
  for _, dilation in DILATED_PATTERNS:
    qpos, kpos = _block_positions(dilation)
    rel = qpos[:, None] - np.concatenate([kpos, qpos])[None, :]
    valid = (rel >= 0) & (rel <= n)
    bucket = _t5_bucket_np(np.maximum(rel, 0) * dilation)
    tiles.append(np.where(valid, bucket, -1))
  return np.stack(tiles).astype(np.int32)


def _bias_kernel(tab_ref, bkt_ref, o_ref):
  table = [[tab_ref[b, h] * LOG2_E for h in range(N_HEADS_DIL)] for b in range(N_REL_BUCKETS)]
  for r0 in range(0, bkt_ref.shape[0], BF16_TILE_ROWS):
    rows = slice(r0, r0 + BF16_TILE_ROWS)
    bkt = bkt_ref[rows, :]
    accs = [jnp.full(bkt.shape, NEG_INF, F32)] * N_HEADS_DIL
    for b in range(N_REL_BUCKETS):
      hit = bkt == b
      accs = [jnp.where(hit, table[b][h], acc) for h, acc in enumerate(accs)]
    for h, acc in enumerate(accs):
      o_ref[h, rows, :] = acc


def _band_bias(rel_table):
  buckets = jnp.asarray(_band_buckets())
  p, n, n2 = buckets.shape
  return pl.pallas_call(
      _bias_kernel,
      grid=(p,),
      in_specs=[
          pl.BlockSpec(memory_space=pltpu.SMEM),
          pl.BlockSpec((None, n, n2), lambda i: (i, 0, 0)),
      ],
      out_specs=pl.BlockSpec((None, N_HEADS_DIL, n, n2), lambda i: (i, 0, 0, 0)),
      out_shape=jax.ShapeDtypeStruct((p, N_HEADS_DIL, n, n2), F32),
      compiler_params=_params("parallel"),
      name="band_bias",
  )(rel_table, buckets)


def _dil_phases(q_ref, k_ref, v_ref, bm_ref, o_ref, *scratch):
  seq = q_ref.shape[0]
  n = WINDOW_KEYS
  seg = seq // RESIDUES
  last = len(DILATED_PATTERNS) - 1
  assert DILATED_PATTERNS[last][1] == RESIDUES and n == seg
  accs, lses = scratch[:last], scratch[last:]

  def gather(ref, starts, size):
    return jnp.concatenate([ref[st:st + size, :] for st in starts], axis=0).astype(BF16)

  def block_rows(segments, nb):
    rows = n // len(segments)
    q_starts = [s * seg + nb * rows for s in segments]
    if nb == 0:
      return rows, q_starts, q_starts, rows
    return rows, q_starts, [st - rows for st in q_starts], 2 * rows

  def scores(p, segments, nb):
    rows, q_starts, k_starts, k_rows = block_rows(segments, nb)
    bm = bm_ref[p, :, 2 * n:3 * n] if nb == 0 else bm_ref[p, :, 0:2 * n]
    return _dot_nt(gather(q_ref, q_starts, rows), gather(k_ref, k_starts, k_rows)) + bm

  def finish(p, segments, nb, e, m):
    rows, q_starts, k_starts, k_rows = block_rows(segments, nb)
    o, l = _weighted_values(e, gather(v_ref, k_starts, k_rows))
    o = o / l
    lse = m + jnp.log2(l)
    if p < last:
      for j, st in enumerate(q_starts):
        accs[p][st:st + rows, :] = o[j * rows:(j + 1) * rows, :]
        lses[p][st:st + rows, :] = lse[j * rows:(j + 1) * rows, :]
      return
    seg_rows = slice(q_starts[0], q_starts[0] + n)
    all_lse = [ref[seg_rows, :] for ref in lses] + [lse]
    all_out = [ref[seg_rows, :] for ref in accs] + [o]
    top = functools.reduce(jnp.maximum, all_lse)
    weights = [jnp.exp2(x - top) for x in all_lse]
    add = lambda a, b: a + b
    mixed = (functools.reduce(add, [w * a for w, a in zip(weights, all_out)])
             / functools.reduce(add, weights))
    o_ref[seg_rows, :] = mixed.astype(o_ref.dtype)

  blocks = [(p, list(range(r, RESIDUES, d)), nb)
            for p, (_, d) in enumerate(DILATED_PATTERNS)
            for r in range(d) for nb in range(seq // (n * d))]
  for g in range(0, len(blocks), DIL_BLOCK_GROUP):
    group = blocks[g:g + DIL_BLOCK_GROUP]
    ss = [scores(*blk) for blk in group]
    yield
    ms = [jnp.max(s, axis=-1, keepdims=True) for s in ss]
    es = [jnp.exp2(s - m) for s, m in zip(ss, ms)]
    yield
    for blk, e, m in zip(group, es, ms):
      finish(*blk, e, m)
    yield


def _mixer_kernel(qa_ref, ka_ref, va_ref, c_ref, qb_ref, kb_ref, vb_ref, bm_ref,
                  oa_ref, ob_ref, *scratch):
  streams = [_dil_phases(qb_ref, kb_ref, vb_ref, bm_ref, ob_ref, *scratch),
             _fox_phases(qa_ref, ka_ref, va_ref, c_ref, oa_ref)]
  while streams:
    for stream in list(streams):
      try:
        next(stream)
      except StopIteration:
        streams.remove(stream)


def _token_mixers(u_a, c, u_b, band_bias, batch, seq):
  assert seq // RESIDUES == WINDOW_KEYS and N_HEADS_FOX == N_HEADS_DIL
  blk = lambda off: pl.BlockSpec((seq, HEAD_DIM), lambda b, h: (b, off + h))
  n_pat = len(DILATED_PATTERNS)
  scratch = [pltpu.VMEM((seq, HEAD_DIM), F32) for _ in range(2 * (n_pat - 1))]
  out_blk = pl.BlockSpec((seq, HEAD_DIM), lambda b, h: (b, h))
  return pl.pallas_call(
      _mixer_kernel,
      grid=(batch, N_HEADS_FOX),
      in_specs=[
          blk(0), blk(N_HEADS_FOX), blk(2 * N_HEADS_FOX),
          pl.BlockSpec((None, N_HEADS_FOX, seq), lambda b, h: (b, 0, 0)),
          blk(0), blk(N_HEADS_DIL), blk(2 * N_HEADS_DIL),
          pl.BlockSpec((n_pat, None, WINDOW_KEYS, 3 * WINDOW_KEYS), lambda b, h: (0, h, 0, 0)),
      ],
      out_specs=[out_blk, out_blk],
      out_shape=[jax.ShapeDtypeStruct((batch * seq, D_FOX), BF16),
                 jax.ShapeDtypeStruct((batch * seq, D_DIL), BF16)],
      scratch_shapes=scratch,
      compiler_params=_params("parallel", "arbitrary"),
      name="token_mixers",
  )(u_a, u_a, u_a, c, u_b, u_b, u_b, band_bias)


def _out_proj_kernel(x_ref, a_ref, b_ref, swap_ref, wa_ref, wb_ref, *rest, jobs):
  n_in = sum(len(job.arrays) for job in jobs)
  side_in, o_ref, side_out = rest[:n_in], rest[n_in], rest[n_in + 1:]
  _run_side_jobs(jobs, side_in, side_out)
  per = PERM_ROWS // RESIDUES
  for a in range(x_ref.shape[0] // PERM_ROWS):
    rows = slice(a * PERM_ROWS, (a + 1) * PERM_ROWS)
    slab = jnp.concatenate([b_ref[r, a * per:(a + 1) * per, :] for r in range(RESIDUES)], axis=0)
    o_b = _dot(swap_ref[...], slab).astype(BF16)
    o_ref[rows, :] = x_ref[rows, :] + _dot(a_ref[rows, :], wa_ref[...]) + _dot(o_b, wb_ref[...])


def _out_proj(x, o_a, o_b, wo, batch, seq, jobs=()):
  m, d = x.shape
  tm = OUT_PROJ_ROW_TILE
  tiles = seq // tm
  ca, cb = o_a.shape[1], o_b.shape[1]
  assert ca == cb and wo.shape[0] == ca + cb
  resident = pl.Buffered(1)
  side_arrays, side_in_specs, side_out_specs, side_out_shapes = _side_args(jobs)
  return pl.pallas_call(
      functools.partial(_out_proj_kernel, jobs=tuple(jobs)),
      grid=(m // tm,),
      in_specs=[
          pl.BlockSpec((tm, d), lambda i: (i, 0)),
          pl.BlockSpec((tm, ca), lambda i: (i, 0)),
          pl.BlockSpec((RESIDUES, tm // RESIDUES, cb), lambda i: (i // tiles, i % tiles, 0)),
          pl.BlockSpec((PERM_ROWS, PERM_ROWS), lambda i: (0, 0)),
          pl.BlockSpec((ca, d), lambda i: (0, 0), pipeline_mode=resident),
          pl.BlockSpec((cb, d), lambda i: (1, 0), pipeline_mode=resident),
      ] + side_in_specs,
      out_specs=[pl.BlockSpec((tm, d), lambda i: (i, 0))] + side_out_specs,
      out_shape=[jax.ShapeDtypeStruct((m, d), F32)] + side_out_shapes,
      compiler_params=_params("parallel"),
      name="out_proj",
  )(x, o_a, o_b.reshape(batch * RESIDUES, seq // RESIDUES, cb), _swap_matrix(), wo, wo,
    *side_arrays)


def _ple_kernel(x_ref, p_ref, g_ref, wg_ref, wp_ref, gf_ref, o_ref, *, final_norm):
  x = x_ref[...]
  h = _rmsnorm(x, g_ref[...]).astype(BF16)
  gate = jax.nn.sigmoid(_dot(h, wg_ref[...]))
  y = x + gate * _dot(p_ref[...].astype(BF16), wp_ref[...])
  o_ref[...] = _rmsnorm(y, gf_ref[...]) if final_norm else y


def _ple(x, p, g, w_gate, w_proj, g_final, final_norm):
  m, d = x.shape
  tm = PROJ_ROW_TILE
  return pl.pallas_call(
      functools.partial(_ple_kernel, final_norm=final_norm),
      grid=(m // tm,),
      in_specs=[
          pl.BlockSpec((tm, d), lambda i: (i, 0)),
          pl.BlockSpec((tm, p.shape[1]), lambda i: (i, 0)),
          pl.BlockSpec((1, d), lambda i: (0, 0)),
          pl.BlockSpec(w_gate.shape, lambda i: (0, 0)),
          pl.BlockSpec(w_proj.shape, lambda i: (0, 0)),
          pl.BlockSpec((1, d), lambda i: (0, 0)),
      ],
      out_specs=pl.BlockSpec((tm, d), lambda i: (i, 0)),
      out_shape=jax.ShapeDtypeStruct((m, d), F32),
      compiler_params=_params("parallel"),
      name="ple",
  )(x, p, g, w_gate, w_proj, g_final)


def kernel(x, p, norm_ffn1, ffn1_w_gate, ffn1_w_up, ffn1_w_down, norm_mix, w_in, b_f, w_o,
           norm_ffn2, ffn2_w_gate, ffn2_w_up, ffn2_w_down, norm_ple, w_ple_gate, w_ple_proj,
           rel_table, norm_final):
  batch, seq, d = x.shape
  depth = p.shape[0]
  m = batch * seq
  bf = lambda w: w.astype(BF16)
  row = lambda g: g.reshape(1, -1).astype(F32)

  band_bias = _band_bias(rel_table.astype(F32))
  xs = x.reshape(m, d).astype(F32)
  for i in range(depth):
    g_ffn1 = row(norm_ffn1[i])
    proj_steps = m // PROJ_ROW_TILE
    head_steps = _ffn_head_steps(ffn1_w_gate[i])
    repack = lambda steps: _w_in_repack_jobs(steps, jnp.swapaxes(w_in[i], 0, 1),
                                             3 * D_FOX, N_HEADS_FOX, 3 * D_DIL)
    a_job, _, f_job = repack(head_steps)
    _, b_job, _ = repack(proj_steps)
    head, wg1, wu1, wd1, w_a, w_f = _ffn_head(
        xs, g_ffn1, ffn1_w_gate[i], ffn1_w_up[i], ffn1_w_down[i], jobs=[a_job, f_job])
    xs = _ffn(xs, g_ffn1, wg1, wu1, wd1, head=head)

    b_f_col = b_f[i].astype(F32).reshape(N_HEADS_FOX, 1)
    g_mix = row(norm_mix[i])
    u_a, f_logit, wo, w_b = _norm_matmul(
        xs, g_mix, w_a, BF16, (D_FOX, 2 * D_FOX), w_narrow=w_f,
        jobs=_slab_cast_jobs(proj_steps, w_o[i]) + [b_job])
    u_b, wg2, wu2, wd2 = _norm_matmul(
        xs, g_mix, w_b, F32, (D_DIL, 2 * D_DIL), residue_major=(batch, seq),
        jobs=_slab_cast_jobs(proj_steps, ffn2_w_gate[i], ffn2_w_up[i])
        + _slab_cast_jobs(proj_steps, ffn2_w_down[i], scale=0.5))

    c = _fox_decay(f_logit, b_f_col, batch, seq)
    o_a, o_b = _token_mixers(u_a, c, u_b, band_bias, batch, seq)
    xs, w_gate, w_ple = _out_proj(
        xs, o_a, o_b, wo, batch, seq,
        jobs=_slab_cast_jobs(m // OUT_PROJ_ROW_TILE, w_ple_gate[i], w_ple_proj[i]))

    xs = _ffn(xs, row(norm_ffn2[i]), wg2, wu2, wd2)
    last = i == depth - 1
    xs = _ple(xs, p[i].reshape(m, -1), row(norm_ple[i]), w_gate, w_ple,
              row(norm_final), final_norm=last)
  return xs.reshape(batch, seq, d).astype(x.dtype)
```

```python
import functools
import math
from typing import Any, Callable, NamedTuple

import jax
import jax.numpy as jnp
import numpy as np
from jax import lax
from jax.experimental import pallas as pl
from jax.experimental.pallas import tpu as pltpu

F32 = jnp.float32
BF16 = jnp.bfloat16

HEAD_DIM = 128
N_HEADS_FOX = 8
N_HEADS_DIL = 8
D_FOX = N_HEADS_FOX * HEAD_DIM
D_DIL = N_HEADS_DIL * HEAD_DIM
DILATED_PATTERNS = ((128, 1), (512, 4), (2048, 16))
WINDOW_KEYS = 128
N_REL_BUCKETS = 32
REL_MAX_DISTANCE = 2048
RMS_EPS = 1e-6
NEG_INF = -1e30
SCALE = HEAD_DIM ** -0.5
LOG2_E = math.log2(math.e)

V7X_LANES = 128
BF16_TILE_ROWS = 16
V7X_VMEM_LIMIT_BYTES = 56 * 1024 * 1024

RESIDUES = max(d for _, d in DILATED_PATTERNS)
PERM_ROWS = RESIDUES * RESIDUES

FFN_ROW_TILE = 1024
FFN_SUB_ROWS = 512
FFN_FF_TILE = 512
FFN_HEAD_FF_TILE = 256
PROJ_ROW_TILE = 512
FOX_Q_TILE = 256
FOX_LOOKAHEAD = 2
DIL_BLOCK_GROUP = 12


def _params(*semantics):
  return pltpu.CompilerParams(dimension_semantics=semantics,
                              vmem_limit_bytes=V7X_VMEM_LIMIT_BYTES)


def _rmsnorm(x, g):
  ms = jnp.mean(x * x, axis=-1, keepdims=True)
  return x * lax.rsqrt(ms + RMS_EPS) * g


def _dot(a, b):
  return jnp.dot(a, b, preferred_element_type=F32)


def _dot_nt(a, b):
  return lax.dot_general(a, b, (((1,), (1,)), ((), ())), preferred_element_type=F32)


def _weighted_values(e, v):
  both = _dot(e.astype(BF16), jnp.concatenate([v, jnp.ones_like(v)], axis=1))
  return both[:, :HEAD_DIM], both[:, HEAD_DIM:]


class _SideJob(NamedTuple):
  arrays: tuple
  in_specs: tuple
  out_shape: Any
  out_spec: Any
  fn: Callable


def _cast_job(a, block, index, scale=1.0):
  spec = pl.BlockSpec(block, index)
  return _SideJob((a,), (spec,), jax.ShapeDtypeStruct(a.shape, BF16), spec,
                  lambda r: r[...] * scale)


def _run_side_jobs(jobs, in_refs, out_refs):
  in_refs = list(in_refs)
  for job, out_ref in zip(jobs, out_refs):
    refs = [in_refs.pop(0) for _ in job.arrays]
    out_ref[...] = job.fn(*refs).astype(out_ref.dtype)


def _row_semantics(jobs):
  return "arbitrary" if jobs else "parallel"


def _side_args(jobs):
  arrays = [a for job in jobs for a in job.arrays]
  in_specs = [s for job in jobs for s in job.in_specs]
  return arrays, in_specs, [job.out_spec for job in jobs], [job.out_shape for job in jobs]


def _ffn_head_kernel(x_ref, g_ref, wg_ref, wu_ref, wd_ref, *rest, jobs):
  n_in = sum(len(job.arrays) for job in jobs)
  side_in, (o_ref, og_ref, ou_ref, od_ref) = rest[:n_in], rest[n_in:n_in + 4]
  side_out, h_ref = rest[n_in + 4:-1], rest[-1]
  _run_side_jobs(jobs, side_in, side_out)

  @pl.when(pl.program_id(0) == 0)
  def _():
    x = x_ref[...]
    h_ref[...] = _rmsnorm(x, g_ref[...]).astype(BF16)
    o_ref[...] = x

  wg = wg_ref[...].astype(BF16)
  wu = wu_ref[...].astype(BF16)
  wd = (wd_ref[...] * 0.5).astype(BF16)
  og_ref[...] = wg
  ou_ref[...] = wu
  od_ref[...] = wd
  for r in range(h_ref.shape[0] // FFN_SUB_ROWS):
    rows = slice(r * FFN_SUB_ROWS, (r + 1) * FFN_SUB_ROWS)
    h = h_ref[rows, :]
    gate = _dot(h, wg)
    up = _dot(h, wu)
    act = (gate * jax.nn.sigmoid(gate)) * up
    o_ref[rows, :] += _dot(act.astype(BF16), wd)


def _ffn_head_steps(w_gate):
  return w_gate.shape[1] // FFN_HEAD_FF_TILE


def _ffn_head(x, g, w_gate, w_up, w_down, jobs=()):
  d = x.shape[1]
  tm, tf = FFN_ROW_TILE, FFN_HEAD_FF_TILE
  side_arrays, side_in_specs, side_out_specs, side_out_shapes = _side_args(jobs)
  return pl.pallas_call(
      functools.partial(_ffn_head_kernel, jobs=tuple(jobs)),
      grid=(_ffn_head_steps(w_gate),),
      in_specs=[
          pl.BlockSpec((tm, d), lambda j: (0, 0), pipeline_mode=pl.Buffered(1)),
          pl.BlockSpec((1, d), lambda j: (0, 0)),
          pl.BlockSpec((d, tf), lambda j: (0, j)),
          pl.BlockSpec((d, tf), lambda j: (0, j)),
          pl.BlockSpec((tf, d), lambda j: (j, 0)),
      ] + side_in_specs,
      out_specs=[
          pl.BlockSpec((tm, d), lambda j: (0, 0)),
          pl.BlockSpec((d, tf), lambda j: (0, j)),
          pl.BlockSpec((d, tf), lambda j: (0, j)),
          pl.BlockSpec((tf, d), lambda j: (j, 0)),
      ] + side_out_specs,
      out_shape=[
          jax.ShapeDtypeStruct((tm, d), F32),
          jax.ShapeDtypeStruct(w_gate.shape, BF16),
          jax.ShapeDtypeStruct(w_up.shape, BF16),
          jax.ShapeDtypeStruct(w_down.shape, BF16),
      ] + side_out_shapes,
      scratch_shapes=[pltpu.VMEM((tm, d), BF16)],
      compiler_params=_params("arbitrary"),
      name="ffn_head",
  )(x, g, w_gate, w_up, w_down, *side_arrays)


def _ffn_kernel(x_ref, g_ref, wg_ref, wu_ref, wd_ref, *rest, has_head):
  if has_head:
    head_hbm, o_ref, h_ref, copy_sem = rest
  else:
    o_ref, h_ref = rest

  def step(first):
    for r in range(h_ref.shape[0] // FFN_SUB_ROWS):
      rows = slice(r * FFN_SUB_ROWS, (r + 1) * FFN_SUB_ROWS)
      if first:
        base = x_ref[rows, :]
        h = _rmsnorm(base, g_ref[...]).astype(BF16)
        h_ref[rows, :] = h
      else:
        base = o_ref[rows, :]
        h = h_ref[rows, :]
      gate = _dot(h, wg_ref[...])
      up = _dot(h, wu_ref[...])
      act = (gate * jax.nn.sigmoid(gate)) * up
      o_ref[rows, :] = base + _dot(act.astype(BF16), wd_ref[...])

  def compute_tile():
    lax.cond(pl.program_id(1) == 0, lambda: step(True), lambda: step(False))

  def copy_head_tile():
    @pl.when(pl.program_id(1) == 0)
    def _():
      copy = pltpu.make_async_copy(head_hbm, o_ref, copy_sem)
      copy.start()
      copy.wait()

  if has_head:
    lax.cond(pl.program_id(0) == 0, copy_head_tile, compute_tile)
  else:
    compute_tile()


def _ffn(x, g, wg, wu, wd_half, head=None):
  m, d = x.shape
  dff = wg.shape[1]
  tm, tf = FFN_ROW_TILE, FFN_FF_TILE
  has_head = head is not None
  chunk = (lambda i, j: jnp.where(i == 0, 0, j)) if has_head else (lambda i, j: j)
  return pl.pallas_call(
      functools.partial(_ffn_kernel, has_head=has_head),
      grid=(m // tm, dff // tf),
      in_specs=[
          pl.BlockSpec((tm, d), lambda i, j: (i, 0)),
          pl.BlockSpec((1, d), lambda i, j: (0, 0)),
          pl.BlockSpec((d, tf), lambda i, j: (0, chunk(i, j))),
          pl.BlockSpec((d, tf), lambda i, j: (0, chunk(i, j))),
          pl.BlockSpec((tf, d), lambda i, j: (chunk(i, j), 0)),
      ] + ([pl.BlockSpec(memory_space=pl.ANY)] if has_head else []),
      out_specs=pl.BlockSpec((tm, d), lambda i, j: (i, 0)),
      out_shape=jax.ShapeDtypeStruct((m, d), F32),
      scratch_shapes=([pltpu.VMEM((tm, d), BF16)]
                      + ([pltpu.SemaphoreType.DMA(())] if has_head else [])),
      compiler_params=_params("arbitrary" if has_head else "parallel", "arbitrary"),
      name="ffn",
  )(x, g, wg, wu, wd_half, *([head] if has_head else []))


def _slab_cast_jobs(steps, *weights, scale=1.0):
  jobs = []
  for w in weights:
    rows = max(BF16_TILE_ROWS, w.shape[0] // steps)
    assert w.shape[0] % rows == 0 and rows % BF16_TILE_ROWS == 0
    last = w.shape[0] // rows - 1
    jobs.append(_cast_job(w, (rows, w.shape[1]), lambda t, last=last: (jnp.minimum(t, last), 0),
                          scale=scale))
  return jobs


def _w_in_repack_jobs(steps, w_t, n_a, n_f, n_b):
  d = w_t.shape[1]
  lanes = V7X_LANES
  slabs = d // lanes
  assert d % lanes == 0 and slabs <= steps and n_f % 8 == 0 and n_f <= lanes
  assert n_a % n_f == 0 and n_b <= n_a + n_f
  slab = lambda t: jnp.minimum(t, slabs - 1)

  def gate_rows(ref):
    return jnp.concatenate([ref[...], jnp.zeros((lanes - n_f, lanes), F32)], axis=0).T

  out = lambda n: jax.ShapeDtypeStruct((d, n), BF16)
  out_spec = lambda n: pl.BlockSpec((lanes, n), lambda t: (slab(t), 0))
  return [
      _SideJob((w_t,), (pl.BlockSpec((n_a, lanes), lambda t: (0, slab(t))),),
               out(n_a), out_spec(n_a), lambda ref: ref[...].T),
      _SideJob((w_t,), (pl.BlockSpec((n_a + n_f, lanes), lambda t: (1, slab(t))),),
               out(n_b), out_spec(n_b), lambda ref: ref[:n_b, :].T),
      _SideJob((w_t,), (pl.BlockSpec((n_f, lanes), lambda t: (n_a // n_f, slab(t))),),
               out(lanes), out_spec(lanes), gate_rows),
  ]


def _swap_matrix():
  i = np.arange(PERM_ROWS)
  src = (i % RESIDUES) * RESIDUES + i // RESIDUES
  return jnp.asarray(np.eye(PERM_ROWS, dtype=np.float32)[src], BF16)


def _norm_matmul_kernel(x_ref, g_ref, w_ref, *rest, permute, narrow, jobs, key_cols):
  rest = list(rest)
  swap_ref = rest.pop(0) if permute else None
  wn_ref = rest.pop(0) if narrow else None
  side_in = [rest.pop(0) for job in jobs for _ in job.arrays]
  o_ref = rest.pop(0)
  on_ref = rest.pop(0) if narrow else None
  side_out = [rest.pop(0) for _ in jobs]
  assert not rest, "unexpected extra refs"
  tm = x_ref.shape[0]
  _run_side_jobs(jobs, side_in, side_out)

  h = _rmsnorm(x_ref[...], g_ref[...]).astype(BF16)
  if permute:
    h = jnp.concatenate(
        [_dot(swap_ref[...], h[a * PERM_ROWS:(a + 1) * PERM_ROWS, :]).astype(BF16)
         for a in range(tm // PERM_ROWS)], axis=0)
  if narrow:
    on_ref[...] = _dot(h, wn_ref[...])
  acc = _dot(h, w_ref[...])
  lo, hi = key_cols
  res = jnp.concatenate([acc[:, :lo], acc[:, lo:hi] * (SCALE * LOG2_E), acc[:, hi:]],
                        axis=1).astype(o_ref.dtype)
  if permute:
    per = PERM_ROWS // RESIDUES
    for a in range(tm // PERM_ROWS):
      for r in range(RESIDUES):
        start = a * PERM_ROWS + r * per
        o_ref[r, a * per:(a + 1) * per, :] = res[start:start + per, :]
  else:
    o_ref[...] = res


def _norm_matmul(x, g, w, out_dtype, key_cols, residue_major=None, w_narrow=None, jobs=()):
  m, d = x.shape
  n = w.shape[1]
  tm = PROJ_ROW_TILE
  narrow = w_narrow is not None
  permute = residue_major is not None
  assert not (narrow and permute)
  in_specs = [
      pl.BlockSpec((tm, d), lambda i: (i, 0)),
      pl.BlockSpec((1, d), lambda i: (0, 0)),
      pl.BlockSpec((d, n), lambda i: (0, 0)),
  ]
  args = [x, g, w]
  if permute:
    batch, seq = residue_major
    tiles = seq // tm
    in_specs.append(pl.BlockSpec((PERM_ROWS, PERM_ROWS), lambda i: (0, 0)))
    args.append(_swap_matrix())
    out_specs = [pl.BlockSpec((RESIDUES, tm // RESIDUES, n), lambda i: (i // tiles, i % tiles, 0))]
    out_shape = [jax.ShapeDtypeStruct((batch * RESIDUES, seq // RESIDUES, n), out_dtype)]
  else:
    out_specs = [pl.BlockSpec((tm, n), lambda i: (i, 0))]
    out_shape = [jax.ShapeDtypeStruct((m, n), out_dtype)]
  if narrow:
    in_specs.append(pl.BlockSpec(w_narrow.shape, lambda i: (0, 0)))
    args.append(w_narrow)
    out_specs.append(pl.BlockSpec((tm, w_narrow.shape[1]), lambda i: (i, 0)))
    out_shape.append(jax.ShapeDtypeStruct((m, w_narrow.shape[1]), F32))
  side_arrays, side_in_specs, side_out_specs, side_out_shapes = _side_args(jobs)
  outs = pl.pallas_call(
      functools.partial(_norm_matmul_kernel, permute=permute, narrow=narrow, jobs=tuple(jobs),
                        key_cols=key_cols),
      grid=(m // tm,),
      in_specs=in_specs + side_in_specs,
      out_specs=out_specs + side_out_specs,
      out_shape=out_shape + side_out_shapes,
      compiler_params=_params(_row_semantics(jobs)),
      name="norm_matmul",
  )(*args, *side_arrays)
  outs = list(outs)
  if permute:
    outs[0] = outs[0].reshape(m, n)
  return outs


def _cumsum_kernel(fl_ref, bf_ref, c_ref):
  s = fl_ref.shape[0]
  z = fl_ref[...].T[0:N_HEADS_FOX, :] + bf_ref[...]
  lt = jnp.minimum(z, 0.0) - jnp.log1p(jnp.exp(-jnp.abs(z)))
  row = lax.broadcasted_iota(jnp.int32, (V7X_LANES, V7X_LANES), 0)
  col = lax.broadcasted_iota(jnp.int32, (V7X_LANES, V7X_LANES), 1)
  upper = (row <= col).astype(F32)
  chunks = [jnp.dot(lt[:, j * V7X_LANES:(j + 1) * V7X_LANES], upper, preferred_element_type=F32,
                    precision=lax.Precision.HIGHEST) for j in range(s // V7X_LANES)]
  offset = jnp.zeros((N_HEADS_FOX, 1), F32)
  for j, cs in enumerate(chunks):
    c_ref[:, j * V7X_LANES:(j + 1) * V7X_LANES] = cs + offset
    offset = offset + cs[:, V7X_LANES - 1:V7X_LANES]


def _fox_decay(f_logit, b_f_col, batch, seq):
  return pl.pallas_call(
      _cumsum_kernel,
      grid=(batch,),
      in_specs=[
          pl.BlockSpec((seq, V7X_LANES), lambda b: (b, 0)),
          pl.BlockSpec((N_HEADS_FOX, 1), lambda b: (0, 0)),
      ],
      out_specs=pl.BlockSpec((None, N_HEADS_FOX, seq), lambda b: (b, 0, 0)),
      out_shape=jax.ShapeDtypeStruct((batch, N_HEADS_FOX, seq), F32),
      compiler_params=_params("parallel"),
      name="fox_decay",
  )(f_logit, b_f_col)


def _fox_phases(q_ref, k_ref, v_ref, c_ref, o_ref):
  seq = q_ref.shape[0]
  tq = FOX_Q_TILE
  h = pl.program_id(1)
  crow = c_ref[pl.ds(h, 1), :] * LOG2_E
  row = lax.broadcasted_iota(jnp.int32, (tq, tq), 0)
  col = lax.broadcasted_iota(jnp.int32, (tq, tq), 1)
  diag_mask = jnp.where(col > row, NEG_INF, 0.0).astype(F32)

  def scores(i):
    t0, t1 = i * tq, (i + 1) * tq
    q = q_ref[t0:t1, :]
    bias = crow[:, t1 - 1:t1] - crow[:, 0:t1]
    s_diag = _dot_nt(q, k_ref[t0:t1, :]) + bias[:, t0:t1] + diag_mask
    s_off = _dot_nt(q, k_ref[0:t0, :]) + bias[:, 0:t0] if i > 0 else None
    return s_diag, s_off

  def finish(i, s_diag, s_off):
    t0, t1 = i * tq, (i + 1) * tq
    m = jnp.max(s_diag, axis=-1, keepdims=True)
    if i > 0:
      m = jnp.maximum(m, jnp.max(s_off, axis=-1, keepdims=True))
    o, l = _weighted_values(jnp.exp2(s_diag - m), v_ref[t0:t1, :])
    if i > 0:
      o_off, l_off = _weighted_values(jnp.exp2(s_off - m), v_ref[0:t0, :])
      o, l = o + o_off, l + l_off
    o_ref[t0:t1, :] = (o / l).astype(o_ref.dtype)

  n_tiles = seq // tq
  pending = [scores(i) for i in range(min(FOX_LOOKAHEAD, n_tiles))]
  yield
  for i in range(n_tiles):
    if i + FOX_LOOKAHEAD < n_tiles:
      pending.append(scores(i + FOX_LOOKAHEAD))
      yield
    finish(i, *pending.pop(0))
    yield


def _t5_bucket_np(dist):
  max_exact = N_REL_BUCKETS // 2
  d = np.maximum(dist, 1).astype(np.float32)
  large = max_exact + (np.log(d / np.float32(max_exact))
                       / np.float32(math.log(REL_MAX_DISTANCE / max_exact))
                       * np.float32(N_REL_BUCKETS - max_exact)).astype(np.int32)
  large = np.minimum(large, N_REL_BUCKETS - 1)
  return np.where(dist < max_exact, dist, large).astype(np.int32)


def _block_positions(dilation):
  n = WINDOW_KEYS
  m = RESIDUES // dilation
  rows = n // m
  j = np.arange(m)[:, None]
  qpos = (n + m * np.arange(rows)[None, :] + j).reshape(-1)
  kpos = (m * np.arange(2 * rows)[None, :] + j).reshape(-1)
  return qpos, kpos


def _band_buckets():
  n = WINDOW_KEYS
  tiles = []
  for _, dilation in DILATED_PATTERNS:
    qpos, kpos = _block_positions(dilation)
    rel = qpos[:, None] - np.concatenate([kpos, qpos])[None, :]
    valid = (rel >= 0) & (rel <= n)
    bucket = _t5_bucket_np(np.maximum(rel, 0) * dilation)
    tiles.append(np.where(valid, bucket, -1))
  return np.stack(tiles).astype(np.int32)


def _bias_kernel(tab_ref, bkt_ref, o_ref):
  table = [[tab_ref[b, h] * LOG2_E for h in range(N_HEADS_DIL)] for b in range(N_REL_BUCKETS)]
  for r0 in range(0, bkt_ref.shape[0], BF16_TILE_ROWS):
    rows = slice(r0, r0 + BF16_TILE_ROWS)
    bkt = bkt_ref[rows, :]
    accs = [jnp.full(bkt.shape, NEG_INF, F32)] * N_HEADS_DIL
    for b in range(N_REL_BUCKETS):
      hit = bkt == b
      accs = [jnp.where(hit, table[b][h], acc) for h, acc in enumerate(accs)]
    for h, acc in enumerate(accs):
      o_ref[h, rows, :] = acc


def _band_bias(rel_table):
  buckets = jnp.asarray(_band_buckets())
  p, n, n2 = buckets.shape
  return pl.pallas_call(
      _bias_kernel,
      grid=(p,),
      in_specs=[
          pl.BlockSpec(memory_space=pltpu.SMEM),
          pl.BlockSpec((None, n, n2), lambda i: (i, 0, 0)),
      ],
      out_specs=pl.BlockSpec((None, N_HEADS_DIL, n, n2), lambda i: (i, 0, 0, 0)),
      out_shape=jax.ShapeDtypeStruct((p, N_HEADS_DIL, n, n2), F32),
      compiler_params=_params("parallel"),
      name="band_bias",
  )(rel_table, buckets)


def _dil_phases(q_ref, k_ref, v_ref, bm_ref, o_ref, *scratch):
  seq = q_ref.shape[0]
  n = WINDOW_KEYS
  seg = seq // RESIDUES
  last = len(DILATED_PATTERNS) - 1
  assert DILATED_PATTERNS[last][1] == RESIDUES and n == seg
  accs, lses = scratch[:last], scratch[last:]

  def gather(ref, starts, size):
    return jnp.concatenate([ref[st:st + size, :] for st in starts], axis=0).astype(BF16)

  def block_rows(segments, nb):
    rows = n // len(segments)
    q_starts = [s * seg + nb * rows for s in segments]
    if nb == 0:
      return rows, q_starts, q_starts, rows
    return rows, q_starts, [st - rows for st in q_starts], 2 * rows

  def scores(p, segments, nb):
    rows, q_starts, k_starts, k_rows = block_rows(segments, nb)
    bm = bm_ref[p, :, 2 * n:3 * n] if nb == 0 else bm_ref[p, :, 0:2 * n]
    return _dot_nt(gather(q_ref, q_starts, rows), gather(k_ref, k_starts, k_rows)) + bm

  def finish(p, segments, nb, e, m):
    rows, q_starts, k_starts, k_rows = block_rows(segments, nb)
    o, l = _weighted_values(e, gather(v_ref, k_starts, k_rows))
    o = o / l
    lse = m + jnp.log2(l)
    if p < last:
      for j, st in enumerate(q_starts):
        accs[p][st:st + rows, :] = o[j * rows:(j + 1) * rows, :]
        lses[p][st:st + rows, :] = lse[j * rows:(j + 1) * rows, :]
      return
    seg_rows = slice(q_starts[0], q_starts[0] + n)
    all_lse = [ref[seg_rows, :] for ref in lses] + [lse]
    all_out = [ref[seg_rows, :] for ref in accs] + [o]
    top = functools.reduce(jnp.maximum, all_lse)
    weights = [jnp.exp2(x - top) for x in all_lse]
    add = lambda a, b: a + b
    mixed = (functools.reduce(add, [w * a for w, a in zip(weights, all_out)])
             / functools.reduce(add, weights))
    o_ref[seg_rows, :] = mixed.astype(o_ref.dtype)

  blocks = [(p, list(range(r, RESIDUES, d)), nb)
            for p, (_, d) in enumerate(DILATED_PATTERNS)
            for r in range(d) for nb in range(seq // (n * d))]
  for g in range(0, len(blocks), DIL_BLOCK_GROUP):
    group = blocks[g:g + DIL_BLOCK_GROUP]
    ss = [scores(*blk) for blk in group]
    yield
    ms = [jnp.max(s, axis=-1, keepdims=True) for s in ss]
    es = [jnp.exp2(s - m) for s, m in zip(ss, ms)]
    yield
    for blk, e, m in zip(group, es, ms):
      finish(*blk, e, m)
    yield


def _mixer_kernel(qa_ref, ka_ref, va_ref, c_ref, qb_ref, kb_ref, vb_ref, bm_ref,
                  oa_ref, ob_ref, *scratch):
  streams = [_dil_phases(qb_ref, kb_ref, vb_ref, bm_ref, ob_ref, *scratch),
             _fox_phases(qa_ref, ka_ref, va_ref, c_ref, oa_ref)]
  while streams:
    for stream in list(streams):
      try:
        next(stream)
      except StopIteration:
        streams.remove(stream)


def _token_mixers(u_a, c, u_b, band_bias, batch, seq):
  assert seq // RESIDUES == WINDOW_KEYS and N_HEADS_FOX == N_HEADS_DIL
  blk = lambda off: pl.BlockSpec((seq, HEAD_DIM), lambda b, h: (b, off + h))
  n_pat = len(DILATED_PATTERNS)
  scratch = [pltpu.VMEM((seq, HEAD_DIM), F32) for _ in range(2 * (n_pat - 1))]
  out_blk = pl.BlockSpec((seq, HEAD_DIM), lambda b, h: (b, h))
  return pl.pallas_call(
      _mixer_kernel,
      grid=(batch, N_HEADS_FOX),
      in_specs=[
          blk(0), blk(N_HEADS_FOX), blk(2 * N_HEADS_FOX),
          pl.BlockSpec((None, N_HEADS_FOX, seq), lambda b, h: (b, 0, 0)),
          blk(0), blk(N_HEADS_DIL), blk(2 * N_HEADS_DIL),
          pl.BlockSpec((n_pat, None, WINDOW_KEYS, 3 * WINDOW_KEYS), lambda b, h: (0, h, 0, 0)),
      ],
      out_specs=[out_blk, out_blk],
      out_shape=[jax.ShapeDtypeStruct((batch * seq, D_FOX), BF16),
                 jax.ShapeDtypeStruct((batch * seq, D_DIL), BF16)],
      scratch_shapes=scratch,
      compiler_params=_params("parallel", "arbitrary"),
      name="token_mixers",
  )(u_a, u_a, u_a, c, u_b, u_b, u_b, band_bias)


def _out_proj_kernel(x_ref, a_ref, b_ref, swap_ref, wa_ref, wb_ref, *rest, jobs):
  n_in = sum(len(job.arrays) for job in jobs)
  side_in, o_ref, side_out = rest[:n_in], rest[n_in], rest[n_in + 1:]
  _run_side_jobs(jobs, side_in, side_out)
  per = PERM_ROWS // RESIDUES
  for a in range(x_ref.shape[0] // PERM_ROWS):
    rows = slice(a * PERM_ROWS, (a + 1) * PERM_ROWS)
    slab = jnp.concatenate([b_ref[r, a * per:(a + 1) * per, :] for r in range(RESIDUES)], axis=0)
    o_b = _dot(swap_ref[...], slab).astype(BF16)
    o_ref[rows, :] = x_ref[rows, :] + _dot(a_ref[rows, :], wa_ref[...]) + _dot(o_b, wb_ref[...])


def _out_proj(x, o_a, o_b, wo, batch, seq, jobs=()):
  m, d = x.shape
  tm = PROJ_ROW_TILE
  tiles = seq // tm
  ca, cb = o_a.shape[1], o_b.shape[1]
  assert ca == cb and wo.shape[0] == ca + cb
  resident = pl.Buffered(1)
  side_arrays, side_in_specs, side_out_specs, side_out_shapes = _side_args(jobs)
  return pl.pallas_call(
      functools.partial(_out_proj_kernel, jobs=tuple(jobs)),
      grid=(m // tm,),
      in_specs=[
          pl.BlockSpec((tm, d), lambda i: (i, 0)),
          pl.BlockSpec((tm, ca), lambda i: (i, 0)),
          pl.BlockSpec((RESIDUES, tm // RESIDUES, cb), lambda i: (i // tiles, i % tiles, 0)),
          pl.BlockSpec((PERM_ROWS, PERM_ROWS), lambda i: (0, 0)),
          pl.BlockSpec((ca, d), lambda i: (0, 0), pipeline_mode=resident),
          pl.BlockSpec((cb, d), lambda i: (1, 0), pipeline_mode=resident),
      ] + side_in_specs,
      out_specs=[pl.BlockSpec((tm, d), lambda i: (i, 0))] + side_out_specs,
      out_shape=[jax.ShapeDtypeStruct((m, d), F32)] + side_out_shapes,
      compiler_params=_params(_row_semantics(jobs)),
      name="out_proj",
  )(x, o_a, o_b.reshape(batch * RESIDUES, seq // RESIDUES, cb), _swap_matrix(), wo, wo,
    *side_arrays)


def _ple_kernel(x_ref, p_ref, g_ref, wg_ref, wp_ref, gf_ref, o_ref, *, final_norm):
  x = x_ref[...]
  h = _rmsnorm(x, g_ref[...]).astype(BF16)
  gate = jax.nn.sigmoid(_dot(h, wg_ref[...]))
  y = x + gate * _dot(p_ref[...].astype(BF16), wp_ref[...])
  o_ref[...] = _rmsnorm(y, gf_ref[...]) if final_norm else y


def _ple(x, p, g, w_gate, w_proj, g_final, final_norm):
  m, d = x.shape
  tm = PROJ_ROW_TILE
  return pl.pallas_call(
      functools.partial(_ple_kernel, final_norm=final_norm),
      grid=(m // tm,),
      in_specs=[
          pl.BlockSpec((tm, d), lambda i: (i, 0)),
          pl.BlockSpec((tm, p.shape[1]), lambda i: (i, 0)),
          pl.BlockSpec((1, d), lambda i: (0, 0)),
          pl.BlockSpec(w_gate.shape, lambda i: (0, 0)),
          pl.BlockSpec(w_proj.shape, lambda i: (0, 0)),
          pl.BlockSpec((1, d), lambda i: (0, 0)),
      ],
      out_specs=pl.BlockSpec((tm, d), lambda i: (i, 0)),
      out_shape=jax.ShapeDtypeStruct((m, d), F32),
      compiler_params=_params("parallel"),
      name="ple",
  )(x, p, g, w_gate, w_proj, g_final)


def kernel(x, p, norm_ffn1, ffn1_w_gate, ffn1_w_up, ffn1_w_down, norm_mix, w_in, b_f, w_o,
           norm_ffn2, ffn2_w_gate, ffn2_w_up, ffn2_w_down, norm_ple, w_ple_gate, w_ple_proj,
           rel_table, norm_final):
  batch, seq, d = x.shape
  depth = p.shape[0]
  m = batch * seq
  bf = lambda w: w.astype(BF16)
  row = lambda g: g.reshape(1, -1).astype(F32)

  band_bias = _band_bias(rel_table.astype(F32))
  xs = x.reshape(m, d).astype(F32)
  for i in range(depth):
    g_ffn1 = row(norm_ffn1[i])
    proj_steps = m // PROJ_ROW_TILE
    head_steps = _ffn_head_steps(ffn1_w_gate[i])
    repack = lambda steps: _w_in_repack_jobs(steps, jnp.swapaxes(w_in[i], 0, 1),
                                             3 * D_FOX, N_HEADS_FOX, 3 * D_DIL)
    a_job, _, f_job = repack(head_steps)
    _, b_job, _ = repack(proj_steps)
    head, wg1, wu1, wd1, w_a, w_f = _ffn_head(
        xs, g_ffn1, ffn1_w_gate[i], ffn1_w_up[i], ffn1_w_down[i], jobs=[a_job, f_job])
    xs = _ffn(xs, g_ffn1, wg1, wu1, wd1, head=head)

    b_f_col = b_f[i].astype(F32).reshape(N_HEADS_FOX, 1)
    g_mix = row(norm_mix[i])
    u_a, f_logit, wo, w_b = _norm_matmul(
        xs, g_mix, w_a, BF16, (D_FOX, 2 * D_FOX), w_narrow=w_f,
        jobs=_slab_cast_jobs(proj_steps, w_o[i]) + [b_job])
    u_b, wg2, wu2, wd2 = _norm_matmul(
        xs, g_mix, w_b, F32, (D_DIL, 2 * D_DIL), residue_major=(batch, seq),
        jobs=_slab_cast_jobs(proj_steps, ffn2_w_gate[i], ffn2_w_up[i])
        + _slab_cast_jobs(proj_steps, ffn2_w_down[i], scale=0.5))

    c = _fox_decay(f_logit, b_f_col, batch, seq)
    o_a, o_b = _token_mixers(u_a, c, u_b, band_bias, batch, seq)
    xs, w_gate, w_ple = _out_proj(xs, o_a, o_b, wo, batch, seq,
                                  jobs=_slab_cast_jobs(proj_steps, w_ple_gate[i], w_ple_proj[i]))

    xs = _ffn(xs, row(norm_ffn2[i]), wg2, wu2, wd2)
    last = i == depth - 1
    xs = _ple(xs, p[i].reshape(m, -1), row(norm_ple[i]), w_gate, w_ple,
              row(norm_final), final_norm=last)
  return xs.reshape(batch, seq, d).astype(x.dtype)
```

```python
import functools
import math
from typing import Any, Callable, NamedTuple

import jax
import jax.numpy as jnp
import numpy as np
from jax import lax
from jax.experimental import pallas as pl
from jax.experimental.pallas import tpu as pltpu

F32 = jnp.float32
BF16 = jnp.bfloat16

HEAD_DIM = 128
N_HEADS_FOX = 8
N_HEADS_DIL = 8
D_FOX = N_HEADS_FOX * HEAD_DIM
D_DIL = N_HEADS_DIL * HEAD_DIM
DILATED_PATTERNS = ((128, 1), (512, 4), (2048, 16))
WINDOW_KEYS = 128
N_REL_BUCKETS = 32
REL_MAX_DISTANCE = 2048
RMS_EPS = 1e-6
NEG_INF = -1e30
SCALE = HEAD_DIM ** -0.5
LOG2_E = math.log2(math.e)

V7X_LANES = 128
BF16_TILE_ROWS = 16
V7X_VMEM_LIMIT_BYTES = 56 * 1024 * 1024

RESIDUES = max(d for _, d in DILATED_PATTERNS)
PERM_ROWS = RESIDUES * RESIDUES

FFN_ROW_TILE = 1024
FFN_SUB_ROWS = 512
FFN_FF_TILE = 512
FFN_HEAD_FF_TILE = 256
PROJ_ROW_TILE = 512
FOX_Q_TILE = 256
FOX_LOOKAHEAD = 2
DIL_BLOCK_GROUP = 12


def _params(*semantics):
  return pltpu.CompilerParams(dimension_semantics=semantics,
                              vmem_limit_bytes=V7X_VMEM_LIMIT_BYTES)


def _rmsnorm(x, g):
  ms = jnp.mean(x * x, axis=-1, keepdims=True)
  return x * lax.rsqrt(ms + RMS_EPS) * g


def _dot(a, b):
  return jnp.dot(a, b, preferred_element_type=F32)


def _dot_nt(a, b):
  return lax.dot_general(a, b, (((1,), (1,)), ((), ())), preferred_element_type=F32)


def _weighted_values(e, v):
  both = _dot(e.astype(BF16), jnp.concatenate([v, jnp.ones_like(v)], axis=1))
  return both[:, :HEAD_DIM], both[:, HEAD_DIM:]


class _SideJob(NamedTuple):
  arrays: tuple
  in_specs: tuple
  out_shape: Any
  out_spec: Any
  fn: Callable


def _cast_job(a, block, index, scale=1.0):
  spec = pl.BlockSpec(block, index)
  return _SideJob((a,), (spec,), jax.ShapeDtypeStruct(a.shape, BF16), spec,
                  lambda r: r[...] * scale)


def _run_side_jobs(jobs, in_refs, out_refs):
  in_refs = list(in_refs)
  for job, out_ref in zip(jobs, out_refs):
    refs = [in_refs.pop(0) for _ in job.arrays]
    out_ref[...] = job.fn(*refs).astype(out_ref.dtype)


def _row_semantics(jobs):
  return "arbitrary" if jobs else "parallel"


def _side_args(jobs):
  arrays = [a for job in jobs for a in job.arrays]
  in_specs = [s for job in jobs for s in job.in_specs]
  return arrays, in_specs, [job.out_spec for job in jobs], [job.out_shape for job in jobs]


def _ffn_head_kernel(x_ref, g_ref, wg_ref, wu_ref, wd_ref, *rest, jobs):
  n_in = sum(len(job.arrays) for job in jobs)
  side_in, (o_ref, og_ref, ou_ref, od_ref) = rest[:n_in], rest[n_in:n_in + 4]
  side_out, h_ref = rest[n_in + 4:-1], rest[-1]
  _run_side_jobs(jobs, side_in, side_out)

  @pl.when(pl.program_id(0) == 0)
  def _():
    x = x_ref[...]
    h_ref[...] = _rmsnorm(x, g_ref[...]).astype(BF16)
    o_ref[...] = x

  wg = wg_ref[...].astype(BF16)
  wu = wu_ref[...].astype(BF16)
  wd = (wd_ref[...] * 0.5).astype(BF16)
  og_ref[...] = wg
  ou_ref[...] = wu
  od_ref[...] = wd
  for r in range(h_ref.shape[0] // FFN_SUB_ROWS):
    rows = slice(r * FFN_SUB_ROWS, (r + 1) * FFN_SUB_ROWS)
    h = h_ref[rows, :]
    gate = _dot(h, wg)
    up = _dot(h, wu)
    act = (gate * jax.nn.sigmoid(gate)) * up
    o_ref[rows, :] += _dot(act.astype(BF16), wd)


def _ffn_head_steps(w_gate):
  return w_gate.shape[1] // FFN_HEAD_FF_TILE


def _ffn_head(x, g, w_gate, w_up, w_down, jobs=()):
  d = x.shape[1]
  tm, tf = FFN_ROW_TILE, FFN_HEAD_FF_TILE
  side_arrays, side_in_specs, side_out_specs, side_out_shapes = _side_args(jobs)
  return pl.pallas_call(
      functools.partial(_ffn_head_kernel, jobs=tuple(jobs)),
      grid=(_ffn_head_steps(w_gate),),
      in_specs=[
          pl.BlockSpec((tm, d), lambda j: (0, 0), pipeline_mode=pl.Buffered(1)),
          pl.BlockSpec((1, d), lambda j: (0, 0)),
          pl.BlockSpec((d, tf), lambda j: (0, j)),
          pl.BlockSpec((d, tf), lambda j: (0, j)),
          pl.BlockSpec((tf, d), lambda j: (j, 0)),
      ] + side_in_specs,
      out_specs=[
          pl.BlockSpec((tm, d), lambda j: (0, 0)),
          pl.BlockSpec((d, tf), lambda j: (0, j)),
          pl.BlockSpec((d, tf), lambda j: (0, j)),
          pl.BlockSpec((tf, d), lambda j: (j, 0)),
      ] + side_out_specs,
      out_shape=[
          jax.ShapeDtypeStruct((tm, d), F32),
          jax.ShapeDtypeStruct(w_gate.shape, BF16),
          jax.ShapeDtypeStruct(w_up.shape, BF16),
          jax.ShapeDtypeStruct(w_down.shape, BF16),
      ] + side_out_shapes,
      scratch_shapes=[pltpu.VMEM((tm, d), BF16)],
      compiler_params=_params("arbitrary"),
      name="ffn_head",
  )(x, g, w_gate, w_up, w_down, *side_arrays)


def _ffn_kernel(x_ref, g_ref, wg_ref, wu_ref, wd_ref, *rest, has_head):
  if has_head:
    head_hbm, o_ref, h_ref, copy_sem = rest
  else:
    o_ref, h_ref = rest

  def step(first):
    for r in range(h_ref.shape[0] // FFN_SUB_ROWS):
      rows = slice(r * FFN_SUB_ROWS, (r + 1) * FFN_SUB_ROWS)
      if first:
        base = x_ref[rows, :]
        h = _rmsnorm(base, g_ref[...]).astype(BF16)
        h_ref[rows, :] = h
      else:
        base = o_ref[rows, :]
        h = h_ref[rows, :]
      gate = _dot(h, wg_ref[...])
      up = _dot(h, wu_ref[...])
      act = (gate * jax.nn.sigmoid(gate)) * up
      o_ref[rows, :] = base + _dot(act.astype(BF16), wd_ref[...])

  def compute_tile():
    lax.cond(pl.program_id(1) == 0, lambda: step(True), lambda: step(False))

  def copy_head_tile():
    @pl.when(pl.program_id(1) == 0)
    def _():
      copy = pltpu.make_async_copy(head_hbm, o_ref, copy_sem)
      copy.start()
      copy.wait()

  if has_head:
    lax.cond(pl.program_id(0) == 0, copy_head_tile, compute_tile)
  else:
    compute_tile()


def _ffn(x, g, wg, wu, wd_half, head=None):
  m, d = x.shape
  dff = wg.shape[1]
  tm, tf = FFN_ROW_TILE, FFN_FF_TILE
  has_head = head is not None
  chunk = (lambda i, j: jnp.where(i == 0, 0, j)) if has_head else (lambda i, j: j)
  return pl.pallas_call(
      functools.partial(_ffn_kernel, has_head=has_head),
      grid=(m // tm, dff // tf),
      in_specs=[
          pl.BlockSpec((tm, d), lambda i, j: (i, 0)),
          pl.BlockSpec((1, d), lambda i, j: (0, 0)),
          pl.BlockSpec((d, tf), lambda i, j: (0, chunk(i, j))),
          pl.BlockSpec((d, tf), lambda i, j: (0, chunk(i, j))),
          pl.BlockSpec((tf, d), lambda i, j: (chunk(i, j), 0)),
      ] + ([pl.BlockSpec(memory_space=pl.ANY)] if has_head else []),
      out_specs=pl.BlockSpec((tm, d), lambda i, j: (i, 0)),
      out_shape=jax.ShapeDtypeStruct((m, d), F32),
      scratch_shapes=([pltpu.VMEM((tm, d), BF16)]
                      + ([pltpu.SemaphoreType.DMA(())] if has_head else [])),
      compiler_params=_params("arbitrary" if has_head else "parallel", "arbitrary"),
      name="ffn",
  )(x, g, wg, wu, wd_half, *([head] if has_head else []))


def _slab_cast_jobs(steps, *weights, scale=1.0):
  jobs = []
  for w in weights:
    rows = max(BF16_TILE_ROWS, w.shape[0] // steps)
    assert w.shape[0] % rows == 0 and rows % BF16_TILE_ROWS == 0
    last = w.shape[0] // rows - 1
    jobs.append(_cast_job(w, (rows, w.shape[1]), lambda t, last=last: (jnp.minimum(t, last), 0),
                          scale=scale))
  return jobs


def _w_in_repack_jobs(steps, w_t, n_a, n_f, n_b):
  d = w_t.shape[1]
  lanes = V7X_LANES
  slabs = d // lanes
  assert d % lanes == 0 and slabs <= steps and n_f % 8 == 0 and n_f <= lanes
  assert n_a % n_f == 0 and n_b <= n_a + n_f
  slab = lambda t: jnp.minimum(t, slabs - 1)

  def gate_rows(ref):
    return jnp.concatenate([ref[...], jnp.zeros((lanes - n_f, lanes), F32)], axis=0).T

  out = lambda n: jax.ShapeDtypeStruct((d, n), BF16)
  out_spec = lambda n: pl.BlockSpec((lanes, n), lambda t: (slab(t), 0))
  return [
      _SideJob((w_t,), (pl.BlockSpec((n_a, lanes), lambda t: (0, slab(t))),),
               out(n_a), out_spec(n_a), lambda ref: ref[...].T),
      _SideJob((w_t,), (pl.BlockSpec((n_a + n_f, lanes), lambda t: (1, slab(t))),),
               out(n_b), out_spec(n_b), lambda ref: ref[:n_b, :].T),
      _SideJob((w_t,), (pl.BlockSpec((n_f, lanes), lambda t: (n_a // n_f, slab(t))),),
               out(lanes), out_spec(lanes), gate_rows),
  ]


def _swap_matrix():
  i = np.arange(PERM_ROWS)
  src = (i % RESIDUES) * RESIDUES + i // RESIDUES
  return jnp.asarray(np.eye(PERM_ROWS, dtype=np.float32)[src], BF16)


def _norm_matmul_kernel(x_ref, g_ref, w_ref, *rest, permute, narrow, jobs, key_cols):
  rest = list(rest)
  swap_ref = rest.pop(0) if permute else None
  wn_ref, bf_ref = (rest.pop(0), rest.pop(0)) if narrow else (None, None)
  side_in = [rest.pop(0) for job in jobs for _ in job.arrays]
  o_ref = rest.pop(0)
  c_ref = rest.pop(0) if narrow else None
  side_out = [rest.pop(0) for _ in jobs]
  carry_ref = rest.pop(0) if narrow else None
  assert not rest, "unexpected extra refs"
  tm = x_ref.shape[0]
  _run_side_jobs(jobs, side_in, side_out)

  h = _rmsnorm(x_ref[...], g_ref[...]).astype(BF16)
  if permute:
    h = jnp.concatenate(
        [_dot(swap_ref[...], h[a * PERM_ROWS:(a + 1) * PERM_ROWS, :]).astype(BF16)
         for a in range(tm // PERM_ROWS)], axis=0)
  if narrow:
    z = _dot(h, wn_ref[...]).T[0:N_HEADS_FOX, :] + bf_ref[...]
    lt = jnp.minimum(z, 0.0) - jnp.log1p(jnp.exp(-jnp.abs(z)))
    row = lax.broadcasted_iota(jnp.int32, (V7X_LANES, V7X_LANES), 0)
    col = lax.broadcasted_iota(jnp.int32, (V7X_LANES, V7X_LANES), 1)
    upper = (row <= col).astype(F32)
    n_chunks = tm // V7X_LANES
    stacked = jnp.concatenate([lt[:, j * V7X_LANES:(j + 1) * V7X_LANES] for j in range(n_chunks)],
                              axis=0)
    sums = jnp.dot(stacked, upper, preferred_element_type=F32, precision=lax.Precision.HIGHEST)
    chunks = [sums[j * N_HEADS_FOX:(j + 1) * N_HEADS_FOX, :] for j in range(n_chunks)]
    starts_sequence = pl.program_id(0) % narrow == 0
    offset = jnp.where(starts_sequence, 0.0, carry_ref[:, 0:1])
    for j, cs in enumerate(chunks):
      c_ref[:, j * V7X_LANES:(j + 1) * V7X_LANES] = cs + offset
      offset = offset + cs[:, V7X_LANES - 1:V7X_LANES]
    carry_ref[...] = jnp.broadcast_to(offset, carry_ref.shape)
  acc = _dot(h, w_ref[...])
  lo, hi = key_cols
  res = jnp.concatenate([acc[:, :lo], acc[:, lo:hi] * (SCALE * LOG2_E), acc[:, hi:]],
                        axis=1).astype(o_ref.dtype)
  if permute:
    per = PERM_ROWS // RESIDUES
    for a in range(tm // PERM_ROWS):
      for r in range(RESIDUES):
        start = a * PERM_ROWS + r * per
        o_ref[r, a * per:(a + 1) * per, :] = res[start:start + per, :]
  else:
    o_ref[...] = res


def _norm_matmul(x, g, w, out_dtype, key_cols, residue_major=None, fox_gate=None, jobs=()):
  m, d = x.shape
  n = w.shape[1]
  tm = PROJ_ROW_TILE
  narrow = None if fox_gate is None else fox_gate[3] // tm
  permute = residue_major is not None
  assert not (narrow and permute)
  in_specs = [
      pl.BlockSpec((tm, d), lambda i: (i, 0)),
      pl.BlockSpec((1, d), lambda i: (0, 0)),
      pl.BlockSpec((d, n), lambda i: (0, 0)),
  ]
  args = [x, g, w]
  if permute:
    batch, seq = residue_major
    tiles = seq // tm
    in_specs.append(pl.BlockSpec((PERM_ROWS, PERM_ROWS), lambda i: (0, 0)))
    args.append(_swap_matrix())
    out_specs = [pl.BlockSpec((RESIDUES, tm // RESIDUES, n), lambda i: (i // tiles, i % tiles, 0))]
    out_shape = [jax.ShapeDtypeStruct((batch * RESIDUES, seq // RESIDUES, n), out_dtype)]
  else:
    out_specs = [pl.BlockSpec((tm, n), lambda i: (i, 0))]
    out_shape = [jax.ShapeDtypeStruct((m, n), out_dtype)]
  scratch = []
  if narrow:
    w_f, b_f, batch, seq = fox_gate
    in_specs += [pl.BlockSpec(w_f.shape, lambda i: (0, 0)), pl.BlockSpec(b_f.shape, lambda i: (0, 0))]
    args += [w_f, b_f]
    out_specs.append(pl.BlockSpec((None, N_HEADS_FOX, tm), lambda i: (i // narrow, 0, i % narrow)))
    out_shape.append(jax.ShapeDtypeStruct((batch, N_HEADS_FOX, seq), F32))
    scratch.append(pltpu.VMEM((N_HEADS_FOX, V7X_LANES), F32))
  side_arrays, side_in_specs, side_out_specs, side_out_shapes = _side_args(jobs)
  outs = pl.pallas_call(
      functools.partial(_norm_matmul_kernel, permute=permute, narrow=narrow, jobs=tuple(jobs),
                        key_cols=key_cols),
      grid=(m // tm,),
      in_specs=in_specs + side_in_specs,
      out_specs=out_specs + side_out_specs,
      out_shape=out_shape + side_out_shapes,
      scratch_shapes=scratch,
      compiler_params=_params("arbitrary" if narrow else _row_semantics(jobs)),
      name="norm_matmul",
  )(*args, *side_arrays)
  outs = list(outs)
  if permute:
    outs[0] = outs[0].reshape(m, n)
  return outs


def _fox_phases(q_ref, k_ref, v_ref, c_ref, o_ref):
  seq = q_ref.shape[0]
  tq = FOX_Q_TILE
  h = pl.program_id(1)
  crow = c_ref[pl.ds(h, 1), :] * LOG2_E
  row = lax.broadcasted_iota(jnp.int32, (tq, tq), 0)
  col = lax.broadcasted_iota(jnp.int32, (tq, tq), 1)
  diag_mask = jnp.where(col > row, NEG_INF, 0.0).astype(F32)

  def scores(i):
    t0, t1 = i * tq, (i + 1) * tq
    q = q_ref[t0:t1, :]
    bias = crow[:, t1 - 1:t1] - crow[:, 0:t1]
    s_diag = _dot_nt(q, k_ref[t0:t1, :]) + bias[:, t0:t1] + diag_mask
    s_off = _dot_nt(q, k_ref[0:t0, :]) + bias[:, 0:t0] if i > 0 else None
    return s_diag, s_off

  def finish(i, s_diag, s_off):
    t0, t1 = i * tq, (i + 1) * tq
    m = jnp.max(s_diag, axis=-1, keepdims=True)
    if i > 0:
      m = jnp.maximum(m, jnp.max(s_off, axis=-1, keepdims=True))
    o, l = _weighted_values(jnp.exp2(s_diag - m), v_ref[t0:t1, :])
    if i > 0:
      o_off, l_off = _weighted_values(jnp.exp2(s_off - m), v_ref[0:t0, :])
      o, l = o + o_off, l + l_off
    o_ref[t0:t1, :] = (o / l).astype(o_ref.dtype)

  n_tiles = seq // tq
  pending = [scores(i) for i in range(min(FOX_LOOKAHEAD, n_tiles))]
  yield
  for i in range(n_tiles):
    if i + FOX_LOOKAHEAD < n_tiles:
      pending.append(scores(i + FOX_LOOKAHEAD))
      yield
    finish(i, *pending.pop(0))
    yield


def _t5_bucket_np(dist):
  max_exact = N_REL_BUCKETS // 2
  d = np.maximum(dist, 1).astype(np.float32)
  large = max_exact + (np.log(d / np.float32(max_exact))
                       / np.float32(math.log(REL_MAX_DISTANCE / max_exact))
                       * np.float32(N_REL_BUCKETS - max_exact)).astype(np.int32)
  large = np.minimum(large, N_REL_BUCKETS - 1)
  return np.where(dist < max_exact, dist, large).astype(np.int32)


def _block_positions(dilation):
  n = WINDOW_KEYS
  m = RESIDUES // dilation
  rows = n // m
  j = np.arange(m)[:, None]
  qpos = (n + m * np.arange(rows)[None, :] + j).reshape(-1)
  kpos = (m * np.arange(2 * rows)[None, :] + j).reshape(-1)
  return qpos, kpos


def _band_buckets():
  n = WINDOW_KEYS
  tiles = []
  for _, dilation in DILATED_PATTERNS:
    qpos, kpos = _block_positions(dilation)
    rel = qpos[:, None] - np.concatenate([kpos, qpos])[None, :]
    valid = (rel >= 0) & (rel <= n)
    bucket = _t5_bucket_np(np.maximum(rel, 0) * dilation)
    tiles.append(np.where(valid, bucket, -1))
  return np.stack(tiles).astype(np.int32)


def _bias_kernel(tab_ref, bkt_ref, o_ref):
  table = [[tab_ref[b, h] * LOG2_E for h in range(N_HEADS_DIL)] for b in range(N_REL_BUCKETS)]
  for r0 in range(0, bkt_ref.shape[0], BF16_TILE_ROWS):
    rows = slice(r0, r0 + BF16_TILE_ROWS)
    bkt = bkt_ref[rows, :]
    accs = [jnp.full(bkt.shape, NEG_INF, F32)] * N_HEADS_DIL
    for b in range(N_REL_BUCKETS):
      hit = bkt == b
      accs = [jnp.where(hit, table[b][h], acc) for h, acc in enumerate(accs)]
    for h, acc in enumerate(accs):
      o_ref[h, rows, :] = acc


def _band_bias(rel_table):
  buckets = jnp.asarray(_band_buckets())
  p, n, n2 = buckets.shape
  return pl.pallas_call(
      _bias_kernel,
      grid=(p,),
      in_specs=[
          pl.BlockSpec(memory_space=pltpu.SMEM),
          pl.BlockSpec((None, n, n2), lambda i: (i, 0, 0)),
      ],
      out_specs=pl.BlockSpec((None, N_HEADS_DIL, n, n2), lambda i: (i, 0, 0, 0)),
      out_shape=jax.ShapeDtypeStruct((p, N_HEADS_DIL, n, n2), F32),
      compiler_params=_params("parallel"),
      name="band_bias",
  )(rel_table, buckets)


def _dil_phases(q_ref, k_ref, v_ref, bm_ref, o_ref, *scratch):
  seq = q_ref.shape[0]
  n = WINDOW_KEYS
  seg = seq // RESIDUES
  last = len(DILATED_PATTERNS) - 1
  assert DILATED_PATTERNS[last][1] == RESIDUES and n == seg
  accs, lses = scratch[:last], scratch[last:]

  def gather(ref, starts, size):
    return jnp.concatenate([ref[st:st + size, :] for st in starts], axis=0).astype(BF16)

  def block_rows(segments, nb):
    rows = n // len(segments)
    q_starts = [s * seg + nb * rows for s in segments]
    if nb == 0:
      return rows, q_starts, q_starts, rows
    return rows, q_starts, [st - rows for st in q_starts], 2 * rows

  def scores(p, segments, nb):
    rows, q_starts, k_starts, k_rows = block_rows(segments, nb)
    bm = bm_ref[p, :, 2 * n:3 * n] if nb == 0 else bm_ref[p, :, 0:2 * n]
    return _dot_nt(gather(q_ref, q_starts, rows), gather(k_ref, k_starts, k_rows)) + bm

  def finish(p, segments, nb, e, m):
    rows, q_starts, k_starts, k_rows = block_rows(segments, nb)
    o, l = _weighted_values(e, gather(v_ref, k_starts, k_rows))
    o = o / l
    lse = m + jnp.log2(l)
    if p < last:
      for j, st in enumerate(q_starts):
        accs[p][st:st + rows, :] = o[j * rows:(j + 1) * rows, :]
        lses[p][st:st + rows, :] = lse[j * rows:(j + 1) * rows, :]
      return
    seg_rows = slice(q_starts[0], q_starts[0] + n)
    all_lse = [ref[seg_rows, :] for ref in lses] + [lse]
    all_out = [ref[seg_rows, :] for ref in accs] + [o]
    top = functools.reduce(jnp.maximum, all_lse)
    weights = [jnp.exp2(x - top) for x in all_lse]
    add = lambda a, b: a + b
    mixed = (functools.reduce(add, [w * a for w, a in zip(weights, all_out)])
             / functools.reduce(add, weights))
    o_ref[seg_rows, :] = mixed.astype(o_ref.dtype)

  blocks = [(p, list(range(r, RESIDUES, d)), nb)
            for p, (_, d) in enumerate(DILATED_PATTERNS)
            for r in range(d) for nb in range(seq // (n * d))]
  for g in range(0, len(blocks), DIL_BLOCK_GROUP):
    group = blocks[g:g + DIL_BLOCK_GROUP]
    ss = [scores(*blk) for blk in group]
    yield
    ms = [jnp.max(s, axis=-1, keepdims=True) for s in ss]
    es = [jnp.exp2(s - m) for s, m in zip(ss, ms)]
    yield
    for blk, e, m in zip(group, es, ms):
      finish(*blk, e, m)
    yield


def _mixer_kernel(qa_ref, ka_ref, va_ref, c_ref, qb_ref, kb_ref, vb_ref, bm_ref,
                  oa_ref, ob_ref, *scratch):
  streams = [_dil_phases(qb_ref, kb_ref, vb_ref, bm_ref, ob_ref, *scratch),
             _fox_phases(qa_ref, ka_ref, va_ref, c_ref, oa_ref)]
  while streams:
    for stream in list(streams):
      try:
        next(stream)
      except StopIteration:
        streams.remove(stream)


def _token_mixers(u_a, c, u_b, band_bias, batch, seq):
  assert seq // RESIDUES == WINDOW_KEYS and N_HEADS_FOX == N_HEADS_DIL
  blk = lambda off: pl.BlockSpec((seq, HEAD_DIM), lambda b, h: (b, off + h))
  n_pat = len(DILATED_PATTERNS)
  scratch = [pltpu.VMEM((seq, HEAD_DIM), F32) for _ in range(2 * (n_pat - 1))]
  out_blk = pl.BlockSpec((seq, HEAD_DIM), lambda b, h: (b, h))
  return pl.pallas_call(
      _mixer_kernel,
      grid=(batch, N_HEADS_FOX),
      in_specs=[
          blk(0), blk(N_HEADS_FOX), blk(2 * N_HEADS_FOX),
          pl.BlockSpec((None, N_HEADS_FOX, seq), lambda b, h: (b, 0, 0)),
          blk(0), blk(N_HEADS_DIL), blk(2 * N_HEADS_DIL),
          pl.BlockSpec((n_pat, None, WINDOW_KEYS, 3 * WINDOW_KEYS), lambda b, h: (0, h, 0, 0)),
      ],
      out_specs=[out_blk, out_blk],
      out_shape=[jax.ShapeDtypeStruct((batch * seq, D_FOX), BF16),
                 jax.ShapeDtypeStruct((batch * seq, D_DIL), BF16)],
      scratch_shapes=scratch,
      compiler_params=_params("parallel", "arbitrary"),
      name="token_mixers",
  )(u_a, u_a, u_a, c, u_b, u_b, u_b, band_bias)


def _out_proj_kernel(x_ref, a_ref, b_ref, swap_ref, wa_ref, wb_ref, *rest, jobs):
  n_in = sum(len(job.arrays) for job in jobs)
  side_in, o_ref, side_out = rest[:n_in], rest[n_in], rest[n_in + 1:]
  _run_side_jobs(jobs, side_in, side_out)
  per = PERM_ROWS // RESIDUES
  for a in range(x_ref.shape[0] // PERM_ROWS):
    rows = slice(a * PERM_ROWS, (a + 1) * PERM_ROWS)
    slab = jnp.concatenate([b_ref[r, a * per:(a + 1) * per, :] for r in range(RESIDUES)], axis=0)
    o_b = _dot(swap_ref[...], slab).astype(BF16)
    o_ref[rows, :] = x_ref[rows, :] + _dot(a_ref[rows, :], wa_ref[...]) + _dot(o_b, wb_ref[...])


def _out_proj(x, o_a, o_b, wo, batch, seq, jobs=()):
  m, d = x.shape
  tm = PROJ_ROW_TILE
  tiles = seq // tm
  ca, cb = o_a.shape[1], o_b.shape[1]
  assert ca == cb and wo.shape[0] == ca + cb
  resident = pl.Buffered(1)
  side_arrays, side_in_specs, side_out_specs, side_out_shapes = _side_args(jobs)
  return pl.pallas_call(
      functools.partial(_out_proj_kernel, jobs=tuple(jobs)),
      grid=(m // tm,),
      in_specs=[
          pl.BlockSpec((tm, d), lambda i: (i, 0)),
          pl.BlockSpec((tm, ca), lambda i: (i, 0)),
          pl.BlockSpec((RESIDUES, tm // RESIDUES, cb), lambda i: (i // tiles, i % tiles, 0)),
          pl.BlockSpec((PERM_ROWS, PERM_ROWS), lambda i: (0, 0)),
          pl.BlockSpec((ca, d), lambda i: (0, 0), pipeline_mode=resident),
          pl.BlockSpec((cb, d), lambda i: (1, 0), pipeline_mode=resident),
      ] + side_in_specs,
      out_specs=[pl.BlockSpec((tm, d), lambda i: (i, 0))] + side_out_specs,
      out_shape=[jax.ShapeDtypeStruct((m, d), F32)] + side_out_shapes,
      compiler_params=_params(_row_semantics(jobs)),
      name="out_proj",
  )(x, o_a, o_b.reshape(batch * RESIDUES, seq // RESIDUES, cb), _swap_matrix(), wo, wo,
    *side_arrays)


def _ple_kernel(x_ref, p_ref, g_ref, wg_ref, wp_ref, gf_ref, o_ref, *, final_norm):
  x = x_ref[...]
  h = _rmsnorm(x, g_ref[...]).astype(BF16)
  gate = jax.nn.sigmoid(_dot(h, wg_ref[...]))
  y = x + gate * _dot(p_ref[...].astype(BF16), wp_ref[...])
  o_ref[...] = _rmsnorm(y, gf_ref[...]) if final_norm else y


def _ple(x, p, g, w_gate, w_proj, g_final, final_norm):
  m, d = x.shape
  tm = PROJ_ROW_TILE
  return pl.pallas_call(
      functools.partial(_ple_kernel, final_norm=final_norm),
      grid=(m // tm,),
      in_specs=[
          pl.BlockSpec((tm, d), lambda i: (i, 0)),
          pl.BlockSpec((tm, p.shape[1]), lambda i: (i, 0)),
          pl.BlockSpec((1, d), lambda i: (0, 0)),
          pl.BlockSpec(w_gate.shape, lambda i: (0, 0)),
          pl.BlockSpec(w_proj.shape, lambda i: (0, 0)),
          pl.BlockSpec((1, d), lambda i: (0, 0)),
      ],
      out_specs=pl.BlockSpec((tm, d), lambda i: (i, 0)),
      out_shape=jax.ShapeDtypeStruct((m, d), F32),
      compiler_params=_params("parallel"),
      name="ple",
  )(x, p, g, w_gate, w_proj, g_final)


def kernel(x, p, norm_ffn1, ffn1_w_gate, ffn1_w_up, ffn1_w_down, norm_mix, w_in, b_f, w_o,
           norm_ffn2, ffn2_w_gate, ffn2_w_up, ffn2_w_down, norm_ple, w_ple_gate, w_ple_proj,
           rel_table, norm_final):
  batch, seq, d = x.shape
  depth = p.shape[0]
  m = batch * seq
  bf = lambda w: w.astype(BF16)
  row = lambda g: g.reshape(1, -1).astype(F32)

  band_bias = _band_bias(rel_table.astype(F32))
  xs = x.reshape(m, d).astype(F32)
  for i in range(depth):
    g_ffn1 = row(norm_ffn1[i])
    proj_steps = m // PROJ_ROW_TILE
    head_steps = _ffn_head_steps(ffn1_w_gate[i])
    repack = lambda steps: _w_in_repack_jobs(steps, jnp.swapaxes(w_in[i], 0, 1),
                                             3 * D_FOX, N_HEADS_FOX, 3 * D_DIL)
    a_job, _, f_job = repack(head_steps)
    _, b_job, _ = repack(proj_steps)
    head, wg1, wu1, wd1, w_a, w_f = _ffn_head(
        xs, g_ffn1, ffn1_w_gate[i], ffn1_w_up[i], ffn1_w_down[i], jobs=[a_job, f_job])
    xs = _ffn(xs, g_ffn1, wg1, wu1, wd1, head=head)

    b_f_col = b_f[i].astype(F32).reshape(N_HEADS_FOX, 1)
    g_mix = row(norm_mix[i])
    u_a, c, wo, w_b = _norm_matmul(
        xs, g_mix, w_a, BF16, (D_FOX, 2 * D_FOX), fox_gate=(w_f, b_f_col, batch, seq),
        jobs=_slab_cast_jobs(proj_steps, w_o[i]) + [b_job])
    u_b, wg2, wu2, wd2 = _norm_matmul(
        xs, g_mix, w_b, F32, (D_DIL, 2 * D_DIL), residue_major=(batch, seq),
        jobs=_slab_cast_jobs(proj_steps, ffn2_w_gate[i], ffn2_w_up[i])
        + _slab_cast_jobs(proj_steps, ffn2_w_down[i], scale=0.5))

    o_a, o_b = _token_mixers(u_a, c, u_b, band_bias, batch, seq)
    xs, w_gate, w_ple = _out_proj(xs, o_a, o_b, wo, batch, seq,
                                  jobs=_slab_cast_jobs(proj_steps, w_ple_gate[i], w_ple_proj[i]))

    xs = _ffn(xs, row(norm_ffn2[i]), wg2, wu2, wd2)
    last = i == depth - 1
    xs = _ple(xs, p[i].reshape(m, -1), row(norm_ple[i]), w_gate, w_ple,
              row(norm_final), final_norm=last)
  return xs.reshape(batch, seq, d).astype(x.dtype)
```

```python
import functools
import math
from typing import Any, Callable, NamedTuple

import jax
import jax.numpy as jnp
import numpy as np
from jax import lax
from jax.experimental import pallas as pl
from jax.experimental.pallas import tpu as pltpu

F32 = jnp.float32
BF16 = jnp.bfloat16

HEAD_DIM = 128
N_HEADS_FOX = 8
N_HEADS_DIL = 8
D_FOX = N_HEADS_FOX * HEAD_DIM
D_DIL = N_HEADS_DIL * HEAD_DIM
DILATED_PATTERNS = ((128, 1), (512, 4), (2048, 16))
WINDOW_KEYS = 128
N_REL_BUCKETS = 32
REL_MAX_DISTANCE = 2048
RMS_EPS = 1e-6
NEG_INF = -1e30
SCALE = HEAD_DIM ** -0.5
LOG2_E = math.log2(math.e)

V7X_LANES = 128
BF16_TILE_ROWS = 16
V7X_VMEM_LIMIT_BYTES = 56 * 1024 * 1024

RESIDUES = max(d for _, d in DILATED_PATTERNS)
PERM_ROWS = RESIDUES * RESIDUES

FFN_ROW_TILE = 1024
FFN_SUB_ROWS = 512
FFN_FF_TILE = 512
FFN_HEAD_FF_TILE = 256
PROJ_ROW_TILE = 512
FOX_Q_TILE = 256
FOX_LOOKAHEAD = 2
DIL_BLOCK_GROUP = 12


def _params(*semantics):
  return pltpu.CompilerParams(dimension_semantics=semantics,
                              vmem_limit_bytes=V7X_VMEM_LIMIT_BYTES)


def _rmsnorm(x, g):
  ms = jnp.mean(x * x, axis=-1, keepdims=True)
  return x * lax.rsqrt(ms + RMS_EPS) * g


def _dot(a, b):
  return jnp.dot(a, b, preferred_element_type=F32)


def _dot_nt(a, b):
  return lax.dot_general(a, b, (((1,), (1,)), ((), ())), preferred_element_type=F32)


def _weighted_values(e, v):
  both = _dot(e.astype(BF16), jnp.concatenate([v, jnp.ones_like(v)], axis=1))
  return both[:, :HEAD_DIM], both[:, HEAD_DIM:]


class _SideJob(NamedTuple):
  arrays: tuple
  in_specs: tuple
  out_shape: Any
  out_spec: Any
  fn: Callable


def _cast_job(a, block, index, scale=1.0):
  spec = pl.BlockSpec(block, index)
  return _SideJob((a,), (spec,), jax.ShapeDtypeStruct(a.shape, BF16), spec,
                  lambda r: r[...] * scale)


def _run_side_jobs(jobs, in_refs, out_refs):
  in_refs = list(in_refs)
  for job, out_ref in zip(jobs, out_refs):
    refs = [in_refs.pop(0) for _ in job.arrays]
    out_ref[...] = job.fn(*refs).astype(out_ref.dtype)


def _row_semantics(jobs):
  return "arbitrary" if jobs else "parallel"


def _side_args(jobs):
  arrays = [a for job in jobs for a in job.arrays]
  in_specs = [s for job in jobs for s in job.in_specs]
  return arrays, in_specs, [job.out_spec for job in jobs], [job.out_shape for job in jobs]


def _ffn_head_kernel(x_ref, g_ref, wg_ref, wu_ref, wd_ref, *rest, jobs):
  n_in = sum(len(job.arrays) for job in jobs)
  side_in, (o_ref, og_ref, ou_ref, od_ref) = rest[:n_in], rest[n_in:n_in + 4]
  side_out, h_ref = rest[n_in + 4:-1], rest[-1]
  _run_side_jobs(jobs, side_in, side_out)

  @pl.when(pl.program_id(0) == 0)
  def _():
    x = x_ref[...]
    h_ref[...] = _rmsnorm(x, g_ref[...]).astype(BF16)
    o_ref[...] = x

  wg = wg_ref[...].astype(BF16)
  wu = wu_ref[...].astype(BF16)
  wd = (wd_ref[...] * 0.5).astype(BF16)
  og_ref[...] = wg
  ou_ref[...] = wu
  od_ref[...] = wd
  for r in range(h_ref.shape[0] // FFN_SUB_ROWS):
    rows = slice(r * FFN_SUB_ROWS, (r + 1) * FFN_SUB_ROWS)
    h = h_ref[rows, :]
    gate = _dot(h, wg)
    up = _dot(h, wu)
    act = (gate * jax.nn.sigmoid(gate)) * up
    o_ref[rows, :] += _dot(act.astype(BF16), wd)


def _ffn_head_steps(w_gate):
  return w_gate.shape[1] // FFN_HEAD_FF_TILE


def _ffn_head(x, g, w_gate, w_up, w_down, jobs=()):
  d = x.shape[1]
  tm, tf = FFN_ROW_TILE, FFN_HEAD_FF_TILE
  side_arrays, side_in_specs, side_out_specs, side_out_shapes = _side_args(jobs)
  return pl.pallas_call(
      functools.partial(_ffn_head_kernel, jobs=tuple(jobs)),
      grid=(_ffn_head_steps(w_gate),),
      in_specs=[
          pl.BlockSpec((tm, d), lambda j: (0, 0), pipeline_mode=pl.Buffered(1)),
          pl.BlockSpec((1, d), lambda j: (0, 0)),
          pl.BlockSpec((d, tf), lambda j: (0, j)),
          pl.BlockSpec((d, tf), lambda j: (0, j)),
          pl.BlockSpec((tf, d), lambda j: (j, 0)),
      ] + side_in_specs,
      out_specs=[
          pl.BlockSpec((tm, d), lambda j: (0, 0)),
          pl.BlockSpec((d, tf), lambda j: (0, j)),
          pl.BlockSpec((d, tf), lambda j: (0, j)),
          pl.BlockSpec((tf, d), lambda j: (j, 0)),
      ] + side_out_specs,
      out_shape=[
          jax.ShapeDtypeStruct((tm, d), F32),
          jax.ShapeDtypeStruct(w_gate.shape, BF16),
          jax.ShapeDtypeStruct(w_up.shape, BF16),
          jax.ShapeDtypeStruct(w_down.shape, BF16),
      ] + side_out_shapes,
      scratch_shapes=[pltpu.VMEM((tm, d), BF16)],
      compiler_params=_params("arbitrary"),
      name="ffn_head",
  )(x, g, w_gate, w_up, w_down, *side_arrays)


def _ffn_kernel(x_ref, g_ref, wg_ref, wu_ref, wd_ref, *rest, has_head):
  if has_head:
    head_hbm, o_ref, h_ref, copy_sem = rest
  else:
    o_ref, h_ref = rest

  def step(first):
    for r in range(h_ref.shape[0] // FFN_SUB_ROWS):
      rows = slice(r * FFN_SUB_ROWS, (r + 1) * FFN_SUB_ROWS)
      if first:
        base = x_ref[rows, :]
        h = _rmsnorm(base, g_ref[...]).astype(BF16)
        h_ref[rows, :] = h
      else:
        base = o_ref[rows, :]
        h = h_ref[rows, :]
      gate = _dot(h, wg_ref[...])
      up = _dot(h, wu_ref[...])
      act = (gate * jax.nn.sigmoid(gate)) * up
      o_ref[rows, :] = base + _dot(act.astype(BF16), wd_ref[...])

  def compute_tile():
    lax.cond(pl.program_id(1) == 0, lambda: step(True), lambda: step(False))

  def copy_head_tile():
    @pl.when(pl.program_id(1) == 0)
    def _():
      copy = pltpu.make_async_copy(head_hbm, o_ref, copy_sem)
      copy.start()
      copy.wait()

  if has_head:
    lax.cond(pl.program_id(0) == 0, copy_head_tile, compute_tile)
  else:
    compute_tile()


def _ffn(x, g, wg, wu, wd_half, head=None):
  m, d = x.shape
  dff = wg.shape[1]
  tm, tf = FFN_ROW_TILE, FFN_FF_TILE
  has_head = head is not None
  chunk = (lambda i, j: jnp.where(i == 0, 0, j)) if has_head else (lambda i, j: j)
  return pl.pallas_call(
      functools.partial(_ffn_kernel, has_head=has_head),
      grid=(m // tm, dff // tf),
      in_specs=[
          pl.BlockSpec((tm, d), lambda i, j: (i, 0)),
          pl.BlockSpec((1, d), lambda i, j: (0, 0)),
          pl.BlockSpec((d, tf), lambda i, j: (0, chunk(i, j))),
          pl.BlockSpec((d, tf), lambda i, j: (0, chunk(i, j))),
          pl.BlockSpec((tf, d), lambda i, j: (chunk(i, j), 0)),
      ] + ([pl.BlockSpec(memory_space=pl.ANY)] if has_head else []),
      out_specs=pl.BlockSpec((tm, d), lambda i, j: (i, 0)),
      out_shape=jax.ShapeDtypeStruct((m, d), F32),
      scratch_shapes=([pltpu.VMEM((tm, d), BF16)]
                      + ([pltpu.SemaphoreType.DMA(())] if has_head else [])),
      compiler_params=_params("arbitrary" if has_head else "parallel", "arbitrary"),
      name="ffn",
  )(x, g, wg, wu, wd_half, *([head] if has_head else []))


def _slab_cast_jobs(steps, *weights, scale=1.0):
  jobs = []
  for w in weights:
    rows = max(BF16_TILE_ROWS, w.shape[0] // steps)
    assert w.shape[0] % rows == 0 and rows % BF16_TILE_ROWS == 0
    last = w.shape[0] // rows - 1
    jobs.append(_cast_job(w, (rows, w.shape[1]), lambda t, last=last: (jnp.minimum(t, last), 0),
                          scale=scale))
  return jobs


def _w_in_repack_jobs(steps, w_t, n_a, n_f, n_b):
  d = w_t.shape[1]
  lanes = V7X_LANES
  slabs = d // lanes
  assert d % lanes == 0 and slabs <= steps and n_f % 8 == 0 and n_f <= lanes
  assert n_a % n_f == 0 and n_b <= n_a + n_f
  slab = lambda t: jnp.minimum(t, slabs - 1)

  def gate_rows(ref):
    return jnp.concatenate([ref[...], jnp.zeros((lanes - n_f, lanes), F32)], axis=0).T

  out = lambda n: jax.ShapeDtypeStruct((d, n), BF16)
  out_spec = lambda n: pl.BlockSpec((lanes, n), lambda t: (slab(t), 0))
  return [
      _SideJob((w_t,), (pl.BlockSpec((n_a, lanes), lambda t: (0, slab(t))),),
               out(n_a), out_spec(n_a), lambda ref: ref[...].T),
      _SideJob((w_t,), (pl.BlockSpec((n_a + n_f, lanes), lambda t: (1, slab(t))),),
               out(n_b), out_spec(n_b), lambda ref: ref[:n_b, :].T),
      _SideJob((w_t,), (pl.BlockSpec((n_f, lanes), lambda t: (n_a // n_f, slab(t))),),
               out(lanes), out_spec(lanes), gate_rows),
  ]


def _swap_matrix():
  i = np.arange(PERM_ROWS)
  src = (i % RESIDUES) * RESIDUES + i // RESIDUES
  return jnp.asarray(np.eye(PERM_ROWS, dtype=np.float32)[src], BF16)


def _norm_matmul_kernel(x_ref, g_ref, w_ref, *rest, permute, narrow, jobs, key_cols):
  rest = list(rest)
  swap_ref = rest.pop(0) if permute else None
  wn_ref = rest.pop(0) if narrow else None
  side_in = [rest.pop(0) for job in jobs for _ in job.arrays]
  o_ref = rest.pop(0)
  on_ref = rest.pop(0) if narrow else None
  side_out = [rest.pop(0) for _ in jobs]
  assert not rest, "unexpected extra refs"
  tm = x_ref.shape[0]
  _run_side_jobs(jobs, side_in, side_out)

  h = _rmsnorm(x_ref[...], g_ref[...]).astype(BF16)
  if permute:
    h = jnp.concatenate(
        [_dot(swap_ref[...], h[a * PERM_ROWS:(a + 1) * PERM_ROWS, :]).astype(BF16)
         for a in range(tm // PERM_ROWS)], axis=0)
  if narrow:
    on_ref[...] = _dot(h, wn_ref[...])
  acc = _dot(h, w_ref[...])
  lo, hi = key_cols
  res = jnp.concatenate([acc[:, :lo], acc[:, lo:hi] * (SCALE * LOG2_E), acc[:, hi:]],
                        axis=1).astype(o_ref.dtype)
  if permute:
    per = PERM_ROWS // RESIDUES
    for a in range(tm // PERM_ROWS):
      for r in range(RESIDUES):
        start = a * PERM_ROWS + r * per
        o_ref[r, a * per:(a + 1) * per, :] = res[start:start + per, :]
  else:
    o_ref[...] = res


def _norm_matmul(x, g, w, out_dtype, key_cols, residue_major=None, w_narrow=None, jobs=()):
  m, d = x.shape
  n = w.shape[1]
  tm = PROJ_ROW_TILE
  narrow = w_narrow is not None
  permute = residue_major is not None
  assert not (narrow and permute)
  in_specs = [
      pl.BlockSpec((tm, d), lambda i: (i, 0)),
      pl.BlockSpec((1, d), lambda i: (0, 0)),
      pl.BlockSpec((d, n), lambda i: (0, 0)),
  ]
  args = [x, g, w]
  if permute:
    batch, seq = residue_major
    tiles = seq // tm
    in_specs.append(pl.BlockSpec((PERM_ROWS, PERM_ROWS), lambda i: (0, 0)))
    args.append(_swap_matrix())
    out_specs = [pl.BlockSpec((RESIDUES, tm // RESIDUES, n), lambda i: (i // tiles, i % tiles, 0))]
    out_shape = [jax.ShapeDtypeStruct((batch * RESIDUES, seq // RESIDUES, n), out_dtype)]
  else:
    out_specs = [pl.BlockSpec((tm, n), lambda i: (i, 0))]
    out_shape = [jax.ShapeDtypeStruct((m, n), out_dtype)]
  if narrow:
    in_specs.append(pl.BlockSpec(w_narrow.shape, lambda i: (0, 0)))
    args.append(w_narrow)
    out_specs.append(pl.BlockSpec((tm, w_narrow.shape[1]), lambda i: (i, 0)))
    out_shape.append(jax.ShapeDtypeStruct((m, w_narrow.shape[1]), F32))
  side_arrays, side_in_specs, side_out_specs, side_out_shapes = _side_args(jobs)
  outs = pl.pallas_call(
      functools.partial(_norm_matmul_kernel, permute=permute, narrow=narrow, jobs=tuple(jobs),
                        key_cols=key_cols),
      grid=(m // tm,),
      in_specs=in_specs + side_in_specs,
      out_specs=out_specs + side_out_specs,
      out_shape=out_shape + side_out_shapes,
      compiler_params=_params(_row_semantics(jobs)),
      name="norm_matmul",
  )(*args, *side_arrays)
  outs = list(outs)
  if permute:
    outs[0] = outs[0].reshape(m, n)
  return outs


def _cumsum_kernel(fl_ref, bf_ref, c_ref):
  s = fl_ref.shape[0]
  z = fl_ref[...].T[0:N_HEADS_FOX, :] + bf_ref[...]
  lt = jnp.minimum(z, 0.0) - jnp.log1p(jnp.exp(-jnp.abs(z)))
  row = lax.broadcasted_iota(jnp.int32, (V7X_LANES, V7X_LANES), 0)
  col = lax.broadcasted_iota(jnp.int32, (V7X_LANES, V7X_LANES), 1)
  upper = (row <= col).astype(F32)
  chunks = [jnp.dot(lt[:, j * V7X_LANES:(j + 1) * V7X_LANES], upper, preferred_element_type=F32,
                    precision=lax.Precision.HIGHEST) for j in range(s // V7X_LANES)]
  offset = jnp.zeros((N_HEADS_FOX, 1), F32)
  for j, cs in enumerate(chunks):
    c_ref[:, j * V7X_LANES:(j + 1) * V7X_LANES] = cs + offset
    offset = offset + cs[:, V7X_LANES - 1:V7X_LANES]


def _fox_decay(f_logit, b_f_col, batch, seq):
  return pl.pallas_call(
      _cumsum_kernel,
      grid=(batch,),
      in_specs=[
          pl.BlockSpec((seq, V7X_LANES), lambda b: (b, 0)),
          pl.BlockSpec((N_HEADS_FOX, 1), lambda b: (0, 0)),
      ],
      out_specs=pl.BlockSpec((None, N_HEADS_FOX, seq), lambda b: (b, 0, 0)),
      out_shape=jax.ShapeDtypeStruct((batch, N_HEADS_FOX, seq), F32),
      compiler_params=_params("parallel"),
      name="fox_decay",
  )(f_logit, b_f_col)


def _fox_phases(q_ref, k_ref, v_ref, c_ref, o_ref):
  seq = q_ref.shape[0]
  tq = FOX_Q_TILE
  h = pl.program_id(0)
  crow = c_ref[pl.ds(h, 1), :] * LOG2_E
  row = lax.broadcasted_iota(jnp.int32, (tq, tq), 0)
  col = lax.broadcasted_iota(jnp.int32, (tq, tq), 1)
  diag_mask = jnp.where(col > row, NEG_INF, 0.0).astype(F32)

  def scores(i):
    t0, t1 = i * tq, (i + 1) * tq
    q = q_ref[t0:t1, :]
    bias = crow[:, t1 - 1:t1] - crow[:, 0:t1]
    s_diag = _dot_nt(q, k_ref[t0:t1, :]) + bias[:, t0:t1] + diag_mask
    s_off = _dot_nt(q, k_ref[0:t0, :]) + bias[:, 0:t0] if i > 0 else None
    return s_diag, s_off

  def finish(i, s_diag, s_off):
    t0, t1 = i * tq, (i + 1) * tq
    m = jnp.max(s_diag, axis=-1, keepdims=True)
    if i > 0:
      m = jnp.maximum(m, jnp.max(s_off, axis=-1, keepdims=True))
    o, l = _weighted_values(jnp.exp2(s_diag - m), v_ref[t0:t1, :])
    if i > 0:
      o_off, l_off = _weighted_values(jnp.exp2(s_off - m), v_ref[0:t0, :])
      o, l = o + o_off, l + l_off
    o_ref[t0:t1, :] = (o / l).astype(o_ref.dtype)

  n_tiles = seq // tq
  pending = [scores(i) for i in range(min(FOX_LOOKAHEAD, n_tiles))]
  yield
  for i in range(n_tiles):
    if i + FOX_LOOKAHEAD < n_tiles:
      pending.append(scores(i + FOX_LOOKAHEAD))
      yield
    finish(i, *pending.pop(0))
    yield


def _t5_bucket_np(dist):
  max_exact = N_REL_BUCKETS // 2
  d = np.maximum(dist, 1).astype(np.float32)
  large = max_exact + (np.log(d / np.float32(max_exact))
                       / np.float32(math.log(REL_MAX_DISTANCE / max_exact))
                       * np.float32(N_REL_BUCKETS - max_exact)).astype(np.int32)
  large = np.minimum(large, N_REL_BUCKETS - 1)
  return np.where(dist < max_exact, dist, large).astype(np.int32)


def _block_positions(dilation):
  n = WINDOW_KEYS
  m = RESIDUES // dilation
  rows = n // m
  j = np.arange(m)[:, None]
  qpos = (n + m * np.arange(rows)[None, :] + j).reshape(-1)
  kpos = (m * np.arange(2 * rows)[None, :] + j).reshape(-1)
  return qpos, kpos


def _band_buckets():
  n = WINDOW_KEYS
  tiles = []
  for _, dilation in DILATED_PATTERNS:
    qpos, kpos = _block_positions(dilation)
    rel = qpos[:, None] - np.concatenate([kpos, qpos])[None, :]
    valid = (rel >= 0) & (rel <= n)
    bucket = _t5_bucket_np(np.maximum(rel, 0) * dilation)
    tiles.append(np.where(valid, bucket, -1))
  return np.stack(tiles).astype(np.int32)


def _bias_kernel(tab_ref, bkt_ref, o_ref):
  table = [[tab_ref[b, h] * LOG2_E for h in range(N_HEADS_DIL)] for b in range(N_REL_BUCKETS)]
  for r0 in range(0, bkt_ref.shape[0], BF16_TILE_ROWS):
    rows = slice(r0, r0 + BF16_TILE_ROWS)
    bkt = bkt_ref[rows, :]
    accs = [jnp.full(bkt.shape, NEG_INF, F32)] * N_HEADS_DIL
    for b in range(N_REL_BUCKETS):
      hit = bkt == b
      accs = [jnp.where(hit, table[b][h], acc) for h, acc in enumerate(accs)]
    for h, acc in enumerate(accs):
      o_ref[h, rows, :] = acc


def _band_bias(rel_table):
  buckets = jnp.asarray(_band_buckets())
  p, n, n2 = buckets.shape
  return pl.pallas_call(
      _bias_kernel,
      grid=(p,),
      in_specs=[
          pl.BlockSpec(memory_space=pltpu.SMEM),
          pl.BlockSpec((None, n, n2), lambda i: (i, 0, 0)),
      ],
      out_specs=pl.BlockSpec((None, N_HEADS_DIL, n, n2), lambda i: (i, 0, 0, 0)),
      out_shape=jax.ShapeDtypeStruct((p, N_HEADS_DIL, n, n2), F32),
      compiler_params=_params("parallel"),
      name="band_bias",
  )(rel_table, buckets)


def _dil_phases(q_ref, k_ref, v_ref, bm_ref, o_ref, *scratch):
  seq = q_ref.shape[0]
  n = WINDOW_KEYS
  seg = seq // RESIDUES
  last = len(DILATED_PATTERNS) - 1
  assert DILATED_PATTERNS[last][1] == RESIDUES and n == seg
  accs, lses = scratch[:last], scratch[last:]

  def gather(ref, starts, size):
    return jnp.concatenate([ref[st:st + size, :] for st in starts], axis=0).astype(BF16)

  def block_rows(segments, nb):
    rows = n // len(segments)
    q_starts = [s * seg + nb * rows for s in segments]
    if nb == 0:
      return rows, q_starts, q_starts, rows
    return rows, q_starts, [st - rows for st in q_starts], 2 * rows

  def scores(p, segments, nb):
    rows, q_starts, k_starts, k_rows = block_rows(segments, nb)
    bm = bm_ref[p, :, 2 * n:3 * n] if nb == 0 else bm_ref[p, :, 0:2 * n]
    return _dot_nt(gather(q_ref, q_starts, rows), gather(k_ref, k_starts, k_rows)) + bm

  def finish(p, segments, nb, e, m):
    rows, q_starts, k_starts, k_rows = block_rows(segments, nb)
    o, l = _weighted_values(e, gather(v_ref, k_starts, k_rows))
    o = o / l
    lse = m + jnp.log2(l)
    if p < last:
      for j, st in enumerate(q_starts):
        accs[p][st:st + rows, :] = o[j * rows:(j + 1) * rows, :]
        lses[p][st:st + rows, :] = lse[j * rows:(j + 1) * rows, :]
      return
    seg_rows = slice(q_starts[0], q_starts[0] + n)
    all_lse = [ref[seg_rows, :] for ref in lses] + [lse]
    all_out = [ref[seg_rows, :] for ref in accs] + [o]
    top = functools.reduce(jnp.maximum, all_lse)
    weights = [jnp.exp2(x - top) for x in all_lse]
    add = lambda a, b: a + b
    mixed = (functools.reduce(add, [w * a for w, a in zip(weights, all_out)])
             / functools.reduce(add, weights))
    o_ref[seg_rows, :] = mixed.astype(o_ref.dtype)

  blocks = [(p, list(range(r, RESIDUES, d)), nb)
            for p, (_, d) in enumerate(DILATED_PATTERNS)
            for r in range(d) for nb in range(seq // (n * d))]
  for g in range(0, len(blocks), DIL_BLOCK_GROUP):
    group = blocks[g:g + DIL_BLOCK_GROUP]
    ss = [scores(*blk) for blk in group]
    yield
    ms = [jnp.max(s, axis=-1, keepdims=True) for s in ss]
    es = [jnp.exp2(s - m) for s, m in zip(ss, ms)]
    yield
    for blk, e, m in zip(group, es, ms):
      finish(*blk, e, m)
    yield


def _mixer_kernel(qa_ref, ka_ref, va_ref, c_ref, qb_ref, kb_ref, vb_ref, bm_ref,
                  oa_ref, ob_ref, *scratch):
  streams = [_dil_phases(qb_ref, kb_ref, vb_ref, bm_ref, ob_ref, *scratch),
             _fox_phases(qa_ref, ka_ref, va_ref, c_ref, oa_ref)]
  while streams:
    for stream in list(streams):
      try:
        next(stream)
      except StopIteration:
        streams.remove(stream)


def _token_mixers(u_a, c, u_b, band_bias, batch, seq):
  assert seq // RESIDUES == WINDOW_KEYS and N_HEADS_FOX == N_HEADS_DIL
  blk = lambda off: pl.BlockSpec((seq, HEAD_DIM), lambda h, b: (b, off + h))
  n_pat = len(DILATED_PATTERNS)
  scratch = [pltpu.VMEM((seq, HEAD_DIM), F32) for _ in range(2 * (n_pat - 1))]
  out_blk = pl.BlockSpec((seq, HEAD_DIM), lambda h, b: (b, h))
  return pl.pallas_call(
      _mixer_kernel,
      grid=(N_HEADS_FOX, batch),
      in_specs=[
          blk(0), blk(N_HEADS_FOX), blk(2 * N_HEADS_FOX),
          pl.BlockSpec((None, N_HEADS_FOX, seq), lambda h, b: (b, 0, 0)),
          blk(0), blk(N_HEADS_DIL), blk(2 * N_HEADS_DIL),
          pl.BlockSpec((n_pat, None, WINDOW_KEYS, 3 * WINDOW_KEYS), lambda h, b: (0, h, 0, 0)),
      ],
      out_specs=[out_blk, out_blk],
      out_shape=[jax.ShapeDtypeStruct((batch * seq, D_FOX), BF16),
                 jax.ShapeDtypeStruct((batch * seq, D_DIL), BF16)],
      scratch_shapes=scratch,
      compiler_params=_params("parallel", "arbitrary"),
      name="token_mixers",
  )(u_a, u_a, u_a, c, u_b, u_b, u_b, band_bias)


def _out_proj_kernel(x_ref, a_ref, b_ref, swap_ref, wa_ref, wb_ref, *rest, jobs):
  n_in = sum(len(job.arrays) for job in jobs)
  side_in, o_ref, side_out = rest[:n_in], rest[n_in], rest[n_in + 1:]
  _run_side_jobs(jobs, side_in, side_out)
  per = PERM_ROWS // RESIDUES
  for a in range(x_ref.shape[0] // PERM_ROWS):
    rows = slice(a * PERM_ROWS, (a + 1) * PERM_ROWS)
    slab = jnp.concatenate([b_ref[r, a * per:(a + 1) * per, :] for r in range(RESIDUES)], axis=0)
    o_b = _dot(swap_ref[...], slab).astype(BF16)
    o_ref[rows, :] = x_ref[rows, :] + _dot(a_ref[rows, :], wa_ref[...]) + _dot(o_b, wb_ref[...])


def _out_proj(x, o_a, o_b, wo, batch, seq, jobs=()):
  m, d = x.shape
  tm = PROJ_ROW_TILE
  tiles = seq // tm
  ca, cb = o_a.shape[1], o_b.shape[1]
  assert ca == cb and wo.shape[0] == ca + cb
  resident = pl.Buffered(1)
  side_arrays, side_in_specs, side_out_specs, side_out_shapes = _side_args(jobs)
  return pl.pallas_call(
      functools.partial(_out_proj_kernel, jobs=tuple(jobs)),
      grid=(m // tm,),
      in_specs=[
          pl.BlockSpec((tm, d), lambda i: (i, 0)),
          pl.BlockSpec((tm, ca), lambda i: (i, 0)),
          pl.BlockSpec((RESIDUES, tm // RESIDUES, cb), lambda i: (i // tiles, i % tiles, 0)),
          pl.BlockSpec((PERM_ROWS, PERM_ROWS), lambda i: (0, 0)),
          pl.BlockSpec((ca, d), lambda i: (0, 0), pipeline_mode=resident),
          pl.BlockSpec((cb, d), lambda i: (1, 0), pipeline_mode=resident),
      ] + side_in_specs,
      out_specs=[pl.BlockSpec((tm, d), lambda i: (i, 0))] + side_out_specs,
      out_shape=[jax.ShapeDtypeStruct((m, d), F32)] + side_out_shapes,
      compiler_params=_params(_row_semantics(jobs)),
      name="out_proj",
  )(x, o_a, o_b.reshape(batch * RESIDUES, seq // RESIDUES, cb), _swap_matrix(), wo, wo,
    *side_arrays)


def _ple_kernel(x_ref, p_ref, g_ref, wg_ref, wp_ref, gf_ref, o_ref, *, final_norm):
  x = x_ref[...]
  h = _rmsnorm(x, g_ref[...]).astype(BF16)
  gate = jax.nn.sigmoid(_dot(h, wg_ref[...]))
  y = x + gate * _dot(p_ref[...].astype(BF16), wp_ref[...])
  o_ref[...] = _rmsnorm(y, gf_ref[...]) if final_norm else y


def _ple(x, p, g, w_gate, w_proj, g_final, final_norm):
  m, d = x.shape
  tm = PROJ_ROW_TILE
  return pl.pallas_call(
      functools.partial(_ple_kernel, final_norm=final_norm),
      grid=(m // tm,),
      in_specs=[
          pl.BlockSpec((tm, d), lambda i: (i, 0)),
          pl.BlockSpec((tm, p.shape[1]), lambda i: (i, 0)),
          pl.BlockSpec((1, d), lambda i: (0, 0)),
          pl.BlockSpec(w_gate.shape, lambda i: (0, 0)),
          pl.BlockSpec(w_proj.shape, lambda i: (0, 0)),
          pl.BlockSpec((1, d), lambda i: (0, 0)),
      ],
      out_specs=pl.BlockSpec((tm, d), lambda i: (i, 0)),
      out_shape=jax.ShapeDtypeStruct((m, d), F32),
      compiler_params=_params("parallel"),
      name="ple",
  )(x, p, g, w_gate, w_proj, g_final)


def kernel(x, p, norm_ffn1, ffn1_w_gate, ffn1_w_up, ffn1_w_down, norm_mix, w_in, b_f, w_o,
           norm_ffn2, ffn2_w_gate, ffn2_w_up, ffn2_w_down, norm_ple, w_ple_gate, w_ple_proj,
           rel_table, norm_final):
  batch, seq, d = x.shape
  depth = p.shape[0]
  m = batch * seq
  bf = lambda w: w.astype(BF16)
  row = lambda g: g.reshape(1, -1).astype(F32)

  band_bias = _band_bias(rel_table.astype(F32))
  xs = x.reshape(m, d).astype(F32)
  for i in range(depth):
    g_ffn1 = row(norm_ffn1[i])
    proj_steps = m // PROJ_ROW_TILE
    head_steps = _ffn_head_steps(ffn1_w_gate[i])
    repack = lambda steps: _w_in_repack_jobs(steps, jnp.swapaxes(w_in[i], 0, 1),
                                             3 * D_FOX, N_HEADS_FOX, 3 * D_DIL)
    a_job, _, f_job = repack(head_steps)
    _, b_job, _ = repack(proj_steps)
    head, wg1, wu1, wd1, w_a, w_f = _ffn_head(
        xs, g_ffn1, ffn1_w_gate[i], ffn1_w_up[i], ffn1_w_down[i], jobs=[a_job, f_job])
    xs = _ffn(xs, g_ffn1, wg1, wu1, wd1, head=head)

    b_f_col = b_f[i].astype(F32).reshape(N_HEADS_FOX, 1)
    g_mix = row(norm_mix[i])
    u_a, f_logit, wo, w_b = _norm_matmul(
        xs, g_mix, w_a, BF16, (D_FOX, 2 * D_FOX), w_narrow=w_f,
        jobs=_slab_cast_jobs(proj_steps, w_o[i]) + [b_job])
    u_b, wg2, wu2, wd2 = _norm_matmul(
        xs, g_mix, w_b, F32, (D_DIL, 2 * D_DIL), residue_major=(batch, seq),
        jobs=_slab_cast_jobs(proj_steps, ffn2_w_gate[i], ffn2_w_up[i])
        + _slab_cast_jobs(proj_steps, ffn2_w_down[i], scale=0.5))

    c = _fox_decay(f_logit, b_f_col, batch, seq)
    o_a, o_b = _token_mixers(u_a, c, u_b, band_bias, batch, seq)
    xs, w_gate, w_ple = _out_proj(xs, o_a, o_b, wo, batch, seq,
                                  jobs=_slab_cast_jobs(proj_steps, w_ple_gate[i], w_ple_proj[i]))

    xs = _ffn(xs, row(norm_ffn2[i]), wg2, wu2, wd2)
    last = i == depth - 1
    xs = _ple(xs, p[i].reshape(m, -1), row(norm_ple[i]), w_gate, w_ple,
              row(norm_final), final_norm=last)
  return xs.reshape(batch, seq, d).astype(x.dtype)
```
